```python
import math
import jax, jax.numpy as jnp
from jax import lax
import numpy as np

D_MODEL = 2048
BATCH = 8
SEQ = 8192
DEPTH = 4

N_MEM = 256
D_MIX = 2 * D_MODEL
D_SSD = D_MIX // 2
D_SC = D_MIX - D_SSD
SSD_HEADDIM = 64
SSD_HEADS = D_SSD // SSD_HEADDIM
SSD_GROUPS = 4
SSD_STATE = 128
SSD_CONV = 4
SSD_CHUNK = 256
D_XBC = D_SSD + 2 * SSD_GROUPS * SSD_STATE
DT_MIN = 1e-3
DT_MAX = 1e-1
SC_CONV = 3
SC_GROUPS = 16
XA_HEADS = 4
XA_HEADDIM = 128
D_XA = XA_HEADS * XA_HEADDIM
D_FF = ((8 * D_MODEL + 3 * 256 - 1) // (3 * 256)) * 256
NORM_EPS = 1e-5
D_IN_PROJ = D_SSD + D_XBC + SSD_HEADS + 3 * D_SC
IN_SPLITS = [D_SSD,
             D_SSD + D_XBC,
             D_SSD + D_XBC + SSD_HEADS,
             D_SSD + D_XBC + SSD_HEADS + D_SC,
             D_SSD + D_XBC + SSD_HEADS + 2 * D_SC]

kernel_name = "hybrid_ssd_shortconv_memxattn_trunk"


def rmsnorm(x, g):
    xf = x.astype(jnp.float32)
    xf = xf * lax.rsqrt(jnp.mean(xf * xf, axis=-1, keepdims=True) + NORM_EPS)
    return (xf * g.astype(jnp.float32)).astype(x.dtype)


def grouped_rmsnorm(x, g, n_groups):
    shp = x.shape
    xf = x.astype(jnp.float32).reshape(shp[:-1] + (n_groups, shp[-1] // n_groups))
    xf = xf * lax.rsqrt(jnp.mean(xf * xf, axis=-1, keepdims=True) + NORM_EPS)
    return (xf.reshape(shp) * g.astype(jnp.float32)).astype(x.dtype)


def causal_dwconv(x, w):
    k, c = w.shape
    return lax.conv_general_dilated(
        x, w[:, None, :].astype(x.dtype), window_strides=(1,), padding=[(k - 1, 0)],
        dimension_numbers=("NWC", "WIO", "NWC"), feature_group_count=c)


def ssd_chunked(xh, dt, a, bm, cm):
    f32 = jnp.float32
    bsz, seqlen, nh, hd = xh.shape
    g, n = bm.shape[2], bm.shape[3]
    r = nh // g
    pad = (-seqlen) % SSD_CHUNK
    xh, dt, bm, cm = (t.astype(f32) for t in (xh, dt, bm, cm))
    if pad:
        xh = jnp.pad(xh, ((0, 0), (0, pad), (0, 0), (0, 0)))
        dt = jnp.pad(dt, ((0, 0), (0, pad), (0, 0)))
        bm = jnp.pad(bm, ((0, 0), (0, pad), (0, 0), (0, 0)))
        cm = jnp.pad(cm, ((0, 0), (0, pad), (0, 0), (0, 0)))
    nc = (seqlen + pad) // SSD_CHUNK
    L = SSD_CHUNK
    x = (xh * dt[..., None]).reshape(bsz, nc, L, g, r, hd)
    a_dt = jnp.moveaxis((dt * a.astype(f32)).reshape(bsz, nc, L, g, r), 2, -1)
    a_cs = jnp.cumsum(a_dt, axis=-1)
    bc = bm.reshape(bsz, nc, L, g, n)
    cc = cm.reshape(bsz, nc, L, g, n)
    causal = jnp.tril(jnp.ones((L, L), dtype=bool))
    decay = jnp.exp(jnp.where(causal, a_cs[..., :, None] - a_cs[..., None, :], -jnp.inf))
    cb = jnp.einsum("bclgn,bcsgn->bcgls", cc, bc)
    scores = cb[:, :, :, None] * decay
    y_diag = jnp.einsum("bcgrls,bcsgrp->bclgrp", scores, x)
    decay_states = jnp.exp(a_cs[..., -1:] - a_cs)
    states = jnp.einsum("bclgn,bcgrl,bclgrp->bcgrpn", bc, decay_states, x)
    chunk_decay = jnp.exp(a_cs[..., -1])

    def step(h, inp):
        s_c, d_c = inp
        return h * d_c[..., None, None] + s_c, h

    h0 = jnp.zeros_like(states[:, 0])
    _, prev = lax.scan(step, h0, (jnp.moveaxis(states, 1, 0), jnp.moveaxis(chunk_decay, 1, 0)))
    prev = jnp.moveaxis(prev, 0, 1)
    y_off = jnp.einsum("bclgn,bcgrpn,bcgrl->bclgrp", cc, prev, jnp.exp(a_cs))
    y = (y_diag + y_off).reshape(bsz, nc * L, nh, hd)
    return y[:, :seqlen]


def hybrid_mixer(h, w_in, ssd_conv_w, ssd_conv_b, dt_bias, a_log, d_skip, ssd_norm,
                 sc_conv_w, sc_norm, w_out):
    bsz, seqlen, _ = h.shape
    proj = h @ w_in
    z, xbc, dt_raw, sc_u, sc_b, sc_c = jnp.split(proj, IN_SPLITS, axis=-1)
    xbc = jax.nn.silu(causal_dwconv(xbc, ssd_conv_w) + ssd_conv_b)
    xs, bm, cm = jnp.split(xbc, [D_SSD, D_SSD + SSD_GROUPS * SSD_STATE], axis=-1)
    xh = xs.reshape(bsz, seqlen, SSD_HEADS, SSD_HEADDIM)
    dt = jax.nn.softplus(dt_raw.astype(jnp.float32) + dt_bias.astype(jnp.float32))
    a = -jnp.exp(a_log.astype(jnp.float32))
    y = ssd_chunked(xh, dt, a,
                    bm.reshape(bsz, seqlen, SSD_GROUPS, SSD_STATE),
                    cm.reshape(bsz, seqlen, SSD_GROUPS, SSD_STATE))
    y = y + xh.astype(jnp.float32) * d_skip.astype(jnp.float32)[:, None]
    y = y.reshape(bsz, seqlen, D_SSD).astype(h.dtype)
    y_ssd = grouped_rmsnorm(y * jax.nn.silu(z), ssd_norm, SSD_GROUPS)
    v = sc_b * causal_dwconv(sc_c * sc_u, sc_conv_w)
    y_sc = grouped_rmsnorm(v, sc_norm, SC_GROUPS)
    return jnp.concatenate([y_ssd, y_sc], axis=-1) @ w_out


def memory_cross_attention(h, mem_n, w_q, w_k, w_v, w_o):
    bsz, seqlen, _ = h.shape
    n_mem = mem_n.shape[1]
    q = (h @ w_q).reshape(bsz, seqlen, XA_HEADS, XA_HEADDIM)
    k = (mem_n @ w_k).reshape(bsz, n_mem, XA_HEADS, XA_HEADDIM)
    v = (mem_n @ w_v).reshape(bsz, n_mem, XA_HEADS, XA_HEADDIM)
    scores = jnp.einsum("bshd,bmhd->bhsm", q, k).astype(jnp.float32) * (XA_HEADDIM ** -0.5)
    p = jax.nn.softmax(scores, axis=-1).astype(v.dtype)
    o = jnp.einsum("bhsm,bmhd->bshd", p, v).reshape(bsz, seqlen, D_XA)
    return o @ w_o


def swiglu(h, w_gate, w_up, w_down):
    return (jax.nn.silu(h @ w_gate) * (h @ w_up)) @ w_down


def _fwd_setup_inputs(seed: int = 0) -> dict:
    key = jax.random.key(seed)
    ks = jax.random.split(key, 24)
    f32 = jnp.float32

    def nrm(k, shape, scale):
        return jax.random.normal(k, shape, f32) * scale

    def gain(k, shape):
        return 1.0 + 0.02 * jax.random.normal(k, shape, f32)

    x = nrm(ks[0], (BATCH, SEQ, D_MODEL), 1.0)
    mem = nrm(ks[1], (BATCH, N_MEM, D_MODEL), 1.0)
    norm_mix = gain(ks[2], (DEPTH, D_MODEL))
    w_in = nrm(ks[3], (DEPTH, D_MODEL, D_IN_PROJ), D_MODEL ** -0.5)
    ssd_conv_w = nrm(ks[4], (DEPTH, SSD_CONV, D_XBC), SSD_CONV ** -0.5)
    ssd_conv_b = nrm(ks[5], (DEPTH, D_XBC), 0.02)
    dt0 = jnp.exp(jax.random.uniform(ks[6], (DEPTH, SSD_HEADS), f32, math.log(DT_MIN), math.log(DT_MAX)))
    dt_bias = dt0 + jnp.log(-jnp.expm1(-dt0))
    a_log = jnp.log(jax.random.uniform(ks[7], (DEPTH, SSD_HEADS), f32, 1.0, 16.0))
    d_skip = gain(ks[8], (DEPTH, SSD_HEADS))
    ssd_norm = gain(ks[9], (DEPTH, D_SSD))
    sc_conv_w = nrm(ks[10], (DEPTH, SC_CONV, D_SC), SC_CONV ** -0.5)
    sc_norm = gain(ks[11], (DEPTH, D_SC))
    w_out = nrm(ks[12], (DEPTH, D_MIX, D_MODEL), D_MIX ** -0.5)
    mem_norm = gain(ks[13], (D_MODEL,))
    norm_xa = gain(ks[14], (DEPTH, D_MODEL))
    w_q = nrm(ks[15], (DEPTH, D_MODEL, D_XA), D_MODEL ** -0.5)
    w_k = nrm(ks[16], (DEPTH, D_MODEL, D_XA), D_MODEL ** -0.5)
    w_v = nrm(ks[17], (DEPTH, D_MODEL, D_XA), D_MODEL ** -0.5)
    w_o = nrm(ks[18], (DEPTH, D_XA, D_MODEL), D_XA ** -0.5)
    norm_ffn = gain(ks[19], (DEPTH, D_MODEL))
    w_gate = nrm(ks[20], (DEPTH, D_MODEL, D_FF), D_MODEL ** -0.5)
    w_up = nrm(ks[21], (DEPTH, D_MODEL, D_FF), D_MODEL ** -0.5)
    w_down = nrm(ks[22], (DEPTH, D_FF, D_MODEL), D_FF ** -0.5)
    norm_final = gain(ks[23], (D_MODEL,))
    return {"x": x, "mem": mem, "norm_mix": norm_mix, "w_in": w_in,
            "ssd_conv_w": ssd_conv_w, "ssd_conv_b": ssd_conv_b, "dt_bias": dt_bias,
            "a_log": a_log, "d_skip": d_skip, "ssd_norm": ssd_norm,
            "sc_conv_w": sc_conv_w, "sc_norm": sc_norm, "w_out": w_out,
            "mem_norm": mem_norm, "norm_xa": norm_xa, "w_q": w_q, "w_k": w_k,
            "w_v": w_v, "w_o": w_o, "norm_ffn": norm_ffn, "w_gate": w_gate,
            "w_up": w_up, "w_down": w_down, "norm_final": norm_final}


def _fwd_reference(x, mem, norm_mix, w_in, ssd_conv_w, ssd_conv_b, dt_bias, a_log, d_skip,
              ssd_norm, sc_conv_w, sc_norm, w_out, mem_norm, norm_xa, w_q, w_k, w_v, w_o,
              norm_ffn, w_gate, w_up, w_down, norm_final):
    mem_n = rmsnorm(mem, mem_norm)
    h = x
    for i in range(DEPTH):
        h = h + hybrid_mixer(rmsnorm(h, norm_mix[i]), w_in[i], ssd_conv_w[i], ssd_conv_b[i],
                             dt_bias[i], a_log[i], d_skip[i], ssd_norm[i],
                             sc_conv_w[i], sc_norm[i], w_out[i])
        h = h + memory_cross_attention(rmsnorm(h, norm_xa[i]), mem_n, w_q[i], w_k[i], w_v[i], w_o[i])
        h = h + swiglu(rmsnorm(h, norm_ffn[i]), w_gate[i], w_up[i], w_down[i])
    return rmsnorm(h, norm_final)


import jax as _jax
import jax.numpy as _jnp

TWIN_FORMAT = 'train_step'
FWD_PARAMS = ['x', 'mem', 'norm_mix', 'w_in', 'ssd_conv_w', 'ssd_conv_b', 'dt_bias', 'a_log', 'd_skip', 'ssd_norm', 'sc_conv_w', 'sc_norm', 'w_out', 'mem_norm', 'norm_xa', 'w_q', 'w_k', 'w_v', 'w_o', 'norm_ffn', 'w_gate', 'w_up', 'w_down', 'norm_final']
TWIN_WEIGHTS = ['norm_mix', 'w_in', 'ssd_conv_w', 'ssd_conv_b', 'dt_bias', 'a_log', 'd_skip', 'ssd_norm', 'sc_conv_w', 'sc_norm', 'w_out', 'mem_norm', 'norm_xa', 'w_q', 'w_k', 'w_v', 'w_o', 'norm_ffn', 'w_gate', 'w_up', 'w_down', 'norm_final']
TWIN_DIFF_INPUT = 'x'
TWIN_INPUTS = ['x', 'mem', 'norm_mix', 'w_in', 'ssd_conv_w', 'ssd_conv_b', 'dt_bias', 'a_log', 'd_skip', 'ssd_norm', 'sc_conv_w', 'sc_norm', 'w_out', 'mem_norm', 'norm_xa', 'w_q', 'w_k', 'w_v', 'w_o', 'norm_ffn', 'w_gate', 'w_up', 'w_down', 'norm_final', 'loss_target', 'm_norm_mix', 'm_w_in', 'm_ssd_conv_w', 'm_ssd_conv_b', 'm_dt_bias', 'm_a_log', 'm_d_skip', 'm_ssd_norm', 'm_sc_conv_w', 'm_sc_norm', 'm_w_out', 'm_mem_norm', 'm_norm_xa', 'm_w_q', 'm_w_k', 'm_w_v', 'm_w_o', 'm_norm_ffn', 'm_w_gate', 'm_w_up', 'm_w_down', 'm_norm_final', 'v_norm_mix', 'v_w_in', 'v_ssd_conv_w', 'v_ssd_conv_b', 'v_dt_bias', 'v_a_log', 'v_d_skip', 'v_ssd_norm', 'v_sc_conv_w', 'v_sc_norm', 'v_w_out', 'v_mem_norm', 'v_norm_xa', 'v_w_q', 'v_w_k', 'v_w_v', 'v_w_o', 'v_norm_ffn', 'v_w_gate', 'v_w_up', 'v_w_down', 'v_norm_final']
TWIN_OUTPUTS = ['loss', 'grad_x', 'grad_norm_mix', 'grad_w_in', 'grad_ssd_conv_w', 'grad_ssd_conv_b', 'grad_dt_bias', 'grad_a_log', 'grad_d_skip', 'grad_ssd_norm', 'grad_sc_conv_w', 'grad_sc_norm', 'grad_w_out', 'grad_mem_norm', 'grad_norm_xa', 'grad_w_q', 'grad_w_k', 'grad_w_v', 'grad_w_o', 'grad_norm_ffn', 'grad_w_gate', 'grad_w_up', 'grad_w_down', 'grad_norm_final', 'delta_norm_mix', 'delta_w_in', 'delta_ssd_conv_w', 'delta_ssd_conv_b', 'delta_dt_bias', 'delta_a_log', 'delta_d_skip', 'delta_ssd_norm', 'delta_sc_conv_w', 'delta_sc_norm', 'delta_w_out', 'delta_mem_norm', 'delta_norm_xa', 'delta_w_q', 'delta_w_k', 'delta_w_v', 'delta_w_o', 'delta_norm_ffn', 'delta_w_gate', 'delta_w_up', 'delta_w_down', 'delta_norm_final', 'new_m_norm_mix', 'new_m_w_in', 'new_m_ssd_conv_w', 'new_m_ssd_conv_b', 'new_m_dt_bias', 'new_m_a_log', 'new_m_d_skip', 'new_m_ssd_norm', 'new_m_sc_conv_w', 'new_m_sc_norm', 'new_m_w_out', 'new_m_mem_norm', 'new_m_norm_xa', 'new_m_w_q', 'new_m_w_k', 'new_m_w_v', 'new_m_w_o', 'new_m_norm_ffn', 'new_m_w_gate', 'new_m_w_up', 'new_m_w_down', 'new_m_norm_final', 'new_v_norm_mix', 'new_v_w_in', 'new_v_ssd_conv_w', 'new_v_ssd_conv_b', 'new_v_dt_bias', 'new_v_a_log', 'new_v_d_skip', 'new_v_ssd_norm', 'new_v_sc_conv_w', 'new_v_sc_norm', 'new_v_w_out', 'new_v_mem_norm', 'new_v_norm_xa', 'new_v_w_q', 'new_v_w_k', 'new_v_w_v', 'new_v_w_o', 'new_v_norm_ffn', 'new_v_w_gate', 'new_v_w_up', 'new_v_w_down', 'new_v_norm_final']
TWIN_LEAF_KINDS = {'loss': 'loss', 'grad_x': 'grad_x', 'grad_norm_mix': 'grad_w', 'grad_w_in': 'grad_w', 'grad_ssd_conv_w': 'grad_w', 'grad_ssd_conv_b': 'grad_w', 'grad_dt_bias': 'grad_w', 'grad_a_log': 'grad_w', 'grad_d_skip': 'grad_w', 'grad_ssd_norm': 'grad_w', 'grad_sc_conv_w': 'grad_w', 'grad_sc_norm': 'grad_w', 'grad_w_out': 'grad_w', 'grad_mem_norm': 'grad_w', 'grad_norm_xa': 'grad_w', 'grad_w_q': 'grad_w', 'grad_w_k': 'grad_w', 'grad_w_v': 'grad_w', 'grad_w_o': 'grad_w', 'grad_norm_ffn': 'grad_w', 'grad_w_gate': 'grad_w', 'grad_w_up': 'grad_w', 'grad_w_down': 'grad_w', 'grad_norm_final': 'grad_w', 'delta_norm_mix': 'delta_w', 'delta_w_in': 'delta_w', 'delta_ssd_conv_w': 'delta_w', 'delta_ssd_conv_b': 'delta_w', 'delta_dt_bias': 'delta_w', 'delta_a_log': 'delta_w', 'delta_d_skip': 'delta_w', 'delta_ssd_norm': 'delta_w', 'delta_sc_conv_w': 'delta_w', 'delta_sc_norm': 'delta_w', 'delta_w_out': 'delta_w', 'delta_mem_norm': 'delta_w', 'delta_norm_xa': 'delta_w', 'delta_w_q': 'delta_w', 'delta_w_k': 'delta_w', 'delta_w_v': 'delta_w', 'delta_w_o': 'delta_w', 'delta_norm_ffn': 'delta_w', 'delta_w_gate': 'delta_w', 'delta_w_up': 'delta_w', 'delta_w_down': 'delta_w', 'delta_norm_final': 'delta_w', 'new_m_norm_mix': 'new_m', 'new_m_w_in': 'new_m', 'new_m_ssd_conv_w': 'new_m', 'new_m_ssd_conv_b': 'new_m', 'new_m_dt_bias': 'new_m', 'new_m_a_log': 'new_m', 'new_m_d_skip': 'new_m', 'new_m_ssd_norm': 'new_m', 'new_m_sc_conv_w': 'new_m', 'new_m_sc_norm': 'new_m', 'new_m_w_out': 'new_m', 'new_m_mem_norm': 'new_m', 'new_m_norm_xa': 'new_m', 'new_m_w_q': 'new_m', 'new_m_w_k': 'new_m', 'new_m_w_v': 'new_m', 'new_m_w_o': 'new_m', 'new_m_norm_ffn': 'new_m', 'new_m_w_gate': 'new_m', 'new_m_w_up': 'new_m', 'new_m_w_down': 'new_m', 'new_m_norm_final': 'new_m', 'new_v_norm_mix': 'new_v', 'new_v_w_in': 'new_v', 'new_v_ssd_conv_w': 'new_v', 'new_v_ssd_conv_b': 'new_v', 'new_v_dt_bias': 'new_v', 'new_v_a_log': 'new_v', 'new_v_d_skip': 'new_v', 'new_v_ssd_norm': 'new_v', 'new_v_sc_conv_w': 'new_v', 'new_v_sc_norm': 'new_v', 'new_v_w_out': 'new_v', 'new_v_mem_norm': 'new_v', 'new_v_norm_xa': 'new_v', 'new_v_w_q': 'new_v', 'new_v_w_k': 'new_v', 'new_v_w_v': 'new_v', 'new_v_w_o': 'new_v', 'new_v_norm_ffn': 'new_v', 'new_v_w_gate': 'new_v', 'new_v_w_up': 'new_v', 'new_v_w_down': 'new_v', 'new_v_norm_final': 'new_v'}


def _forward(args):
    return _fwd_reference(*[args[k] for k in FWD_PARAMS])


def _output_shape():
    def fwd():
        inp = _fwd_setup_inputs(0)
        return _fwd_reference(*[inp[k] for k in FWD_PARAMS])
    out = _jax.eval_shape(fwd)
    return out.shape, out.dtype

N_MICROBATCH = 1
ADAM_LR = 0.001
ADAM_B1 = 0.9
ADAM_B2 = 0.999
ADAM_EPS = 1e-08
ADAM_WD = 0.01
ADAM_STEP = 10
PER_EXAMPLE_BATCH_AXIS = {'x': 0, 'mem': 0, 'loss_target': 0}
SHARED_INPUTS = []
_WEIGHT_DTYPES = {'norm_mix': _jnp.float32, 'w_in': _jnp.float32, 'ssd_conv_w': _jnp.float32, 'ssd_conv_b': _jnp.float32, 'dt_bias': _jnp.float32, 'a_log': _jnp.float32, 'd_skip': _jnp.float32, 'ssd_norm': _jnp.float32, 'sc_conv_w': _jnp.float32, 'sc_norm': _jnp.float32, 'w_out': _jnp.float32, 'mem_norm': _jnp.float32, 'norm_xa': _jnp.float32, 'w_q': _jnp.float32, 'w_k': _jnp.float32, 'w_v': _jnp.float32, 'w_o': _jnp.float32, 'norm_ffn': _jnp.float32, 'w_gate': _jnp.float32, 'w_up': _jnp.float32, 'w_down': _jnp.float32, 'norm_final': _jnp.float32}
MOMENT_SCALE = {'norm_mix': 1.832386e-01, 'w_in': 7.705915e-02, 'ssd_conv_w': 6.673210e-02, 'ssd_conv_b': 9.504238e-02, 'dt_bias': 2.140834e-01, 'a_log': 2.101822e-01, 'd_skip': 4.267651e-01, 'ssd_norm': 7.825214e-02, 'sc_conv_w': 8.004111e-02, 'sc_norm': 7.812756e-02, 'w_out': 1.095127e-01, 'mem_norm': 3.397733e-02, 'norm_xa': 1.142642e-02, 'w_q': 2.235741e-02, 'w_k': 2.245710e-02, 'w_v': 2.334775e-02, 'w_o': 1.176833e-02, 'norm_ffn': 8.260968e-02, 'w_gate': 3.567882e-02, 'w_up': 3.455922e-02, 'w_down': 5.730404e-02, 'norm_final': 3.195367e+01}


def _to_microbatches(a, axis):
    t = _jnp.moveaxis(a, axis, 0)
    t = t.reshape((N_MICROBATCH, t.shape[0] // N_MICROBATCH) + t.shape[1:])
    return _jnp.moveaxis(t, 1, axis + 1)


def setup_inputs(seed: int = 0) -> dict:
    inp = _fwd_setup_inputs(seed)
    key = _jax.random.fold_in(_jax.random.key(seed), 7919)
    shape, _ = _output_shape()
    out = dict(inp)
    out["loss_target"] = _jax.random.normal(_jax.random.fold_in(key, 0), shape, _jnp.float32)
    for i, name in enumerate(TWIN_WEIGHTS):
        w = inp[name].astype(_jnp.float32)
        if MOMENT_SCALE is None:
            s = _jnp.sqrt(_jnp.mean(_jnp.square(w)) + 1e-30)
        else:
            s = MOMENT_SCALE[name]
        km, kv = _jax.random.split(_jax.random.fold_in(key, i + 1))
        out[name] = w
        out["m_" + name] = s * _jax.random.normal(km, w.shape, _jnp.float32)
        out["v_" + name] = (s * s) * _jax.random.uniform(kv, w.shape, _jnp.float32, 0.5, 1.5)
    if N_MICROBATCH > 1:
        for name, axis in PER_EXAMPLE_BATCH_AXIS.items():
            out[name] = _to_microbatches(out[name], axis)
    return {'x': out['x'], 'mem': out['mem'], 'norm_mix': out['norm_mix'], 'w_in': out['w_in'], 'ssd_conv_w': out['ssd_conv_w'], 'ssd_conv_b': out['ssd_conv_b'], 'dt_bias': out['dt_bias'], 'a_log': out['a_log'], 'd_skip': out['d_skip'], 'ssd_norm': out['ssd_norm'], 'sc_conv_w': out['sc_conv_w'], 'sc_norm': out['sc_norm'], 'w_out': out['w_out'], 'mem_norm': out['mem_norm'], 'norm_xa': out['norm_xa'], 'w_q': out['w_q'], 'w_k': out['w_k'], 'w_v': out['w_v'], 'w_o': out['w_o'], 'norm_ffn': out['norm_ffn'], 'w_gate': out['w_gate'], 'w_up': out['w_up'], 'w_down': out['w_down'], 'norm_final': out['norm_final'], 'loss_target': out['loss_target'], 'm_norm_mix': out['m_norm_mix'], 'm_w_in': out['m_w_in'], 'm_ssd_conv_w': out['m_ssd_conv_w'], 'm_ssd_conv_b': out['m_ssd_conv_b'], 'm_dt_bias': out['m_dt_bias'], 'm_a_log': out['m_a_log'], 'm_d_skip': out['m_d_skip'], 'm_ssd_norm': out['m_ssd_norm'], 'm_sc_conv_w': out['m_sc_conv_w'], 'm_sc_norm': out['m_sc_norm'], 'm_w_out': out['m_w_out'], 'm_mem_norm': out['m_mem_norm'], 'm_norm_xa': out['m_norm_xa'], 'm_w_q': out['m_w_q'], 'm_w_k': out['m_w_k'], 'm_w_v': out['m_w_v'], 'm_w_o': out['m_w_o'], 'm_norm_ffn': out['m_norm_ffn'], 'm_w_gate': out['m_w_gate'], 'm_w_up': out['m_w_up'], 'm_w_down': out['m_w_down'], 'm_norm_final': out['m_norm_final'], 'v_norm_mix': out['v_norm_mix'], 'v_w_in': out['v_w_in'], 'v_ssd_conv_w': out['v_ssd_conv_w'], 'v_ssd_conv_b': out['v_ssd_conv_b'], 'v_dt_bias': out['v_dt_bias'], 'v_a_log': out['v_a_log'], 'v_d_skip': out['v_d_skip'], 'v_ssd_norm': out['v_ssd_norm'], 'v_sc_conv_w': out['v_sc_conv_w'], 'v_sc_norm': out['v_sc_norm'], 'v_w_out': out['v_w_out'], 'v_mem_norm': out['v_mem_norm'], 'v_norm_xa': out['v_norm_xa'], 'v_w_q': out['v_w_q'], 'v_w_k': out['v_w_k'], 'v_w_v': out['v_w_v'], 'v_w_o': out['v_w_o'], 'v_norm_ffn': out['v_norm_ffn'], 'v_w_gate': out['v_w_gate'], 'v_w_up': out['v_w_up'], 'v_w_down': out['v_w_down'], 'v_norm_final': out['v_norm_final']}


def _loss(weights, diff, rest, loss_target):
    with _jax.named_scope("forward"):
        args = {**rest, TWIN_DIFF_INPUT: diff, **{k: w.astype(_WEIGHT_DTYPES[k]) for k, w in weights.items()}}
        y = _forward(args)
    with _jax.named_scope("loss_head"):
        err = _jnp.square(y.astype(_jnp.float32) - loss_target)
        return 0.5 * _jnp.sum(_jnp.mean(err, axis=-1)) if err.ndim else 0.5 * err


def _adamw(w, g, m, v):
    m = ADAM_B1 * m + (1.0 - ADAM_B1) * g
    v = ADAM_B2 * v + (1.0 - ADAM_B2) * _jnp.square(g)
    m_hat = m / (1.0 - ADAM_B1 ** ADAM_STEP)
    v_hat = v / (1.0 - ADAM_B2 ** ADAM_STEP)
    delta = -ADAM_LR * (m_hat / (_jnp.sqrt(v_hat) + ADAM_EPS) + ADAM_WD * w)
    return delta, m, v


def reference(x, mem, norm_mix, w_in, ssd_conv_w, ssd_conv_b, dt_bias, a_log, d_skip, ssd_norm, sc_conv_w, sc_norm, w_out, mem_norm, norm_xa, w_q, w_k, w_v, w_o, norm_ffn, w_gate, w_up, w_down, norm_final, loss_target, m_norm_mix, m_w_in, m_ssd_conv_w, m_ssd_conv_b, m_dt_bias, m_a_log, m_d_skip, m_ssd_norm, m_sc_conv_w, m_sc_norm, m_w_out, m_mem_norm, m_norm_xa, m_w_q, m_w_k, m_w_v, m_w_o, m_norm_ffn, m_w_gate, m_w_up, m_w_down, m_norm_final, v_norm_mix, v_w_in, v_ssd_conv_w, v_ssd_conv_b, v_dt_bias, v_a_log, v_d_skip, v_ssd_norm, v_sc_conv_w, v_sc_norm, v_w_out, v_mem_norm, v_norm_xa, v_w_q, v_w_k, v_w_v, v_w_o, v_norm_ffn, v_w_gate, v_w_up, v_w_down, v_norm_final):
    given = dict(x=x, mem=mem, norm_mix=norm_mix, w_in=w_in, ssd_conv_w=ssd_conv_w, ssd_conv_b=ssd_conv_b, dt_bias=dt_bias, a_log=a_log, d_skip=d_skip, ssd_norm=ssd_norm, sc_conv_w=sc_conv_w, sc_norm=sc_norm, w_out=w_out, mem_norm=mem_norm, norm_xa=norm_xa, w_q=w_q, w_k=w_k, w_v=w_v, w_o=w_o, norm_ffn=norm_ffn, w_gate=w_gate, w_up=w_up, w_down=w_down, norm_final=norm_final, loss_target=loss_target, m_norm_mix=m_norm_mix, m_w_in=m_w_in, m_ssd_conv_w=m_ssd_conv_w, m_ssd_conv_b=m_ssd_conv_b, m_dt_bias=m_dt_bias, m_a_log=m_a_log, m_d_skip=m_d_skip, m_ssd_norm=m_ssd_norm, m_sc_conv_w=m_sc_conv_w, m_sc_norm=m_sc_norm, m_w_out=m_w_out, m_mem_norm=m_mem_norm, m_norm_xa=m_norm_xa, m_w_q=m_w_q, m_w_k=m_w_k, m_w_v=m_w_v, m_w_o=m_w_o, m_norm_ffn=m_norm_ffn, m_w_gate=m_w_gate, m_w_up=m_w_up, m_w_down=m_w_down, m_norm_final=m_norm_final, v_norm_mix=v_norm_mix, v_w_in=v_w_in, v_ssd_conv_w=v_ssd_conv_w, v_ssd_conv_b=v_ssd_conv_b, v_dt_bias=v_dt_bias, v_a_log=v_a_log, v_d_skip=v_d_skip, v_ssd_norm=v_ssd_norm, v_sc_conv_w=v_sc_conv_w, v_sc_norm=v_sc_norm, v_w_out=v_w_out, v_mem_norm=v_mem_norm, v_norm_xa=v_norm_xa, v_w_q=v_w_q, v_w_k=v_w_k, v_w_v=v_w_v, v_w_o=v_w_o, v_norm_ffn=v_norm_ffn, v_w_gate=v_w_gate, v_w_up=v_w_up, v_w_down=v_w_down, v_norm_final=v_norm_final)
    weights = {n: given[n] for n in TWIN_WEIGHTS}
    shared = {n: given[n] for n in SHARED_INPUTS}
    per_example = {n: given[n] for n in ['x', 'mem']}
    grad_fn = _jax.value_and_grad(_loss, argnums=(0, 1))

    def one_microbatch(ex, loss_target):
        ex = dict(ex)
        diff = ex.pop(TWIN_DIFF_INPUT)
        return grad_fn(weights, diff, {**shared, **ex}, loss_target)

    if N_MICROBATCH == 1:
        loss, (grad_w, grad_x) = one_microbatch(per_example, given["loss_target"])
    else:
        def body(carry, xs):
            loss_sum, grad_sum = carry
            l_k, (gw_k, gx_k) = one_microbatch(xs[0], xs[1])
            with _jax.named_scope("update"):
                return (loss_sum + l_k, _jax.tree.map(_jnp.add, grad_sum, gw_k)), gx_k

        init = (_jnp.zeros((), _jnp.float32), _jax.tree.map(_jnp.zeros_like, weights))
        (loss, grad_w), grad_x = _jax.lax.scan(body, init, (per_example, given["loss_target"]))
    with _jax.named_scope("update"):
        delta_w, new_m, new_v = {}, {}, {}
        for n in TWIN_WEIGHTS:
            delta_w[n], new_m[n], new_v[n] = _adamw(weights[n], grad_w[n], given["m_" + n], given["v_" + n])
    return (loss, grad_x, *[grad_w[n] for n in TWIN_WEIGHTS], *[delta_w[n] for n in TWIN_WEIGHTS],
            *[new_m[n] for n in TWIN_WEIGHTS], *[new_v[n] for n in TWIN_WEIGHTS])
```

```python
import functools

import jax
import jax.numpy as jnp
from jax import lax
from jax.experimental import pallas as pl
from jax.experimental.pallas import tpu as pltpu

F32 = jnp.float32
BF16 = jnp.bfloat16
MESH = pl.DeviceIdType.MESH

D = 2048
D_SSD = 2048
N_GROUPS = 4
GROUP_W = 768
D_XBC = 3072
N_STATE = 128
HEADS = 32
PAIRS_PER_GROUP = 4
CHUNK = 256
DT_PAD = 128
D_SC = 2048
SC_GROUP = 128
XA_HEADS = 4
XA_HD = 128
D_XA = 512
D_FF = 5632
EPS = 1e-5
HALO = 16
VMEM_LIMIT = 56 * 1024 * 1024

ADAM_LR, ADAM_B1, ADAM_B2, ADAM_EPS, ADAM_WD, ADAM_STEP = 0.001, 0.9, 0.999, 1e-08, 0.01, 10

NT = (((1,), (1,)), ((), ()))
TN = (((0,), (0,)), ((), ()))
NN = (((1,), (0,)), ((), ()))


def _pcall(body, **kw):
    return pl.pallas_call(body, **kw)


def _params(**kw):
    return pltpu.CompilerParams(vmem_limit_bytes=VMEM_LIMIT, **kw)


def _sds(shape, dtype):
    return jax.ShapeDtypeStruct(shape, dtype)


def _sig(x):
    return 1.0 / (1.0 + jnp.exp(-x))


def _dot(a, b, dims=NN):
    return lax.dot_general(a, b, dims, preferred_element_type=F32)


def _mm(a, b, *, mode, grid, a_spec, b_spec, o_spec, o_tile, out_sds, add=None, name):
    gk = grid[2]
    dims = {"nn": NN, "nt": NT, "tn": TN}[mode]
    has_add = add is not None

    def body(*refs):
        a_ref, b_ref = refs[0], refs[1]
        add_ref = refs[2] if has_add else None
        o_ref = refs[2 + has_add]
        p = _dot(a_ref[...].astype(BF16), b_ref[...].astype(BF16), dims)

        def finish(acc):
            if has_add:
                acc = acc + add_ref[...]
            o_ref[...] = acc.astype(o_ref.dtype)

        if gk == 1:
            finish(p)
        else:
            acc_ref = refs[3 + has_add]
            k = pl.program_id(2)

            @pl.when(k == 0)
            def _():
                acc_ref[...] = p

            @pl.when(k > 0)
            def _():
                acc_ref[...] += p

            @pl.when(k == gk - 1)
            def _():
                finish(acc_ref[...])

    in_specs = [a_spec, b_spec] + ([o_spec] if has_add else [])
    args = (a, b) + ((add,) if has_add else ())
    scratch = [pltpu.VMEM(o_tile, F32)] if gk > 1 else []
    return _pcall(body, grid=grid, in_specs=in_specs, out_specs=o_spec, out_shape=out_sds, scratch_shapes=scratch,
                  compiler_params=_params(dimension_semantics=("parallel", "parallel", "arbitrary")), name=name)(*args)


def _tile(n, pref):
    t = min(n, pref)
    assert n % t == 0, (n, pref)
    return t


def _mm_nn(a, w, *, tn, tk=None, out_dtype=BF16, add=None, name):
    m, k = a.shape
    n = w.shape[1]
    tm = _tile(m, 1024)
    tk = k if tk is None else tk
    grid = (m // tm, n // tn, k // tk)
    return _mm(a, w, mode="nn", grid=grid,
               a_spec=pl.BlockSpec((tm, tk), lambda i, j, kk: (i, kk)),
               b_spec=pl.BlockSpec((tk, tn), lambda i, j, kk: (kk, j)),
               o_spec=pl.BlockSpec((tm, tn), lambda i, j, kk: (i, j)), o_tile=(tm, tn),
               out_sds=_sds((m, n), out_dtype), add=add, name=name)


def _mm_nn_sm(a, w4, *, out_dtype=BF16, add=None, name):
    m, k = a.shape
    n = w4.shape[2]
    tm = _tile(m, 1024)
    return _mm(a, w4, mode="nn", grid=(m // tm, 4, 1),
               a_spec=pl.BlockSpec((tm, k), lambda i, j, kk: (i, 0)),
               b_spec=pl.BlockSpec((None, k, n), lambda i, j, kk: (j, 0, 0)),
               o_spec=pl.BlockSpec((tm, n), lambda i, j, kk: (i, j)), o_tile=(tm, n),
               out_sds=_sds((m, 4 * n), out_dtype), add=add, name=name)


def _mm_nt(a, w, *, tn, tk=None, out_dtype=BF16, add=None, name):
    m, k = a.shape
    n = w.shape[0]
    tm = _tile(m, 1024)
    tk = k if tk is None else tk
    grid = (m // tm, n // tn, k // tk)
    return _mm(a, w, mode="nt", grid=grid,
               a_spec=pl.BlockSpec((tm, tk), lambda i, j, kk: (i, kk)),
               b_spec=pl.BlockSpec((tn, tk), lambda i, j, kk: (j, kk)),
               o_spec=pl.BlockSpec((tm, tn), lambda i, j, kk: (i, j)), o_tile=(tm, tn),
               out_sds=_sds((m, n), out_dtype), add=add, name=name)


def _mm_nt_sm(a, w4, *, tn, out_dtype=BF16, add=None, name):
    m = a.shape[0]
    _, k, n = w4.shape
    tm = _tile(m, 1024)
    tn = _tile(k, tn)
    return _mm(a, w4, mode="nt", grid=(m // tm, k // tn, 4),
               a_spec=pl.BlockSpec((tm, n), lambda i, j, kk: (i, kk)),
               b_spec=pl.BlockSpec((None, tn, n), lambda i, j, kk: (kk, j, 0)),
               o_spec=pl.BlockSpec((tm, tn), lambda i, j, kk: (i, j)), o_tile=(tm, tn),
               out_sds=_sds((m, k), out_dtype), add=add, name=name)


def _mm_tn(a, g, *, tm, tn, out_dtype=BF16, name):
    t, m = a.shape
    n = g.shape[1]
    tk = _tile(t, 512)
    return _mm(a, g, mode="tn", grid=(m // tm, n // tn, t // tk),
               a_spec=pl.BlockSpec((tk, tm), lambda i, j, kk: (kk, i)),
               b_spec=pl.BlockSpec((tk, tn), lambda i, j, kk: (kk, j)),
               o_spec=pl.BlockSpec((tm, tn), lambda i, j, kk: (i, j)), o_tile=(tm, tn),
               out_sds=_sds((m, n), out_dtype), name=name)


def _mm_tn_sm(a, g, *, tm, out_dtype=BF16, name):
    t, m = a.shape
    n = g.shape[1] // 4
    tk = _tile(t, 512)
    return _mm(a, g, mode="tn", grid=(m // tm, 4, t // tk),
               a_spec=pl.BlockSpec((tk, tm), lambda i, j, kk: (kk, i)),
               b_spec=pl.BlockSpec((tk, n), lambda i, j, kk: (kk, j)),
               o_spec=pl.BlockSpec((None, tm, n), lambda i, j, kk: (j, i, 0)), o_tile=(tm, n),
               out_sds=_sds((4, m, n), out_dtype), name=name)


def _rms_fwd(h, g, *, name):
    t, d = h.shape
    tr = _tile(t, 512)

    def body(h_ref, g_ref, o_ref):
        x = h_ref[...]
        r = lax.rsqrt(jnp.mean(x * x, axis=-1, keepdims=True) + EPS)
        o_ref[...] = (x * r * g_ref[...]).astype(o_ref.dtype)

    return _pcall(body, grid=(t // tr,),
                  in_specs=[pl.BlockSpec((tr, d), lambda i: (i, 0)), pl.BlockSpec((1, d), lambda i: (0, 0))],
                  out_specs=pl.BlockSpec((tr, d), lambda i: (i, 0)), out_shape=_sds((t, d), BF16),
                  compiler_params=_params(dimension_semantics=("parallel",)), name=name)(h, g)


def _rms_bwd(h, g, dy, dres, *, name):
    t, d = h.shape
    tr = _tile(t, 256)

    def body(h_ref, g_ref, dy_ref, dres_ref, dh_ref, dg_ref):
        i = pl.program_id(0)
        x = h_ref[...]
        r = lax.rsqrt(jnp.mean(x * x, axis=-1, keepdims=True) + EPS)
        xh = x * r
        dyv = dy_ref[...].astype(F32)
        dxh = dyv * g_ref[...]
        dh_ref[...] = dres_ref[...] + r * (dxh - xh * jnp.mean(dxh * xh, axis=-1, keepdims=True))
        part = jnp.sum(dyv * xh, axis=0, keepdims=True)

        @pl.when(i == 0)
        def _():
            dg_ref[...] = part

        @pl.when(i > 0)
        def _():
            dg_ref[...] += part

    row = pl.BlockSpec((tr, d), lambda i: (i, 0))
    vec = pl.BlockSpec((1, d), lambda i: (0, 0))
    return _pcall(body, grid=(t // tr,), in_specs=[row, vec, row, row], out_specs=[row, vec],
                  out_shape=[_sds((t, d), F32), _sds((1, d), F32)],
                  compiler_params=_params(dimension_semantics=("arbitrary",)), name=name)(h, g, dy, dres)


def _final(h, g, tgt, *, name):
    t, d = h.shape
    tr = _tile(t, 256)

    def body(h_ref, g_ref, t_ref, loss_ref, dh_ref, dg_ref):
        i = pl.program_id(0)
        x = h_ref[...]
        gv = g_ref[...]
        r = lax.rsqrt(jnp.mean(x * x, axis=-1, keepdims=True) + EPS)
        xh = x * r
        e = xh * gv - t_ref[...]
        lpart = jnp.zeros((1, 128), F32) + 0.5 * jnp.sum(jnp.mean(e * e, axis=-1, keepdims=True))
        dyv = e * (1.0 / d)
        dxh = dyv * gv
        dh_ref[...] = r * (dxh - xh * jnp.mean(dxh * xh, axis=-1, keepdims=True))
        part = jnp.sum(dyv * xh, axis=0, keepdims=True)

        @pl.when(i == 0)
        def _():
            dg_ref[...] = part
            loss_ref[...] = lpart

        @pl.when(i > 0)
        def _():
            dg_ref[...] += part
            loss_ref[...] += lpart

    row = pl.BlockSpec((tr, d), lambda i: (i, 0))
    vec = pl.BlockSpec((1, d), lambda i: (0, 0))
    return _pcall(body, grid=(t // tr,), in_specs=[row, vec, row],
                  out_specs=[pl.BlockSpec((1, 128), lambda i: (0, 0)), row, vec],
                  out_shape=[_sds((1, 128), F32), _sds((t, d), F32), _sds((1, d), F32)],
                  compiler_params=_params(dimension_semantics=("arbitrary",)), name=name)(h, g, tgt)


def _conv_taps(ext, w, ntap, rows):
    n = ext.shape[0]
    acc = w[ntap - 1:ntap, :] * ext[HALO:HALO + rows]
    for k in range(1, ntap):
        acc = acc + w[ntap - 1 - k:ntap - k, :] * pltpu.roll(ext, k, axis=0)[HALO:HALO + rows]
    del n
    return acc


def _conv_fwd(xbc, w, b, *, name):
    t, c = xbc.shape
    rows = CHUNK
    cw = GROUP_W
    hb = rows // HALO

    def body(cur_ref, prev_ref, w_ref, b_ref, o_ref):
        i = pl.program_id(1)
        cur = cur_ref[...].astype(F32)
        prev = jnp.where(i > 0, prev_ref[...].astype(F32), 0.0)
        ext = jnp.concatenate([prev, cur], axis=0)
        pre = _conv_taps(ext, w_ref[...], 4, rows) + b_ref[...]
        o_ref[...] = (pre * _sig(pre)).astype(o_ref.dtype)

    return _pcall(body, grid=(c // cw, t // rows),
                  in_specs=[pl.BlockSpec((rows, cw), lambda j, i: (i, j)),
                            pl.BlockSpec((HALO, cw), lambda j, i: (jnp.maximum(i * hb - 1, 0), j)),
                            pl.BlockSpec((4, cw), lambda j, i: (0, j)),
                            pl.BlockSpec((1, cw), lambda j, i: (0, j))],
                  out_specs=pl.BlockSpec((rows, cw), lambda j, i: (i, j)), out_shape=_sds((t, c), BF16),
                  compiler_params=_params(dimension_semantics=("parallel", "parallel")), name=name)(xbc, xbc, w, b)


def _conv_bwd(dxc, xbc, w, b, *, name):
    t, c = xbc.shape
    rows = CHUNK
    cw = GROUP_W
    hb = rows // HALO
    nblk = t // rows
    nhalo = t // HALO

    def body(d_ref, dn_ref, cur_ref, prev_ref, next_ref, w_ref, b_ref, dx_ref, dw_ref, db_ref):
        i = pl.program_id(1)
        last = i == nblk - 1
        wv = w_ref[...]
        xe = jnp.concatenate([jnp.where(i > 0, prev_ref[...].astype(F32), 0.0), cur_ref[...].astype(F32),
                              jnp.where(last, 0.0, next_ref[...].astype(F32))], axis=0)
        n = rows + 2 * HALO
        sh = [xe] + [pltpu.roll(xe, k, axis=0) for k in range(1, 4)]
        pre = wv[3:4, :] * sh[0] + wv[2:3, :] * sh[1] + wv[1:2, :] * sh[2] + wv[0:1, :] * sh[3] + b_ref[...]
        de = jnp.concatenate([jnp.zeros((HALO, cw), F32), d_ref[...].astype(F32),
                              jnp.where(last, 0.0, dn_ref[...].astype(F32))], axis=0)
        s = _sig(pre)
        dpre = de * (s * (1.0 + pre * (1.0 - s)))
        dx = wv[3:4, :] * dpre
        for m in range(1, 4):
            dx = dx + wv[3 - m:4 - m, :] * pltpu.roll(dpre, n - m, axis=0)
        dx_ref[...] = dx[HALO:HALO + rows].astype(dx_ref.dtype)
        dcur = dpre[HALO:HALO + rows]
        dwv = jnp.concatenate([jnp.sum(dcur * sh[3 - j][HALO:HALO + rows], axis=0, keepdims=True) for j in range(4)], axis=0)
        dbv = jnp.sum(dcur, axis=0, keepdims=True)

        @pl.when(i == 0)
        def _():
            dw_ref[...] = dwv
            db_ref[...] = dbv

        @pl.when(i > 0)
        def _():
            dw_ref[...] += dwv
            db_ref[...] += dbv

    cur = pl.BlockSpec((rows, cw), lambda j, i: (i, j))
    prev = pl.BlockSpec((HALO, cw), lambda j, i: (jnp.maximum(i * hb - 1, 0), j))
    nxt = pl.BlockSpec((HALO, cw), lambda j, i: (jnp.minimum((i + 1) * hb, nhalo - 1), j))
    return _pcall(body, grid=(c // cw, nblk),
                  in_specs=[cur, nxt, cur, prev, nxt, pl.BlockSpec((4, cw), lambda j, i: (0, j)),
                            pl.BlockSpec((1, cw), lambda j, i: (0, j))],
                  out_specs=[cur, pl.BlockSpec((4, cw), lambda j, i: (0, j)), pl.BlockSpec((1, cw), lambda j, i: (0, j))],
                  out_shape=[_sds((t, c), BF16), _sds((4, c), F32), _sds((1, c), F32)],
                  compiler_params=_params(dimension_semantics=("parallel", "arbitrary")), name=name)(dxc, dxc, xbc, xbc, xbc, w, b)


def _neg_exp_alog(alog):
    lane = lax.broadcasted_iota(jnp.int32, alog.shape, 1)
    return jnp.where(lane < HEADS, -jnp.exp(alog), 0.0)


def _dt_prep(dtr, bias, alog, *, name):
    t = dtr.shape[0]
    rows = CHUNK

    def body(r_ref, b_ref, a_ref, dt_ref, dtg_ref, acsg_ref, acst_ref):
        raw = r_ref[...] + b_ref[...]
        dt = jnp.maximum(raw, 0.0) + jnp.log(1.0 + jnp.exp(-jnp.abs(raw)))
        a = _neg_exp_alog(a_ref[...])
        adt = dt * a
        ri = lax.broadcasted_iota(jnp.int32, (rows, rows), 0)
        ci = lax.broadcasted_iota(jnp.int32, (rows, rows), 1)
        tri = (ri >= ci).astype(F32)
        acs = jnp.dot(tri, adt, precision=lax.Precision.HIGHEST, preferred_element_type=F32)
        dt_ref[...] = dt
        acst_ref[...] = acs.T
        for g in range(N_GROUPS):
            sh = (128 - 8 * g) % 128
            dtg_ref[g] = dt if sh == 0 else pltpu.roll(dt, sh, axis=1)
            acsg_ref[g] = acs if sh == 0 else pltpu.roll(acs, sh, axis=1)

    row = pl.BlockSpec((rows, DT_PAD), lambda i: (i, 0))
    vec = pl.BlockSpec((1, DT_PAD), lambda i: (0, 0))
    grp = pl.BlockSpec((N_GROUPS, rows, DT_PAD), lambda i: (0, i, 0))
    return _pcall(body, grid=(t // rows,), in_specs=[row, vec, vec],
                  out_specs=[row, grp, grp, pl.BlockSpec((DT_PAD, rows), lambda i: (0, i))],
                  out_shape=[_sds((t, DT_PAD), F32), _sds((N_GROUPS, t, DT_PAD), F32), _sds((N_GROUPS, t, DT_PAD), F32),
                             _sds((DT_PAD, t), F32)],
                  compiler_params=_params(dimension_semantics=("parallel",)), name=name)(dtr, bias, alog)


def _dt_bwd(ddtg, dacg, dart, dt, dtr, bias, alog, *, name):
    t = dtr.shape[0]
    rows = CHUNK

    def body(ddtg_ref, dacg_ref, dart_ref, dt_ref, r_ref, b_ref, a_ref, dr_ref, db_ref, da_ref):
        i = pl.program_id(0)
        lane = lax.broadcasted_iota(jnp.int32, (rows, DT_PAD), 1)
        ddt = jnp.zeros((rows, DT_PAD), F32)
        dacs = jnp.concatenate([dart_ref[...], jnp.zeros((DT_PAD - HEADS, rows), F32)], axis=0).T
        for g in range(N_GROUPS):
            sel = (lane >= 8 * g) & (lane < 8 * g + 8)
            dd = ddtg_ref[g]
            da = dacg_ref[g]
            if g:
                dd = pltpu.roll(dd, 8 * g, axis=1)
                da = pltpu.roll(da, 8 * g, axis=1)
            ddt = ddt + jnp.where(sel, dd, 0.0)
            dacs = dacs + jnp.where(sel, da, 0.0)
        ri = lax.broadcasted_iota(jnp.int32, (rows, rows), 0)
        ci = lax.broadcasted_iota(jnp.int32, (rows, rows), 1)
        triu = (ci >= ri).astype(F32)
        rev = jnp.dot(triu, dacs, precision=lax.Precision.HIGHEST, preferred_element_type=F32)
        a = _neg_exp_alog(a_ref[...])
        dtv = dt_ref[...]
        raw = r_ref[...] + b_ref[...]
        draw = (ddt + a * rev) * _sig(raw)
        dr_ref[...] = draw
        dbv = jnp.sum(draw, axis=0, keepdims=True)
        dav = jnp.sum(dtv * rev, axis=0, keepdims=True) * a

        @pl.when(i == 0)
        def _():
            db_ref[...] = dbv
            da_ref[...] = dav

        @pl.when(i > 0)
        def _():
            db_ref[...] += dbv
            da_ref[...] += dav

    row = pl.BlockSpec((rows, DT_PAD), lambda i: (i, 0))
    vec = pl.BlockSpec((1, DT_PAD), lambda i: (0, 0))
    grp = pl.BlockSpec((N_GROUPS, rows, DT_PAD), lambda i: (0, i, 0))
    return _pcall(body, grid=(t // rows,),
                  in_specs=[grp, grp, pl.BlockSpec((HEADS, rows), lambda i: (0, i)), row, row, vec, vec],
                  out_specs=[row, vec, vec], out_shape=[_sds((t, DT_PAD), F32), _sds((1, DT_PAD), F32), _sds((1, DT_PAD), F32)],
                  compiler_params=_params(dimension_semantics=("arbitrary",)), name=name)(ddtg, dacg, dart, dt, dtr, bias, alog)


def _pair_cols(col_ref_val, p, lo):
    return jnp.where(lo, col_ref_val[:, 2 * p:2 * p + 1], col_ref_val[:, 2 * p + 1:2 * p + 2])


def _ssd_fwd(xc, dtg, acsg, acst, z, dskip, nw, *, name):
    t = xc.shape[0]
    L = CHUNK
    nc = t // L

    def body(xc_ref, dtg_ref, acsg_ref, acst_ref, z_ref, dsk_ref, nw_ref, y_ref, st_ref, mix_ref, s_ref):
        c = pl.program_id(1)

        @pl.when(c == 0)
        def _():
            s_ref[...] = jnp.zeros_like(s_ref)

        blk = xc_ref[...]
        bm = blk[:, 512:640]
        cm = blk[:, 640:768]
        cb = _dot(cm, bm, NT)
        dtv = dtg_ref[...]
        acs = acsg_ref[...]
        acst_v = acst_ref[...]
        ri = lax.broadcasted_iota(jnp.int32, (L, L), 0)
        ci = lax.broadcasted_iota(jnp.int32, (L, L), 1)
        causal = ri >= ci
        lo = lax.broadcasted_iota(jnp.int32, (1, 128), 1) < 64
        lo_rows = lax.broadcasted_iota(jnp.int32, (128, 1), 0) < 64
        dskv = dsk_ref[...]
        ys = []
        for p in range(PAIRS_PER_GROUP):
            xp = blk[:, 128 * p:128 * p + 128].astype(F32)
            dt_p = _pair_cols(dtv, p, lo)
            a_p = _pair_cols(acs, p, lo)
            alast = acs[L - 1:L, :]
            al_p = _pair_cols(alast, p, lo)
            xdt = xp * dt_p
            xdt_b = xdt.astype(BF16)
            yd = []
            for hh in range(2):
                j = 2 * p + hh
                seg = acs[:, j:j + 1] - acst_v[j:j + 1, :]
                lam = jnp.exp(jnp.where(causal, seg, -1e30))
                w = (cb * lam).astype(BF16)
                yd.append(_dot(w, xdt_b))
            y = jnp.where(lo, yd[0], yd[1])
            sp = s_ref[p]
            st_ref[p] = sp
            y = y + _dot(cm, sp.astype(BF16), NT) * jnp.exp(a_p)
            dsc = jnp.exp(al_p - a_p)
            snew = _dot((xdt * dsc).astype(BF16), bm, TN)
            al_rows = jnp.where(lo_rows, alast[:, 2 * p:2 * p + 1], alast[:, 2 * p + 1:2 * p + 2])
            s_ref[p] = sp * jnp.exp(al_rows) + snew
            ys.append(y + xp * dskv[:, 128 * p:128 * p + 128])
        yfull = jnp.concatenate(ys, axis=1)
        y_ref[...] = yfull.astype(y_ref.dtype)
        zz = z_ref[...].astype(F32)
        yg = yfull * (zz * _sig(zz))
        r = lax.rsqrt(jnp.mean(yg * yg, axis=-1, keepdims=True) + EPS)
        mix_ref[...] = (yg * r * nw_ref[...]).astype(mix_ref.dtype)

    grp = pl.BlockSpec((None, L, DT_PAD), lambda g, c: (g, c, 0))
    return _pcall(body, grid=(N_GROUPS, nc),
                  in_specs=[pl.BlockSpec((L, GROUP_W), lambda g, c: (c, g)), grp, grp,
                            pl.BlockSpec((8, L), lambda g, c: (g, c)),
                            pl.BlockSpec((L, 512), lambda g, c: (c, g)),
                            pl.BlockSpec((1, 512), lambda g, c: (0, g)), pl.BlockSpec((1, 512), lambda g, c: (0, g))],
                  out_specs=[pl.BlockSpec((L, 512), lambda g, c: (c, g)),
                             pl.BlockSpec((None, PAIRS_PER_GROUP, 128, N_STATE), lambda g, c: (c, g, 0, 0)),
                             pl.BlockSpec((L, 512), lambda g, c: (c, g))],
                  out_shape=[_sds((t, D_SSD), BF16), _sds((nc, N_GROUPS * PAIRS_PER_GROUP, 128, N_STATE), F32),
                             _sds((t, D_SSD + D_SC), BF16)],
                  scratch_shapes=[pltpu.VMEM((PAIRS_PER_GROUP, 128, N_STATE), F32)],
                  compiler_params=_params(dimension_semantics=("parallel", "arbitrary")), name=name)(
                      xc, dtg, acsg, acst, z, dskip, nw)


def _ssd_bwd(xc, dtg, acsg, acst, z, y, states, dmix, dskip, nw, *, name):
    t = xc.shape[0]
    L = CHUNK
    nc = t // L

    def body(xc_ref, dtg_ref, acsg_ref, acst_ref, z_ref, y_ref, st_ref, dm_ref, dsk_ref, nw_ref,
             dxc_ref, dz_ref, ddt_ref, dac_ref, dar_ref, ddsk_ref, dnw_ref, ds_ref):
        c = pl.program_id(1)

        @pl.when(c == 0)
        def _():
            ds_ref[...] = jnp.zeros_like(ds_ref)
            ddsk_ref[...] = jnp.zeros_like(ddsk_ref)
            dnw_ref[...] = jnp.zeros_like(dnw_ref)

        blk = xc_ref[...]
        xs = blk[:, :512].astype(F32)
        bm = blk[:, 512:640]
        cm = blk[:, 640:768]
        bmf = bm.astype(F32)
        yv = y_ref[...].astype(F32)
        zz = z_ref[...].astype(F32)
        nwv = nw_ref[...]
        dout = dm_ref[...].astype(F32)
        sz = _sig(zz)
        silu = zz * sz
        yg = yv * silu
        r = lax.rsqrt(jnp.mean(yg * yg, axis=-1, keepdims=True) + EPS)
        xh = yg * r
        dnw_ref[...] += jnp.sum(dout * xh, axis=0, keepdims=True)
        dyn = dout * nwv
        dyg = r * (dyn - xh * jnp.mean(dyn * xh, axis=-1, keepdims=True))
        dy = dyg * silu
        dz_ref[...] = (dyg * yv * (sz * (1.0 + zz * (1.0 - sz)))).astype(dz_ref.dtype)
        ddsk_ref[...] += jnp.sum(dy * xs, axis=0, keepdims=True)

        cb = _dot(cm, bm, NT)
        dtv = dtg_ref[...]
        acs = acsg_ref[...]
        acst_v = acst_ref[...]
        alast = acs[L - 1:L, :]
        ri = lax.broadcasted_iota(jnp.int32, (L, L), 0)
        ci = lax.broadcasted_iota(jnp.int32, (L, L), 1)
        causal = ri >= ci
        lane = lax.broadcasted_iota(jnp.int32, (1, 128), 1)
        lo = lane < 64
        lo_rows = lax.broadcasted_iota(jnp.int32, (128, 1), 0) < 64
        lane_l = lax.broadcasted_iota(jnp.int32, (L, DT_PAD), 1)
        row_l = lax.broadcasted_iota(jnp.int32, (L, 1), 0)
        sub8 = lax.broadcasted_iota(jnp.int32, (8, L), 0)
        dskv = dsk_ref[...]
        dm_acc = jnp.zeros((L, L), F32)
        db_acc = jnp.zeros((L, N_STATE), F32)
        dc_acc = jnp.zeros((L, N_STATE), F32)
        ddt_out = jnp.zeros((L, DT_PAD), F32)
        dac_out = jnp.zeros((L, DT_PAD), F32)
        dar_out = jnp.zeros((8, L), F32)
        dxs = []
        for p in range(PAIRS_PER_GROUP):
            xp = xs[:, 128 * p:128 * p + 128]
            dyp = dy[:, 128 * p:128 * p + 128]
            dt_p = _pair_cols(dtv, p, lo)
            a_p = _pair_cols(acs, p, lo)
            al_p = _pair_cols(alast, p, lo)
            xdt = xp * dt_p
            xdt_b = xdt.astype(BF16)
            ea_p = jnp.exp(a_p)
            dsc_p = jnp.exp(al_p - a_p)
            sp = st_ref[p]
            sp_b = sp.astype(BF16)
            dsp = ds_ref[p]
            dsp_b = dsp.astype(BF16)
            cs = _dot(cm, sp_b, NT)
            dye_b = (dyp * ea_p).astype(BF16)
            dc_acc = dc_acc + _dot(dye_b, sp_b)
            ds_prev = _dot(dye_b, cm, TN)
            bds = _dot(bm, dsp_b, NT)
            al_rows = jnp.where(lo_rows, alast[:, 2 * p:2 * p + 1], alast[:, 2 * p + 1:2 * p + 2])
            ds_prev = ds_prev + jnp.exp(al_rows) * dsp
            prod_off = dyp * cs
            prod_st = dsp * sp
            dxdt_h = []
            for hh in range(2):
                j = 2 * p + hh
                hm = lo if hh == 0 else jnp.logical_not(lo)
                hm_rows = lo_rows if hh == 0 else jnp.logical_not(lo_rows)
                a_col = acs[:, j:j + 1]
                seg = a_col - acst_v[j:j + 1, :]
                lam = jnp.exp(jnp.where(causal, seg, -1e30))
                wf = cb * lam
                w = wf.astype(BF16)
                dy_h = jnp.where(hm, dyp, 0.0).astype(BF16)
                dw = _dot(dy_h, xdt_b, NT)
                dxdt_h.append(_dot(w, dyp.astype(BF16), TN))
                dm_acc = dm_acc + dw * lam
                e = dw * wf
                dac = jnp.sum(e, axis=1, keepdims=True)
                dar = -jnp.sum(e, axis=0, keepdims=True)
                ea_col = jnp.exp(a_col)
                dac = dac + ea_col * jnp.sum(jnp.where(hm, prod_off, 0.0), axis=1, keepdims=True)
                al_h = alast[:, j:j + 1]
                dal = jnp.exp(al_h) * jnp.sum(jnp.sum(jnp.where(hm_rows, prod_st, 0.0), axis=1, keepdims=True), axis=0, keepdims=True)
                xds_h = _dot(jnp.where(hm, xdt, 0.0).astype(BF16), dsp_b)
                dsc_col = jnp.exp(al_h - a_col)
                db_acc = db_acc + dsc_col * xds_h
                tt = jnp.sum(xds_h * bmf, axis=1, keepdims=True) * dsc_col
                dal = dal + jnp.sum(tt, axis=0, keepdims=True)
                dac = dac - tt + jnp.where(row_l == L - 1, dal, 0.0)
                dac_out = jnp.where(lane_l == j, dac, dac_out)
                dar_out = jnp.where(sub8 == j, dar, dar_out)
            dxdt = jnp.where(lo, dxdt_h[0], dxdt_h[1]) + dsc_p * bds
            dxs.append(dxdt * dt_p + dyp * dskv[:, 128 * p:128 * p + 128])
            prod_dt = dxdt * xp
            for hh in range(2):
                j = 2 * p + hh
                hm = lo if hh == 0 else jnp.logical_not(lo)
                ddt_col = jnp.sum(jnp.where(hm, prod_dt, 0.0), axis=1, keepdims=True)
                ddt_out = jnp.where(lane_l == j, ddt_col, ddt_out)
            ds_ref[p] = ds_prev
        dm_b = dm_acc.astype(BF16)
        dc_acc = dc_acc + _dot(dm_b, bm)
        db_acc = db_acc + _dot(dm_b, cm, TN)
        dxc_ref[...] = jnp.concatenate(dxs + [db_acc, dc_acc], axis=1).astype(dxc_ref.dtype)
        ddt_ref[...] = ddt_out
        dac_ref[...] = dac_out
        dar_ref[...] = dar_out

    rc = lambda g, c: (nc - 1 - c, g)
    grp = pl.BlockSpec((None, L, DT_PAD), lambda g, c: (g, nc - 1 - c, 0))
    vec = pl.BlockSpec((1, 512), lambda g, c: (0, g))
    return _pcall(body, grid=(N_GROUPS, nc),
                  in_specs=[pl.BlockSpec((L, GROUP_W), rc), grp, grp,
                            pl.BlockSpec((8, L), lambda g, c: (g, nc - 1 - c)),
                            pl.BlockSpec((L, 512), rc), pl.BlockSpec((L, 512), rc),
                            pl.BlockSpec((None, PAIRS_PER_GROUP, 128, N_STATE), lambda g, c: (nc - 1 - c, g, 0, 0)),
                            pl.BlockSpec((L, 512), rc), vec, vec],
                  out_specs=[pl.BlockSpec((L, GROUP_W), rc), pl.BlockSpec((L, 512), rc), grp, grp,
                             pl.BlockSpec((8, L), lambda g, c: (g, nc - 1 - c)), vec, vec],
                  out_shape=[_sds((t, D_XBC), BF16), _sds((t, D_SSD), BF16), _sds((N_GROUPS, t, DT_PAD), F32),
                             _sds((N_GROUPS, t, DT_PAD), F32), _sds((HEADS, t), F32), _sds((1, D_SSD), F32),
                             _sds((1, D_SSD), F32)],
                  scratch_shapes=[pltpu.VMEM((PAIRS_PER_GROUP, 128, N_STATE), F32)],
                  compiler_params=_params(dimension_semantics=("parallel", "arbitrary")), name=name)(
                      xc, dtg, acsg, acst, z, y, states, dmix, dskip, nw)


SC_CW = 1024


def _group_rstd(v):
    outs = []
    for q in range(v.shape[1] // SC_GROUP):
        vq = v[:, SC_GROUP * q:SC_GROUP * (q + 1)]
        outs.append(jnp.broadcast_to(lax.rsqrt(jnp.mean(vq * vq, axis=-1, keepdims=True) + EPS), vq.shape))
    return jnp.concatenate(outs, axis=1)


def _group_mean(v):
    outs = []
    for q in range(v.shape[1] // SC_GROUP):
        vq = v[:, SC_GROUP * q:SC_GROUP * (q + 1)]
        outs.append(jnp.broadcast_to(jnp.mean(vq, axis=-1, keepdims=True), vq.shape))
    return jnp.concatenate(outs, axis=1)


def _sc_fwd(ps, w, nw, mix, *, name):
    t = ps.shape[1]
    rows = CHUNK
    hb = rows // HALO
    cw = SC_CW
    off = D_SSD // cw

    def body(cur_ref, prev_ref, w_ref, nw_ref, mix_in_ref, o_ref):
        del mix_in_ref
        i = pl.program_id(1)
        u = cur_ref[0].astype(F32)
        gb = cur_ref[1].astype(F32)
        gc = cur_ref[2].astype(F32)
        cu_prev = jnp.where(i > 0, prev_ref[2].astype(F32) * prev_ref[0].astype(F32), 0.0)
        ext = jnp.concatenate([cu_prev, gc * u], axis=0)
        v = gb * _conv_taps(ext, w_ref[...], 3, rows)
        o_ref[...] = (v * _group_rstd(v) * nw_ref[...]).astype(o_ref.dtype)

    return _pcall(body, grid=(D_SC // cw, t // rows),
                  in_specs=[pl.BlockSpec((3, rows, cw), lambda j, i: (0, i, j)),
                            pl.BlockSpec((3, HALO, cw), lambda j, i: (0, jnp.maximum(i * hb - 1, 0), j)),
                            pl.BlockSpec((3, cw), lambda j, i: (0, j)), pl.BlockSpec((1, cw), lambda j, i: (0, j)),
                            pl.BlockSpec(memory_space=pl.ANY)],
                  out_specs=pl.BlockSpec((rows, cw), lambda j, i: (i, off + j)),
                  out_shape=_sds(mix.shape, mix.dtype), input_output_aliases={4: 0},
                  compiler_params=_params(dimension_semantics=("parallel", "parallel")), name=name)(ps, ps, w, nw, mix)


def _sc_bwd(ps, w, nw, dmix, *, name):
    t = ps.shape[1]
    rows = CHUNK
    hb = rows // HALO
    cw = SC_CW
    off = D_SSD // cw
    nblk = t // rows
    nhalo = t // HALO
    n = rows + 2 * HALO

    def body(cur_ref, prev_ref, next_ref, w_ref, nw_ref, d_ref, dn_ref, dps_ref, dw_ref, dnw_ref):
        i = pl.program_id(1)
        first = i == 0
        last = i == nblk - 1

        def ext(k):
            return jnp.concatenate([jnp.where(first, 0.0, prev_ref[k].astype(F32)), cur_ref[k].astype(F32),
                                    jnp.where(last, 0.0, next_ref[k].astype(F32))], axis=0)

        ue, gbe, gce = ext(0), ext(1), ext(2)
        wv = w_ref[...]
        nwv = nw_ref[...]
        cue = gce * ue
        cu1 = pltpu.roll(cue, 1, axis=0)
        cu2 = pltpu.roll(cue, 2, axis=0)
        conv = wv[2:3, :] * cue + wv[1:2, :] * cu1 + wv[0:1, :] * cu2
        ve = gbe * conv
        doe = jnp.concatenate([jnp.zeros((HALO, cw), F32), d_ref[...].astype(F32),
                               jnp.where(last, 0.0, dn_ref[...].astype(F32))], axis=0)
        r = _group_rstd(ve)
        xh = ve * r
        dvn = doe * nwv
        dv = r * (dvn - xh * _group_mean(dvn * xh))
        dconv = dv * gbe
        dcu = wv[2:3, :] * dconv + wv[1:2, :] * pltpu.roll(dconv, n - 1, axis=0) + wv[0:1, :] * pltpu.roll(dconv, n - 2, axis=0)
        sl = slice(HALO, HALO + rows)
        dps_ref[0] = (dcu * gce)[sl].astype(dps_ref.dtype)
        dps_ref[1] = (dv * conv)[sl].astype(dps_ref.dtype)
        dps_ref[2] = (dcu * ue)[sl].astype(dps_ref.dtype)
        dc = dconv[sl]
        dwv = jnp.concatenate([jnp.sum(dc * cu2[sl], axis=0, keepdims=True), jnp.sum(dc * cu1[sl], axis=0, keepdims=True),
                               jnp.sum(dc * cue[sl], axis=0, keepdims=True)], axis=0)
        dnv = jnp.sum((doe * xh)[sl], axis=0, keepdims=True)

        @pl.when(first)
        def _():
            dw_ref[...] = dwv
            dnw_ref[...] = dnv

        @pl.when(i > 0)
        def _():
            dw_ref[...] += dwv
            dnw_ref[...] += dnv

    cur = pl.BlockSpec((3, rows, cw), lambda j, i: (0, i, j))
    prev = pl.BlockSpec((3, HALO, cw), lambda j, i: (0, jnp.maximum(i * hb - 1, 0), j))
    nxt = pl.BlockSpec((3, HALO, cw), lambda j, i: (0, jnp.minimum((i + 1) * hb, nhalo - 1), j))
    return _pcall(body, grid=(D_SC // cw, nblk),
                  in_specs=[cur, prev, nxt, pl.BlockSpec((3, cw), lambda j, i: (0, j)), pl.BlockSpec((1, cw), lambda j, i: (0, j)),
                            pl.BlockSpec((rows, cw), lambda j, i: (i, off + j)),
                            pl.BlockSpec((HALO, cw), lambda j, i: (jnp.minimum((i + 1) * hb, nhalo - 1), off + j))],
                  out_specs=[cur, pl.BlockSpec((3, cw), lambda j, i: (0, j)), pl.BlockSpec((1, cw), lambda j, i: (0, j))],
                  out_shape=[_sds(ps.shape, BF16), _sds((3, D_SC), F32), _sds((1, D_SC), F32)],
                  compiler_params=_params(dimension_semantics=("parallel", "arbitrary")), name=name)(ps, ps, ps, w, nw, dmix, dmix)


XA_SCALE = XA_HD ** -0.5


def _softmax(s):
    m = jnp.max(s, axis=-1, keepdims=True)
    e = jnp.exp(s - m)
    return e / jnp.sum(e, axis=-1, keepdims=True)


def _xa_fwd(q, k, v, *, name):
    t = q.shape[0]
    nm = k.shape[0]
    tq = _tile(t, 512)

    def body(q_ref, k_ref, v_ref, o_ref):
        outs = []
        for h in range(XA_HEADS):
            sl = slice(XA_HD * h, XA_HD * (h + 1))
            s = _dot(q_ref[:, sl], k_ref[:, sl], NT) * XA_SCALE
            outs.append(_dot(_softmax(s).astype(BF16), v_ref[:, sl]))
        o_ref[...] = jnp.concatenate(outs, axis=1).astype(o_ref.dtype)

    row = pl.BlockSpec((tq, D_XA), lambda i: (i, 0))
    kv = pl.BlockSpec((nm, D_XA), lambda i: (0, 0))
    return _pcall(body, grid=(t // tq,), in_specs=[row, kv, kv], out_specs=row, out_shape=_sds((t, D_XA), BF16),
                  compiler_params=_params(dimension_semantics=("parallel",)), name=name)(q, k, v)


def _xa_bwd(q, k, v, do, *, name):
    t = q.shape[0]
    nm = k.shape[0]
    tq = _tile(t, 512)

    def body(q_ref, k_ref, v_ref, do_ref, dq_ref, dk_ref, dv_ref):
        i = pl.program_id(0)
        dqs, dks, dvs = [], [], []
        for h in range(XA_HEADS):
            sl = slice(XA_HD * h, XA_HD * (h + 1))
            qh, kh, vh, doh = q_ref[:, sl], k_ref[:, sl], v_ref[:, sl], do_ref[:, sl]
            p = _softmax(_dot(qh, kh, NT) * XA_SCALE)
            dvs.append(_dot(p.astype(BF16), doh, TN))
            dp = _dot(doh, vh, NT)
            ds = (p * (dp - jnp.sum(dp * p, axis=-1, keepdims=True)) * XA_SCALE).astype(BF16)
            dqs.append(_dot(ds, kh))
            dks.append(_dot(ds, qh, TN))
        dq_ref[...] = jnp.concatenate(dqs, axis=1).astype(dq_ref.dtype)
        dkv = jnp.concatenate(dks, axis=1)
        dvv = jnp.concatenate(dvs, axis=1)

        @pl.when(i == 0)
        def _():
            dk_ref[...] = dkv
            dv_ref[...] = dvv

        @pl.when(i > 0)
        def _():
            dk_ref[...] += dkv
            dv_ref[...] += dvv

    row = pl.BlockSpec((tq, D_XA), lambda i: (i, 0))
    kv = pl.BlockSpec((nm, D_XA), lambda i: (0, 0))
    return _pcall(body, grid=(t // tq,), in_specs=[row, kv, kv, row], out_specs=[row, kv, kv],
                  out_shape=[_sds((t, D_XA), BF16), _sds((nm, D_XA), F32), _sds((nm, D_XA), F32)],
                  compiler_params=_params(dimension_semantics=("arbitrary",)), name=name)(q, k, v, do)


FF_CW = 1408


def _swiglu_fwd(g, u, *, name):
    t, f = g.shape
    tr = _tile(t, 512)

    def body(g_ref, u_ref, o_ref):
        gv = g_ref[...].astype(F32)
        o_ref[...] = (gv * _sig(gv) * u_ref[...].astype(F32)).astype(o_ref.dtype)

    blk = pl.BlockSpec((tr, FF_CW), lambda i, j: (i, j))
    return _pcall(body, grid=(t // tr, f // FF_CW), in_specs=[blk, blk], out_specs=blk, out_shape=_sds((t, f), BF16),
                  compiler_params=_params(dimension_semantics=("parallel", "parallel")), name=name)(g, u)


def _swiglu_bwd(g, u, dact, *, name):
    t, f = g.shape
    tr = _tile(t, 512)

    def body(g_ref, u_ref, d_ref, dg_ref, du_ref):
        gv = g_ref[...].astype(F32)
        uv = u_ref[...].astype(F32)
        dv = d_ref[...].astype(F32)
        s = _sig(gv)
        dg_ref[...] = (dv * uv * (s * (1.0 + gv * (1.0 - s)))).astype(dg_ref.dtype)
        du_ref[...] = (dv * gv * s).astype(du_ref.dtype)

    blk = pl.BlockSpec((tr, FF_CW), lambda i, j: (i, j))
    return _pcall(body, grid=(t // tr, f // FF_CW), in_specs=[blk, blk, blk], out_specs=[blk, blk],
                  out_shape=[_sds((t, f), BF16), _sds((t, f), BF16)],
                  compiler_params=_params(dimension_semantics=("parallel", "parallel")), name=name)(g, u, dact)


def _row_tile(n):
    for cand in (128, 64, 32, 16):
        if n % cand == 0:
            return cand
    raise ValueError(n)


def _add_halves(g4, rb, c_arr, *, name):
    _, r, cdim = g4.shape
    hr = r // 2
    rt = _row_tile(hr)
    nb = hr // rt

    def body(c_ref, g_ref, rb_ref, o_ref):
        del c_ref
        o_ref[...] = (g_ref[...].astype(F32) + rb_ref[...].astype(F32)).astype(o_ref.dtype)

    gs = pltpu.PrefetchScalarGridSpec(
        num_scalar_prefetch=1, grid=(4, nb),
        in_specs=[pl.BlockSpec((None, rt, cdim), lambda q, i, c: (q, c[0] * nb + i, 0)),
                  pl.BlockSpec((None, rt, cdim), lambda q, i, c: (q, i, 0))],
        out_specs=pl.BlockSpec((None, rt, cdim), lambda q, i, c: (q, i, 0)))
    return _pcall(body, grid_spec=gs, out_shape=_sds((4, hr, cdim), BF16),
                  compiler_params=_params(dimension_semantics=("parallel", "parallel")), name=name)(c_arr, g4, rb)


def _sum_chips(p4, rc, q_arr, *, name):
    _, hr, cdim = p4.shape
    rt = _row_tile(hr)

    def body(q_ref, p_ref, rc_ref, o_ref):
        del q_ref
        o_ref[...] = ((p_ref[...].astype(F32) + rc_ref[0].astype(F32)) + rc_ref[1].astype(F32)) + rc_ref[2].astype(F32)

    gs = pltpu.PrefetchScalarGridSpec(
        num_scalar_prefetch=1, grid=(hr // rt,),
        in_specs=[pl.BlockSpec((None, rt, cdim), lambda i, q: (q[0], i, 0)),
                  pl.BlockSpec((3, rt, cdim), lambda i, q: (0, i, 0))],
        out_specs=pl.BlockSpec((rt, cdim), lambda i, q: (i, 0)))
    return _pcall(body, grid_spec=gs, out_shape=_sds((hr, cdim), F32),
                  compiler_params=_params(dimension_semantics=("parallel",)), name=name)(q_arr, p4, rc)


def _adam_math(w, g, m, v):
    m = ADAM_B1 * m + (1.0 - ADAM_B1) * g
    v = ADAM_B2 * v + (1.0 - ADAM_B2) * (g * g)
    m_hat = m / (1.0 - ADAM_B1 ** ADAM_STEP)
    v_hat = v / (1.0 - ADAM_B2 ** ADAM_STEP)
    delta = -ADAM_LR * (m_hat / (jnp.sqrt(v_hat) + ADAM_EPS) + ADAM_WD * w)
    return delta, m, v


def _adamw_layer(layer, w, m, v, g, prev, *, name):
    depth, r, cdim = w.shape
    rt = _row_tile(r)
    n_prev = 0 if prev is None else 4

    def body(*refs):
        w_ref, m_ref, v_ref, g_ref = refs[:4]
        go_ref, d_ref, mo_ref, vo_ref = refs[4 + n_prev:]
        gv = g_ref[...]
        delta, mn, vn = _adam_math(w_ref[...], gv, m_ref[...], v_ref[...])
        go_ref[...] = gv
        d_ref[...] = delta
        mo_ref[...] = mn
        vo_ref[...] = vn

    st = pl.BlockSpec((None, rt, cdim), lambda i: (layer, i, 0))
    in_specs = [st, st, st, pl.BlockSpec((rt, cdim), lambda i: (i, 0))] + [pl.BlockSpec(memory_space=pl.ANY)] * n_prev
    args = (w, m, v, g) + (tuple(prev) if prev is not None else ())
    return _pcall(body, grid=(r // rt,), in_specs=in_specs, out_specs=[st] * 4,
                  out_shape=[_sds((depth, r, cdim), F32)] * 4,
                  input_output_aliases={4 + i: i for i in range(n_prev)},
                  compiler_params=_params(dimension_semantics=("parallel",)), name=name)(*args)


def _adamw_flat(w, g, m, v, *, name):
    def body(w_ref, g_ref, m_ref, v_ref, d_ref, mo_ref, vo_ref):
        delta, mn, vn = _adam_math(w_ref[...], g_ref[...], m_ref[...], v_ref[...])
        d_ref[...] = delta
        mo_ref[...] = mn
        vo_ref[...] = vn

    return _pcall(body, out_shape=[_sds(w.shape, F32)] * 3, compiler_params=_params(), name=name)(w, g, m, v)


def _place():
    x, y, c = lax.axis_index("x"), lax.axis_index("y"), lax.axis_index("c")
    chips = [(1 - x, y), (x, 1 - y), (1 - x, 1 - y)]
    return x, y, c, chips


def _gather_weights(shards, *, name):
    n = len(shards)

    def body(*refs):
        ins, outs = refs[:n], refs[n:2 * n]
        send1, recv1, send2, recv2, lsem = refs[2 * n:]
        x, y, c, chips = _place()
        qme = 2 * x + y
        sib = (x, y, 1 - c)
        local = []
        for t in range(n):
            cp = pltpu.make_async_copy(ins[t], outs[t].at[qme], lsem.at[t])
            cp.start()
            local.append(cp)

        def half(t, which):
            hr = ins[t].shape[0] // 2
            return pl.ds(which * hr, hr)

        first = []
        for t in range(n):
            for k, (px, py) in enumerate(chips):
                cp = pltpu.make_async_remote_copy(src_ref=ins[t].at[half(t, c)], dst_ref=outs[t].at[qme, half(t, c)],
                                                  send_sem=send1.at[t, k], recv_sem=recv1.at[t, k],
                                                  device_id=(px, py, c), device_id_type=MESH)
                cp.start()
                first.append(cp)
        second = []
        for t in range(n):
            for k, (px, py) in enumerate(chips):
                qk = 2 * px + py
                landed = outs[t].at[qk, half(t, c)]
                pltpu.make_async_remote_copy(src_ref=ins[t].at[half(t, c)], dst_ref=landed,
                                             send_sem=send1.at[t, k], recv_sem=recv1.at[t, k],
                                             device_id=(px, py, c), device_id_type=MESH).wait_recv()
                cp = pltpu.make_async_remote_copy(src_ref=landed, dst_ref=landed, send_sem=send2.at[t, k],
                                                  recv_sem=recv2.at[t, k], device_id=sib, device_id_type=MESH)
                cp.start()
                second.append(cp)
        for t in range(n):
            for k, (px, py) in enumerate(chips):
                other = outs[t].at[2 * px + py, half(t, 1 - c)]
                pltpu.make_async_remote_copy(src_ref=other, dst_ref=other, send_sem=send2.at[t, k],
                                             recv_sem=recv2.at[t, k], device_id=sib, device_id_type=MESH).wait_recv()
        for cp in first + second:
            cp.wait_send()
        for cp in local:
            cp.wait()

    anyspec = pl.BlockSpec(memory_space=pl.ANY)
    return _pcall(body, in_specs=[anyspec] * n, out_specs=[anyspec] * n,
                  out_shape=[_sds((4,) + s.shape, s.dtype) for s in shards],
                  scratch_shapes=[pltpu.SemaphoreType.DMA((n, 3))] * 4 + [pltpu.SemaphoreType.DMA((n,))],
                  compiler_params=pltpu.CompilerParams(has_side_effects=True), name=name)(*shards)


def _swap_halves(grads, *, name):
    n = len(grads)

    def body(*refs):
        ins, outs = refs[:n], refs[n:2 * n]
        send, recv = refs[2 * n:]
        x, y, c, _ = _place()
        cps = []
        for t in range(n):
            hr = ins[t].shape[1] // 2
            cp = pltpu.make_async_remote_copy(src_ref=ins[t].at[:, pl.ds((1 - c) * hr, hr), :], dst_ref=outs[t],
                                              send_sem=send.at[t], recv_sem=recv.at[t],
                                              device_id=(x, y, 1 - c), device_id_type=MESH)
            cp.start()
            cps.append(cp)
        for cp in cps:
            cp.wait()

    anyspec = pl.BlockSpec(memory_space=pl.ANY)
    return _pcall(body, in_specs=[anyspec] * n, out_specs=[anyspec] * n,
                  out_shape=[_sds((4, g.shape[1] // 2, g.shape[2]), g.dtype) for g in grads],
                  scratch_shapes=[pltpu.SemaphoreType.DMA((n,))] * 2,
                  compiler_params=pltpu.CompilerParams(has_side_effects=True), name=name)(*grads)


def _send_to_owners(parts, *, name):
    n = len(parts)

    def body(*refs):
        ins, outs = refs[:n], refs[n:2 * n]
        send, recv = refs[2 * n:]
        x, y, c, chips = _place()
        cps = []
        for t in range(n):
            for k, (px, py) in enumerate(chips):
                cp = pltpu.make_async_remote_copy(src_ref=ins[t].at[2 * px + py], dst_ref=outs[t].at[k],
                                                  send_sem=send.at[t, k], recv_sem=recv.at[t, k],
                                                  device_id=(px, py, c), device_id_type=MESH)
                cp.start()
                cps.append(cp)
        for cp in cps:
            cp.wait()

    anyspec = pl.BlockSpec(memory_space=pl.ANY)
    return _pcall(body, in_specs=[anyspec] * n, out_specs=[anyspec] * n,
                  out_shape=[_sds((3,) + p.shape[1:], p.dtype) for p in parts],
                  scratch_shapes=[pltpu.SemaphoreType.DMA((n, 3))] * 2,
                  compiler_params=pltpu.CompilerParams(has_side_effects=True), name=name)(*parts)


def _join_halves(halves, *, name):
    n = len(halves)

    def body(*refs):
        ins, outs = refs[:n], refs[n:2 * n]
        send, recv, lsem = refs[2 * n:]
        x, y, c, _ = _place()
        cps, local = [], []
        for t in range(n):
            hr = ins[t].shape[0]
            mine = outs[t].at[pl.ds(c * hr, hr)]
            lc = pltpu.make_async_copy(ins[t], mine, lsem.at[t])
            lc.start()
            local.append(lc)
            cp = pltpu.make_async_remote_copy(src_ref=ins[t], dst_ref=mine, send_sem=send.at[t], recv_sem=recv.at[t],
                                              device_id=(x, y, 1 - c), device_id_type=MESH)
            cp.start()
            cps.append(cp)
        for t in range(n):
            hr = ins[t].shape[0]
            theirs = outs[t].at[pl.ds((1 - c) * hr, hr)]
            pltpu.make_async_remote_copy(src_ref=ins[t], dst_ref=theirs, send_sem=send.at[t], recv_sem=recv.at[t],
                                         device_id=(x, y, 1 - c), device_id_type=MESH).wait_recv()
        for cp in cps:
            cp.wait_send()
        for lc in local:
            lc.wait()

    anyspec = pl.BlockSpec(memory_space=pl.ANY)
    return _pcall(body, in_specs=[anyspec] * n, out_specs=[anyspec] * n,
                  out_shape=[_sds((2 * h.shape[0], h.shape[1]), h.dtype) for h in halves],
                  scratch_shapes=[pltpu.SemaphoreType.DMA((n,))] * 3,
                  compiler_params=pltpu.CompilerParams(has_side_effects=True), name=name)(*halves)


def _allreduce_small(buf, *, name):
    rows = buf.shape[0]
    rels = [(dx, dy, dc) for dx in (0, 1) for dy in (0, 1) for dc in (0, 1)][1:]

    def body(in_ref, out_ref, gbuf, send, recv):
        x, y, c = lax.axis_index("x"), lax.axis_index("y"), lax.axis_index("c")
        me = 4 * x + 2 * y + c
        gbuf[me] = in_ref[...]
        cps = []
        for k, (dx, dy, dc) in enumerate(rels):
            peer = (x + dx - 2 * x * dx, y + dy - 2 * y * dy, c + dc - 2 * c * dc)
            cp = pltpu.make_async_remote_copy(src_ref=in_ref, dst_ref=gbuf.at[me], send_sem=send.at[k], recv_sem=recv.at[k],
                                              device_id=peer, device_id_type=MESH)
            cp.start()
            cps.append(cp)
        for k, (dx, dy, dc) in enumerate(rels):
            px, py, pc = x + dx - 2 * x * dx, y + dy - 2 * y * dy, c + dc - 2 * c * dc
            pltpu.make_async_remote_copy(src_ref=in_ref, dst_ref=gbuf.at[4 * px + 2 * py + pc], send_sem=send.at[k],
                                         recv_sem=recv.at[k], device_id=(px, py, pc), device_id_type=MESH).wait_recv()
        for cp in cps:
            cp.wait_send()
        acc = gbuf[0]
        for d in range(1, 8):
            acc = acc + gbuf[d]
        out_ref[...] = acc

    vm = pl.BlockSpec(memory_space=pltpu.VMEM)
    return _pcall(body, in_specs=[vm], out_specs=vm, out_shape=_sds(buf.shape, F32),
                  scratch_shapes=[pltpu.VMEM((8, rows, 128), F32), pltpu.SemaphoreType.DMA((7,)), pltpu.SemaphoreType.DMA((7,))],
                  compiler_params=_params(has_side_effects=True), name=name)(buf)


def _pack(arrs):
    flat = jnp.concatenate([a.reshape(-1).astype(F32) for a in arrs])
    n = flat.shape[0]
    rows = -(-n // 1024) * 8
    return jnp.pad(flat, (0, rows * 128 - n)).reshape(rows, 128)


def _unpack(buf, shapes):
    flat = buf.reshape(-1)
    out, o = [], 0
    for s in shapes:
        n = 1
        for d in s:
            n *= d
        out.append(flat[o:o + n].reshape(s))
        o += n
    return out


def _perm_xbc(a):
    parts = []
    for g in range(N_GROUPS):
        parts += [a[..., 512 * g:512 * (g + 1)], a[..., 2048 + 128 * g:2048 + 128 * (g + 1)],
                  a[..., 2560 + 128 * g:2560 + 128 * (g + 1)]]
    return jnp.concatenate(parts, axis=-1)


def _unperm_xbc(a):
    xs = [a[..., GROUP_W * g:GROUP_W * g + 512] for g in range(N_GROUPS)]
    bs = [a[..., GROUP_W * g + 512:GROUP_W * g + 640] for g in range(N_GROUPS)]
    cs = [a[..., GROUP_W * g + 640:GROUP_W * (g + 1)] for g in range(N_GROUPS)]
    return jnp.concatenate(xs + bs + cs, axis=-1)


def _split_w_in(w4):
    k = w4.shape[1]
    nat = jnp.transpose(w4, (1, 0, 2)).reshape(k, -1)
    w_z = nat[:, :2048]
    w_xbc = _perm_xbc(nat[:, 2048:5120])
    w_dt = jnp.pad(nat[:, 5120:5152], ((0, 0), (0, DT_PAD - HEADS)))
    w_s = nat[:, 5152:]
    return w_z, w_xbc, w_dt, w_s


def _join_dw_in(dw_z, dw_xbc, dw_dt, dw_s):
    k = dw_z.shape[0]
    nat = jnp.concatenate([dw_z, _unperm_xbc(dw_xbc), dw_dt[:, :HEADS], dw_s], axis=1)
    return jnp.transpose(nat.reshape(k, 4, -1), (1, 0, 2))


def kernel(x, mem, norm_mix, w_in, ssd_conv_w, ssd_conv_b, dt_bias, a_log, d_skip, ssd_norm, sc_conv_w, sc_norm, w_out, mem_norm, norm_xa, w_q, w_k, w_v, w_o, norm_ffn, w_gate, w_up, w_down, norm_final, loss_target, m_norm_mix, m_w_in, m_ssd_conv_w, m_ssd_conv_b, m_dt_bias, m_a_log, m_d_skip, m_ssd_norm, m_sc_conv_w, m_sc_norm, m_w_out, m_mem_norm, m_norm_xa, m_w_q, m_w_k, m_w_v, m_w_o, m_norm_ffn, m_w_gate, m_w_up, m_w_down, m_norm_final, v_norm_mix, v_w_in, v_ssd_conv_w, v_ssd_conv_b, v_dt_bias, v_a_log, v_d_skip, v_ssd_norm, v_sc_conv_w, v_sc_norm, v_w_out, v_mem_norm, v_norm_xa, v_w_q, v_w_k, v_w_v, v_w_o, v_norm_ffn, v_w_gate, v_w_up, v_w_down, v_norm_final):
    depth = w_in.shape[0]
    ix, iy, ic = lax.axis_index("x"), lax.axis_index("y"), lax.axis_index("c")
    qme = 2 * ix + iy
    c_arr = jnp.reshape(ic, (1,)).astype(jnp.int32)
    q_arr = jnp.reshape(qme, (1,)).astype(jnp.int32)
    h = x[0]
    tgt = loss_target[0]

    big = dict(w_in=w_in, w_out=w_out, w_q=w_q, w_k=w_k, w_v=w_v, w_o=w_o, w_gate=w_gate, w_up=w_up, w_down=w_down)
    big_m = dict(w_in=m_w_in, w_out=m_w_out, w_q=m_w_q, w_k=m_w_k, w_v=m_w_v, w_o=m_w_o, w_gate=m_w_gate, w_up=m_w_up, w_down=m_w_down)
    big_v = dict(w_in=v_w_in, w_out=v_w_out, w_q=v_w_q, w_k=v_w_k, w_v=v_w_v, w_o=v_w_o, w_gate=v_w_gate, w_up=v_w_up, w_down=v_w_down)
    names = list(big)

    conv_full = jnp.zeros((depth, 4, D_XBC), F32)
    conv_full = lax.dynamic_update_slice(conv_full, jnp.where(ic == 0, ssd_conv_w, 0.0), (0, 0, qme * (D_XBC // 4)))
    sc_full = jnp.zeros((depth, 3, D_SC), F32)
    sc_full = lax.dynamic_update_slice(sc_full, jnp.where(ic == 0, sc_conv_w, 0.0), (0, 0, qme * (D_SC // 4)))
    conv_full, sc_full = _unpack(_allreduce_small(_pack([conv_full, sc_full]), name="gather_conv_w"),
                                 [conv_full.shape, sc_full.shape])
    conv_p = _perm_xbc(conv_full)
    convb_p = _perm_xbc(ssd_conv_b)

    pad_h = lambda a: jnp.pad(a, ((0, 0), (0, DT_PAD - HEADS)))
    dt_bias_p, a_log_p = pad_h(dt_bias), pad_h(a_log)
    dskip_ch = jnp.repeat(d_skip, D_SSD // HEADS, axis=1)

    gathered = []
    for l in range(depth):
        outs = _gather_weights([big[n][l].astype(BF16) for n in names], name=f"gather_w{l}")
        gathered.append(dict(zip(names, outs)))

    memn = _rms_fwd(mem[0], mem_norm.reshape(1, D), name="memn")

    saved = []
    for l in range(depth):
        gw = gathered[l]
        w_z, w_xbc, w_dt, w_s = _split_w_in(gw["w_in"])
        wo2 = gw["w_out"].reshape(-1, D)
        wq2, wk2, wv2 = (gw[n].reshape(D, D_XA) for n in ("w_q", "w_k", "w_v"))
        wd2 = gw["w_down"].reshape(D_FF, D)
        s = dict(h0=h, w_z=w_z, w_xbc=w_xbc, w_dt=w_dt, w_s=w_s, wo2=wo2, wq2=wq2, wk2=wk2, wv2=wv2, wd2=wd2)
        hn1 = _rms_fwd(h, norm_mix[l:l + 1], name=f"rms_mix{l}")
        pz = _mm_nn(hn1, w_z, tn=1024, name=f"proj_z{l}")
        pxbc = _mm_nn(hn1, w_xbc, tn=1024, name=f"proj_xbc{l}")
        dtr = _mm_nn(hn1, w_dt, tn=DT_PAD, out_dtype=F32, name=f"proj_dt{l}")
        t = h.shape[0]
        tm = _tile(t, 1024)
        ps = _mm(hn1, w_s, mode="nn", grid=(t // tm, 6, 1),
                 a_spec=pl.BlockSpec((tm, D), lambda i, j, kk: (i, 0)),
                 b_spec=pl.BlockSpec((D, 1024), lambda i, j, kk: (0, j)),
                 o_spec=pl.BlockSpec((None, tm, 1024), lambda i, j, kk: (j // 2, i, j % 2)), o_tile=(tm, 1024),
                 out_sds=_sds((3, t, D_SC), BF16), name=f"proj_s{l}")
        xc = _conv_fwd(pxbc, conv_p[l], convb_p[l:l + 1], name=f"conv{l}")
        dt, dtg, acsg, acst = _dt_prep(dtr, dt_bias_p[l:l + 1], a_log_p[l:l + 1], name=f"dt_prep{l}")
        y, states, mix = _ssd_fwd(xc, dtg, acsg, acst, pz, dskip_ch[l:l + 1], ssd_norm[l:l + 1], name=f"ssd{l}")
        mix = _sc_fwd(ps, sc_full[l], sc_norm[l:l + 1], mix, name=f"sc{l}")
        h1 = _mm_nn(mix, wo2, tn=1024, tk=2048, out_dtype=F32, add=h, name=f"out_proj{l}")
        hn2 = _rms_fwd(h1, norm_xa[l:l + 1], name=f"rms_xa{l}")
        q = _mm_nn(hn2, wq2, tn=D_XA, name=f"q{l}")
        k = _mm_nn(memn, wk2, tn=D_XA, name=f"k{l}")
        v = _mm_nn(memn, wv2, tn=D_XA, name=f"v{l}")
        o = _xa_fwd(q, k, v, name=f"xa{l}")
        h2 = _mm_nn_sm(o, gw["w_o"], out_dtype=F32, add=h1, name=f"xa_out{l}")
        hn3 = _rms_fwd(h2, norm_ffn[l:l + 1], name=f"rms_ffn{l}")
        fg = _mm_nn_sm(hn3, gw["w_gate"], name=f"ff_gate{l}")
        fu = _mm_nn_sm(hn3, gw["w_up"], name=f"ff_up{l}")
        act = _swiglu_fwd(fg, fu, name=f"swiglu{l}")
        h3 = _mm_nn(act, wd2, tn=1024, tk=FF_CW, out_dtype=F32, add=h2, name=f"ff_down{l}")
        s.update(hn1=hn1, pz=pz, pxbc=pxbc, dtr=dtr, ps=ps, xc=xc, dt=dt, dtg=dtg, acsg=acsg, acst=acst, y=y,
                 states=states, mix=mix, h1=h1, hn2=hn2, q=q, k=k, v=v, o=o, h2=h2, hn3=hn3, fg=fg, fu=fu, act=act)
        saved.append(s)
        h = h3

    loss_vec, dh, d_norm_final = _final(h, norm_final.reshape(1, D), tgt, name="final")
    loss = lax.psum(loss_vec[0, 0], ("x", "y", "c"))

    small = dict(norm_mix=[], ssd_conv_w=[], ssd_conv_b=[], dt_bias=[], a_log=[], d_skip=[], ssd_norm=[], sc_conv_w=[],
                 sc_norm=[], norm_xa=[], norm_ffn=[])
    dmemn = None
    carried = {n: None for n in names}
    for l in reversed(range(depth)):
        s, gw = saved[l], gathered[l]
        t = dh.shape[0]
        tm = _tile(t, 1024)
        dact = _mm_nt(dh, s["wd2"], tn=FF_CW, name=f"d_act{l}")
        dw_down = _mm_tn(s["act"], dh, tm=FF_CW, tn=1024, name=f"dw_down{l}")
        dg, du = _swiglu_bwd(s["fg"], s["fu"], dact, name=f"d_swiglu{l}")
        dw_gate = _mm_tn_sm(s["hn3"], dg, tm=1024, name=f"dw_gate{l}")
        dw_up = _mm_tn_sm(s["hn3"], du, tm=1024, name=f"dw_up{l}")
        dhn = _mm_nt_sm(dg, gw["w_gate"], tn=1024, out_dtype=F32, name=f"d_hn3a{l}")
        dhn = _mm_nt_sm(du, gw["w_up"], tn=1024, out_dtype=F32, add=dhn, name=f"d_hn3b{l}")
        dh, dn = _rms_bwd(s["h2"], norm_ffn[l:l + 1], dhn, dh, name=f"d_rms_ffn{l}")
        small["norm_ffn"].append(dn)
        do = _mm_nt_sm(dh, gw["w_o"], tn=D_XA, name=f"d_o{l}")
        dw_o = _mm_tn_sm(s["o"], dh, tm=D_XA, name=f"dw_o{l}")
        dq, dk, dv = _xa_bwd(s["q"], s["k"], s["v"], do, name=f"d_xa{l}")
        dw_q = _mm_tn(s["hn2"], dq, tm=1024, tn=D_XA, name=f"dw_q{l}")
        dw_k = _mm_tn(memn, dk, tm=1024, tn=D_XA, name=f"dw_k{l}")
        dw_v = _mm_tn(memn, dv, tm=1024, tn=D_XA, name=f"dw_v{l}")
        dhn = _mm_nt(dq, s["wq2"], tn=1024, out_dtype=F32, name=f"d_hn2{l}")
        dmemn = _mm_nt(dk, s["wk2"], tn=1024, out_dtype=F32, add=dmemn, name=f"d_memn_k{l}")
        dmemn = _mm_nt(dv, s["wv2"], tn=1024, out_dtype=F32, add=dmemn, name=f"d_memn_v{l}")
        dh, dn = _rms_bwd(s["h1"], norm_xa[l:l + 1], dhn, dh, name=f"d_rms_xa{l}")
        small["norm_xa"].append(dn)
        dmix = _mm_nt(dh, s["wo2"], tn=1024, name=f"d_mix{l}")
        dw_out = _mm_tn(s["mix"], dh, tm=1024, tn=1024, name=f"dw_out{l}")
        dps, d_scw, d_scn = _sc_bwd(s["ps"], sc_full[l], sc_norm[l:l + 1], dmix, name=f"d_sc{l}")
        dxc, dz, ddtg, dacg, dart, ddsk, d_ssdn = _ssd_bwd(s["xc"], s["dtg"], s["acsg"], s["acst"], s["pz"], s["y"], s["states"],
                                                           dmix, dskip_ch[l:l + 1], ssd_norm[l:l + 1], name=f"d_ssd{l}")
        dxbc, d_cw, d_cb = _conv_bwd(dxc, s["pxbc"], conv_p[l], convb_p[l:l + 1], name=f"d_conv{l}")
        ddtr, d_dtb, d_alog = _dt_bwd(ddtg, dacg, dart, s["dt"], s["dtr"], dt_bias_p[l:l + 1], a_log_p[l:l + 1], name=f"d_dt{l}")
        small["sc_conv_w"].append(d_scw)
        small["sc_norm"].append(d_scn)
        small["ssd_norm"].append(d_ssdn)
        small["d_skip"].append(jnp.sum(ddsk.reshape(HEADS, D_SSD // HEADS), axis=1).reshape(1, HEADS))
        small["ssd_conv_w"].append(_unperm_xbc(d_cw))
        small["ssd_conv_b"].append(_unperm_xbc(d_cb))
        small["dt_bias"].append(d_dtb[:, :HEADS])
        small["a_log"].append(d_alog[:, :HEADS])
        hn1 = s["hn1"]
        dw_z = _mm_tn(hn1, dz, tm=1024, tn=1024, name=f"dw_z{l}")
        dw_xbc = _mm_tn(hn1, dxbc, tm=1024, tn=1024, name=f"dw_xbc{l}")
        dw_dt = _mm_tn(hn1, ddtr, tm=1024, tn=DT_PAD, name=f"dw_dt{l}")
        tk = _tile(t, 512)
        dw_s = _mm(hn1, dps, mode="tn", grid=(2, 6, t // tk),
                   a_spec=pl.BlockSpec((tk, 1024), lambda i, j, kk: (kk, i)),
                   b_spec=pl.BlockSpec((None, tk, 1024), lambda i, j, kk: (j // 2, kk, j % 2)),
                   o_spec=pl.BlockSpec((1024, 1024), lambda i, j, kk: (i, j)), o_tile=(1024, 1024),
                   out_sds=_sds((D, 3 * D_SC), BF16), name=f"dw_s{l}")
        dhn = _mm_nt(dz, s["w_z"], tn=1024, out_dtype=F32, name=f"d_hn1z{l}")
        dhn = _mm_nt(dxbc, s["w_xbc"], tn=1024, out_dtype=F32, add=dhn, name=f"d_hn1x{l}")
        dhn = _mm_nt(ddtr, s["w_dt"], tn=1024, out_dtype=F32, add=dhn, name=f"d_hn1d{l}")
        dhn = _mm(dps, s["w_s"], mode="nt", grid=(t // tm, 2, 6),
                  a_spec=pl.BlockSpec((None, tm, 1024), lambda i, j, kk: (kk // 2, i, kk % 2)),
                  b_spec=pl.BlockSpec((1024, 1024), lambda i, j, kk: (j, kk)),
                  o_spec=pl.BlockSpec((tm, 1024), lambda i, j, kk: (i, j)), o_tile=(tm, 1024),
                  out_sds=_sds((t, D), F32), add=dhn, name=f"d_hn1s{l}")
        dh, dn = _rms_bwd(s["h0"], norm_mix[l:l + 1], dhn, dh, name=f"d_rms_mix{l}")
        small["norm_mix"].append(dn)

        grads = dict(w_in=_join_dw_in(dw_z, dw_xbc, dw_dt, dw_s), w_out=dw_out.reshape(4, -1, D),
                     w_q=dw_q.reshape(4, -1, D_XA), w_k=dw_k.reshape(4, -1, D_XA), w_v=dw_v.reshape(4, -1, D_XA),
                     w_o=dw_o, w_gate=dw_gate, w_up=dw_up, w_down=dw_down.reshape(4, -1, D))
        g_list = [grads[n] for n in names]
        recv_sib = _swap_halves(g_list, name=f"swap_halves{l}")
        parts = [_add_halves(g, rb, c_arr, name=f"add_halves_{n}{l}") for n, g, rb in zip(names, g_list, recv_sib)]
        recv_chips = _send_to_owners(parts, name=f"send_to_owners{l}")
        halves = [_sum_chips(p, rc, q_arr, name=f"sum_chips_{n}{l}") for n, p, rc in zip(names, parts, recv_chips)]
        full = _join_halves(halves, name=f"join_halves{l}")
        for n, g in zip(names, full):
            carried[n] = _adamw_layer(l, big[n], big_m[n], big_v[n], g, carried[n], name=f"adamw_{n}{l}")

    grad_x = dh[None]

    _, d_mem_norm = _rms_bwd(mem[0], mem_norm.reshape(1, D), dmemn, jnp.zeros_like(dmemn), name="d_mem_norm")
    stack = lambda n: jnp.concatenate(small[n][::-1], axis=0) if small[n][0].ndim == 2 and small[n][0].shape[0] == 1 \
        else jnp.stack(small[n][::-1], axis=0)
    small_names = ["norm_mix", "ssd_conv_w", "ssd_conv_b", "dt_bias", "a_log", "d_skip", "ssd_norm", "sc_conv_w", "sc_norm",
                   "mem_norm", "norm_xa", "norm_ffn", "norm_final"]
    local_g = dict(mem_norm=d_mem_norm.reshape(D), norm_final=d_norm_final.reshape(D))
    for n in small:
        local_g[n] = stack(n)
    shapes = [local_g[n].shape for n in small_names]
    red = dict(zip(small_names, _unpack(_allreduce_small(_pack([local_g[n] for n in small_names]), name="allreduce_small"), shapes)))
    red["ssd_conv_w"] = lax.dynamic_slice(red["ssd_conv_w"], (0, 0, qme * (D_XBC // 4)), ssd_conv_w.shape)
    red["sc_conv_w"] = lax.dynamic_slice(red["sc_conv_w"], (0, 0, qme * (D_SC // 4)), sc_conv_w.shape)
    sw = dict(norm_mix=norm_mix, ssd_conv_w=ssd_conv_w, ssd_conv_b=ssd_conv_b, dt_bias=dt_bias, a_log=a_log, d_skip=d_skip,
              ssd_norm=ssd_norm, sc_conv_w=sc_conv_w, sc_norm=sc_norm, mem_norm=mem_norm, norm_xa=norm_xa, norm_ffn=norm_ffn,
              norm_final=norm_final)
    sm = dict(norm_mix=m_norm_mix, ssd_conv_w=m_ssd_conv_w, ssd_conv_b=m_ssd_conv_b, dt_bias=m_dt_bias, a_log=m_a_log,
              d_skip=m_d_skip, ssd_norm=m_ssd_norm, sc_conv_w=m_sc_conv_w, sc_norm=m_sc_norm, mem_norm=m_mem_norm,
              norm_xa=m_norm_xa, norm_ffn=m_norm_ffn, norm_final=m_norm_final)
    sv = dict(norm_mix=v_norm_mix, ssd_conv_w=v_ssd_conv_w, ssd_conv_b=v_ssd_conv_b, dt_bias=v_dt_bias, a_log=v_a_log,
              d_skip=v_d_skip, ssd_norm=v_ssd_norm, sc_conv_w=v_sc_conv_w, sc_norm=v_sc_norm, mem_norm=v_mem_norm,
              norm_xa=v_norm_xa, norm_ffn=v_norm_ffn, norm_final=v_norm_final)
    shard_shapes = [sw[n].shape for n in small_names]
    pk = lambda d: _pack([d[n] for n in small_names])
    sd, snm, snv = _adamw_flat(pk(sw), pk(red), pk(sm), pk(sv), name="adamw_small")
    s_delta = dict(zip(small_names, _unpack(sd, shard_shapes)))
    s_newm = dict(zip(small_names, _unpack(snm, shard_shapes)))
    s_newv = dict(zip(small_names, _unpack(snv, shard_shapes)))

    order = ["norm_mix", "w_in", "ssd_conv_w", "ssd_conv_b", "dt_bias", "a_log", "d_skip", "ssd_norm", "sc_conv_w", "sc_norm",
             "w_out", "mem_norm", "norm_xa", "w_q", "w_k", "w_v", "w_o", "norm_ffn", "w_gate", "w_up", "w_down", "norm_final"]

    def pick(kind):
        out = []
        for n in order:
            if n in carried:
                out.append(carried[n][kind])
            else:
                out.append([red, s_delta, s_newm, s_newv][kind][n])
        return out

    return (loss, grad_x, *pick(0), *pick(1), *pick(2), *pick(3))
```

```python
import functools

import jax
import jax.numpy as jnp
from jax import lax
from jax.experimental import pallas as pl
from jax.experimental.pallas import tpu as pltpu

F32 = jnp.float32
BF16 = jnp.bfloat16
MESH = pl.DeviceIdType.MESH

D = 2048
D_SSD = 2048
N_GROUPS = 4
GROUP_W = 768
D_XBC = 3072
N_STATE = 128
HEADS = 32
PAIRS_PER_GROUP = 4
CHUNK = 256
DT_PAD = 128
D_SC = 2048
SC_GROUP = 128
XA_HEADS = 4
XA_HD = 128
D_XA = 512
D_FF = 5632
EPS = 1e-5
HALO = 16
TN_TK = 2048
VMEM_LIMIT = 56 * 1024 * 1024

ADAM_LR, ADAM_B1, ADAM_B2, ADAM_EPS, ADAM_WD, ADAM_STEP = 0.001, 0.9, 0.999, 1e-08, 0.01, 10

NT = (((1,), (1,)), ((), ()))
TN = (((0,), (0,)), ((), ()))
NN = (((1,), (0,)), ((), ()))


def _pcall(body, **kw):
    return pl.pallas_call(body, **kw)


def _params(**kw):
    return pltpu.CompilerParams(vmem_limit_bytes=VMEM_LIMIT, **kw)


def _sds(shape, dtype):
    return jax.ShapeDtypeStruct(shape, dtype)


def _sig(x):
    return 1.0 / (1.0 + jnp.exp(-x))


def _dot(a, b, dims=NN):
    return lax.dot_general(a, b, dims, preferred_element_type=F32)


def _mm(a, b, *, mode, grid, a_spec, b_spec, o_spec, o_tile, out_sds, add=None, name):
    gk = grid[2]
    dims = {"nn": NN, "nt": NT, "tn": TN}[mode]
    has_add = add is not None

    def body(*refs):
        a_ref, b_ref = refs[0], refs[1]
        add_ref = refs[2] if has_add else None
        o_ref = refs[2 + has_add]
        p = _dot(a_ref[...].astype(BF16), b_ref[...].astype(BF16), dims)

        def finish(acc):
            if has_add:
                acc = acc + add_ref[...]
            o_ref[...] = acc.astype(o_ref.dtype)

        if gk == 1:
            finish(p)
        else:
            acc_ref = refs[3 + has_add]
            k = pl.program_id(2)

            @pl.when(k == 0)
            def _():
                acc_ref[...] = p

            @pl.when(k > 0)
            def _():
                acc_ref[...] += p

            @pl.when(k == gk - 1)
            def _():
                finish(acc_ref[...])

    in_specs = [a_spec, b_spec] + ([o_spec] if has_add else [])
    args = (a, b) + ((add,) if has_add else ())
    scratch = [pltpu.VMEM(o_tile, F32)] if gk > 1 else []
    return _pcall(body, grid=grid, in_specs=in_specs, out_specs=o_spec, out_shape=out_sds, scratch_shapes=scratch,
                  compiler_params=_params(dimension_semantics=("parallel", "parallel", "arbitrary")), name=name)(*args)


def _tile(n, pref):
    t = min(n, pref)
    assert n % t == 0, (n, pref)
    return t


def _mm_nn(a, w, *, tn, tk=None, tm=1024, out_dtype=BF16, add=None, name):
    m, k = a.shape
    n = w.shape[1]
    tm = _tile(m, tm)
    tk = k if tk is None else tk
    grid = (m // tm, n // tn, k // tk)
    return _mm(a, w, mode="nn", grid=grid,
               a_spec=pl.BlockSpec((tm, tk), lambda i, j, kk: (i, kk)),
               b_spec=pl.BlockSpec((tk, tn), lambda i, j, kk: (kk, j)),
               o_spec=pl.BlockSpec((tm, tn), lambda i, j, kk: (i, j)), o_tile=(tm, tn),
               out_sds=_sds((m, n), out_dtype), add=add, name=name)


def _mm_nn_sm(a, w4, *, out_dtype=BF16, add=None, name):
    m, k = a.shape
    n = w4.shape[2]
    tm = _tile(m, 1024)
    return _mm(a, w4, mode="nn", grid=(m // tm, 4, 1),
               a_spec=pl.BlockSpec((tm, k), lambda i, j, kk: (i, 0)),
               b_spec=pl.BlockSpec((None, k, n), lambda i, j, kk: (j, 0, 0)),
               o_spec=pl.BlockSpec((tm, n), lambda i, j, kk: (i, j)), o_tile=(tm, n),
               out_sds=_sds((m, 4 * n), out_dtype), add=add, name=name)


def _mm_nt(a, w, *, tn, tk=None, out_dtype=BF16, add=None, name):
    m, k = a.shape
    n = w.shape[0]
    tm = _tile(m, 1024)
    tk = k if tk is None else tk
    grid = (m // tm, n // tn, k // tk)
    return _mm(a, w, mode="nt", grid=grid,
               a_spec=pl.BlockSpec((tm, tk), lambda i, j, kk: (i, kk)),
               b_spec=pl.BlockSpec((tn, tk), lambda i, j, kk: (j, kk)),
               o_spec=pl.BlockSpec((tm, tn), lambda i, j, kk: (i, j)), o_tile=(tm, tn),
               out_sds=_sds((m, n), out_dtype), add=add, name=name)


def _mm_nt_sm(a, w4, *, tn, out_dtype=BF16, add=None, name):
    m = a.shape[0]
    _, k, n = w4.shape
    tm = _tile(m, 1024)
    tn = _tile(k, tn)
    return _mm(a, w4, mode="nt", grid=(m // tm, k // tn, 4),
               a_spec=pl.BlockSpec((tm, n), lambda i, j, kk: (i, kk)),
               b_spec=pl.BlockSpec((None, tn, n), lambda i, j, kk: (kk, j, 0)),
               o_spec=pl.BlockSpec((tm, tn), lambda i, j, kk: (i, j)), o_tile=(tm, tn),
               out_sds=_sds((m, k), out_dtype), add=add, name=name)


def _mm_tn(a, g, *, tm, tn, out_dtype=BF16, name):
    t, m = a.shape
    n = g.shape[1]
    tk = _tile(t, TN_TK)
    return _mm(a, g, mode="tn", grid=(m // tm, n // tn, t // tk),
               a_spec=pl.BlockSpec((tk, tm), lambda i, j, kk: (kk, i)),
               b_spec=pl.BlockSpec((tk, tn), lambda i, j, kk: (kk, j)),
               o_spec=pl.BlockSpec((tm, tn), lambda i, j, kk: (i, j)), o_tile=(tm, tn),
               out_sds=_sds((m, n), out_dtype), name=name)


def _mm_tn_sm(a, g, *, tm, out_dtype=BF16, name):
    t, m = a.shape
    n = g.shape[1] // 4
    tk = _tile(t, TN_TK)
    return _mm(a, g, mode="tn", grid=(m // tm, 4, t // tk),
               a_spec=pl.BlockSpec((tk, tm), lambda i, j, kk: (kk, i)),
               b_spec=pl.BlockSpec((tk, n), lambda i, j, kk: (kk, j)),
               o_spec=pl.BlockSpec((None, tm, n), lambda i, j, kk: (j, i, 0)), o_tile=(tm, n),
               out_sds=_sds((4, m, n), out_dtype), name=name)


def _rms_fwd(h, g, *, name):
    t, d = h.shape
    tr = _tile(t, 512)

    def body(h_ref, g_ref, o_ref):
        x = h_ref[...]
        r = lax.rsqrt(jnp.mean(x * x, axis=-1, keepdims=True) + EPS)
        o_ref[...] = (x * r * g_ref[...]).astype(o_ref.dtype)

    return _pcall(body, grid=(t // tr,),
                  in_specs=[pl.BlockSpec((tr, d), lambda i: (i, 0)), pl.BlockSpec((1, d), lambda i: (0, 0))],
                  out_specs=pl.BlockSpec((tr, d), lambda i: (i, 0)), out_shape=_sds((t, d), BF16),
                  compiler_params=_params(dimension_semantics=("parallel",)), name=name)(h, g)


def _rms_bwd(h, g, dy, dres, *, name):
    t, d = h.shape
    tr = _tile(t, 256)

    def body(h_ref, g_ref, dy_ref, dres_ref, dh_ref, dhb_ref, dg_ref):
        i = pl.program_id(0)
        x = h_ref[...]
        r = lax.rsqrt(jnp.mean(x * x, axis=-1, keepdims=True) + EPS)
        xh = x * r
        dyv = dy_ref[...].astype(F32)
        dxh = dyv * g_ref[...]
        dh = dres_ref[...] + r * (dxh - xh * jnp.mean(dxh * xh, axis=-1, keepdims=True))
        dh_ref[...] = dh
        dhb_ref[...] = dh.astype(BF16)
        part = jnp.sum(dyv * xh, axis=0, keepdims=True)

        @pl.when(i == 0)
        def _():
            dg_ref[...] = part

        @pl.when(i > 0)
        def _():
            dg_ref[...] += part

    row = pl.BlockSpec((tr, d), lambda i: (i, 0))
    vec = pl.BlockSpec((1, d), lambda i: (0, 0))
    return _pcall(body, grid=(t // tr,), in_specs=[row, vec, row, row], out_specs=[row, row, vec],
                  out_shape=[_sds((t, d), F32), _sds((t, d), BF16), _sds((1, d), F32)],
                  compiler_params=_params(dimension_semantics=("arbitrary",)), name=name)(h, g, dy, dres)


def _final(h, g, tgt, *, name):
    t, d = h.shape
    tr = _tile(t, 256)

    def body(h_ref, g_ref, t_ref, loss_ref, dh_ref, dhb_ref, dg_ref):
        i = pl.program_id(0)
        x = h_ref[...]
        gv = g_ref[...]
        r = lax.rsqrt(jnp.mean(x * x, axis=-1, keepdims=True) + EPS)
        xh = x * r
        e = xh * gv - t_ref[...]
        lpart = jnp.zeros((1, 128), F32) + 0.5 * jnp.sum(jnp.mean(e * e, axis=-1, keepdims=True))
        dyv = e * (1.0 / d)
        dxh = dyv * gv
        dh = r * (dxh - xh * jnp.mean(dxh * xh, axis=-1, keepdims=True))
        dh_ref[...] = dh
        dhb_ref[...] = dh.astype(BF16)
        part = jnp.sum(dyv * xh, axis=0, keepdims=True)

        @pl.when(i == 0)
        def _():
            dg_ref[...] = part
            loss_ref[...] = lpart

        @pl.when(i > 0)
        def _():
            dg_ref[...] += part
            loss_ref[...] += lpart

    row = pl.BlockSpec((tr, d), lambda i: (i, 0))
    vec = pl.BlockSpec((1, d), lambda i: (0, 0))
    return _pcall(body, grid=(t // tr,), in_specs=[row, vec, row],
                  out_specs=[pl.BlockSpec((1, 128), lambda i: (0, 0)), row, row, vec],
                  out_shape=[_sds((1, 128), F32), _sds((t, d), F32), _sds((t, d), BF16), _sds((1, d), F32)],
                  compiler_params=_params(dimension_semantics=("arbitrary",)), name=name)(h, g, tgt)


def _conv_taps(ext, w, ntap, rows):
    n = ext.shape[0]
    acc = w[ntap - 1:ntap, :] * ext[HALO:HALO + rows]
    for k in range(1, ntap):
        acc = acc + w[ntap - 1 - k:ntap - k, :] * pltpu.roll(ext, k, axis=0)[HALO:HALO + rows]
    del n
    return acc


def _conv_fwd(xbc, w, b, *, name):
    t, c = xbc.shape
    rows = CHUNK
    cw = GROUP_W
    hb = rows // HALO

    def body(cur_ref, prev_ref, w_ref, b_ref, o_ref):
        i = pl.program_id(1)
        cur = cur_ref[...].astype(F32)
        prev = jnp.where(i > 0, prev_ref[...].astype(F32), 0.0)
        ext = jnp.concatenate([prev, cur], axis=0)
        pre = _conv_taps(ext, w_ref[...], 4, rows) + b_ref[...]
        o_ref[...] = (pre * _sig(pre)).astype(o_ref.dtype)

    return _pcall(body, grid=(c // cw, t // rows),
                  in_specs=[pl.BlockSpec((rows, cw), lambda j, i: (i, j)),
                            pl.BlockSpec((HALO, cw), lambda j, i: (jnp.maximum(i * hb - 1, 0), j)),
                            pl.BlockSpec((4, cw), lambda j, i: (0, j)),
                            pl.BlockSpec((1, cw), lambda j, i: (0, j))],
                  out_specs=pl.BlockSpec((rows, cw), lambda j, i: (i, j)), out_shape=_sds((t, c), BF16),
                  compiler_params=_params(dimension_semantics=("parallel", "parallel")), name=name)(xbc, xbc, w, b)


def _conv_bwd(dxc, xbc, w, b, *, name):
    t, c = xbc.shape
    rows = CHUNK
    cw = GROUP_W
    hb = rows // HALO
    nblk = t // rows
    nhalo = t // HALO

    def body(d_ref, dn_ref, cur_ref, prev_ref, next_ref, w_ref, b_ref, dx_ref, dw_ref, db_ref):
        i = pl.program_id(1)
        last = i == nblk - 1
        wv = w_ref[...]
        xe = jnp.concatenate([jnp.where(i > 0, prev_ref[...].astype(F32), 0.0), cur_ref[...].astype(F32),
                              jnp.where(last, 0.0, next_ref[...].astype(F32))], axis=0)
        n = rows + 2 * HALO
        sh = [xe] + [pltpu.roll(xe, k, axis=0) for k in range(1, 4)]
        pre = wv[3:4, :] * sh[0] + wv[2:3, :] * sh[1] + wv[1:2, :] * sh[2] + wv[0:1, :] * sh[3] + b_ref[...]
        de = jnp.concatenate([jnp.zeros((HALO, cw), F32), d_ref[...].astype(F32),
                              jnp.where(last, 0.0, dn_ref[...].astype(F32))], axis=0)
        s = _sig(pre)
        dpre = de * (s * (1.0 + pre * (1.0 - s)))
        dx = wv[3:4, :] * dpre
        for m in range(1, 4):
            dx = dx + wv[3 - m:4 - m, :] * pltpu.roll(dpre, n - m, axis=0)
        dx_ref[...] = dx[HALO:HALO + rows].astype(dx_ref.dtype)
        dcur = dpre[HALO:HALO + rows]
        dwv = jnp.concatenate([jnp.sum(dcur * sh[3 - j][HALO:HALO + rows], axis=0, keepdims=True) for j in range(4)], axis=0)
        dbv = jnp.sum(dcur, axis=0, keepdims=True)

        @pl.when(i == 0)
        def _():
            dw_ref[...] = dwv
            db_ref[...] = dbv

        @pl.when(i > 0)
        def _():
            dw_ref[...] += dwv
            db_ref[...] += dbv

    cur = pl.BlockSpec((rows, cw), lambda j, i: (i, j))
    prev = pl.BlockSpec((HALO, cw), lambda j, i: (jnp.maximum(i * hb - 1, 0), j))
    nxt = pl.BlockSpec((HALO, cw), lambda j, i: (jnp.minimum((i + 1) * hb, nhalo - 1), j))
    return _pcall(body, grid=(c // cw, nblk),
                  in_specs=[cur, nxt, cur, prev, nxt, pl.BlockSpec((4, cw), lambda j, i: (0, j)),
                            pl.BlockSpec((1, cw), lambda j, i: (0, j))],
                  out_specs=[cur, pl.BlockSpec((4, cw), lambda j, i: (0, j)), pl.BlockSpec((1, cw), lambda j, i: (0, j))],
                  out_shape=[_sds((t, c), BF16), _sds((4, c), F32), _sds((1, c), F32)],
                  compiler_params=_params(dimension_semantics=("parallel", "arbitrary")), name=name)(dxc, dxc, xbc, xbc, xbc, w, b)


def _neg_exp_alog(alog):
    lane = lax.broadcasted_iota(jnp.int32, alog.shape, 1)
    return jnp.where(lane < HEADS, -jnp.exp(alog), 0.0)


def _dt_prep(dtr, bias, alog, *, name):
    t = dtr.shape[0]
    rows = CHUNK

    def body(r_ref, b_ref, a_ref, dt_ref, dtg_ref, acsg_ref, acst_ref):
        raw = r_ref[...] + b_ref[...]
        dt = jnp.maximum(raw, 0.0) + jnp.log(1.0 + jnp.exp(-jnp.abs(raw)))
        a = _neg_exp_alog(a_ref[...])
        adt = dt * a
        ri = lax.broadcasted_iota(jnp.int32, (rows, rows), 0)
        ci = lax.broadcasted_iota(jnp.int32, (rows, rows), 1)
        tri = (ri >= ci).astype(F32)
        acs = jnp.dot(tri, adt, precision=lax.Precision.HIGHEST, preferred_element_type=F32)
        dt_ref[...] = dt
        acst_ref[...] = acs.T
        for g in range(N_GROUPS):
            sh = (128 - 8 * g) % 128
            dtg_ref[g] = dt if sh == 0 else pltpu.roll(dt, sh, axis=1)
            acsg_ref[g] = acs if sh == 0 else pltpu.roll(acs, sh, axis=1)

    row = pl.BlockSpec((rows, DT_PAD), lambda i: (i, 0))
    vec = pl.BlockSpec((1, DT_PAD), lambda i: (0, 0))
    grp = pl.BlockSpec((N_GROUPS, rows, DT_PAD), lambda i: (0, i, 0))
    return _pcall(body, grid=(t // rows,), in_specs=[row, vec, vec],
                  out_specs=[row, grp, grp, pl.BlockSpec((DT_PAD, rows), lambda i: (0, i))],
                  out_shape=[_sds((t, DT_PAD), F32), _sds((N_GROUPS, t, DT_PAD), F32), _sds((N_GROUPS, t, DT_PAD), F32),
                             _sds((DT_PAD, t), F32)],
                  compiler_params=_params(dimension_semantics=("parallel",)), name=name)(dtr, bias, alog)


def _dt_bwd(ddtg, dacg, dart, dt, dtr, bias, alog, *, name):
    t = dtr.shape[0]
    rows = CHUNK

    def body(ddtg_ref, dacg_ref, dart_ref, dt_ref, r_ref, b_ref, a_ref, dr_ref, db_ref, da_ref):
        i = pl.program_id(0)
        lane = lax.broadcasted_iota(jnp.int32, (rows, DT_PAD), 1)
        ddt = jnp.zeros((rows, DT_PAD), F32)
        dacs = jnp.concatenate([dart_ref[...], jnp.zeros((DT_PAD - HEADS, rows), F32)], axis=0).T
        for g in range(N_GROUPS):
            sel = (lane >= 8 * g) & (lane < 8 * g + 8)
            dd = ddtg_ref[g]
            da = dacg_ref[g]
            if g:
                dd = pltpu.roll(dd, 8 * g, axis=1)
                da = pltpu.roll(da, 8 * g, axis=1)
            ddt = ddt + jnp.where(sel, dd, 0.0)
            dacs = dacs + jnp.where(sel, da, 0.0)
        ri = lax.broadcasted_iota(jnp.int32, (rows, rows), 0)
        ci = lax.broadcasted_iota(jnp.int32, (rows, rows), 1)
        triu = (ci >= ri).astype(F32)
        rev = jnp.dot(triu, dacs, precision=lax.Precision.HIGHEST, preferred_element_type=F32)
        a = _neg_exp_alog(a_ref[...])
        dtv = dt_ref[...]
        raw = r_ref[...] + b_ref[...]
        draw = (ddt + a * rev) * _sig(raw)
        dr_ref[...] = draw
        dbv = jnp.sum(draw, axis=0, keepdims=True)
        dav = jnp.sum(dtv * rev, axis=0, keepdims=True) * a

        @pl.when(i == 0)
        def _():
            db_ref[...] = dbv
            da_ref[...] = dav

        @pl.when(i > 0)
        def _():
            db_ref[...] += dbv
            da_ref[...] += dav

    row = pl.BlockSpec((rows, DT_PAD), lambda i: (i, 0))
    vec = pl.BlockSpec((1, DT_PAD), lambda i: (0, 0))
    grp = pl.BlockSpec((N_GROUPS, rows, DT_PAD), lambda i: (0, i, 0))
    return _pcall(body, grid=(t // rows,),
                  in_specs=[grp, grp, pl.BlockSpec((HEADS, rows), lambda i: (0, i)), row, row, vec, vec],
                  out_specs=[row, vec, vec], out_shape=[_sds((t, DT_PAD), F32), _sds((1, DT_PAD), F32), _sds((1, DT_PAD), F32)],
                  compiler_params=_params(dimension_semantics=("arbitrary",)), name=name)(ddtg, dacg, dart, dt, dtr, bias, alog)


def _pair_cols(col_ref_val, p, lo):
    return jnp.where(lo, col_ref_val[:, 2 * p:2 * p + 1], col_ref_val[:, 2 * p + 1:2 * p + 2])


def _ssd_fwd(xc, dtg, acsg, acst, z, dskip, nw, *, name):
    t = xc.shape[0]
    L = CHUNK
    nc = t // L

    def body(xc_ref, dtg_ref, acsg_ref, acst_ref, z_ref, dsk_ref, nw_ref, y_ref, st_ref, mix_ref, s_ref):
        c = pl.program_id(1)

        @pl.when(c == 0)
        def _():
            s_ref[...] = jnp.zeros_like(s_ref)

        blk = xc_ref[...]
        bm = blk[:, 512:640]
        cm = blk[:, 640:768]
        cb = _dot(cm, bm, NT)
        dtv = dtg_ref[...]
        acs = acsg_ref[...]
        acst_v = acst_ref[...]
        ri = lax.broadcasted_iota(jnp.int32, (L, L), 0)
        ci = lax.broadcasted_iota(jnp.int32, (L, L), 1)
        causal = ri >= ci
        lo = lax.broadcasted_iota(jnp.int32, (1, 128), 1) < 64
        lo_rows = lax.broadcasted_iota(jnp.int32, (128, 1), 0) < 64
        dskv = dsk_ref[...]
        ys = []
        for p in range(PAIRS_PER_GROUP):
            xp = blk[:, 128 * p:128 * p + 128].astype(F32)
            dt_p = _pair_cols(dtv, p, lo)
            a_p = _pair_cols(acs, p, lo)
            alast = acs[L - 1:L, :]
            al_p = _pair_cols(alast, p, lo)
            xdt = xp * dt_p
            xdt_b = xdt.astype(BF16)
            yd = []
            for hh in range(2):
                j = 2 * p + hh
                seg = acs[:, j:j + 1] - acst_v[j:j + 1, :]
                lam = jnp.exp(jnp.where(causal, seg, -1e30))
                w = (cb * lam).astype(BF16)
                yd.append(_dot(w, xdt_b))
            y = jnp.where(lo, yd[0], yd[1])
            sp = s_ref[p]
            st_ref[p] = sp
            y = y + _dot(cm, sp.astype(BF16), NT) * jnp.exp(a_p)
            dsc = jnp.exp(al_p - a_p)
            snew = _dot((xdt * dsc).astype(BF16), bm, TN)
            al_rows = jnp.where(lo_rows, alast[:, 2 * p:2 * p + 1], alast[:, 2 * p + 1:2 * p + 2])
            s_ref[p] = sp * jnp.exp(al_rows) + snew
            ys.append(y + xp * dskv[:, 128 * p:128 * p + 128])
        yfull = jnp.concatenate(ys, axis=1)
        y_ref[...] = yfull.astype(y_ref.dtype)
        zz = z_ref[...].astype(F32)
        yg = yfull * (zz * _sig(zz))
        r = lax.rsqrt(jnp.mean(yg * yg, axis=-1, keepdims=True) + EPS)
        mix_ref[...] = (yg * r * nw_ref[...]).astype(mix_ref.dtype)

    grp = pl.BlockSpec((None, L, DT_PAD), lambda g, c: (g, c, 0))
    return _pcall(body, grid=(N_GROUPS, nc),
                  in_specs=[pl.BlockSpec((L, GROUP_W), lambda g, c: (c, g)), grp, grp,
                            pl.BlockSpec((8, L), lambda g, c: (g, c)),
                            pl.BlockSpec((L, 512), lambda g, c: (c, g)),
                            pl.BlockSpec((1, 512), lambda g, c: (0, g)), pl.BlockSpec((1, 512), lambda g, c: (0, g))],
                  out_specs=[pl.BlockSpec((L, 512), lambda g, c: (c, g)),
                             pl.BlockSpec((None, PAIRS_PER_GROUP, 128, N_STATE), lambda g, c: (c, g, 0, 0)),
                             pl.BlockSpec((L, 512), lambda g, c: (c, g))],
                  out_shape=[_sds((t, D_SSD), BF16), _sds((nc, N_GROUPS * PAIRS_PER_GROUP, 128, N_STATE), F32),
                             _sds((t, D_SSD + D_SC), BF16)],
                  scratch_shapes=[pltpu.VMEM((PAIRS_PER_GROUP, 128, N_STATE), F32)],
                  compiler_params=_params(dimension_semantics=("parallel", "arbitrary")), name=name)(
                      xc, dtg, acsg, acst, z, dskip, nw)


def _ssd_bwd(xc, dtg, acsg, acst, z, y, states, dmix, dskip, nw, *, name):
    t = xc.shape[0]
    L = CHUNK
    nc = t // L

    def body(xc_ref, dtg_ref, acsg_ref, acst_ref, z_ref, y_ref, st_ref, dm_ref, dsk_ref, nw_ref,
             dxc_ref, dz_ref, ddt_ref, dac_ref, dar_ref, ddsk_ref, dnw_ref, ds_ref):
        c = pl.program_id(1)

        @pl.when(c == 0)
        def _():
            ds_ref[...] = jnp.zeros_like(ds_ref)
            ddsk_ref[...] = jnp.zeros_like(ddsk_ref)
            dnw_ref[...] = jnp.zeros_like(dnw_ref)

        blk = xc_ref[...]
        xs = blk[:, :512].astype(F32)
        bm = blk[:, 512:640]
        cm = blk[:, 640:768]
        bmf = bm.astype(F32)
        yv = y_ref[...].astype(F32)
        zz = z_ref[...].astype(F32)
        nwv = nw_ref[...]
        dout = dm_ref[...].astype(F32)
        sz = _sig(zz)
        silu = zz * sz
        yg = yv * silu
        r = lax.rsqrt(jnp.mean(yg * yg, axis=-1, keepdims=True) + EPS)
        xh = yg * r
        dnw_ref[...] += jnp.sum(dout * xh, axis=0, keepdims=True)
        dyn = dout * nwv
        dyg = r * (dyn - xh * jnp.mean(dyn * xh, axis=-1, keepdims=True))
        dy = dyg * silu
        dz_ref[...] = (dyg * yv * (sz * (1.0 + zz * (1.0 - sz)))).astype(dz_ref.dtype)
        ddsk_ref[...] += jnp.sum(dy * xs, axis=0, keepdims=True)

        cb = _dot(cm, bm, NT)
        dtv = dtg_ref[...]
        acs = acsg_ref[...]
        acst_v = acst_ref[...]
        alast = acs[L - 1:L, :]
        ri = lax.broadcasted_iota(jnp.int32, (L, L), 0)
        ci = lax.broadcasted_iota(jnp.int32, (L, L), 1)
        causal = ri >= ci
        lane = lax.broadcasted_iota(jnp.int32, (1, 128), 1)
        lo = lane < 64
        lo_rows = lax.broadcasted_iota(jnp.int32, (128, 1), 0) < 64
        lane_l = lax.broadcasted_iota(jnp.int32, (L, DT_PAD), 1)
        row_l = lax.broadcasted_iota(jnp.int32, (L, 1), 0)
        sub8 = lax.broadcasted_iota(jnp.int32, (8, L), 0)
        dskv = dsk_ref[...]
        dm_acc = jnp.zeros((L, L), F32)
        db_acc = jnp.zeros((L, N_STATE), F32)
        dc_acc = jnp.zeros((L, N_STATE), F32)
        ddt_out = jnp.zeros((L, DT_PAD), F32)
        dac_out = jnp.zeros((L, DT_PAD), F32)
        dar_out = jnp.zeros((8, L), F32)
        dxs = []
        for p in range(PAIRS_PER_GROUP):
            xp = xs[:, 128 * p:128 * p + 128]
            dyp = dy[:, 128 * p:128 * p + 128]
            dt_p = _pair_cols(dtv, p, lo)
            a_p = _pair_cols(acs, p, lo)
            al_p = _pair_cols(alast, p, lo)
            xdt = xp * dt_p
            xdt_b = xdt.astype(BF16)
            ea_p = jnp.exp(a_p)
            dsc_p = jnp.exp(al_p - a_p)
            sp = st_ref[p]
            sp_b = sp.astype(BF16)
            dsp = ds_ref[p]
            dsp_b = dsp.astype(BF16)
            cs = _dot(cm, sp_b, NT)
            dye_b = (dyp * ea_p).astype(BF16)
            dc_acc = dc_acc + _dot(dye_b, sp_b)
            ds_prev = _dot(dye_b, cm, TN)
            bds = _dot(bm, dsp_b, NT)
            al_rows = jnp.where(lo_rows, alast[:, 2 * p:2 * p + 1], alast[:, 2 * p + 1:2 * p + 2])
            ds_prev = ds_prev + jnp.exp(al_rows) * dsp
            prod_off = dyp * cs
            prod_st = dsp * sp
            dxdt_h = []
            for hh in range(2):
                j = 2 * p + hh
                hm = lo if hh == 0 else jnp.logical_not(lo)
                hm_rows = lo_rows if hh == 0 else jnp.logical_not(lo_rows)
                a_col = acs[:, j:j + 1]
                seg = a_col - acst_v[j:j + 1, :]
                lam = jnp.exp(jnp.where(causal, seg, -1e30))
                wf = cb * lam
                w = wf.astype(BF16)
                dy_h = jnp.where(hm, dyp, 0.0).astype(BF16)
                dw = _dot(dy_h, xdt_b, NT)
                dxdt_h.append(_dot(w, dyp.astype(BF16), TN))
                dm_acc = dm_acc + dw * lam
                e = dw * wf
                dac = jnp.sum(e, axis=1, keepdims=True)
                dar = -jnp.sum(e, axis=0, keepdims=True)
                ea_col = jnp.exp(a_col)
                dac = dac + ea_col * jnp.sum(jnp.where(hm, prod_off, 0.0), axis=1, keepdims=True)
                al_h = alast[:, j:j + 1]
                dal = jnp.exp(al_h) * jnp.sum(jnp.sum(jnp.where(hm_rows, prod_st, 0.0), axis=1, keepdims=True), axis=0, keepdims=True)
                xds_h = _dot(jnp.where(hm, xdt, 0.0).astype(BF16), dsp_b)
                dsc_col = jnp.exp(al_h - a_col)
                db_acc = db_acc + dsc_col * xds_h
                tt = jnp.sum(xds_h * bmf, axis=1, keepdims=True) * dsc_col
                dal = dal + jnp.sum(tt, axis=0, keepdims=True)
                dac = dac - tt + jnp.where(row_l == L - 1, dal, 0.0)
                dac_out = jnp.where(lane_l == j, dac, dac_out)
                dar_out = jnp.where(sub8 == j, dar, dar_out)
            dxdt = jnp.where(lo, dxdt_h[0], dxdt_h[1]) + dsc_p * bds
            dxs.append(dxdt * dt_p + dyp * dskv[:, 128 * p:128 * p + 128])
            prod_dt = dxdt * xp
            for hh in range(2):
                j = 2 * p + hh
                hm = lo if hh == 0 else jnp.logical_not(lo)
                ddt_col = jnp.sum(jnp.where(hm, prod_dt, 0.0), axis=1, keepdims=True)
                ddt_out = jnp.where(lane_l == j, ddt_col, ddt_out)
            ds_ref[p] = ds_prev
        dm_b = dm_acc.astype(BF16)
        dc_acc = dc_acc + _dot(dm_b, bm)
        db_acc = db_acc + _dot(dm_b, cm, TN)
        dxc_ref[...] = jnp.concatenate(dxs + [db_acc, dc_acc], axis=1).astype(dxc_ref.dtype)
        ddt_ref[...] = ddt_out
        dac_ref[...] = dac_out
        dar_ref[...] = dar_out

    rc = lambda g, c: (nc - 1 - c, g)
    grp = pl.BlockSpec((None, L, DT_PAD), lambda g, c: (g, nc - 1 - c, 0))
    vec = pl.BlockSpec((1, 512), lambda g, c: (0, g))
    return _pcall(body, grid=(N_GROUPS, nc),
                  in_specs=[pl.BlockSpec((L, GROUP_W), rc), grp, grp,
                            pl.BlockSpec((8, L), lambda g, c: (g, nc - 1 - c)),
                            pl.BlockSpec((L, 512), rc), pl.BlockSpec((L, 512), rc),
                            pl.BlockSpec((None, PAIRS_PER_GROUP, 128, N_STATE), lambda g, c: (nc - 1 - c, g, 0, 0)),
                            pl.BlockSpec((L, 512), rc), vec, vec],
                  out_specs=[pl.BlockSpec((L, GROUP_W), rc), pl.BlockSpec((L, 512), rc), grp, grp,
                             pl.BlockSpec((8, L), lambda g, c: (g, nc - 1 - c)), vec, vec],
                  out_shape=[_sds((t, D_XBC), BF16), _sds((t, D_SSD), BF16), _sds((N_GROUPS, t, DT_PAD), F32),
                             _sds((N_GROUPS, t, DT_PAD), F32), _sds((HEADS, t), F32), _sds((1, D_SSD), F32),
                             _sds((1, D_SSD), F32)],
                  scratch_shapes=[pltpu.VMEM((PAIRS_PER_GROUP, 128, N_STATE), F32)],
                  compiler_params=_params(dimension_semantics=("parallel", "arbitrary")), name=name)(
                      xc, dtg, acsg, acst, z, y, states, dmix, dskip, nw)


SC_CW = 1024


def _group_rstd(v):
    outs = []
    for q in range(v.shape[1] // SC_GROUP):
        vq = v[:, SC_GROUP * q:SC_GROUP * (q + 1)]
        outs.append(jnp.broadcast_to(lax.rsqrt(jnp.mean(vq * vq, axis=-1, keepdims=True) + EPS), vq.shape))
    return jnp.concatenate(outs, axis=1)


def _group_mean(v):
    outs = []
    for q in range(v.shape[1] // SC_GROUP):
        vq = v[:, SC_GROUP * q:SC_GROUP * (q + 1)]
        outs.append(jnp.broadcast_to(jnp.mean(vq, axis=-1, keepdims=True), vq.shape))
    return jnp.concatenate(outs, axis=1)


def _sc_fwd(ps, w, nw, mix, *, name):
    t = ps.shape[1]
    rows = CHUNK
    hb = rows // HALO
    cw = SC_CW
    off = D_SSD // cw

    def body(cur_ref, prev_ref, w_ref, nw_ref, mix_in_ref, o_ref):
        del mix_in_ref
        i = pl.program_id(1)
        u = cur_ref[0].astype(F32)
        gb = cur_ref[1].astype(F32)
        gc = cur_ref[2].astype(F32)
        cu_prev = jnp.where(i > 0, prev_ref[2].astype(F32) * prev_ref[0].astype(F32), 0.0)
        ext = jnp.concatenate([cu_prev, gc * u], axis=0)
        v = gb * _conv_taps(ext, w_ref[...], 3, rows)
        o_ref[...] = (v * _group_rstd(v) * nw_ref[...]).astype(o_ref.dtype)

    return _pcall(body, grid=(D_SC // cw, t // rows),
                  in_specs=[pl.BlockSpec((3, rows, cw), lambda j, i: (0, i, j)),
                            pl.BlockSpec((3, HALO, cw), lambda j, i: (0, jnp.maximum(i * hb - 1, 0), j)),
                            pl.BlockSpec((3, cw), lambda j, i: (0, j)), pl.BlockSpec((1, cw), lambda j, i: (0, j)),
                            pl.BlockSpec(memory_space=pl.ANY)],
                  out_specs=pl.BlockSpec((rows, cw), lambda j, i: (i, off + j)),
                  out_shape=_sds(mix.shape, mix.dtype), input_output_aliases={4: 0},
                  compiler_params=_params(dimension_semantics=("parallel", "parallel")), name=name)(ps, ps, w, nw, mix)


def _sc_bwd(ps, w, nw, dmix, *, name):
    t = ps.shape[1]
    rows = CHUNK
    hb = rows // HALO
    cw = SC_CW
    off = D_SSD // cw
    nblk = t // rows
    nhalo = t // HALO
    n = rows + 2 * HALO

    def body(cur_ref, prev_ref, next_ref, w_ref, nw_ref, d_ref, dn_ref, dps_ref, dw_ref, dnw_ref):
        i = pl.program_id(1)
        first = i == 0
        last = i == nblk - 1

        def ext(k):
            return jnp.concatenate([jnp.where(first, 0.0, prev_ref[k].astype(F32)), cur_ref[k].astype(F32),
                                    jnp.where(last, 0.0, next_ref[k].astype(F32))], axis=0)

        ue, gbe, gce = ext(0), ext(1), ext(2)
        wv = w_ref[...]
        nwv = nw_ref[...]
        cue = gce * ue
        cu1 = pltpu.roll(cue, 1, axis=0)
        cu2 = pltpu.roll(cue, 2, axis=0)
        conv = wv[2:3, :] * cue + wv[1:2, :] * cu1 + wv[0:1, :] * cu2
        ve = gbe * conv
        doe = jnp.concatenate([jnp.zeros((HALO, cw), F32), d_ref[...].astype(F32),
                               jnp.where(last, 0.0, dn_ref[...].astype(F32))], axis=0)
        r = _group_rstd(ve)
        xh = ve * r
        dvn = doe * nwv
        dv = r * (dvn - xh * _group_mean(dvn * xh))
        dconv = dv * gbe
        dcu = wv[2:3, :] * dconv + wv[1:2, :] * pltpu.roll(dconv, n - 1, axis=0) + wv[0:1, :] * pltpu.roll(dconv, n - 2, axis=0)
        sl = slice(HALO, HALO + rows)
        dps_ref[0] = (dcu * gce)[sl].astype(dps_ref.dtype)
        dps_ref[1] = (dv * conv)[sl].astype(dps_ref.dtype)
        dps_ref[2] = (dcu * ue)[sl].astype(dps_ref.dtype)
        dc = dconv[sl]
        dwv = jnp.concatenate([jnp.sum(dc * cu2[sl], axis=0, keepdims=True), jnp.sum(dc * cu1[sl], axis=0, keepdims=True),
                               jnp.sum(dc * cue[sl], axis=0, keepdims=True)], axis=0)
        dnv = jnp.sum((doe * xh)[sl], axis=0, keepdims=True)

        @pl.when(first)
        def _():
            dw_ref[...] = dwv
            dnw_ref[...] = dnv

        @pl.when(i > 0)
        def _():
            dw_ref[...] += dwv
            dnw_ref[...] += dnv

    cur = pl.BlockSpec((3, rows, cw), lambda j, i: (0, i, j))
    prev = pl.BlockSpec((3, HALO, cw), lambda j, i: (0, jnp.maximum(i * hb - 1, 0), j))
    nxt = pl.BlockSpec((3, HALO, cw), lambda j, i: (0, jnp.minimum((i + 1) * hb, nhalo - 1), j))
    return _pcall(body, grid=(D_SC // cw, nblk),
                  in_specs=[cur, prev, nxt, pl.BlockSpec((3, cw), lambda j, i: (0, j)), pl.BlockSpec((1, cw), lambda j, i: (0, j)),
                            pl.BlockSpec((rows, cw), lambda j, i: (i, off + j)),
                            pl.BlockSpec((HALO, cw), lambda j, i: (jnp.minimum((i + 1) * hb, nhalo - 1), off + j))],
                  out_specs=[cur, pl.BlockSpec((3, cw), lambda j, i: (0, j)), pl.BlockSpec((1, cw), lambda j, i: (0, j))],
                  out_shape=[_sds(ps.shape, BF16), _sds((3, D_SC), F32), _sds((1, D_SC), F32)],
                  compiler_params=_params(dimension_semantics=("parallel", "arbitrary")), name=name)(ps, ps, ps, w, nw, dmix, dmix)


XA_SCALE = XA_HD ** -0.5


def _softmax(s):
    m = jnp.max(s, axis=-1, keepdims=True)
    e = jnp.exp(s - m)
    return e / jnp.sum(e, axis=-1, keepdims=True)


def _xa_fwd(q, k, v, *, name):
    t = q.shape[0]
    nm = k.shape[0]
    tq = _tile(t, 512)

    def body(q_ref, k_ref, v_ref, o_ref):
        outs = []
        for h in range(XA_HEADS):
            sl = slice(XA_HD * h, XA_HD * (h + 1))
            s = _dot(q_ref[:, sl], k_ref[:, sl], NT) * XA_SCALE
            outs.append(_dot(_softmax(s).astype(BF16), v_ref[:, sl]))
        o_ref[...] = jnp.concatenate(outs, axis=1).astype(o_ref.dtype)

    row = pl.BlockSpec((tq, D_XA), lambda i: (i, 0))
    kv = pl.BlockSpec((nm, D_XA), lambda i: (0, 0))
    return _pcall(body, grid=(t // tq,), in_specs=[row, kv, kv], out_specs=row, out_shape=_sds((t, D_XA), BF16),
                  compiler_params=_params(dimension_semantics=("parallel",)), name=name)(q, k, v)


def _xa_bwd(q, k, v, do, *, name):
    t = q.shape[0]
    nm = k.shape[0]
    tq = _tile(t, 512)

    def body(q_ref, k_ref, v_ref, do_ref, dq_ref, dk_ref, dv_ref):
        i = pl.program_id(0)
        dqs, dks, dvs = [], [], []
        for h in range(XA_HEADS):
            sl = slice(XA_HD * h, XA_HD * (h + 1))
            qh, kh, vh, doh = q_ref[:, sl], k_ref[:, sl], v_ref[:, sl], do_ref[:, sl]
            p = _softmax(_dot(qh, kh, NT) * XA_SCALE)
            dvs.append(_dot(p.astype(BF16), doh, TN))
            dp = _dot(doh, vh, NT)
            ds = (p * (dp - jnp.sum(dp * p, axis=-1, keepdims=True)) * XA_SCALE).astype(BF16)
            dqs.append(_dot(ds, kh))
            dks.append(_dot(ds, qh, TN))
        dq_ref[...] = jnp.concatenate(dqs, axis=1).astype(dq_ref.dtype)
        dkv = jnp.concatenate(dks, axis=1)
        dvv = jnp.concatenate(dvs, axis=1)

        @pl.when(i == 0)
        def _():
            dk_ref[...] = dkv
            dv_ref[...] = dvv

        @pl.when(i > 0)
        def _():
            dk_ref[...] += dkv
            dv_ref[...] += dvv

    row = pl.BlockSpec((tq, D_XA), lambda i: (i, 0))
    kv = pl.BlockSpec((nm, D_XA), lambda i: (0, 0))
    return _pcall(body, grid=(t // tq,), in_specs=[row, kv, kv, row], out_specs=[row, kv, kv],
                  out_shape=[_sds((t, D_XA), BF16), _sds((nm, D_XA), F32), _sds((nm, D_XA), F32)],
                  compiler_params=_params(dimension_semantics=("arbitrary",)), name=name)(q, k, v, do)


FF_CW = 1408


def _swiglu_fwd(g, u, *, name):
    t, f = g.shape
    tr = _tile(t, 512)

    def body(g_ref, u_ref, o_ref):
        gv = g_ref[...].astype(F32)
        o_ref[...] = (gv * _sig(gv) * u_ref[...].astype(F32)).astype(o_ref.dtype)

    blk = pl.BlockSpec((tr, FF_CW), lambda i, j: (i, j))
    return _pcall(body, grid=(t // tr, f // FF_CW), in_specs=[blk, blk], out_specs=blk, out_shape=_sds((t, f), BF16),
                  compiler_params=_params(dimension_semantics=("parallel", "parallel")), name=name)(g, u)


def _swiglu_bwd(g, u, dact, *, name):
    t, f = g.shape
    tr = _tile(t, 512)

    def body(g_ref, u_ref, d_ref, dg_ref, du_ref):
        gv = g_ref[...].astype(F32)
        uv = u_ref[...].astype(F32)
        dv = d_ref[...].astype(F32)
        s = _sig(gv)
        dg_ref[...] = (dv * uv * (s * (1.0 + gv * (1.0 - s)))).astype(dg_ref.dtype)
        du_ref[...] = (dv * gv * s).astype(du_ref.dtype)

    blk = pl.BlockSpec((tr, FF_CW), lambda i, j: (i, j))
    return _pcall(body, grid=(t // tr, f // FF_CW), in_specs=[blk, blk, blk], out_specs=[blk, blk],
                  out_shape=[_sds((t, f), BF16), _sds((t, f), BF16)],
                  compiler_params=_params(dimension_semantics=("parallel", "parallel")), name=name)(g, u, dact)


def _row_tile(n):
    for cand in (128, 64, 32, 16):
        if n % cand == 0:
            return cand
    raise ValueError(n)


def _add_halves(g4, rb, c_arr, *, name):
    _, r, cdim = g4.shape
    hr = r // 2
    rt = _row_tile(hr)
    nb = hr // rt

    def body(c_ref, g_ref, rb_ref, o_ref):
        del c_ref
        o_ref[...] = (g_ref[...].astype(F32) + rb_ref[...].astype(F32)).astype(o_ref.dtype)

    gs = pltpu.PrefetchScalarGridSpec(
        num_scalar_prefetch=1, grid=(4, nb),
        in_specs=[pl.BlockSpec((None, rt, cdim), lambda q, i, c: (q, c[0] * nb + i, 0)),
                  pl.BlockSpec((None, rt, cdim), lambda q, i, c: (q, i, 0))],
        out_specs=pl.BlockSpec((None, rt, cdim), lambda q, i, c: (q, i, 0)))
    return _pcall(body, grid_spec=gs, out_shape=_sds((4, hr, cdim), BF16),
                  compiler_params=_params(dimension_semantics=("parallel", "parallel")), name=name)(c_arr, g4, rb)


def _sum_chips(p4, rc, q_arr, c_arr, *, name):
    _, hr, cdim = p4.shape
    rt = _row_tile(hr)
    nb = hr // rt

    def body(q_ref, c_ref, p_ref, rc_ref, o_ref):
        del q_ref, c_ref
        o_ref[...] = ((p_ref[...].astype(F32) + rc_ref[0].astype(F32)) + rc_ref[1].astype(F32)) + rc_ref[2].astype(F32)

    gs = pltpu.PrefetchScalarGridSpec(
        num_scalar_prefetch=2, grid=(nb,),
        in_specs=[pl.BlockSpec((None, rt, cdim), lambda i, q, c: (q[0], i, 0)),
                  pl.BlockSpec((3, rt, cdim), lambda i, q, c: (0, i, 0))],
        out_specs=pl.BlockSpec((rt, cdim), lambda i, q, c: (c[0] * nb + i, 0)))
    return _pcall(body, grid_spec=gs, out_shape=_sds((2 * hr, cdim), F32),
                  compiler_params=_params(dimension_semantics=("parallel",)), name=name)(q_arr, c_arr, p4, rc)


def _adam_math(w, g, m, v):
    m = ADAM_B1 * m + (1.0 - ADAM_B1) * g
    v = ADAM_B2 * v + (1.0 - ADAM_B2) * (g * g)
    m_hat = m / (1.0 - ADAM_B1 ** ADAM_STEP)
    v_hat = v / (1.0 - ADAM_B2 ** ADAM_STEP)
    delta = -ADAM_LR * (m_hat / (jnp.sqrt(v_hat) + ADAM_EPS) + ADAM_WD * w)
    return delta, m, v


def _adamw_layer(layer, w, m, v, g, prev, *, name):
    depth, r, cdim = w.shape
    rt = _row_tile(r)
    n_prev = 0 if prev is None else 4

    def body(*refs):
        w_ref, m_ref, v_ref, g_ref = refs[:4]
        go_ref, d_ref, mo_ref, vo_ref = refs[4 + n_prev:]
        gv = g_ref[...]
        delta, mn, vn = _adam_math(w_ref[...], gv, m_ref[...], v_ref[...])
        go_ref[...] = gv
        d_ref[...] = delta
        mo_ref[...] = mn
        vo_ref[...] = vn

    st = pl.BlockSpec((None, rt, cdim), lambda i: (layer, i, 0))
    in_specs = [st, st, st, pl.BlockSpec((rt, cdim), lambda i: (i, 0))] + [pl.BlockSpec(memory_space=pl.ANY)] * n_prev
    args = (w, m, v, g) + (tuple(prev) if prev is not None else ())
    return _pcall(body, grid=(r // rt,), in_specs=in_specs, out_specs=[st] * 4,
                  out_shape=[_sds((depth, r, cdim), F32)] * 4,
                  input_output_aliases={4 + i: i for i in range(n_prev)},
                  compiler_params=_params(dimension_semantics=("parallel",)), name=name)(*args)


def _adamw_flat(w, g, m, v, *, name):
    def body(w_ref, g_ref, m_ref, v_ref, d_ref, mo_ref, vo_ref):
        delta, mn, vn = _adam_math(w_ref[...], g_ref[...], m_ref[...], v_ref[...])
        d_ref[...] = delta
        mo_ref[...] = mn
        vo_ref[...] = vn

    return _pcall(body, out_shape=[_sds(w.shape, F32)] * 3, compiler_params=_params(), name=name)(w, g, m, v)


def _place():
    x, y, c = lax.axis_index("x"), lax.axis_index("y"), lax.axis_index("c")
    chips = [(1 - x, y), (x, 1 - y), (1 - x, 1 - y)]
    return x, y, c, chips


def _cast_place(w, q_arr, *, name):
    r, cdim = w.shape
    rt = _row_tile(r)

    def body(q_ref, w_ref, o_ref):
        del q_ref
        o_ref[...] = w_ref[...].astype(o_ref.dtype)

    gs = pltpu.PrefetchScalarGridSpec(
        num_scalar_prefetch=1, grid=(r // rt,),
        in_specs=[pl.BlockSpec((rt, cdim), lambda i, q: (i, 0))],
        out_specs=pl.BlockSpec((None, rt, cdim), lambda i, q: (q[0], i, 0)))
    return _pcall(body, grid_spec=gs, out_shape=_sds((4, r, cdim), BF16),
                  compiler_params=_params(dimension_semantics=("parallel",)), name=name)(q_arr, w)


def _gather_weights(bufs, *, name):
    n = len(bufs)

    def body(*refs):
        outs = refs[n:2 * n]
        send1, recv1, send2, recv2 = refs[2 * n:]
        x, y, c, chips = _place()
        qme = 2 * x + y
        sib = (x, y, 1 - c)

        def half(t, which):
            hr = outs[t].shape[1] // 2
            return pl.ds(which * hr, hr)

        first = []
        for t in range(n):
            mine = outs[t].at[qme, half(t, c)]
            for k, (px, py) in enumerate(chips):
                cp = pltpu.make_async_remote_copy(src_ref=mine, dst_ref=mine, send_sem=send1.at[t, k], recv_sem=recv1.at[t, k],
                                                  device_id=(px, py, c), device_id_type=MESH)
                cp.start()
                first.append(cp)
        second = []
        for t in range(n):
            for k, (px, py) in enumerate(chips):
                landed = outs[t].at[2 * px + py, half(t, c)]
                pltpu.make_async_remote_copy(src_ref=landed, dst_ref=landed, send_sem=send1.at[t, k], recv_sem=recv1.at[t, k],
                                             device_id=(px, py, c), device_id_type=MESH).wait_recv()
                cp = pltpu.make_async_remote_copy(src_ref=landed, dst_ref=landed, send_sem=send2.at[t, k],
                                                  recv_sem=recv2.at[t, k], device_id=sib, device_id_type=MESH)
                cp.start()
                second.append(cp)
        for t in range(n):
            for k, (px, py) in enumerate(chips):
                other = outs[t].at[2 * px + py, half(t, 1 - c)]
                pltpu.make_async_remote_copy(src_ref=other, dst_ref=other, send_sem=send2.at[t, k],
                                             recv_sem=recv2.at[t, k], device_id=sib, device_id_type=MESH).wait_recv()
        for cp in first + second:
            cp.wait_send()

    anyspec = pl.BlockSpec(memory_space=pl.ANY)
    return _pcall(body, in_specs=[anyspec] * n, out_specs=[anyspec] * n,
                  out_shape=[_sds(b.shape, b.dtype) for b in bufs], input_output_aliases={t: t for t in range(n)},
                  scratch_shapes=[pltpu.SemaphoreType.DMA((n, 3))] * 4,
                  compiler_params=pltpu.CompilerParams(has_side_effects=True), name=name)(*bufs)


def _swap_halves(grads, *, name):
    n = len(grads)

    def body(*refs):
        ins, outs = refs[:n], refs[n:2 * n]
        send, recv = refs[2 * n:]
        x, y, c, _ = _place()
        cps = []
        for t in range(n):
            hr = ins[t].shape[1] // 2
            cp = pltpu.make_async_remote_copy(src_ref=ins[t].at[:, pl.ds((1 - c) * hr, hr), :], dst_ref=outs[t],
                                              send_sem=send.at[t], recv_sem=recv.at[t],
                                              device_id=(x, y, 1 - c), device_id_type=MESH)
            cp.start()
            cps.append(cp)
        for cp in cps:
            cp.wait()

    anyspec = pl.BlockSpec(memory_space=pl.ANY)
    return _pcall(body, in_specs=[anyspec] * n, out_specs=[anyspec] * n,
                  out_shape=[_sds((4, g.shape[1] // 2, g.shape[2]), g.dtype) for g in grads],
                  scratch_shapes=[pltpu.SemaphoreType.DMA((n,))] * 2,
                  compiler_params=pltpu.CompilerParams(has_side_effects=True), name=name)(*grads)


def _send_to_owners(parts, *, name):
    n = len(parts)

    def body(*refs):
        ins, outs = refs[:n], refs[n:2 * n]
        send, recv = refs[2 * n:]
        x, y, c, chips = _place()
        cps = []
        for t in range(n):
            for k, (px, py) in enumerate(chips):
                cp = pltpu.make_async_remote_copy(src_ref=ins[t].at[2 * px + py], dst_ref=outs[t].at[k],
                                                  send_sem=send.at[t, k], recv_sem=recv.at[t, k],
                                                  device_id=(px, py, c), device_id_type=MESH)
                cp.start()
                cps.append(cp)
        for cp in cps:
            cp.wait()

    anyspec = pl.BlockSpec(memory_space=pl.ANY)
    return _pcall(body, in_specs=[anyspec] * n, out_specs=[anyspec] * n,
                  out_shape=[_sds((3,) + p.shape[1:], p.dtype) for p in parts],
                  scratch_shapes=[pltpu.SemaphoreType.DMA((n, 3))] * 2,
                  compiler_params=pltpu.CompilerParams(has_side_effects=True), name=name)(*parts)


def _join_halves(bufs, *, name):
    n = len(bufs)

    def body(*refs):
        outs = refs[n:2 * n]
        send, recv = refs[2 * n:]
        x, y, c, _ = _place()
        cps = []
        for t in range(n):
            hr = outs[t].shape[0] // 2
            mine = outs[t].at[pl.ds(c * hr, hr)]
            cp = pltpu.make_async_remote_copy(src_ref=mine, dst_ref=mine, send_sem=send.at[t], recv_sem=recv.at[t],
                                              device_id=(x, y, 1 - c), device_id_type=MESH)
            cp.start()
            cps.append(cp)
        for t in range(n):
            hr = outs[t].shape[0] // 2
            theirs = outs[t].at[pl.ds((1 - c) * hr, hr)]
            pltpu.make_async_remote_copy(src_ref=theirs, dst_ref=theirs, send_sem=send.at[t], recv_sem=recv.at[t],
                                         device_id=(x, y, 1 - c), device_id_type=MESH).wait_recv()
        for cp in cps:
            cp.wait_send()

    anyspec = pl.BlockSpec(memory_space=pl.ANY)
    return _pcall(body, in_specs=[anyspec] * n, out_specs=[anyspec] * n,
                  out_shape=[_sds(b.shape, b.dtype) for b in bufs], input_output_aliases={t: t for t in range(n)},
                  scratch_shapes=[pltpu.SemaphoreType.DMA((n,))] * 2,
                  compiler_params=pltpu.CompilerParams(has_side_effects=True), name=name)(*bufs)


def _allreduce_small(buf, *, name):
    rows = buf.shape[0]
    rels = [(dx, dy, dc) for dx in (0, 1) for dy in (0, 1) for dc in (0, 1)][1:]

    def body(in_ref, out_ref, gbuf, send, recv):
        x, y, c = lax.axis_index("x"), lax.axis_index("y"), lax.axis_index("c")
        me = 4 * x + 2 * y + c
        gbuf[me] = in_ref[...]
        cps = []
        for k, (dx, dy, dc) in enumerate(rels):
            peer = (x + dx - 2 * x * dx, y + dy - 2 * y * dy, c + dc - 2 * c * dc)
            cp = pltpu.make_async_remote_copy(src_ref=in_ref, dst_ref=gbuf.at[me], send_sem=send.at[k], recv_sem=recv.at[k],
                                              device_id=peer, device_id_type=MESH)
            cp.start()
            cps.append(cp)
        for k, (dx, dy, dc) in enumerate(rels):
            px, py, pc = x + dx - 2 * x * dx, y + dy - 2 * y * dy, c + dc - 2 * c * dc
            pltpu.make_async_remote_copy(src_ref=in_ref, dst_ref=gbuf.at[4 * px + 2 * py + pc], send_sem=send.at[k],
                                         recv_sem=recv.at[k], device_id=(px, py, pc), device_id_type=MESH).wait_recv()
        for cp in cps:
            cp.wait_send()
        acc = gbuf[0]
        for d in range(1, 8):
            acc = acc + gbuf[d]
        out_ref[...] = acc

    vm = pl.BlockSpec(memory_space=pltpu.VMEM)
    return _pcall(body, in_specs=[vm], out_specs=vm, out_shape=_sds(buf.shape, F32),
                  scratch_shapes=[pltpu.VMEM((8, rows, 128), F32), pltpu.SemaphoreType.DMA((7,)), pltpu.SemaphoreType.DMA((7,))],
                  compiler_params=_params(has_side_effects=True), name=name)(buf)


def _pack(arrs):
    flat = jnp.concatenate([a.reshape(-1).astype(F32) for a in arrs])
    n = flat.shape[0]
    rows = -(-n // 1024) * 8
    return jnp.pad(flat, (0, rows * 128 - n)).reshape(rows, 128)


def _unpack(buf, shapes):
    flat = buf.reshape(-1)
    out, o = [], 0
    for s in shapes:
        n = 1
        for d in s:
            n *= d
        out.append(flat[o:o + n].reshape(s))
        o += n
    return out


def _perm_xbc(a):
    parts = []
    for g in range(N_GROUPS):
        parts += [a[..., 512 * g:512 * (g + 1)], a[..., 2048 + 128 * g:2048 + 128 * (g + 1)],
                  a[..., 2560 + 128 * g:2560 + 128 * (g + 1)]]
    return jnp.concatenate(parts, axis=-1)


def _unperm_xbc(a):
    xs = [a[..., GROUP_W * g:GROUP_W * g + 512] for g in range(N_GROUPS)]
    bs = [a[..., GROUP_W * g + 512:GROUP_W * g + 640] for g in range(N_GROUPS)]
    cs = [a[..., GROUP_W * g + 640:GROUP_W * (g + 1)] for g in range(N_GROUPS)]
    return jnp.concatenate(xs + bs + cs, axis=-1)


def _split_w_in(w4):
    k = w4.shape[1]
    nat = jnp.transpose(w4, (1, 0, 2)).reshape(k, -1)
    w_z = nat[:, :2048]
    w_xbc = _perm_xbc(nat[:, 2048:5120])
    w_dt = jnp.pad(nat[:, 5120:5152], ((0, 0), (0, DT_PAD - HEADS)))
    w_s = nat[:, 5152:]
    return w_z, w_xbc, w_dt, w_s


def _join_dw_in(dw_z, dw_xbc, dw_dt, dw_s):
    k = dw_z.shape[0]
    nat = jnp.concatenate([dw_z, _unperm_xbc(dw_xbc), dw_dt[:, :HEADS], dw_s], axis=1)
    return jnp.transpose(nat.reshape(k, 4, -1), (1, 0, 2))


def kernel(x, mem, norm_mix, w_in, ssd_conv_w, ssd_conv_b, dt_bias, a_log, d_skip, ssd_norm, sc_conv_w, sc_norm, w_out, mem_norm, norm_xa, w_q, w_k, w_v, w_o, norm_ffn, w_gate, w_up, w_down, norm_final, loss_target, m_norm_mix, m_w_in, m_ssd_conv_w, m_ssd_conv_b, m_dt_bias, m_a_log, m_d_skip, m_ssd_norm, m_sc_conv_w, m_sc_norm, m_w_out, m_mem_norm, m_norm_xa, m_w_q, m_w_k, m_w_v, m_w_o, m_norm_ffn, m_w_gate, m_w_up, m_w_down, m_norm_final, v_norm_mix, v_w_in, v_ssd_conv_w, v_ssd_conv_b, v_dt_bias, v_a_log, v_d_skip, v_ssd_norm, v_sc_conv_w, v_sc_norm, v_w_out, v_mem_norm, v_norm_xa, v_w_q, v_w_k, v_w_v, v_w_o, v_norm_ffn, v_w_gate, v_w_up, v_w_down, v_norm_final):
    depth = w_in.shape[0]
    ix, iy, ic = lax.axis_index("x"), lax.axis_index("y"), lax.axis_index("c")
    qme = 2 * ix + iy
    c_arr = jnp.reshape(ic, (1,)).astype(jnp.int32)
    q_arr = jnp.reshape(qme, (1,)).astype(jnp.int32)
    h = x[0]
    tgt = loss_target[0]

    big = dict(w_in=w_in, w_out=w_out, w_q=w_q, w_k=w_k, w_v=w_v, w_o=w_o, w_gate=w_gate, w_up=w_up, w_down=w_down)
    big_m = dict(w_in=m_w_in, w_out=m_w_out, w_q=m_w_q, w_k=m_w_k, w_v=m_w_v, w_o=m_w_o, w_gate=m_w_gate, w_up=m_w_up, w_down=m_w_down)
    big_v = dict(w_in=v_w_in, w_out=v_w_out, w_q=v_w_q, w_k=v_w_k, w_v=v_w_v, w_o=v_w_o, w_gate=v_w_gate, w_up=v_w_up, w_down=v_w_down)
    names = list(big)

    conv_full = jnp.zeros((depth, 4, D_XBC), F32)
    conv_full = lax.dynamic_update_slice(conv_full, jnp.where(ic == 0, ssd_conv_w, 0.0), (0, 0, qme * (D_XBC // 4)))
    sc_full = jnp.zeros((depth, 3, D_SC), F32)
    sc_full = lax.dynamic_update_slice(sc_full, jnp.where(ic == 0, sc_conv_w, 0.0), (0, 0, qme * (D_SC // 4)))
    conv_full, sc_full = _unpack(_allreduce_small(_pack([conv_full, sc_full]), name="gather_conv_w"),
                                 [conv_full.shape, sc_full.shape])
    conv_p = _perm_xbc(conv_full)
    convb_p = _perm_xbc(ssd_conv_b)

    pad_h = lambda a: jnp.pad(a, ((0, 0), (0, DT_PAD - HEADS)))
    dt_bias_p, a_log_p = pad_h(dt_bias), pad_h(a_log)
    dskip_ch = jnp.repeat(d_skip, D_SSD // HEADS, axis=1)

    gathered = []
    for l in range(depth):
        outs = _gather_weights([_cast_place(big[n][l], q_arr, name=f"cast_{n}{l}") for n in names], name=f"gather_w{l}")
        gathered.append(dict(zip(names, outs)))

    memn = _rms_fwd(mem[0], mem_norm.reshape(1, D), name="memn")

    saved = []
    for l in range(depth):
        gw = gathered[l]
        w_z, w_xbc, w_dt, w_s = _split_w_in(gw["w_in"])
        wo2 = gw["w_out"].reshape(-1, D)
        wq2, wk2, wv2 = (gw[n].reshape(D, D_XA) for n in ("w_q", "w_k", "w_v"))
        wd2 = gw["w_down"].reshape(D_FF, D)
        s = dict(h0=h, w_z=w_z, w_xbc=w_xbc, w_dt=w_dt, w_s=w_s, wo2=wo2, wq2=wq2, wk2=wk2, wv2=wv2, wd2=wd2)
        hn1 = _rms_fwd(h, norm_mix[l:l + 1], name=f"rms_mix{l}")
        pz = _mm_nn(hn1, w_z, tn=1024, name=f"proj_z{l}")
        pxbc = _mm_nn(hn1, w_xbc, tn=1024, name=f"proj_xbc{l}")
        dtr = _mm_nn(hn1, w_dt, tn=DT_PAD, out_dtype=F32, name=f"proj_dt{l}")
        t = h.shape[0]
        tm = _tile(t, 1024)
        ps = _mm(hn1, w_s, mode="nn", grid=(t // tm, 6, 1),
                 a_spec=pl.BlockSpec((tm, D), lambda i, j, kk: (i, 0)),
                 b_spec=pl.BlockSpec((D, 1024), lambda i, j, kk: (0, j)),
                 o_spec=pl.BlockSpec((None, tm, 1024), lambda i, j, kk: (j // 2, i, j % 2)), o_tile=(tm, 1024),
                 out_sds=_sds((3, t, D_SC), BF16), name=f"proj_s{l}")
        xc = _conv_fwd(pxbc, conv_p[l], convb_p[l:l + 1], name=f"conv{l}")
        dt, dtg, acsg, acst = _dt_prep(dtr, dt_bias_p[l:l + 1], a_log_p[l:l + 1], name=f"dt_prep{l}")
        y, states, mix = _ssd_fwd(xc, dtg, acsg, acst, pz, dskip_ch[l:l + 1], ssd_norm[l:l + 1], name=f"ssd{l}")
        mix = _sc_fwd(ps, sc_full[l], sc_norm[l:l + 1], mix, name=f"sc{l}")
        h1 = _mm_nn(mix, wo2, tn=1024, tm=512, out_dtype=F32, add=h, name=f"out_proj{l}")
        hn2 = _rms_fwd(h1, norm_xa[l:l + 1], name=f"rms_xa{l}")
        q = _mm_nn(hn2, wq2, tn=D_XA, name=f"q{l}")
        k = _mm_nn(memn, wk2, tn=D_XA, name=f"k{l}")
        v = _mm_nn(memn, wv2, tn=D_XA, name=f"v{l}")
        o = _xa_fwd(q, k, v, name=f"xa{l}")
        h2 = _mm_nn_sm(o, gw["w_o"], out_dtype=F32, add=h1, name=f"xa_out{l}")
        hn3 = _rms_fwd(h2, norm_ffn[l:l + 1], name=f"rms_ffn{l}")
        fg = _mm_nn_sm(hn3, gw["w_gate"], name=f"ff_gate{l}")
        fu = _mm_nn_sm(hn3, gw["w_up"], name=f"ff_up{l}")
        act = _swiglu_fwd(fg, fu, name=f"swiglu{l}")
        h3 = _mm_nn(act, wd2, tn=1024, tm=512, out_dtype=F32, add=h2, name=f"ff_down{l}")
        s.update(hn1=hn1, pz=pz, pxbc=pxbc, dtr=dtr, ps=ps, xc=xc, dt=dt, dtg=dtg, acsg=acsg, acst=acst, y=y,
                 states=states, mix=mix, h1=h1, hn2=hn2, q=q, k=k, v=v, o=o, h2=h2, hn3=hn3, fg=fg, fu=fu, act=act)
        saved.append(s)
        h = h3

    loss_vec, dh, dhb, d_norm_final = _final(h, norm_final.reshape(1, D), tgt, name="final")
    loss = lax.psum(loss_vec[0, 0], ("x", "y", "c"))

    small = dict(norm_mix=[], ssd_conv_w=[], ssd_conv_b=[], dt_bias=[], a_log=[], d_skip=[], ssd_norm=[], sc_conv_w=[],
                 sc_norm=[], norm_xa=[], norm_ffn=[])
    dmemn = None
    carried = {n: None for n in names}
    for l in reversed(range(depth)):
        s, gw = saved[l], gathered[l]
        t = dh.shape[0]
        tm = _tile(t, 1024)
        dact = _mm_nt(dhb, s["wd2"], tn=FF_CW, name=f"d_act{l}")
        dw_down = _mm_tn(s["act"], dhb, tm=FF_CW, tn=1024, name=f"dw_down{l}")
        dg, du = _swiglu_bwd(s["fg"], s["fu"], dact, name=f"d_swiglu{l}")
        dw_gate = _mm_tn_sm(s["hn3"], dg, tm=1024, name=f"dw_gate{l}")
        dw_up = _mm_tn_sm(s["hn3"], du, tm=1024, name=f"dw_up{l}")
        dhn = _mm_nt_sm(dg, gw["w_gate"], tn=1024, out_dtype=F32, name=f"d_hn3a{l}")
        dhn = _mm_nt_sm(du, gw["w_up"], tn=1024, out_dtype=F32, add=dhn, name=f"d_hn3b{l}")
        dh, dhb, dn = _rms_bwd(s["h2"], norm_ffn[l:l + 1], dhn, dh, name=f"d_rms_ffn{l}")
        small["norm_ffn"].append(dn)
        do = _mm_nt_sm(dhb, gw["w_o"], tn=D_XA, name=f"d_o{l}")
        dw_o = _mm_tn_sm(s["o"], dhb, tm=D_XA, name=f"dw_o{l}")
        dq, dk, dv = _xa_bwd(s["q"], s["k"], s["v"], do, name=f"d_xa{l}")
        dw_q = _mm_tn(s["hn2"], dq, tm=1024, tn=D_XA, name=f"dw_q{l}")
        dw_k = _mm_tn(memn, dk, tm=1024, tn=D_XA, name=f"dw_k{l}")
        dw_v = _mm_tn(memn, dv, tm=1024, tn=D_XA, name=f"dw_v{l}")
        dhn = _mm_nt(dq, s["wq2"], tn=1024, out_dtype=F32, name=f"d_hn2{l}")
        dmemn = _mm_nt(dk, s["wk2"], tn=1024, out_dtype=F32, add=dmemn, name=f"d_memn_k{l}")
        dmemn = _mm_nt(dv, s["wv2"], tn=1024, out_dtype=F32, add=dmemn, name=f"d_memn_v{l}")
        dh, dhb, dn = _rms_bwd(s["h1"], norm_xa[l:l + 1], dhn, dh, name=f"d_rms_xa{l}")
        small["norm_xa"].append(dn)
        dmix = _mm_nt(dhb, s["wo2"], tn=1024, name=f"d_mix{l}")
        dw_out = _mm_tn(s["mix"], dhb, tm=1024, tn=1024, name=f"dw_out{l}")
        dps, d_scw, d_scn = _sc_bwd(s["ps"], sc_full[l], sc_norm[l:l + 1], dmix, name=f"d_sc{l}")
        dxc, dz, ddtg, dacg, dart, ddsk, d_ssdn = _ssd_bwd(s["xc"], s["dtg"], s["acsg"], s["acst"], s["pz"], s["y"], s["states"],
                                                           dmix, dskip_ch[l:l + 1], ssd_norm[l:l + 1], name=f"d_ssd{l}")
        dxbc, d_cw, d_cb = _conv_bwd(dxc, s["pxbc"], conv_p[l], convb_p[l:l + 1], name=f"d_conv{l}")
        ddtr, d_dtb, d_alog = _dt_bwd(ddtg, dacg, dart, s["dt"], s["dtr"], dt_bias_p[l:l + 1], a_log_p[l:l + 1], name=f"d_dt{l}")
        small["sc_conv_w"].append(d_scw)
        small["sc_norm"].append(d_scn)
        small["ssd_norm"].append(d_ssdn)
        small["d_skip"].append(jnp.sum(ddsk.reshape(HEADS, D_SSD // HEADS), axis=1).reshape(1, HEADS))
        small["ssd_conv_w"].append(_unperm_xbc(d_cw))
        small["ssd_conv_b"].append(_unperm_xbc(d_cb))
        small["dt_bias"].append(d_dtb[:, :HEADS])
        small["a_log"].append(d_alog[:, :HEADS])
        hn1 = s["hn1"]
        dw_z = _mm_tn(hn1, dz, tm=1024, tn=1024, name=f"dw_z{l}")
        dw_xbc = _mm_tn(hn1, dxbc, tm=1024, tn=1024, name=f"dw_xbc{l}")
        dw_dt = _mm_tn(hn1, ddtr, tm=1024, tn=DT_PAD, name=f"dw_dt{l}")
        tk = _tile(t, TN_TK)
        dw_s = _mm(hn1, dps, mode="tn", grid=(2, 6, t // tk),
                   a_spec=pl.BlockSpec((tk, 1024), lambda i, j, kk: (kk, i)),
                   b_spec=pl.BlockSpec((None, tk, 1024), lambda i, j, kk: (j // 2, kk, j % 2)),
                   o_spec=pl.BlockSpec((1024, 1024), lambda i, j, kk: (i, j)), o_tile=(1024, 1024),
                   out_sds=_sds((D, 3 * D_SC), BF16), name=f"dw_s{l}")
        dhn = _mm_nt(dz, s["w_z"], tn=1024, out_dtype=F32, name=f"d_hn1z{l}")
        dhn = _mm_nt(dxbc, s["w_xbc"], tn=1024, out_dtype=F32, add=dhn, name=f"d_hn1x{l}")
        dhn = _mm_nt(ddtr, s["w_dt"], tn=1024, out_dtype=F32, add=dhn, name=f"d_hn1d{l}")
        dhn = _mm(dps, s["w_s"], mode="nt", grid=(t // tm, 2, 3),
                  a_spec=pl.BlockSpec((None, tm, D_SC), lambda i, j, kk: (kk, i, 0)),
                  b_spec=pl.BlockSpec((1024, D_SC), lambda i, j, kk: (j, kk)),
                  o_spec=pl.BlockSpec((tm, 1024), lambda i, j, kk: (i, j)), o_tile=(tm, 1024),
                  out_sds=_sds((t, D), F32), add=dhn, name=f"d_hn1s{l}")
        dh, dhb, dn = _rms_bwd(s["h0"], norm_mix[l:l + 1], dhn, dh, name=f"d_rms_mix{l}")
        small["norm_mix"].append(dn)

        grads = dict(w_in=_join_dw_in(dw_z, dw_xbc, dw_dt, dw_s), w_out=dw_out.reshape(4, -1, D),
                     w_q=dw_q.reshape(4, -1, D_XA), w_k=dw_k.reshape(4, -1, D_XA), w_v=dw_v.reshape(4, -1, D_XA),
                     w_o=dw_o, w_gate=dw_gate, w_up=dw_up, w_down=dw_down.reshape(4, -1, D))
        g_list = [grads[n] for n in names]
        recv_sib = _swap_halves(g_list, name=f"swap_halves{l}")
        parts = [_add_halves(g, rb, c_arr, name=f"add_halves_{n}{l}") for n, g, rb in zip(names, g_list, recv_sib)]
        recv_chips = _send_to_owners(parts, name=f"send_to_owners{l}")
        halves = [_sum_chips(p, rc, q_arr, c_arr, name=f"sum_chips_{n}{l}") for n, p, rc in zip(names, parts, recv_chips)]
        full = _join_halves(halves, name=f"join_halves{l}")
        for n, g in zip(names, full):
            carried[n] = _adamw_layer(l, big[n], big_m[n], big_v[n], g, carried[n], name=f"adamw_{n}{l}")

    grad_x = dh[None]

    _, _, d_mem_norm = _rms_bwd(mem[0], mem_norm.reshape(1, D), dmemn, jnp.zeros_like(dmemn), name="d_mem_norm")
    stack = lambda n: jnp.concatenate(small[n][::-1], axis=0) if small[n][0].ndim == 2 and small[n][0].shape[0] == 1 \
        else jnp.stack(small[n][::-1], axis=0)
    small_names = ["norm_mix", "ssd_conv_w", "ssd_conv_b", "dt_bias", "a_log", "d_skip", "ssd_norm", "sc_conv_w", "sc_norm",
                   "mem_norm", "norm_xa", "norm_ffn", "norm_final"]
    local_g = dict(mem_norm=d_mem_norm.reshape(D), norm_final=d_norm_final.reshape(D))
    for n in small:
        local_g[n] = stack(n)
    shapes = [local_g[n].shape for n in small_names]
    red = dict(zip(small_names, _unpack(_allreduce_small(_pack([local_g[n] for n in small_names]), name="allreduce_small"), shapes)))
    red["ssd_conv_w"] = lax.dynamic_slice(red["ssd_conv_w"], (0, 0, qme * (D_XBC // 4)), ssd_conv_w.shape)
    red["sc_conv_w"] = lax.dynamic_slice(red["sc_conv_w"], (0, 0, qme * (D_SC // 4)), sc_conv_w.shape)
    sw = dict(norm_mix=norm_mix, ssd_conv_w=ssd_conv_w, ssd_conv_b=ssd_conv_b, dt_bias=dt_bias, a_log=a_log, d_skip=d_skip,
              ssd_norm=ssd_norm, sc_conv_w=sc_conv_w, sc_norm=sc_norm, mem_norm=mem_norm, norm_xa=norm_xa, norm_ffn=norm_ffn,
              norm_final=norm_final)
    sm = dict(norm_mix=m_norm_mix, ssd_conv_w=m_ssd_conv_w, ssd_conv_b=m_ssd_conv_b, dt_bias=m_dt_bias, a_log=m_a_log,
              d_skip=m_d_skip, ssd_norm=m_ssd_norm, sc_conv_w=m_sc_conv_w, sc_norm=m_sc_norm, mem_norm=m_mem_norm,
              norm_xa=m_norm_xa, norm_ffn=m_norm_ffn, norm_final=m_norm_final)
    sv = dict(norm_mix=v_norm_mix, ssd_conv_w=v_ssd_conv_w, ssd_conv_b=v_ssd_conv_b, dt_bias=v_dt_bias, a_log=v_a_log,
              d_skip=v_d_skip, ssd_norm=v_ssd_norm, sc_conv_w=v_sc_conv_w, sc_norm=v_sc_norm, mem_norm=v_mem_norm,
              norm_xa=v_norm_xa, norm_ffn=v_norm_ffn, norm_final=v_norm_final)
    shard_shapes = [sw[n].shape for n in small_names]
    pk = lambda d: _pack([d[n] for n in small_names])
    sd, snm, snv = _adamw_flat(pk(sw), pk(red), pk(sm), pk(sv), name="adamw_small")
    s_delta = dict(zip(small_names, _unpack(sd, shard_shapes)))
    s_newm = dict(zip(small_names, _unpack(snm, shard_shapes)))
    s_newv = dict(zip(small_names, _unpack(snv, shard_shapes)))

    order = ["norm_mix", "w_in", "ssd_conv_w", "ssd_conv_b", "dt_bias", "a_log", "d_skip", "ssd_norm", "sc_conv_w", "sc_norm",
             "w_out", "mem_norm", "norm_xa", "w_q", "w_k", "w_v", "w_o", "norm_ffn", "w_gate", "w_up", "w_down", "norm_final"]

    def pick(kind):
        out = []
        for n in order:
            if n in carried:
                out.append(carried[n][kind])
            else:
                out.append([red, s_delta, s_newm, s_newv][kind][n])
        return out

    return (loss, grad_x, *pick(0), *pick(1), *pick(2), *pick(3))
```

```python
import functools

import jax
import jax.numpy as jnp
from jax import lax
from jax.experimental import pallas as pl
from jax.experimental.pallas import tpu as pltpu

F32 = jnp.float32
BF16 = jnp.bfloat16
MESH = pl.DeviceIdType.MESH

D = 2048
D_SSD = 2048
N_GROUPS = 4
GROUP_W = 768
D_XBC = 3072
N_STATE = 128
HEADS = 32
PAIRS_PER_GROUP = 4
CHUNK = 256
DT_PAD = 128
D_SC = 2048
SC_GROUP = 128
XA_HEADS = 4
XA_HD = 128
D_XA = 512
D_FF = 5632
EPS = 1e-5
HALO = 16
TN_TK = 2048
VMEM_LIMIT = 56 * 1024 * 1024

ADAM_LR, ADAM_B1, ADAM_B2, ADAM_EPS, ADAM_WD, ADAM_STEP = 0.001, 0.9, 0.999, 1e-08, 0.01, 10

NT = (((1,), (1,)), ((), ()))
TN = (((0,), (0,)), ((), ()))
NN = (((1,), (0,)), ((), ()))


def _pcall(body, **kw):
    return pl.pallas_call(body, **kw)


def _params(**kw):
    return pltpu.CompilerParams(vmem_limit_bytes=VMEM_LIMIT, **kw)


def _sds(shape, dtype):
    return jax.ShapeDtypeStruct(shape, dtype)


def _sig(x):
    return 1.0 / (1.0 + jnp.exp(-x))


def _dot(a, b, dims=NN):
    return lax.dot_general(a, b, dims, preferred_element_type=F32)


def _mm(a, b, *, mode, grid, a_spec, b_spec, o_spec, o_tile, out_sds, add=None, dep=None, name):
    gk = grid[2]
    dims = {"nn": NN, "nt": NT, "tn": TN}[mode]
    has_add = add is not None
    n_dep = 0 if dep is None else 1

    def body(*refs):
        a_ref, b_ref = refs[0], refs[1]
        add_ref = refs[2] if has_add else None
        refs = refs[:2 + has_add] + refs[2 + has_add + n_dep:]
        o_ref = refs[2 + has_add]
        p = _dot(a_ref[...].astype(BF16), b_ref[...].astype(BF16), dims)

        def finish(acc):
            if has_add:
                acc = acc + add_ref[...]
            o_ref[...] = acc.astype(o_ref.dtype)

        if gk == 1:
            finish(p)
        else:
            acc_ref = refs[3 + has_add]
            k = pl.program_id(2)

            @pl.when(k == 0)
            def _():
                acc_ref[...] = p

            @pl.when(k > 0)
            def _():
                acc_ref[...] += p

            @pl.when(k == gk - 1)
            def _():
                finish(acc_ref[...])

    in_specs = [a_spec, b_spec] + ([o_spec] if has_add else []) + [pl.BlockSpec(memory_space=pl.ANY)] * n_dep
    args = (a, b) + ((add,) if has_add else ()) + ((dep,) if n_dep else ())
    scratch = [pltpu.VMEM(o_tile, F32)] if gk > 1 else []
    return _pcall(body, grid=grid, in_specs=in_specs, out_specs=o_spec, out_shape=out_sds, scratch_shapes=scratch,
                  compiler_params=_params(dimension_semantics=("parallel", "parallel", "arbitrary")), name=name)(*args)


def _tile(n, pref):
    t = min(n, pref)
    assert n % t == 0, (n, pref)
    return t


def _mm_nn(a, w, *, tn, tk=None, tm=1024, out_dtype=BF16, add=None, name):
    m, k = a.shape
    n = w.shape[1]
    tm = _tile(m, tm)
    tk = k if tk is None else tk
    grid = (m // tm, n // tn, k // tk)
    return _mm(a, w, mode="nn", grid=grid,
               a_spec=pl.BlockSpec((tm, tk), lambda i, j, kk: (i, kk)),
               b_spec=pl.BlockSpec((tk, tn), lambda i, j, kk: (kk, j)),
               o_spec=pl.BlockSpec((tm, tn), lambda i, j, kk: (i, j)), o_tile=(tm, tn),
               out_sds=_sds((m, n), out_dtype), add=add, name=name)


def _mm_nn_sm(a, w4, *, out_dtype=BF16, add=None, name):
    m, k = a.shape
    n = w4.shape[2]
    tm = _tile(m, 1024)
    return _mm(a, w4, mode="nn", grid=(m // tm, 4, 1),
               a_spec=pl.BlockSpec((tm, k), lambda i, j, kk: (i, 0)),
               b_spec=pl.BlockSpec((None, k, n), lambda i, j, kk: (j, 0, 0)),
               o_spec=pl.BlockSpec((tm, n), lambda i, j, kk: (i, j)), o_tile=(tm, n),
               out_sds=_sds((m, 4 * n), out_dtype), add=add, name=name)


def _mm_nt(a, w, *, tn, tk=None, out_dtype=BF16, add=None, dep=None, name):
    m, k = a.shape
    n = w.shape[0]
    tm = _tile(m, 1024)
    tk = k if tk is None else tk
    grid = (m // tm, n // tn, k // tk)
    return _mm(a, w, mode="nt", grid=grid,
               a_spec=pl.BlockSpec((tm, tk), lambda i, j, kk: (i, kk)),
               b_spec=pl.BlockSpec((tn, tk), lambda i, j, kk: (j, kk)),
               o_spec=pl.BlockSpec((tm, tn), lambda i, j, kk: (i, j)), o_tile=(tm, tn),
               out_sds=_sds((m, n), out_dtype), add=add, dep=dep, name=name)


def _mm_nt_sm(a, w4, *, tn, out_dtype=BF16, add=None, name):
    m = a.shape[0]
    _, k, n = w4.shape
    tm = _tile(m, 1024)
    tn = _tile(k, tn)
    return _mm(a, w4, mode="nt", grid=(m // tm, k // tn, 4),
               a_spec=pl.BlockSpec((tm, n), lambda i, j, kk: (i, kk)),
               b_spec=pl.BlockSpec((None, tn, n), lambda i, j, kk: (kk, j, 0)),
               o_spec=pl.BlockSpec((tm, tn), lambda i, j, kk: (i, j)), o_tile=(tm, tn),
               out_sds=_sds((m, k), out_dtype), add=add, name=name)


def _mm_tn(a, g, *, tm, tn, out_dtype=BF16, name):
    t, m = a.shape
    n = g.shape[1]
    tk = _tile(t, TN_TK)
    return _mm(a, g, mode="tn", grid=(m // tm, n // tn, t // tk),
               a_spec=pl.BlockSpec((tk, tm), lambda i, j, kk: (kk, i)),
               b_spec=pl.BlockSpec((tk, tn), lambda i, j, kk: (kk, j)),
               o_spec=pl.BlockSpec((tm, tn), lambda i, j, kk: (i, j)), o_tile=(tm, tn),
               out_sds=_sds((m, n), out_dtype), name=name)


def _mm_tn_sm(a, g, *, tm, out_dtype=BF16, name):
    t, m = a.shape
    n = g.shape[1] // 4
    tk = _tile(t, TN_TK)
    return _mm(a, g, mode="tn", grid=(m // tm, 4, t // tk),
               a_spec=pl.BlockSpec((tk, tm), lambda i, j, kk: (kk, i)),
               b_spec=pl.BlockSpec((tk, n), lambda i, j, kk: (kk, j)),
               o_spec=pl.BlockSpec((None, tm, n), lambda i, j, kk: (j, i, 0)), o_tile=(tm, n),
               out_sds=_sds((4, m, n), out_dtype), name=name)


def _rms_fwd(h, g, *, name):
    t, d = h.shape
    tr = _tile(t, 512)

    def body(h_ref, g_ref, o_ref):
        x = h_ref[...]
        r = lax.rsqrt(jnp.mean(x * x, axis=-1, keepdims=True) + EPS)
        o_ref[...] = (x * r * g_ref[...]).astype(o_ref.dtype)

    return _pcall(body, grid=(t // tr,),
                  in_specs=[pl.BlockSpec((tr, d), lambda i: (i, 0)), pl.BlockSpec((1, d), lambda i: (0, 0))],
                  out_specs=pl.BlockSpec((tr, d), lambda i: (i, 0)), out_shape=_sds((t, d), BF16),
                  compiler_params=_params(dimension_semantics=("parallel",)), name=name)(h, g)


def _rms_bwd(h, g, dy, dres, *, name):
    t, d = h.shape
    tr = _tile(t, 256)

    def body(h_ref, g_ref, dy_ref, dres_ref, dh_ref, dhb_ref, dg_ref):
        i = pl.program_id(0)
        x = h_ref[...]
        r = lax.rsqrt(jnp.mean(x * x, axis=-1, keepdims=True) + EPS)
        xh = x * r
        dyv = dy_ref[...].astype(F32)
        dxh = dyv * g_ref[...]
        dh = dres_ref[...] + r * (dxh - xh * jnp.mean(dxh * xh, axis=-1, keepdims=True))
        dh_ref[...] = dh
        dhb_ref[...] = dh.astype(BF16)
        part = jnp.sum(dyv * xh, axis=0, keepdims=True)

        @pl.when(i == 0)
        def _():
            dg_ref[...] = part

        @pl.when(i > 0)
        def _():
            dg_ref[...] += part

    row = pl.BlockSpec((tr, d), lambda i: (i, 0))
    vec = pl.BlockSpec((1, d), lambda i: (0, 0))
    return _pcall(body, grid=(t // tr,), in_specs=[row, vec, row, row], out_specs=[row, row, vec],
                  out_shape=[_sds((t, d), F32), _sds((t, d), BF16), _sds((1, d), F32)],
                  compiler_params=_params(dimension_semantics=("arbitrary",)), name=name)(h, g, dy, dres)


def _final(h, g, tgt, *, name):
    t, d = h.shape
    tr = _tile(t, 256)

    def body(h_ref, g_ref, t_ref, loss_ref, dh_ref, dhb_ref, dg_ref):
        i = pl.program_id(0)
        x = h_ref[...]
        gv = g_ref[...]
        r = lax.rsqrt(jnp.mean(x * x, axis=-1, keepdims=True) + EPS)
        xh = x * r
        e = xh * gv - t_ref[...]
        lpart = jnp.zeros((1, 128), F32) + 0.5 * jnp.sum(jnp.mean(e * e, axis=-1, keepdims=True))
        dyv = e * (1.0 / d)
        dxh = dyv * gv
        dh = r * (dxh - xh * jnp.mean(dxh * xh, axis=-1, keepdims=True))
        dh_ref[...] = dh
        dhb_ref[...] = dh.astype(BF16)
        part = jnp.sum(dyv * xh, axis=0, keepdims=True)

        @pl.when(i == 0)
        def _():
            dg_ref[...] = part
            loss_ref[...] = lpart

        @pl.when(i > 0)
        def _():
            dg_ref[...] += part
            loss_ref[...] += lpart

    row = pl.BlockSpec((tr, d), lambda i: (i, 0))
    vec = pl.BlockSpec((1, d), lambda i: (0, 0))
    return _pcall(body, grid=(t // tr,), in_specs=[row, vec, row],
                  out_specs=[pl.BlockSpec((1, 128), lambda i: (0, 0)), row, row, vec],
                  out_shape=[_sds((1, 128), F32), _sds((t, d), F32), _sds((t, d), BF16), _sds((1, d), F32)],
                  compiler_params=_params(dimension_semantics=("arbitrary",)), name=name)(h, g, tgt)


def _conv_taps(ext, w, ntap, rows):
    n = ext.shape[0]
    acc = w[ntap - 1:ntap, :] * ext[HALO:HALO + rows]
    for k in range(1, ntap):
        acc = acc + w[ntap - 1 - k:ntap - k, :] * pltpu.roll(ext, k, axis=0)[HALO:HALO + rows]
    del n
    return acc


def _conv_fwd(xbc, w, b, *, name):
    t, c = xbc.shape
    rows = CHUNK
    cw = GROUP_W
    hb = rows // HALO

    def body(cur_ref, prev_ref, w_ref, b_ref, o_ref):
        i = pl.program_id(1)
        cur = cur_ref[...].astype(F32)
        prev = jnp.where(i > 0, prev_ref[...].astype(F32), 0.0)
        ext = jnp.concatenate([prev, cur], axis=0)
        pre = _conv_taps(ext, w_ref[...], 4, rows) + b_ref[...]
        o_ref[...] = (pre * _sig(pre)).astype(o_ref.dtype)

    return _pcall(body, grid=(c // cw, t // rows),
                  in_specs=[pl.BlockSpec((rows, cw), lambda j, i: (i, j)),
                            pl.BlockSpec((HALO, cw), lambda j, i: (jnp.maximum(i * hb - 1, 0), j)),
                            pl.BlockSpec((4, cw), lambda j, i: (0, j)),
                            pl.BlockSpec((1, cw), lambda j, i: (0, j))],
                  out_specs=pl.BlockSpec((rows, cw), lambda j, i: (i, j)), out_shape=_sds((t, c), BF16),
                  compiler_params=_params(dimension_semantics=("parallel", "parallel")), name=name)(xbc, xbc, w, b)


def _conv_bwd(dxc, xbc, w, b, *, name):
    t, c = xbc.shape
    rows = CHUNK
    cw = GROUP_W
    hb = rows // HALO
    nblk = t // rows
    nhalo = t // HALO

    def body(d_ref, dn_ref, cur_ref, prev_ref, next_ref, w_ref, b_ref, dx_ref, dw_ref, db_ref):
        i = pl.program_id(1)
        last = i == nblk - 1
        wv = w_ref[...]
        xe = jnp.concatenate([jnp.where(i > 0, prev_ref[...].astype(F32), 0.0), cur_ref[...].astype(F32),
                              jnp.where(last, 0.0, next_ref[...].astype(F32))], axis=0)
        n = rows + 2 * HALO
        sh = [xe] + [pltpu.roll(xe, k, axis=0) for k in range(1, 4)]
        pre = wv[3:4, :] * sh[0] + wv[2:3, :] * sh[1] + wv[1:2, :] * sh[2] + wv[0:1, :] * sh[3] + b_ref[...]
        de = jnp.concatenate([jnp.zeros((HALO, cw), F32), d_ref[...].astype(F32),
                              jnp.where(last, 0.0, dn_ref[...].astype(F32))], axis=0)
        s = _sig(pre)
        dpre = de * (s * (1.0 + pre * (1.0 - s)))
        dx = wv[3:4, :] * dpre
        for m in range(1, 4):
            dx = dx + wv[3 - m:4 - m, :] * pltpu.roll(dpre, n - m, axis=0)
        dx_ref[...] = dx[HALO:HALO + rows].astype(dx_ref.dtype)
        dcur = dpre[HALO:HALO + rows]
        dwv = jnp.concatenate([jnp.sum(dcur * sh[3 - j][HALO:HALO + rows], axis=0, keepdims=True) for j in range(4)], axis=0)
        dbv = jnp.sum(dcur, axis=0, keepdims=True)

        @pl.when(i == 0)
        def _():
            dw_ref[...] = dwv
            db_ref[...] = dbv

        @pl.when(i > 0)
        def _():
            dw_ref[...] += dwv
            db_ref[...] += dbv

    cur = pl.BlockSpec((rows, cw), lambda j, i: (i, j))
    prev = pl.BlockSpec((HALO, cw), lambda j, i: (jnp.maximum(i * hb - 1, 0), j))
    nxt = pl.BlockSpec((HALO, cw), lambda j, i: (jnp.minimum((i + 1) * hb, nhalo - 1), j))
    return _pcall(body, grid=(c // cw, nblk),
                  in_specs=[cur, nxt, cur, prev, nxt, pl.BlockSpec((4, cw), lambda j, i: (0, j)),
                            pl.BlockSpec((1, cw), lambda j, i: (0, j))],
                  out_specs=[cur, pl.BlockSpec((4, cw), lambda j, i: (0, j)), pl.BlockSpec((1, cw), lambda j, i: (0, j))],
                  out_shape=[_sds((t, c), BF16), _sds((4, c), F32), _sds((1, c), F32)],
                  compiler_params=_params(dimension_semantics=("parallel", "arbitrary")), name=name)(dxc, dxc, xbc, xbc, xbc, w, b)


def _neg_exp_alog(alog):
    lane = lax.broadcasted_iota(jnp.int32, alog.shape, 1)
    return jnp.where(lane < HEADS, -jnp.exp(alog), 0.0)


def _dt_prep(dtr, bias, alog, *, name):
    t = dtr.shape[0]
    rows = CHUNK

    def body(r_ref, b_ref, a_ref, dt_ref, dtg_ref, acsg_ref, acst_ref):
        raw = r_ref[...] + b_ref[...]
        dt = jnp.maximum(raw, 0.0) + jnp.log(1.0 + jnp.exp(-jnp.abs(raw)))
        a = _neg_exp_alog(a_ref[...])
        adt = dt * a
        ri = lax.broadcasted_iota(jnp.int32, (rows, rows), 0)
        ci = lax.broadcasted_iota(jnp.int32, (rows, rows), 1)
        tri = (ri >= ci).astype(F32)
        acs = jnp.dot(tri, adt, precision=lax.Precision.HIGHEST, preferred_element_type=F32)
        dt_ref[...] = dt
        acst_ref[...] = acs.T
        for g in range(N_GROUPS):
            sh = (128 - 8 * g) % 128
            dtg_ref[g] = dt if sh == 0 else pltpu.roll(dt, sh, axis=1)
            acsg_ref[g] = acs if sh == 0 else pltpu.roll(acs, sh, axis=1)

    row = pl.BlockSpec((rows, DT_PAD), lambda i: (i, 0))
    vec = pl.BlockSpec((1, DT_PAD), lambda i: (0, 0))
    grp = pl.BlockSpec((N_GROUPS, rows, DT_PAD), lambda i: (0, i, 0))
    return _pcall(body, grid=(t // rows,), in_specs=[row, vec, vec],
                  out_specs=[row, grp, grp, pl.BlockSpec((DT_PAD, rows), lambda i: (0, i))],
                  out_shape=[_sds((t, DT_PAD), F32), _sds((N_GROUPS, t, DT_PAD), F32), _sds((N_GROUPS, t, DT_PAD), F32),
                             _sds((DT_PAD, t), F32)],
                  compiler_params=_params(dimension_semantics=("parallel",)), name=name)(dtr, bias, alog)


def _dt_bwd(ddtg, dacg, dart, dt, dtr, bias, alog, *, name):
    t = dtr.shape[0]
    rows = CHUNK

    def body(ddtg_ref, dacg_ref, dart_ref, dt_ref, r_ref, b_ref, a_ref, dr_ref, db_ref, da_ref):
        i = pl.program_id(0)
        lane = lax.broadcasted_iota(jnp.int32, (rows, DT_PAD), 1)
        ddt = jnp.zeros((rows, DT_PAD), F32)
        dacs = jnp.concatenate([dart_ref[...], jnp.zeros((DT_PAD - HEADS, rows), F32)], axis=0).T
        for g in range(N_GROUPS):
            sel = (lane >= 8 * g) & (lane < 8 * g + 8)
            dd = ddtg_ref[g]
            da = dacg_ref[g]
            if g:
                dd = pltpu.roll(dd, 8 * g, axis=1)
                da = pltpu.roll(da, 8 * g, axis=1)
            ddt = ddt + jnp.where(sel, dd, 0.0)
            dacs = dacs + jnp.where(sel, da, 0.0)
        ri = lax.broadcasted_iota(jnp.int32, (rows, rows), 0)
        ci = lax.broadcasted_iota(jnp.int32, (rows, rows), 1)
        triu = (ci >= ri).astype(F32)
        rev = jnp.dot(triu, dacs, precision=lax.Precision.HIGHEST, preferred_element_type=F32)
        a = _neg_exp_alog(a_ref[...])
        dtv = dt_ref[...]
        raw = r_ref[...] + b_ref[...]
        draw = (ddt + a * rev) * _sig(raw)
        dr_ref[...] = draw
        dbv = jnp.sum(draw, axis=0, keepdims=True)
        dav = jnp.sum(dtv * rev, axis=0, keepdims=True) * a

        @pl.when(i == 0)
        def _():
            db_ref[...] = dbv
            da_ref[...] = dav

        @pl.when(i > 0)
        def _():
            db_ref[...] += dbv
            da_ref[...] += dav

    row = pl.BlockSpec((rows, DT_PAD), lambda i: (i, 0))
    vec = pl.BlockSpec((1, DT_PAD), lambda i: (0, 0))
    grp = pl.BlockSpec((N_GROUPS, rows, DT_PAD), lambda i: (0, i, 0))
    return _pcall(body, grid=(t // rows,),
                  in_specs=[grp, grp, pl.BlockSpec((HEADS, rows), lambda i: (0, i)), row, row, vec, vec],
                  out_specs=[row, vec, vec], out_shape=[_sds((t, DT_PAD), F32), _sds((1, DT_PAD), F32), _sds((1, DT_PAD), F32)],
                  compiler_params=_params(dimension_semantics=("arbitrary",)), name=name)(ddtg, dacg, dart, dt, dtr, bias, alog)


def _pair_cols(col_ref_val, p, lo):
    return jnp.where(lo, col_ref_val[:, 2 * p:2 * p + 1], col_ref_val[:, 2 * p + 1:2 * p + 2])


def _ssd_fwd(xc, dtg, acsg, acst, z, dskip, nw, *, name):
    t = xc.shape[0]
    L = CHUNK
    nc = t // L

    def body(xc_ref, dtg_ref, acsg_ref, acst_ref, z_ref, dsk_ref, nw_ref, y_ref, st_ref, mix_ref, s_ref):
        c = pl.program_id(1)

        @pl.when(c == 0)
        def _():
            s_ref[...] = jnp.zeros_like(s_ref)

        blk = xc_ref[...]
        bm = blk[:, 512:640]
        cm = blk[:, 640:768]
        cb = _dot(cm, bm, NT)
        dtv = dtg_ref[...]
        acs = acsg_ref[...]
        acst_v = acst_ref[...]
        ri = lax.broadcasted_iota(jnp.int32, (L, L), 0)
        ci = lax.broadcasted_iota(jnp.int32, (L, L), 1)
        causal = ri >= ci
        lo = lax.broadcasted_iota(jnp.int32, (1, 128), 1) < 64
        lo_rows = lax.broadcasted_iota(jnp.int32, (128, 1), 0) < 64
        dskv = dsk_ref[...]
        ys = []
        for p in range(PAIRS_PER_GROUP):
            xp = blk[:, 128 * p:128 * p + 128].astype(F32)
            dt_p = _pair_cols(dtv, p, lo)
            a_p = _pair_cols(acs, p, lo)
            alast = acs[L - 1:L, :]
            al_p = _pair_cols(alast, p, lo)
            xdt = xp * dt_p
            xdt_b = xdt.astype(BF16)
            yd = []
            for hh in range(2):
                j = 2 * p + hh
                seg = acs[:, j:j + 1] - acst_v[j:j + 1, :]
                lam = jnp.exp(jnp.where(causal, seg, -1e30))
                w = (cb * lam).astype(BF16)
                yd.append(_dot(w, xdt_b))
            y = jnp.where(lo, yd[0], yd[1])
            sp = s_ref[p]
            st_ref[p] = sp
            y = y + _dot(cm, sp.astype(BF16), NT) * jnp.exp(a_p)
            dsc = jnp.exp(al_p - a_p)
            snew = _dot((xdt * dsc).astype(BF16), bm, TN)
            al_rows = jnp.where(lo_rows, alast[:, 2 * p:2 * p + 1], alast[:, 2 * p + 1:2 * p + 2])
            s_ref[p] = sp * jnp.exp(al_rows) + snew
            ys.append(y + xp * dskv[:, 128 * p:128 * p + 128])
        yfull = jnp.concatenate(ys, axis=1)
        y_ref[...] = yfull.astype(y_ref.dtype)
        zz = z_ref[...].astype(F32)
        yg = yfull * (zz * _sig(zz))
        r = lax.rsqrt(jnp.mean(yg * yg, axis=-1, keepdims=True) + EPS)
        mix_ref[...] = (yg * r * nw_ref[...]).astype(mix_ref.dtype)

    grp = pl.BlockSpec((None, L, DT_PAD), lambda g, c: (g, c, 0))
    return _pcall(body, grid=(N_GROUPS, nc),
                  in_specs=[pl.BlockSpec((L, GROUP_W), lambda g, c: (c, g)), grp, grp,
                            pl.BlockSpec((8, L), lambda g, c: (g, c)),
                            pl.BlockSpec((L, 512), lambda g, c: (c, g)),
                            pl.BlockSpec((1, 512), lambda g, c: (0, g)), pl.BlockSpec((1, 512), lambda g, c: (0, g))],
                  out_specs=[pl.BlockSpec((L, 512), lambda g, c: (c, g)),
                             pl.BlockSpec((None, PAIRS_PER_GROUP, 128, N_STATE), lambda g, c: (c, g, 0, 0)),
                             pl.BlockSpec((L, 512), lambda g, c: (c, g))],
                  out_shape=[_sds((t, D_SSD), BF16), _sds((nc, N_GROUPS * PAIRS_PER_GROUP, 128, N_STATE), F32),
                             _sds((t, D_SSD + D_SC), BF16)],
                  scratch_shapes=[pltpu.VMEM((PAIRS_PER_GROUP, 128, N_STATE), F32)],
                  compiler_params=_params(dimension_semantics=("parallel", "arbitrary")), name=name)(
                      xc, dtg, acsg, acst, z, dskip, nw)


def _ssd_bwd(xc, dtg, acsg, acst, z, y, states, dmix, dskip, nw, *, name):
    t = xc.shape[0]
    L = CHUNK
    nc = t // L

    def body(xc_ref, dtg_ref, acsg_ref, acst_ref, z_ref, y_ref, st_ref, dm_ref, dsk_ref, nw_ref,
             dxc_ref, dz_ref, ddt_ref, dac_ref, dar_ref, ddsk_ref, dnw_ref, ds_ref):
        c = pl.program_id(1)

        @pl.when(c == 0)
        def _():
            ds_ref[...] = jnp.zeros_like(ds_ref)
            ddsk_ref[...] = jnp.zeros_like(ddsk_ref)
            dnw_ref[...] = jnp.zeros_like(dnw_ref)

        blk = xc_ref[...]
        xs = blk[:, :512].astype(F32)
        bm = blk[:, 512:640]
        cm = blk[:, 640:768]
        bmf = bm.astype(F32)
        yv = y_ref[...].astype(F32)
        zz = z_ref[...].astype(F32)
        nwv = nw_ref[...]
        dout = dm_ref[...].astype(F32)
        sz = _sig(zz)
        silu = zz * sz
        yg = yv * silu
        r = lax.rsqrt(jnp.mean(yg * yg, axis=-1, keepdims=True) + EPS)
        xh = yg * r
        dnw_ref[...] += jnp.sum(dout * xh, axis=0, keepdims=True)
        dyn = dout * nwv
        dyg = r * (dyn - xh * jnp.mean(dyn * xh, axis=-1, keepdims=True))
        dy = dyg * silu
        dz_ref[...] = (dyg * yv * (sz * (1.0 + zz * (1.0 - sz)))).astype(dz_ref.dtype)
        ddsk_ref[...] += jnp.sum(dy * xs, axis=0, keepdims=True)

        cb = _dot(cm, bm, NT)
        dtv = dtg_ref[...]
        acs = acsg_ref[...]
        acst_v = acst_ref[...]
        alast = acs[L - 1:L, :]
        ri = lax.broadcasted_iota(jnp.int32, (L, L), 0)
        ci = lax.broadcasted_iota(jnp.int32, (L, L), 1)
        causal = ri >= ci
        lane = lax.broadcasted_iota(jnp.int32, (1, 128), 1)
        lo = lane < 64
        lo_rows = lax.broadcasted_iota(jnp.int32, (128, 1), 0) < 64
        lane_l = lax.broadcasted_iota(jnp.int32, (L, DT_PAD), 1)
        row_l = lax.broadcasted_iota(jnp.int32, (L, 1), 0)
        sub8 = lax.broadcasted_iota(jnp.int32, (8, L), 0)
        dskv = dsk_ref[...]
        dm_acc = jnp.zeros((L, L), F32)
        db_acc = jnp.zeros((L, N_STATE), F32)
        dc_acc = jnp.zeros((L, N_STATE), F32)
        ddt_out = jnp.zeros((L, DT_PAD), F32)
        dac_out = jnp.zeros((L, DT_PAD), F32)
        dar_out = jnp.zeros((8, L), F32)
        dxs = []
        for p in range(PAIRS_PER_GROUP):
            xp = xs[:, 128 * p:128 * p + 128]
            dyp = dy[:, 128 * p:128 * p + 128]
            dt_p = _pair_cols(dtv, p, lo)
            a_p = _pair_cols(acs, p, lo)
            al_p = _pair_cols(alast, p, lo)
            xdt = xp * dt_p
            xdt_b = xdt.astype(BF16)
            ea_p = jnp.exp(a_p)
            dsc_p = jnp.exp(al_p - a_p)
            sp = st_ref[p]
            sp_b = sp.astype(BF16)
            dsp = ds_ref[p]
            dsp_b = dsp.astype(BF16)
            cs = _dot(cm, sp_b, NT)
            dye_b = (dyp * ea_p).astype(BF16)
            dc_acc = dc_acc + _dot(dye_b, sp_b)
            ds_prev = _dot(dye_b, cm, TN)
            bds = _dot(bm, dsp_b, NT)
            al_rows = jnp.where(lo_rows, alast[:, 2 * p:2 * p + 1], alast[:, 2 * p + 1:2 * p + 2])
            ds_prev = ds_prev + jnp.exp(al_rows) * dsp
            prod_off = dyp * cs
            prod_st = dsp * sp
            dxdt_h = []
            for hh in range(2):
                j = 2 * p + hh
                hm = lo if hh == 0 else jnp.logical_not(lo)
                hm_rows = lo_rows if hh == 0 else jnp.logical_not(lo_rows)
                a_col = acs[:, j:j + 1]
                seg = a_col - acst_v[j:j + 1, :]
                lam = jnp.exp(jnp.where(causal, seg, -1e30))
                wf = cb * lam
                w = wf.astype(BF16)
                dy_h = jnp.where(hm, dyp, 0.0).astype(BF16)
                dw = _dot(dy_h, xdt_b, NT)
                dxdt_h.append(_dot(w, dyp.astype(BF16), TN))
                dm_acc = dm_acc + dw * lam
                e = dw * wf
                dac = jnp.sum(e, axis=1, keepdims=True)
                dar = -jnp.sum(e, axis=0, keepdims=True)
                ea_col = jnp.exp(a_col)
                dac = dac + ea_col * jnp.sum(jnp.where(hm, prod_off, 0.0), axis=1, keepdims=True)
                al_h = alast[:, j:j + 1]
                dal = jnp.exp(al_h) * jnp.sum(jnp.sum(jnp.where(hm_rows, prod_st, 0.0), axis=1, keepdims=True), axis=0, keepdims=True)
                xds_h = _dot(jnp.where(hm, xdt, 0.0).astype(BF16), dsp_b)
                dsc_col = jnp.exp(al_h - a_col)
                db_acc = db_acc + dsc_col * xds_h
                tt = jnp.sum(xds_h * bmf, axis=1, keepdims=True) * dsc_col
                dal = dal + jnp.sum(tt, axis=0, keepdims=True)
                dac = dac - tt + jnp.where(row_l == L - 1, dal, 0.0)
                dac_out = jnp.where(lane_l == j, dac, dac_out)
                dar_out = jnp.where(sub8 == j, dar, dar_out)
            dxdt = jnp.where(lo, dxdt_h[0], dxdt_h[1]) + dsc_p * bds
            dxs.append(dxdt * dt_p + dyp * dskv[:, 128 * p:128 * p + 128])
            prod_dt = dxdt * xp
            for hh in range(2):
                j = 2 * p + hh
                hm = lo if hh == 0 else jnp.logical_not(lo)
                ddt_col = jnp.sum(jnp.where(hm, prod_dt, 0.0), axis=1, keepdims=True)
                ddt_out = jnp.where(lane_l == j, ddt_col, ddt_out)
            ds_ref[p] = ds_prev
        dm_b = dm_acc.astype(BF16)
        dc_acc = dc_acc + _dot(dm_b, bm)
        db_acc = db_acc + _dot(dm_b, cm, TN)
        dxc_ref[...] = jnp.concatenate(dxs + [db_acc, dc_acc], axis=1).astype(dxc_ref.dtype)
        ddt_ref[...] = ddt_out
        dac_ref[...] = dac_out
        dar_ref[...] = dar_out

    rc = lambda g, c: (nc - 1 - c, g)
    grp = pl.BlockSpec((None, L, DT_PAD), lambda g, c: (g, nc - 1 - c, 0))
    vec = pl.BlockSpec((1, 512), lambda g, c: (0, g))
    return _pcall(body, grid=(N_GROUPS, nc),
                  in_specs=[pl.BlockSpec((L, GROUP_W), rc), grp, grp,
                            pl.BlockSpec((8, L), lambda g, c: (g, nc - 1 - c)),
                            pl.BlockSpec((L, 512), rc), pl.BlockSpec((L, 512), rc),
                            pl.BlockSpec((None, PAIRS_PER_GROUP, 128, N_STATE), lambda g, c: (nc - 1 - c, g, 0, 0)),
                            pl.BlockSpec((L, 512), rc), vec, vec],
                  out_specs=[pl.BlockSpec((L, GROUP_W), rc), pl.BlockSpec((L, 512), rc), grp, grp,
                             pl.BlockSpec((8, L), lambda g, c: (g, nc - 1 - c)), vec, vec],
                  out_shape=[_sds((t, D_XBC), BF16), _sds((t, D_SSD), BF16), _sds((N_GROUPS, t, DT_PAD), F32),
                             _sds((N_GROUPS, t, DT_PAD), F32), _sds((HEADS, t), F32), _sds((1, D_SSD), F32),
                             _sds((1, D_SSD), F32)],
                  scratch_shapes=[pltpu.VMEM((PAIRS_PER_GROUP, 128, N_STATE), F32)],
                  compiler_params=_params(dimension_semantics=("parallel", "arbitrary")), name=name)(
                      xc, dtg, acsg, acst, z, y, states, dmix, dskip, nw)


SC_CW = 1024


def _group_rstd(v):
    outs = []
    for q in range(v.shape[1] // SC_GROUP):
        vq = v[:, SC_GROUP * q:SC_GROUP * (q + 1)]
        outs.append(jnp.broadcast_to(lax.rsqrt(jnp.mean(vq * vq, axis=-1, keepdims=True) + EPS), vq.shape))
    return jnp.concatenate(outs, axis=1)


def _group_mean(v):
    outs = []
    for q in range(v.shape[1] // SC_GROUP):
        vq = v[:, SC_GROUP * q:SC_GROUP * (q + 1)]
        outs.append(jnp.broadcast_to(jnp.mean(vq, axis=-1, keepdims=True), vq.shape))
    return jnp.concatenate(outs, axis=1)


def _sc_fwd(ps, w, nw, mix, *, name):
    t = ps.shape[1]
    rows = CHUNK
    hb = rows // HALO
    cw = SC_CW
    off = D_SSD // cw

    def body(cur_ref, prev_ref, w_ref, nw_ref, mix_in_ref, o_ref):
        del mix_in_ref
        i = pl.program_id(1)
        u = cur_ref[0].astype(F32)
        gb = cur_ref[1].astype(F32)
        gc = cur_ref[2].astype(F32)
        cu_prev = jnp.where(i > 0, prev_ref[2].astype(F32) * prev_ref[0].astype(F32), 0.0)
        ext = jnp.concatenate([cu_prev, gc * u], axis=0)
        v = gb * _conv_taps(ext, w_ref[...], 3, rows)
        o_ref[...] = (v * _group_rstd(v) * nw_ref[...]).astype(o_ref.dtype)

    return _pcall(body, grid=(D_SC // cw, t // rows),
                  in_specs=[pl.BlockSpec((3, rows, cw), lambda j, i: (0, i, j)),
                            pl.BlockSpec((3, HALO, cw), lambda j, i: (0, jnp.maximum(i * hb - 1, 0), j)),
                            pl.BlockSpec((3, cw), lambda j, i: (0, j)), pl.BlockSpec((1, cw), lambda j, i: (0, j)),
                            pl.BlockSpec(memory_space=pl.ANY)],
                  out_specs=pl.BlockSpec((rows, cw), lambda j, i: (i, off + j)),
                  out_shape=_sds(mix.shape, mix.dtype), input_output_aliases={4: 0},
                  compiler_params=_params(dimension_semantics=("parallel", "parallel")), name=name)(ps, ps, w, nw, mix)


def _sc_bwd(ps, w, nw, dmix, *, name):
    t = ps.shape[1]
    rows = CHUNK
    hb = rows // HALO
    cw = SC_CW
    off = D_SSD // cw
    nblk = t // rows
    nhalo = t // HALO
    n = rows + 2 * HALO

    def body(cur_ref, prev_ref, next_ref, w_ref, nw_ref, d_ref, dn_ref, dps_ref, dw_ref, dnw_ref):
        i = pl.program_id(1)
        first = i == 0
        last = i == nblk - 1

        def ext(k):
            return jnp.concatenate([jnp.where(first, 0.0, prev_ref[k].astype(F32)), cur_ref[k].astype(F32),
                                    jnp.where(last, 0.0, next_ref[k].astype(F32))], axis=0)

        ue, gbe, gce = ext(0), ext(1), ext(2)
        wv = w_ref[...]
        nwv = nw_ref[...]
        cue = gce * ue
        cu1 = pltpu.roll(cue, 1, axis=0)
        cu2 = pltpu.roll(cue, 2, axis=0)
        conv = wv[2:3, :] * cue + wv[1:2, :] * cu1 + wv[0:1, :] * cu2
        ve = gbe * conv
        doe = jnp.concatenate([jnp.zeros((HALO, cw), F32), d_ref[...].astype(F32),
                               jnp.where(last, 0.0, dn_ref[...].astype(F32))], axis=0)
        r = _group_rstd(ve)
        xh = ve * r
        dvn = doe * nwv
        dv = r * (dvn - xh * _group_mean(dvn * xh))
        dconv = dv * gbe
        dcu = wv[2:3, :] * dconv + wv[1:2, :] * pltpu.roll(dconv, n - 1, axis=0) + wv[0:1, :] * pltpu.roll(dconv, n - 2, axis=0)
        sl = slice(HALO, HALO + rows)
        dps_ref[0] = (dcu * gce)[sl].astype(dps_ref.dtype)
        dps_ref[1] = (dv * conv)[sl].astype(dps_ref.dtype)
        dps_ref[2] = (dcu * ue)[sl].astype(dps_ref.dtype)
        dc = dconv[sl]
        dwv = jnp.concatenate([jnp.sum(dc * cu2[sl], axis=0, keepdims=True), jnp.sum(dc * cu1[sl], axis=0, keepdims=True),
                               jnp.sum(dc * cue[sl], axis=0, keepdims=True)], axis=0)
        dnv = jnp.sum((doe * xh)[sl], axis=0, keepdims=True)

        @pl.when(first)
        def _():
            dw_ref[...] = dwv
            dnw_ref[...] = dnv

        @pl.when(i > 0)
        def _():
            dw_ref[...] += dwv
            dnw_ref[...] += dnv

    cur = pl.BlockSpec((3, rows, cw), lambda j, i: (0, i, j))
    prev = pl.BlockSpec((3, HALO, cw), lambda j, i: (0, jnp.maximum(i * hb - 1, 0), j))
    nxt = pl.BlockSpec((3, HALO, cw), lambda j, i: (0, jnp.minimum((i + 1) * hb, nhalo - 1), j))
    return _pcall(body, grid=(D_SC // cw, nblk),
                  in_specs=[cur, prev, nxt, pl.BlockSpec((3, cw), lambda j, i: (0, j)), pl.BlockSpec((1, cw), lambda j, i: (0, j)),
                            pl.BlockSpec((rows, cw), lambda j, i: (i, off + j)),
                            pl.BlockSpec((HALO, cw), lambda j, i: (jnp.minimum((i + 1) * hb, nhalo - 1), off + j))],
                  out_specs=[cur, pl.BlockSpec((3, cw), lambda j, i: (0, j)), pl.BlockSpec((1, cw), lambda j, i: (0, j))],
                  out_shape=[_sds(ps.shape, BF16), _sds((3, D_SC), F32), _sds((1, D_SC), F32)],
                  compiler_params=_params(dimension_semantics=("parallel", "arbitrary")), name=name)(ps, ps, ps, w, nw, dmix, dmix)


XA_SCALE = XA_HD ** -0.5


def _softmax(s):
    m = jnp.max(s, axis=-1, keepdims=True)
    e = jnp.exp(s - m)
    return e / jnp.sum(e, axis=-1, keepdims=True)


def _xa_fwd(q, k, v, *, name):
    t = q.shape[0]
    nm = k.shape[0]
    tq = _tile(t, 512)

    def body(q_ref, k_ref, v_ref, o_ref):
        outs = []
        for h in range(XA_HEADS):
            sl = slice(XA_HD * h, XA_HD * (h + 1))
            s = _dot(q_ref[:, sl], k_ref[:, sl], NT) * XA_SCALE
            outs.append(_dot(_softmax(s).astype(BF16), v_ref[:, sl]))
        o_ref[...] = jnp.concatenate(outs, axis=1).astype(o_ref.dtype)

    row = pl.BlockSpec((tq, D_XA), lambda i: (i, 0))
    kv = pl.BlockSpec((nm, D_XA), lambda i: (0, 0))
    return _pcall(body, grid=(t // tq,), in_specs=[row, kv, kv], out_specs=row, out_shape=_sds((t, D_XA), BF16),
                  compiler_params=_params(dimension_semantics=("parallel",)), name=name)(q, k, v)


def _xa_bwd(q, k, v, do, *, name):
    t = q.shape[0]
    nm = k.shape[0]
    tq = _tile(t, 512)

    def body(q_ref, k_ref, v_ref, do_ref, dq_ref, dk_ref, dv_ref):
        i = pl.program_id(0)
        dqs, dks, dvs = [], [], []
        for h in range(XA_HEADS):
            sl = slice(XA_HD * h, XA_HD * (h + 1))
            qh, kh, vh, doh = q_ref[:, sl], k_ref[:, sl], v_ref[:, sl], do_ref[:, sl]
            p = _softmax(_dot(qh, kh, NT) * XA_SCALE)
            dvs.append(_dot(p.astype(BF16), doh, TN))
            dp = _dot(doh, vh, NT)
            ds = (p * (dp - jnp.sum(dp * p, axis=-1, keepdims=True)) * XA_SCALE).astype(BF16)
            dqs.append(_dot(ds, kh))
            dks.append(_dot(ds, qh, TN))
        dq_ref[...] = jnp.concatenate(dqs, axis=1).astype(dq_ref.dtype)
        dkv = jnp.concatenate(dks, axis=1)
        dvv = jnp.concatenate(dvs, axis=1)

        @pl.when(i == 0)
        def _():
            dk_ref[...] = dkv
            dv_ref[...] = dvv

        @pl.when(i > 0)
        def _():
            dk_ref[...] += dkv
            dv_ref[...] += dvv

    row = pl.BlockSpec((tq, D_XA), lambda i: (i, 0))
    kv = pl.BlockSpec((nm, D_XA), lambda i: (0, 0))
    return _pcall(body, grid=(t // tq,), in_specs=[row, kv, kv, row], out_specs=[row, kv, kv],
                  out_shape=[_sds((t, D_XA), BF16), _sds((nm, D_XA), F32), _sds((nm, D_XA), F32)],
                  compiler_params=_params(dimension_semantics=("arbitrary",)), name=name)(q, k, v, do)


FF_CW = 1408


def _swiglu_fwd(g, u, *, name):
    t, f = g.shape
    tr = _tile(t, 512)

    def body(g_ref, u_ref, o_ref):
        gv = g_ref[...].astype(F32)
        o_ref[...] = (gv * _sig(gv) * u_ref[...].astype(F32)).astype(o_ref.dtype)

    blk = pl.BlockSpec((tr, FF_CW), lambda i, j: (i, j))
    return _pcall(body, grid=(t // tr, f // FF_CW), in_specs=[blk, blk], out_specs=blk, out_shape=_sds((t, f), BF16),
                  compiler_params=_params(dimension_semantics=("parallel", "parallel")), name=name)(g, u)


def _swiglu_bwd(g, u, dact, *, name):
    t, f = g.shape
    tr = _tile(t, 512)

    def body(g_ref, u_ref, d_ref, dg_ref, du_ref):
        gv = g_ref[...].astype(F32)
        uv = u_ref[...].astype(F32)
        dv = d_ref[...].astype(F32)
        s = _sig(gv)
        dg_ref[...] = (dv * uv * (s * (1.0 + gv * (1.0 - s)))).astype(dg_ref.dtype)
        du_ref[...] = (dv * gv * s).astype(du_ref.dtype)

    blk = pl.BlockSpec((tr, FF_CW), lambda i, j: (i, j))
    return _pcall(body, grid=(t // tr, f // FF_CW), in_specs=[blk, blk, blk], out_specs=[blk, blk],
                  out_shape=[_sds((t, f), BF16), _sds((t, f), BF16)],
                  compiler_params=_params(dimension_semantics=("parallel", "parallel")), name=name)(g, u, dact)


def _row_tile(n):
    for cand in (128, 64, 32, 16):
        if n % cand == 0:
            return cand
    raise ValueError(n)


def _add_halves(g4, rb, c_arr, *, name):
    _, r, cdim = g4.shape
    hr = r // 2
    rt = _row_tile(hr)
    nb = hr // rt

    def body(c_ref, g_ref, rb_ref, o_ref):
        del c_ref
        o_ref[...] = (g_ref[...].astype(F32) + rb_ref[...].astype(F32)).astype(o_ref.dtype)

    gs = pltpu.PrefetchScalarGridSpec(
        num_scalar_prefetch=1, grid=(4, nb),
        in_specs=[pl.BlockSpec((None, rt, cdim), lambda q, i, c: (q, c[0] * nb + i, 0)),
                  pl.BlockSpec((None, rt, cdim), lambda q, i, c: (q, i, 0))],
        out_specs=pl.BlockSpec((None, rt, cdim), lambda q, i, c: (q, i, 0)))
    return _pcall(body, grid_spec=gs, out_shape=_sds((4, hr, cdim), BF16),
                  compiler_params=_params(dimension_semantics=("parallel", "parallel")), name=name)(c_arr, g4, rb)


def _sum_chips(p4, rc, q_arr, c_arr, *, name):
    _, hr, cdim = p4.shape
    rt = _row_tile(hr)
    nb = hr // rt

    def body(q_ref, c_ref, p_ref, rc_ref, o_ref):
        del q_ref, c_ref
        o_ref[...] = ((p_ref[...].astype(F32) + rc_ref[0].astype(F32)) + rc_ref[1].astype(F32)) + rc_ref[2].astype(F32)

    gs = pltpu.PrefetchScalarGridSpec(
        num_scalar_prefetch=2, grid=(nb,),
        in_specs=[pl.BlockSpec((None, rt, cdim), lambda i, q, c: (q[0], i, 0)),
                  pl.BlockSpec((3, rt, cdim), lambda i, q, c: (0, i, 0))],
        out_specs=pl.BlockSpec((rt, cdim), lambda i, q, c: (c[0] * nb + i, 0)))
    return _pcall(body, grid_spec=gs, out_shape=_sds((2 * hr, cdim), F32),
                  compiler_params=_params(dimension_semantics=("parallel",)), name=name)(q_arr, c_arr, p4, rc)


def _adam_math(w, g, m, v):
    m = ADAM_B1 * m + (1.0 - ADAM_B1) * g
    v = ADAM_B2 * v + (1.0 - ADAM_B2) * (g * g)
    m_hat = m / (1.0 - ADAM_B1 ** ADAM_STEP)
    v_hat = v / (1.0 - ADAM_B2 ** ADAM_STEP)
    delta = -ADAM_LR * (m_hat / (jnp.sqrt(v_hat) + ADAM_EPS) + ADAM_WD * w)
    return delta, m, v


def _adamw_layer(layer, w, m, v, g, prev, *, name):
    depth, r, cdim = w.shape
    rt = _row_tile(r)
    n_prev = 0 if prev is None else 4

    def body(*refs):
        w_ref, m_ref, v_ref, g_ref = refs[:4]
        go_ref, d_ref, mo_ref, vo_ref = refs[4 + n_prev:]
        gv = g_ref[...]
        delta, mn, vn = _adam_math(w_ref[...], gv, m_ref[...], v_ref[...])
        go_ref[...] = gv
        d_ref[...] = delta
        mo_ref[...] = mn
        vo_ref[...] = vn

    st = pl.BlockSpec((None, rt, cdim), lambda i: (layer, i, 0))
    in_specs = [st, st, st, pl.BlockSpec((rt, cdim), lambda i: (i, 0))] + [pl.BlockSpec(memory_space=pl.ANY)] * n_prev
    args = (w, m, v, g) + (tuple(prev) if prev is not None else ())
    return _pcall(body, grid=(r // rt,), in_specs=in_specs, out_specs=[st] * 4,
                  out_shape=[_sds((depth, r, cdim), F32)] * 4,
                  input_output_aliases={4 + i: i for i in range(n_prev)},
                  compiler_params=_params(dimension_semantics=("parallel",)), name=name)(*args)


def _adamw_flat(w, g, m, v, *, name):
    def body(w_ref, g_ref, m_ref, v_ref, d_ref, mo_ref, vo_ref):
        delta, mn, vn = _adam_math(w_ref[...], g_ref[...], m_ref[...], v_ref[...])
        d_ref[...] = delta
        mo_ref[...] = mn
        vo_ref[...] = vn

    return _pcall(body, out_shape=[_sds(w.shape, F32)] * 3, compiler_params=_params(), name=name)(w, g, m, v)


def _place():
    x, y, c = lax.axis_index("x"), lax.axis_index("y"), lax.axis_index("c")
    chips = [(1 - x, y), (x, 1 - y), (1 - x, 1 - y)]
    return x, y, c, chips


def _cast_place(w, q_arr, *, name):
    r, cdim = w.shape
    rt = _row_tile(r)

    def body(q_ref, w_ref, o_ref):
        del q_ref
        o_ref[...] = w_ref[...].astype(o_ref.dtype)

    gs = pltpu.PrefetchScalarGridSpec(
        num_scalar_prefetch=1, grid=(r // rt,),
        in_specs=[pl.BlockSpec((rt, cdim), lambda i, q: (i, 0))],
        out_specs=pl.BlockSpec((None, rt, cdim), lambda i, q: (q[0], i, 0)))
    return _pcall(body, grid_spec=gs, out_shape=_sds((4, r, cdim), BF16),
                  compiler_params=_params(dimension_semantics=("parallel",)), name=name)(q_arr, w)


HBM_SPEC = pl.BlockSpec(memory_space=pltpu.HBM)
SEM_SPEC = pl.BlockSpec(memory_space=pltpu.SEMAPHORE)
ANY_SPEC = pl.BlockSpec(memory_space=pl.ANY)
SPLIT_PARAMS = pltpu.CompilerParams(has_side_effects=pltpu.SideEffectType.DATAFLOW_SIDE_EFFECTING)


def _gather_copies(bufs, sems):
    n = len(bufs)
    x, y, c, chips = _place()
    qme = 2 * x + y
    cps = []
    for t in range(n):
        hr = bufs[t].shape[1] // 2
        mine = bufs[t].at[qme, pl.ds(c * hr, hr)]
        for k, (px, py) in enumerate(chips):
            landed = bufs[t].at[2 * px + py, pl.ds(c * hr, hr)]
            peer = dict(device_id=(px, py, c), device_id_type=MESH)
            cps.append((pltpu.make_async_remote_copy(src_ref=mine, dst_ref=mine, send_sem=sems[3 * t + k],
                                                     recv_sem=sems[3 * n + 3 * t + k], **peer),
                        pltpu.make_async_remote_copy(src_ref=mine, dst_ref=landed, send_sem=sems[3 * t + k],
                                                     recv_sem=sems[3 * n + 3 * t + k], **peer)))
    return cps


def _gather_start(bufs, after, *, name):
    n = len(bufs)

    def body(*refs):
        ins = refs[:n]
        sems = refs[n + 1:7 * n + 1]
        token = refs[8 * n + 1]
        for send, _ in _gather_copies(ins, sems):
            send.start()
        token[...] = jnp.zeros_like(token)

    res = _pcall(body, in_specs=[HBM_SPEC] * n + [ANY_SPEC],
                 out_specs=(SEM_SPEC,) * (6 * n) + (HBM_SPEC,) * n + (pl.BlockSpec(memory_space=pltpu.VMEM),),
                 out_shape=(pltpu.SemaphoreType.DMA(()),) * (6 * n) + tuple(pltpu.HBM(b.shape, b.dtype) for b in bufs)
                 + (_sds((8, 128), F32),),
                 input_output_aliases={t: 6 * n + t for t in range(n)}, compiler_params=SPLIT_PARAMS, name=name)(*bufs, after)
    return list(res[:6 * n]), list(res[6 * n:7 * n]), res[7 * n]


def _gather_wait(bufs, sems, after, *, name):
    n = len(bufs)

    def body(*refs):
        ins = refs[:n]
        sem_refs = refs[n:7 * n]
        for send, arrive in _gather_copies(ins, sem_refs):
            send.wait_send()
            arrive.wait_recv()

    return list(_pcall(body, in_specs=[HBM_SPEC] * n + [SEM_SPEC] * (6 * n) + [ANY_SPEC], out_specs=(HBM_SPEC,) * n,
                       out_shape=tuple(pltpu.HBM(b.shape, b.dtype) for b in bufs),
                       input_output_aliases={t: t for t in range(n)}, compiler_params=SPLIT_PARAMS, name=name)(*bufs, *sems, after))


def _gather_forward(bufs, *, name):
    n = len(bufs)

    def body(*refs):
        outs = refs[n:2 * n]
        send, recv = refs[2 * n:]
        x, y, c, chips = _place()
        sib = dict(device_id=(x, y, 1 - c), device_id_type=MESH)
        cps = []
        for t in range(n):
            hr = outs[t].shape[1] // 2
            for k, (px, py) in enumerate(chips):
                landed = outs[t].at[2 * px + py, pl.ds(c * hr, hr)]
                cp = pltpu.make_async_remote_copy(src_ref=landed, dst_ref=landed, send_sem=send.at[t, k],
                                                  recv_sem=recv.at[t, k], **sib)
                cp.start()
                cps.append(cp)
        for t in range(n):
            hr = outs[t].shape[1] // 2
            for k, (px, py) in enumerate(chips):
                other = outs[t].at[2 * px + py, pl.ds((1 - c) * hr, hr)]
                pltpu.make_async_remote_copy(src_ref=other, dst_ref=other, send_sem=send.at[t, k], recv_sem=recv.at[t, k],
                                             **sib).wait_recv()
        for cp in cps:
            cp.wait_send()

    return _pcall(body, in_specs=[ANY_SPEC] * n, out_specs=[ANY_SPEC] * n,
                  out_shape=[_sds(b.shape, b.dtype) for b in bufs], input_output_aliases={t: t for t in range(n)},
                  scratch_shapes=[pltpu.SemaphoreType.DMA((n, 3))] * 2,
                  compiler_params=pltpu.CompilerParams(has_side_effects=True), name=name)(*bufs)


def _owner_copies(parts, lands, sems):
    n = len(parts)
    _, _, c, chips = _place()
    cps = []
    for t in range(n):
        for k, (px, py) in enumerate(chips):
            cps.append(pltpu.make_async_remote_copy(src_ref=parts[t].at[2 * px + py], dst_ref=lands[t].at[k],
                                                    send_sem=sems[3 * t + k], recv_sem=sems[3 * n + 3 * t + k],
                                                    device_id=(px, py, c), device_id_type=MESH))
    return cps


def _owners_start(parts, *, name):
    n = len(parts)
    lands = [pltpu.with_memory_space_constraint(lax.empty((3,) + p.shape[1:], p.dtype), pltpu.HBM) for p in parts]

    def body(*refs):
        ins, lnd = refs[:n], refs[n:2 * n]
        sems = refs[2 * n:8 * n]
        token = refs[10 * n]
        for cp in _owner_copies(ins, lnd, sems):
            cp.start()
        token[...] = jnp.zeros_like(token)

    res = _pcall(body, in_specs=[HBM_SPEC] * (2 * n),
                 out_specs=(SEM_SPEC,) * (6 * n) + (HBM_SPEC,) * (2 * n) + (pl.BlockSpec(memory_space=pltpu.VMEM),),
                 out_shape=(pltpu.SemaphoreType.DMA(()),) * (6 * n)
                 + tuple(pltpu.HBM(b.shape, b.dtype) for b in list(parts) + lands) + (_sds((8, 128), F32),),
                 input_output_aliases={t: 6 * n + t for t in range(2 * n)}, compiler_params=SPLIT_PARAMS, name=name)(*parts, *lands)
    return list(res[:6 * n]), list(res[6 * n:7 * n]), list(res[7 * n:8 * n]), res[8 * n]


def _owners_wait(parts, lands, sems, after, *, name):
    n = len(parts)

    def body(*refs):
        ins, lnd = refs[:n], refs[n:2 * n]
        sem_refs = refs[2 * n:8 * n]
        for cp in _owner_copies(ins, lnd, sem_refs):
            cp.wait_send()
            cp.wait_recv()

    res = _pcall(body, in_specs=[HBM_SPEC] * (2 * n) + [SEM_SPEC] * (6 * n) + [ANY_SPEC], out_specs=(HBM_SPEC,) * (2 * n),
                 out_shape=tuple(pltpu.HBM(b.shape, b.dtype) for b in list(parts) + list(lands)),
                 input_output_aliases={t: t for t in range(2 * n)}, compiler_params=SPLIT_PARAMS, name=name)(
                     *parts, *lands, *sems, after)
    return list(res[:n]), list(res[n:])


def _gather_weights(bufs, *, name):
    n = len(bufs)

    def body(*refs):
        outs = refs[n:2 * n]
        send1, recv1, send2, recv2 = refs[2 * n:]
        x, y, c, chips = _place()
        qme = 2 * x + y
        sib = (x, y, 1 - c)

        def half(t, which):
            hr = outs[t].shape[1] // 2
            return pl.ds(which * hr, hr)

        first = []
        for t in range(n):
            mine = outs[t].at[qme, half(t, c)]
            for k, (px, py) in enumerate(chips):
                cp = pltpu.make_async_remote_copy(src_ref=mine, dst_ref=mine, send_sem=send1.at[t, k], recv_sem=recv1.at[t, k],
                                                  device_id=(px, py, c), device_id_type=MESH)
                cp.start()
                first.append(cp)
        second = []
        for t in range(n):
            for k, (px, py) in enumerate(chips):
                landed = outs[t].at[2 * px + py, half(t, c)]
                pltpu.make_async_remote_copy(src_ref=landed, dst_ref=landed, send_sem=send1.at[t, k], recv_sem=recv1.at[t, k],
                                             device_id=(px, py, c), device_id_type=MESH).wait_recv()
                cp = pltpu.make_async_remote_copy(src_ref=landed, dst_ref=landed, send_sem=send2.at[t, k],
                                                  recv_sem=recv2.at[t, k], device_id=sib, device_id_type=MESH)
                cp.start()
                second.append(cp)
        for t in range(n):
            for k, (px, py) in enumerate(chips):
                other = outs[t].at[2 * px + py, half(t, 1 - c)]
                pltpu.make_async_remote_copy(src_ref=other, dst_ref=other, send_sem=send2.at[t, k],
                                             recv_sem=recv2.at[t, k], device_id=sib, device_id_type=MESH).wait_recv()
        for cp in first + second:
            cp.wait_send()

    anyspec = pl.BlockSpec(memory_space=pl.ANY)
    return _pcall(body, in_specs=[anyspec] * n, out_specs=[anyspec] * n,
                  out_shape=[_sds(b.shape, b.dtype) for b in bufs], input_output_aliases={t: t for t in range(n)},
                  scratch_shapes=[pltpu.SemaphoreType.DMA((n, 3))] * 4,
                  compiler_params=pltpu.CompilerParams(has_side_effects=True), name=name)(*bufs)


def _swap_halves(grads, *, name):
    n = len(grads)

    def body(*refs):
        ins, outs = refs[:n], refs[n:2 * n]
        send, recv = refs[2 * n:]
        x, y, c, _ = _place()
        cps = []
        for t in range(n):
            hr = ins[t].shape[1] // 2
            cp = pltpu.make_async_remote_copy(src_ref=ins[t].at[:, pl.ds((1 - c) * hr, hr), :], dst_ref=outs[t],
                                              send_sem=send.at[t], recv_sem=recv.at[t],
                                              device_id=(x, y, 1 - c), device_id_type=MESH)
            cp.start()
            cps.append(cp)
        for cp in cps:
            cp.wait()

    anyspec = pl.BlockSpec(memory_space=pl.ANY)
    return _pcall(body, in_specs=[anyspec] * n, out_specs=[anyspec] * n,
                  out_shape=[_sds((4, g.shape[1] // 2, g.shape[2]), g.dtype) for g in grads],
                  scratch_shapes=[pltpu.SemaphoreType.DMA((n,))] * 2,
                  compiler_params=pltpu.CompilerParams(has_side_effects=True), name=name)(*grads)


def _send_to_owners(parts, *, name):
    n = len(parts)

    def body(*refs):
        ins, outs = refs[:n], refs[n:2 * n]
        send, recv = refs[2 * n:]
        x, y, c, chips = _place()
        cps = []
        for t in range(n):
            for k, (px, py) in enumerate(chips):
                cp = pltpu.make_async_remote_copy(src_ref=ins[t].at[2 * px + py], dst_ref=outs[t].at[k],
                                                  send_sem=send.at[t, k], recv_sem=recv.at[t, k],
                                                  device_id=(px, py, c), device_id_type=MESH)
                cp.start()
                cps.append(cp)
        for cp in cps:
            cp.wait()

    anyspec = pl.BlockSpec(memory_space=pl.ANY)
    return _pcall(body, in_specs=[anyspec] * n, out_specs=[anyspec] * n,
                  out_shape=[_sds((3,) + p.shape[1:], p.dtype) for p in parts],
                  scratch_shapes=[pltpu.SemaphoreType.DMA((n, 3))] * 2,
                  compiler_params=pltpu.CompilerParams(has_side_effects=True), name=name)(*parts)


def _join_halves(bufs, *, name):
    n = len(bufs)

    def body(*refs):
        outs = refs[n:2 * n]
        send, recv = refs[2 * n:]
        x, y, c, _ = _place()
        cps = []
        for t in range(n):
            hr = outs[t].shape[0] // 2
            mine = outs[t].at[pl.ds(c * hr, hr)]
            cp = pltpu.make_async_remote_copy(src_ref=mine, dst_ref=mine, send_sem=send.at[t], recv_sem=recv.at[t],
                                              device_id=(x, y, 1 - c), device_id_type=MESH)
            cp.start()
            cps.append(cp)
        for t in range(n):
            hr = outs[t].shape[0] // 2
            theirs = outs[t].at[pl.ds((1 - c) * hr, hr)]
            pltpu.make_async_remote_copy(src_ref=theirs, dst_ref=theirs, send_sem=send.at[t], recv_sem=recv.at[t],
                                         device_id=(x, y, 1 - c), device_id_type=MESH).wait_recv()
        for cp in cps:
            cp.wait_send()

    anyspec = pl.BlockSpec(memory_space=pl.ANY)
    return _pcall(body, in_specs=[anyspec] * n, out_specs=[anyspec] * n,
                  out_shape=[_sds(b.shape, b.dtype) for b in bufs], input_output_aliases={t: t for t in range(n)},
                  scratch_shapes=[pltpu.SemaphoreType.DMA((n,))] * 2,
                  compiler_params=pltpu.CompilerParams(has_side_effects=True), name=name)(*bufs)


def _allreduce_small(buf, *, name):
    rows = buf.shape[0]
    rels = [(dx, dy, dc) for dx in (0, 1) for dy in (0, 1) for dc in (0, 1)][1:]

    def body(in_ref, out_ref, gbuf, send, recv):
        x, y, c = lax.axis_index("x"), lax.axis_index("y"), lax.axis_index("c")
        me = 4 * x + 2 * y + c
        gbuf[me] = in_ref[...]
        cps = []
        for k, (dx, dy, dc) in enumerate(rels):
            peer = (x + dx - 2 * x * dx, y + dy - 2 * y * dy, c + dc - 2 * c * dc)
            cp = pltpu.make_async_remote_copy(src_ref=in_ref, dst_ref=gbuf.at[me], send_sem=send.at[k], recv_sem=recv.at[k],
                                              device_id=peer, device_id_type=MESH)
            cp.start()
            cps.append(cp)
        for k, (dx, dy, dc) in enumerate(rels):
            px, py, pc = x + dx - 2 * x * dx, y + dy - 2 * y * dy, c + dc - 2 * c * dc
            pltpu.make_async_remote_copy(src_ref=in_ref, dst_ref=gbuf.at[4 * px + 2 * py + pc], send_sem=send.at[k],
                                         recv_sem=recv.at[k], device_id=(px, py, pc), device_id_type=MESH).wait_recv()
        for cp in cps:
            cp.wait_send()
        acc = gbuf[0]
        for d in range(1, 8):
            acc = acc + gbuf[d]
        out_ref[...] = acc

    vm = pl.BlockSpec(memory_space=pltpu.VMEM)
    return _pcall(body, in_specs=[vm], out_specs=vm, out_shape=_sds(buf.shape, F32),
                  scratch_shapes=[pltpu.VMEM((8, rows, 128), F32), pltpu.SemaphoreType.DMA((7,)), pltpu.SemaphoreType.DMA((7,))],
                  compiler_params=_params(has_side_effects=True), name=name)(buf)


def _pack(arrs):
    flat = jnp.concatenate([a.reshape(-1).astype(F32) for a in arrs])
    n = flat.shape[0]
    rows = -(-n // 1024) * 8
    return jnp.pad(flat, (0, rows * 128 - n)).reshape(rows, 128)


def _unpack(buf, shapes):
    flat = buf.reshape(-1)
    out, o = [], 0
    for s in shapes:
        n = 1
        for d in s:
            n *= d
        out.append(flat[o:o + n].reshape(s))
        o += n
    return out


def _perm_xbc(a):
    parts = []
    for g in range(N_GROUPS):
        parts += [a[..., 512 * g:512 * (g + 1)], a[..., 2048 + 128 * g:2048 + 128 * (g + 1)],
                  a[..., 2560 + 128 * g:2560 + 128 * (g + 1)]]
    return jnp.concatenate(parts, axis=-1)


def _unperm_xbc(a):
    xs = [a[..., GROUP_W * g:GROUP_W * g + 512] for g in range(N_GROUPS)]
    bs = [a[..., GROUP_W * g + 512:GROUP_W * g + 640] for g in range(N_GROUPS)]
    cs = [a[..., GROUP_W * g + 640:GROUP_W * (g + 1)] for g in range(N_GROUPS)]
    return jnp.concatenate(xs + bs + cs, axis=-1)


def _split_w_in(w4):
    k = w4.shape[1]
    nat = jnp.transpose(w4, (1, 0, 2)).reshape(k, -1)
    w_z = nat[:, :2048]
    w_xbc = _perm_xbc(nat[:, 2048:5120])
    w_dt = jnp.pad(nat[:, 5120:5152], ((0, 0), (0, DT_PAD - HEADS)))
    w_s = nat[:, 5152:]
    return w_z, w_xbc, w_dt, w_s


def _join_dw_in(dw_z, dw_xbc, dw_dt, dw_s):
    k = dw_z.shape[0]
    nat = jnp.concatenate([dw_z, _unperm_xbc(dw_xbc), dw_dt[:, :HEADS], dw_s], axis=1)
    return jnp.transpose(nat.reshape(k, 4, -1), (1, 0, 2))


def kernel(x, mem, norm_mix, w_in, ssd_conv_w, ssd_conv_b, dt_bias, a_log, d_skip, ssd_norm, sc_conv_w, sc_norm, w_out, mem_norm, norm_xa, w_q, w_k, w_v, w_o, norm_ffn, w_gate, w_up, w_down, norm_final, loss_target, m_norm_mix, m_w_in, m_ssd_conv_w, m_ssd_conv_b, m_dt_bias, m_a_log, m_d_skip, m_ssd_norm, m_sc_conv_w, m_sc_norm, m_w_out, m_mem_norm, m_norm_xa, m_w_q, m_w_k, m_w_v, m_w_o, m_norm_ffn, m_w_gate, m_w_up, m_w_down, m_norm_final, v_norm_mix, v_w_in, v_ssd_conv_w, v_ssd_conv_b, v_dt_bias, v_a_log, v_d_skip, v_ssd_norm, v_sc_conv_w, v_sc_norm, v_w_out, v_mem_norm, v_norm_xa, v_w_q, v_w_k, v_w_v, v_w_o, v_norm_ffn, v_w_gate, v_w_up, v_w_down, v_norm_final):
    depth = w_in.shape[0]
    ix, iy, ic = lax.axis_index("x"), lax.axis_index("y"), lax.axis_index("c")
    qme = 2 * ix + iy
    c_arr = jnp.reshape(ic, (1,)).astype(jnp.int32)
    q_arr = jnp.reshape(qme, (1,)).astype(jnp.int32)
    h = x[0]
    tgt = loss_target[0]

    big = dict(w_in=w_in, w_out=w_out, w_q=w_q, w_k=w_k, w_v=w_v, w_o=w_o, w_gate=w_gate, w_up=w_up, w_down=w_down)
    big_m = dict(w_in=m_w_in, w_out=m_w_out, w_q=m_w_q, w_k=m_w_k, w_v=m_w_v, w_o=m_w_o, w_gate=m_w_gate, w_up=m_w_up, w_down=m_w_down)
    big_v = dict(w_in=v_w_in, w_out=v_w_out, w_q=v_w_q, w_k=v_w_k, w_v=v_w_v, w_o=v_w_o, w_gate=v_w_gate, w_up=v_w_up, w_down=v_w_down)
    names = list(big)

    conv_full = jnp.zeros((depth, 4, D_XBC), F32)
    conv_full = lax.dynamic_update_slice(conv_full, jnp.where(ic == 0, ssd_conv_w, 0.0), (0, 0, qme * (D_XBC // 4)))
    sc_full = jnp.zeros((depth, 3, D_SC), F32)
    sc_full = lax.dynamic_update_slice(sc_full, jnp.where(ic == 0, sc_conv_w, 0.0), (0, 0, qme * (D_SC // 4)))
    conv_full, sc_full = _unpack(_allreduce_small(_pack([conv_full, sc_full]), name="gather_conv_w"),
                                 [conv_full.shape, sc_full.shape])
    conv_p = _perm_xbc(conv_full)
    convb_p = _perm_xbc(ssd_conv_b)

    pad_h = lambda a: jnp.pad(a, ((0, 0), (0, DT_PAD - HEADS)))
    dt_bias_p, a_log_p = pad_h(dt_bias), pad_h(a_log)
    dskip_ch = jnp.repeat(d_skip, D_SSD // HEADS, axis=1)

    placed = [[_cast_place(big[n][l], q_arr, name=f"cast_{n}{l}") for n in names] for l in range(depth)]

    def finish_gather(l, sems, flying, after):
        landed = _gather_wait(flying, sems, after, name=f"gather_wait{l}")
        return dict(zip(names, _gather_forward(landed, name=f"gather_fwd{l}")))

    sems, flying, token = _gather_start(placed[0], q_arr, name="gather_start0")
    gathered = [finish_gather(0, sems, flying, token)]

    memn = _rms_fwd(mem[0], mem_norm.reshape(1, D), name="memn")

    saved = []
    for l in range(depth):
        gw = gathered[l]
        gain = norm_mix[l:l + 1]
        if l + 1 < depth:
            sems, flying, token = _gather_start(placed[l + 1], gw["w_out"], name=f"gather_start{l + 1}")
            gain = gain + token[:1, :1]
        w_z, w_xbc, w_dt, w_s = _split_w_in(gw["w_in"])
        wo2 = gw["w_out"].reshape(-1, D)
        wq2, wk2, wv2 = (gw[n].reshape(D, D_XA) for n in ("w_q", "w_k", "w_v"))
        wd2 = gw["w_down"].reshape(D_FF, D)
        s = dict(h0=h, w_z=w_z, w_xbc=w_xbc, w_dt=w_dt, w_s=w_s, wo2=wo2, wq2=wq2, wk2=wk2, wv2=wv2, wd2=wd2)
        hn1 = _rms_fwd(h, gain, name=f"rms_mix{l}")
        pz = _mm_nn(hn1, w_z, tn=1024, name=f"proj_z{l}")
        pxbc = _mm_nn(hn1, w_xbc, tn=1024, name=f"proj_xbc{l}")
        dtr = _mm_nn(hn1, w_dt, tn=DT_PAD, out_dtype=F32, name=f"proj_dt{l}")
        t = h.shape[0]
        tm = _tile(t, 1024)
        ps = _mm(hn1, w_s, mode="nn", grid=(t // tm, 6, 1),
                 a_spec=pl.BlockSpec((tm, D), lambda i, j, kk: (i, 0)),
                 b_spec=pl.BlockSpec((D, 1024), lambda i, j, kk: (0, j)),
                 o_spec=pl.BlockSpec((None, tm, 1024), lambda i, j, kk: (j // 2, i, j % 2)), o_tile=(tm, 1024),
                 out_sds=_sds((3, t, D_SC), BF16), name=f"proj_s{l}")
        xc = _conv_fwd(pxbc, conv_p[l], convb_p[l:l + 1], name=f"conv{l}")
        dt, dtg, acsg, acst = _dt_prep(dtr, dt_bias_p[l:l + 1], a_log_p[l:l + 1], name=f"dt_prep{l}")
        y, states, mix = _ssd_fwd(xc, dtg, acsg, acst, pz, dskip_ch[l:l + 1], ssd_norm[l:l + 1], name=f"ssd{l}")
        mix = _sc_fwd(ps, sc_full[l], sc_norm[l:l + 1], mix, name=f"sc{l}")
        h1 = _mm_nn(mix, wo2, tn=1024, tm=512, out_dtype=F32, add=h, name=f"out_proj{l}")
        hn2 = _rms_fwd(h1, norm_xa[l:l + 1], name=f"rms_xa{l}")
        q = _mm_nn(hn2, wq2, tn=D_XA, name=f"q{l}")
        k = _mm_nn(memn, wk2, tn=D_XA, name=f"k{l}")
        v = _mm_nn(memn, wv2, tn=D_XA, name=f"v{l}")
        o = _xa_fwd(q, k, v, name=f"xa{l}")
        h2 = _mm_nn_sm(o, gw["w_o"], out_dtype=F32, add=h1, name=f"xa_out{l}")
        hn3 = _rms_fwd(h2, norm_ffn[l:l + 1], name=f"rms_ffn{l}")
        fg = _mm_nn_sm(hn3, gw["w_gate"], name=f"ff_gate{l}")
        fu = _mm_nn_sm(hn3, gw["w_up"], name=f"ff_up{l}")
        act = _swiglu_fwd(fg, fu, name=f"swiglu{l}")
        h3 = _mm_nn(act, wd2, tn=1024, tm=512, out_dtype=F32, add=h2, name=f"ff_down{l}")
        s.update(hn1=hn1, pz=pz, pxbc=pxbc, dtr=dtr, ps=ps, xc=xc, dt=dt, dtg=dtg, acsg=acsg, acst=acst, y=y,
                 states=states, mix=mix, h1=h1, hn2=hn2, q=q, k=k, v=v, o=o, h2=h2, hn3=hn3, fg=fg, fu=fu, act=act)
        saved.append(s)
        h = h3
        if l + 1 < depth:
            gathered.append(finish_gather(l + 1, sems, flying, h3))

    loss_vec, dh, dhb, d_norm_final = _final(h, norm_final.reshape(1, D), tgt, name="final")
    loss = lax.psum(loss_vec[0, 0], ("x", "y", "c"))

    small = dict(norm_mix=[], ssd_conv_w=[], ssd_conv_b=[], dt_bias=[], a_log=[], d_skip=[], ssd_norm=[], sc_conv_w=[],
                 sc_norm=[], norm_xa=[], norm_ffn=[])
    dmemn = None
    carried = {n: None for n in names}
    pending = None

    def finish_reduce(pend, after):
        lyr, sems_r, parts_r, lands_r, _ = pend
        parts_r, lands_r = _owners_wait(parts_r, lands_r, sems_r, after, name=f"owners_wait{lyr}")
        halves = [_sum_chips(p, rc, q_arr, c_arr, name=f"sum_chips_{n}{lyr}") for n, p, rc in zip(names, parts_r, lands_r)]
        full = _join_halves(halves, name=f"join_halves{lyr}")
        for n, g in zip(names, full):
            carried[n] = _adamw_layer(lyr, big[n], big_m[n], big_v[n], g, carried[n], name=f"adamw_{n}{lyr}")

    for l in reversed(range(depth)):
        s, gw = saved[l], gathered[l]
        t = dh.shape[0]
        tm = _tile(t, 1024)
        dact = _mm_nt(dhb, s["wd2"], tn=FF_CW, dep=None if pending is None else pending[4], name=f"d_act{l}")
        dw_down = _mm_tn(s["act"], dhb, tm=FF_CW, tn=1024, name=f"dw_down{l}")
        dg, du = _swiglu_bwd(s["fg"], s["fu"], dact, name=f"d_swiglu{l}")
        dw_gate = _mm_tn_sm(s["hn3"], dg, tm=1024, name=f"dw_gate{l}")
        dw_up = _mm_tn_sm(s["hn3"], du, tm=1024, name=f"dw_up{l}")
        dhn = _mm_nt_sm(dg, gw["w_gate"], tn=1024, out_dtype=F32, name=f"d_hn3a{l}")
        dhn = _mm_nt_sm(du, gw["w_up"], tn=1024, out_dtype=F32, add=dhn, name=f"d_hn3b{l}")
        dh, dhb, dn = _rms_bwd(s["h2"], norm_ffn[l:l + 1], dhn, dh, name=f"d_rms_ffn{l}")
        small["norm_ffn"].append(dn)
        do = _mm_nt_sm(dhb, gw["w_o"], tn=D_XA, name=f"d_o{l}")
        dw_o = _mm_tn_sm(s["o"], dhb, tm=D_XA, name=f"dw_o{l}")
        dq, dk, dv = _xa_bwd(s["q"], s["k"], s["v"], do, name=f"d_xa{l}")
        dw_q = _mm_tn(s["hn2"], dq, tm=1024, tn=D_XA, name=f"dw_q{l}")
        dw_k = _mm_tn(memn, dk, tm=1024, tn=D_XA, name=f"dw_k{l}")
        dw_v = _mm_tn(memn, dv, tm=1024, tn=D_XA, name=f"dw_v{l}")
        dhn = _mm_nt(dq, s["wq2"], tn=1024, out_dtype=F32, name=f"d_hn2{l}")
        dmemn = _mm_nt(dk, s["wk2"], tn=1024, out_dtype=F32, add=dmemn, name=f"d_memn_k{l}")
        dmemn = _mm_nt(dv, s["wv2"], tn=1024, out_dtype=F32, add=dmemn, name=f"d_memn_v{l}")
        dh, dhb, dn = _rms_bwd(s["h1"], norm_xa[l:l + 1], dhn, dh, name=f"d_rms_xa{l}")
        small["norm_xa"].append(dn)
        dmix = _mm_nt(dhb, s["wo2"], tn=1024, name=f"d_mix{l}")
        dw_out = _mm_tn(s["mix"], dhb, tm=1024, tn=1024, name=f"dw_out{l}")
        dps, d_scw, d_scn = _sc_bwd(s["ps"], sc_full[l], sc_norm[l:l + 1], dmix, name=f"d_sc{l}")
        dxc, dz, ddtg, dacg, dart, ddsk, d_ssdn = _ssd_bwd(s["xc"], s["dtg"], s["acsg"], s["acst"], s["pz"], s["y"], s["states"],
                                                           dmix, dskip_ch[l:l + 1], ssd_norm[l:l + 1], name=f"d_ssd{l}")
        dxbc, d_cw, d_cb = _conv_bwd(dxc, s["pxbc"], conv_p[l], convb_p[l:l + 1], name=f"d_conv{l}")
        ddtr, d_dtb, d_alog = _dt_bwd(ddtg, dacg, dart, s["dt"], s["dtr"], dt_bias_p[l:l + 1], a_log_p[l:l + 1], name=f"d_dt{l}")
        small["sc_conv_w"].append(d_scw)
        small["sc_norm"].append(d_scn)
        small["ssd_norm"].append(d_ssdn)
        small["d_skip"].append(jnp.sum(ddsk.reshape(HEADS, D_SSD // HEADS), axis=1).reshape(1, HEADS))
        small["ssd_conv_w"].append(_unperm_xbc(d_cw))
        small["ssd_conv_b"].append(_unperm_xbc(d_cb))
        small["dt_bias"].append(d_dtb[:, :HEADS])
        small["a_log"].append(d_alog[:, :HEADS])
        hn1 = s["hn1"]
        dw_z = _mm_tn(hn1, dz, tm=1024, tn=1024, name=f"dw_z{l}")
        dw_xbc = _mm_tn(hn1, dxbc, tm=1024, tn=1024, name=f"dw_xbc{l}")
        dw_dt = _mm_tn(hn1, ddtr, tm=1024, tn=DT_PAD, name=f"dw_dt{l}")
        tk = _tile(t, TN_TK)
        dw_s = _mm(hn1, dps, mode="tn", grid=(2, 6, t // tk),
                   a_spec=pl.BlockSpec((tk, 1024), lambda i, j, kk: (kk, i)),
                   b_spec=pl.BlockSpec((None, tk, 1024), lambda i, j, kk: (j // 2, kk, j % 2)),
                   o_spec=pl.BlockSpec((1024, 1024), lambda i, j, kk: (i, j)), o_tile=(1024, 1024),
                   out_sds=_sds((D, 3 * D_SC), BF16), name=f"dw_s{l}")
        dhn = _mm_nt(dz, s["w_z"], tn=1024, out_dtype=F32, name=f"d_hn1z{l}")
        dhn = _mm_nt(dxbc, s["w_xbc"], tn=1024, out_dtype=F32, add=dhn, name=f"d_hn1x{l}")
        dhn = _mm_nt(ddtr, s["w_dt"], tn=1024, out_dtype=F32, add=dhn, name=f"d_hn1d{l}")
        dhn = _mm(dps, s["w_s"], mode="nt", grid=(t // tm, 2, 3),
                  a_spec=pl.BlockSpec((None, tm, D_SC), lambda i, j, kk: (kk, i, 0)),
                  b_spec=pl.BlockSpec((1024, D_SC), lambda i, j, kk: (j, kk)),
                  o_spec=pl.BlockSpec((tm, 1024), lambda i, j, kk: (i, j)), o_tile=(tm, 1024),
                  out_sds=_sds((t, D), F32), add=dhn, name=f"d_hn1s{l}")
        dh, dhb, dn = _rms_bwd(s["h0"], norm_mix[l:l + 1], dhn, dh, name=f"d_rms_mix{l}")
        small["norm_mix"].append(dn)

        grads = dict(w_in=_join_dw_in(dw_z, dw_xbc, dw_dt, dw_s), w_out=dw_out.reshape(4, -1, D),
                     w_q=dw_q.reshape(4, -1, D_XA), w_k=dw_k.reshape(4, -1, D_XA), w_v=dw_v.reshape(4, -1, D_XA),
                     w_o=dw_o, w_gate=dw_gate, w_up=dw_up, w_down=dw_down.reshape(4, -1, D))
        g_list = [grads[n] for n in names]
        if pending is not None:
            finish_reduce(pending, dh)
        recv_sib = _swap_halves(g_list, name=f"swap_halves{l}")
        parts = [_add_halves(g, rb, c_arr, name=f"add_halves_{n}{l}") for n, g, rb in zip(names, g_list, recv_sib)]
        pending = (l,) + _owners_start(parts, name=f"owners_start{l}")

    finish_reduce(pending, pending[4])
    grad_x = dh[None]

    _, _, d_mem_norm = _rms_bwd(mem[0], mem_norm.reshape(1, D), dmemn, jnp.zeros_like(dmemn), name="d_mem_norm")
    stack = lambda n: jnp.concatenate(small[n][::-1], axis=0) if small[n][0].ndim == 2 and small[n][0].shape[0] == 1 \
        else jnp.stack(small[n][::-1], axis=0)
    small_names = ["norm_mix", "ssd_conv_w", "ssd_conv_b", "dt_bias", "a_log", "d_skip", "ssd_norm", "sc_conv_w", "sc_norm",
                   "mem_norm", "norm_xa", "norm_ffn", "norm_final"]
    local_g = dict(mem_norm=d_mem_norm.reshape(D), norm_final=d_norm_final.reshape(D))
    for n in small:
        local_g[n] = stack(n)
    shapes = [local_g[n].shape for n in small_names]
    red = dict(zip(small_names, _unpack(_allreduce_small(_pack([local_g[n] for n in small_names]), name="allreduce_small"), shapes)))
    red["ssd_conv_w"] = lax.dynamic_slice(red["ssd_conv_w"], (0, 0, qme * (D_XBC // 4)), ssd_conv_w.shape)
    red["sc_conv_w"] = lax.dynamic_slice(red["sc_conv_w"], (0, 0, qme * (D_SC // 4)), sc_conv_w.shape)
    sw = dict(norm_mix=norm_mix, ssd_conv_w=ssd_conv_w, ssd_conv_b=ssd_conv_b, dt_bias=dt_bias, a_log=a_log, d_skip=d_skip,
              ssd_norm=ssd_norm, sc_conv_w=sc_conv_w, sc_norm=sc_norm, mem_norm=mem_norm, norm_xa=norm_xa, norm_ffn=norm_ffn,
              norm_final=norm_final)
    sm = dict(norm_mix=m_norm_mix, ssd_conv_w=m_ssd_conv_w, ssd_conv_b=m_ssd_conv_b, dt_bias=m_dt_bias, a_log=m_a_log,
              d_skip=m_d_skip, ssd_norm=m_ssd_norm, sc_conv_w=m_sc_conv_w, sc_norm=m_sc_norm, mem_norm=m_mem_norm,
              norm_xa=m_norm_xa, norm_ffn=m_norm_ffn, norm_final=m_norm_final)
    sv = dict(norm_mix=v_norm_mix, ssd_conv_w=v_ssd_conv_w, ssd_conv_b=v_ssd_conv_b, dt_bias=v_dt_bias, a_log=v_a_log,
              d_skip=v_d_skip, ssd_norm=v_ssd_norm, sc_conv_w=v_sc_conv_w, sc_norm=v_sc_norm, mem_norm=v_mem_norm,
              norm_xa=v_norm_xa, norm_ffn=v_norm_ffn, norm_final=v_norm_final)
    shard_shapes = [sw[n].shape for n in small_names]
    pk = lambda d: _pack([d[n] for n in small_names])
    sd, snm, snv = _adamw_flat(pk(sw), pk(red), pk(sm), pk(sv), name="adamw_small")
    s_delta = dict(zip(small_names, _unpack(sd, shard_shapes)))
    s_newm = dict(zip(small_names, _unpack(snm, shard_shapes)))
    s_newv = dict(zip(small_names, _unpack(snv, shard_shapes)))

    order = ["norm_mix", "w_in", "ssd_conv_w", "ssd_conv_b", "dt_bias", "a_log", "d_skip", "ssd_norm", "sc_conv_w", "sc_norm",
             "w_out", "mem_norm", "norm_xa", "w_q", "w_k", "w_v", "w_o", "norm_ffn", "w_gate", "w_up", "w_down", "norm_final"]

    def pick(kind):
        out = []
        for n in order:
            if n in carried:
                out.append(carried[n][kind])
            else:
                out.append([red, s_delta, s_newm, s_newv][kind][n])
        return out

    return (loss, grad_x, *pick(0), *pick(1), *pick(2), *pick(3))
```

```python
import functools

import jax
import jax.numpy as jnp
from jax import lax
from jax.experimental import pallas as pl
from jax.experimental.pallas import tpu as pltpu

F32 = jnp.float32
BF16 = jnp.bfloat16
MESH = pl.DeviceIdType.MESH

D = 2048
D_SSD = 2048
N_GROUPS = 4
GROUP_W = 768
D_XBC = 3072
N_STATE = 128
HEADS = 32
PAIRS_PER_GROUP = 4
CHUNK = 256
DT_PAD = 128
D_SC = 2048
SC_GROUP = 128
XA_HEADS = 4
XA_HD = 128
D_XA = 512
D_FF = 5632
EPS = 1e-5
HALO = 16
TN_TK = 2048
VMEM_LIMIT = 56 * 1024 * 1024

ADAM_LR, ADAM_B1, ADAM_B2, ADAM_EPS, ADAM_WD, ADAM_STEP = 0.001, 0.9, 0.999, 1e-08, 0.01, 10

NT = (((1,), (1,)), ((), ()))
TN = (((0,), (0,)), ((), ()))
NN = (((1,), (0,)), ((), ()))


def _pcall(body, **kw):
    return pl.pallas_call(body, **kw)


def _params(**kw):
    return pltpu.CompilerParams(vmem_limit_bytes=VMEM_LIMIT, **kw)


def _sds(shape, dtype):
    return jax.ShapeDtypeStruct(shape, dtype)


def _sig(x):
    return 1.0 / (1.0 + jnp.exp(-x))


def _dot(a, b, dims=NN):
    return lax.dot_general(a, b, dims, preferred_element_type=F32)


def _mm(a, b, *, mode, grid, a_spec, b_spec, o_spec, o_tile, out_sds, add=None, dep=None, name):
    gk = grid[2]
    dims = {"nn": NN, "nt": NT, "tn": TN, "nt4": NT}[mode]
    has_add = add is not None
    n_dep = 0 if dep is None else 1

    def body(*refs):
        a_ref, b_ref = refs[0], refs[1]
        add_ref = refs[2] if has_add else None
        refs = refs[:2 + has_add] + refs[2 + has_add + n_dep:]
        o_ref = refs[2 + has_add]
        if mode == "nt4":
            bv = jnp.concatenate([b_ref[s] for s in range(4)], axis=1)
        else:
            bv = b_ref[...].astype(BF16)
        p = _dot(a_ref[...].astype(BF16), bv, dims)

        def finish(acc):
            if has_add:
                acc = acc + add_ref[...]
            o_ref[...] = acc.astype(o_ref.dtype)

        if gk == 1:
            finish(p)
        else:
            acc_ref = refs[3 + has_add]
            k = pl.program_id(2)

            @pl.when(k == 0)
            def _():
                acc_ref[...] = p

            @pl.when(k > 0)
            def _():
                acc_ref[...] += p

            @pl.when(k == gk - 1)
            def _():
                finish(acc_ref[...])

    in_specs = [a_spec, b_spec] + ([o_spec] if has_add else []) + [pl.BlockSpec(memory_space=pl.ANY)] * n_dep
    args = (a, b) + ((add,) if has_add else ()) + ((dep,) if n_dep else ())
    scratch = [pltpu.VMEM(o_tile, F32)] if gk > 1 else []
    return _pcall(body, grid=grid, in_specs=in_specs, out_specs=o_spec, out_shape=out_sds, scratch_shapes=scratch,
                  compiler_params=_params(dimension_semantics=("parallel", "parallel", "arbitrary")), name=name)(*args)


def _tile(n, pref):
    t = min(n, pref)
    assert n % t == 0, (n, pref)
    return t


def _mm_nn(a, w, *, tn, tk=None, tm=1024, out_dtype=BF16, add=None, name):
    m, k = a.shape
    n = w.shape[1]
    tm = _tile(m, tm)
    tk = k if tk is None else tk
    grid = (m // tm, n // tn, k // tk)
    return _mm(a, w, mode="nn", grid=grid,
               a_spec=pl.BlockSpec((tm, tk), lambda i, j, kk: (i, kk)),
               b_spec=pl.BlockSpec((tk, tn), lambda i, j, kk: (kk, j)),
               o_spec=pl.BlockSpec((tm, tn), lambda i, j, kk: (i, j)), o_tile=(tm, tn),
               out_sds=_sds((m, n), out_dtype), add=add, name=name)


def _mm_nn_sm(a, w4, *, out_dtype=BF16, add=None, name):
    m, k = a.shape
    n = w4.shape[2]
    tm = _tile(m, 1024)
    return _mm(a, w4, mode="nn", grid=(m // tm, 4, 1),
               a_spec=pl.BlockSpec((tm, k), lambda i, j, kk: (i, 0)),
               b_spec=pl.BlockSpec((None, k, n), lambda i, j, kk: (j, 0, 0)),
               o_spec=pl.BlockSpec((tm, n), lambda i, j, kk: (i, j)), o_tile=(tm, n),
               out_sds=_sds((m, 4 * n), out_dtype), add=add, name=name)


def _mm_nt(a, w, *, tn, tk=None, out_dtype=BF16, add=None, dep=None, name):
    m, k = a.shape
    n = w.shape[0]
    tm = _tile(m, 1024)
    tk = k if tk is None else tk
    grid = (m // tm, n // tn, k // tk)
    return _mm(a, w, mode="nt", grid=grid,
               a_spec=pl.BlockSpec((tm, tk), lambda i, j, kk: (i, kk)),
               b_spec=pl.BlockSpec((tn, tk), lambda i, j, kk: (j, kk)),
               o_spec=pl.BlockSpec((tm, tn), lambda i, j, kk: (i, j)), o_tile=(tm, tn),
               out_sds=_sds((m, n), out_dtype), add=add, dep=dep, name=name)


def _mm_nt_sm(a, w4, *, tn, out_dtype=BF16, add=None, dep=None, name):
    m = a.shape[0]
    _, k, n = w4.shape
    tm = _tile(m, 512)
    tn = _tile(k, tn)
    return _mm(a, w4, mode="nt4", grid=(m // tm, k // tn, 1),
               a_spec=pl.BlockSpec((tm, 4 * n), lambda i, j, kk: (i, 0)),
               b_spec=pl.BlockSpec((4, tn, n), lambda i, j, kk: (0, j, 0)),
               o_spec=pl.BlockSpec((tm, tn), lambda i, j, kk: (i, j)), o_tile=(tm, tn),
               out_sds=_sds((m, k), out_dtype), add=add, dep=dep, name=name)


def _mm_tn(a, g, *, tm, tn, out_dtype=BF16, name):
    t, m = a.shape
    n = g.shape[1]
    tk = _tile(t, TN_TK)
    return _mm(a, g, mode="tn", grid=(m // tm, n // tn, t // tk),
               a_spec=pl.BlockSpec((tk, tm), lambda i, j, kk: (kk, i)),
               b_spec=pl.BlockSpec((tk, tn), lambda i, j, kk: (kk, j)),
               o_spec=pl.BlockSpec((tm, tn), lambda i, j, kk: (i, j)), o_tile=(tm, tn),
               out_sds=_sds((m, n), out_dtype), name=name)


def _mm_tn_sm(a, g, *, tm, out_dtype=BF16, name):
    t, m = a.shape
    n = g.shape[1] // 4
    tk = _tile(t, TN_TK)
    return _mm(a, g, mode="tn", grid=(m // tm, 4, t // tk),
               a_spec=pl.BlockSpec((tk, tm), lambda i, j, kk: (kk, i)),
               b_spec=pl.BlockSpec((tk, n), lambda i, j, kk: (kk, j)),
               o_spec=pl.BlockSpec((None, tm, n), lambda i, j, kk: (j, i, 0)), o_tile=(tm, n),
               out_sds=_sds((4, m, n), out_dtype), name=name)


def _rms_fwd(h, g, *, name):
    t, d = h.shape
    tr = _tile(t, 512)

    def body(h_ref, g_ref, o_ref):
        x = h_ref[...]
        r = lax.rsqrt(jnp.mean(x * x, axis=-1, keepdims=True) + EPS)
        o_ref[...] = (x * r * g_ref[...]).astype(o_ref.dtype)

    return _pcall(body, grid=(t // tr,),
                  in_specs=[pl.BlockSpec((tr, d), lambda i: (i, 0)), pl.BlockSpec((1, d), lambda i: (0, 0))],
                  out_specs=pl.BlockSpec((tr, d), lambda i: (i, 0)), out_shape=_sds((t, d), BF16),
                  compiler_params=_params(dimension_semantics=("parallel",)), name=name)(h, g)


def _rms_bwd(h, g, dy, dres, *, name):
    t, d = h.shape
    tr = _tile(t, 256)

    def body(h_ref, g_ref, dy_ref, dres_ref, dh_ref, dhb_ref, dg_ref):
        i = pl.program_id(0)
        x = h_ref[...]
        r = lax.rsqrt(jnp.mean(x * x, axis=-1, keepdims=True) + EPS)
        xh = x * r
        dyv = dy_ref[...].astype(F32)
        dxh = dyv * g_ref[...]
        dh = dres_ref[...] + r * (dxh - xh * jnp.mean(dxh * xh, axis=-1, keepdims=True))
        dh_ref[...] = dh
        dhb_ref[...] = dh.astype(BF16)
        part = jnp.sum(dyv * xh, axis=0, keepdims=True)

        @pl.when(i == 0)
        def _():
            dg_ref[...] = part

        @pl.when(i > 0)
        def _():
            dg_ref[...] += part

    row = pl.BlockSpec((tr, d), lambda i: (i, 0))
    vec = pl.BlockSpec((1, d), lambda i: (0, 0))
    return _pcall(body, grid=(t // tr,), in_specs=[row, vec, row, row], out_specs=[row, row, vec],
                  out_shape=[_sds((t, d), F32), _sds((t, d), BF16), _sds((1, d), F32)],
                  compiler_params=_params(dimension_semantics=("arbitrary",)), name=name)(h, g, dy, dres)


def _final(h, g, tgt, *, name):
    t, d = h.shape
    tr = _tile(t, 256)

    def body(h_ref, g_ref, t_ref, loss_ref, dh_ref, dhb_ref, dg_ref):
        i = pl.program_id(0)
        x = h_ref[...]
        gv = g_ref[...]
        r = lax.rsqrt(jnp.mean(x * x, axis=-1, keepdims=True) + EPS)
        xh = x * r
        e = xh * gv - t_ref[...]
        lpart = jnp.zeros((1, 128), F32) + 0.5 * jnp.sum(jnp.mean(e * e, axis=-1, keepdims=True))
        dyv = e * (1.0 / d)
        dxh = dyv * gv
        dh = r * (dxh - xh * jnp.mean(dxh * xh, axis=-1, keepdims=True))
        dh_ref[...] = dh
        dhb_ref[...] = dh.astype(BF16)
        part = jnp.sum(dyv * xh, axis=0, keepdims=True)

        @pl.when(i == 0)
        def _():
            dg_ref[...] = part
            loss_ref[...] = lpart

        @pl.when(i > 0)
        def _():
            dg_ref[...] += part
            loss_ref[...] += lpart

    row = pl.BlockSpec((tr, d), lambda i: (i, 0))
    vec = pl.BlockSpec((1, d), lambda i: (0, 0))
    return _pcall(body, grid=(t // tr,), in_specs=[row, vec, row],
                  out_specs=[pl.BlockSpec((1, 128), lambda i: (0, 0)), row, row, vec],
                  out_shape=[_sds((1, 128), F32), _sds((t, d), F32), _sds((t, d), BF16), _sds((1, d), F32)],
                  compiler_params=_params(dimension_semantics=("arbitrary",)), name=name)(h, g, tgt)


def _conv_taps(ext, w, ntap, rows):
    n = ext.shape[0]
    acc = w[ntap - 1:ntap, :] * ext[HALO:HALO + rows]
    for k in range(1, ntap):
        acc = acc + w[ntap - 1 - k:ntap - k, :] * pltpu.roll(ext, k, axis=0)[HALO:HALO + rows]
    del n
    return acc


def _conv_fwd(xbc, w, b, *, name):
    t, c = xbc.shape
    rows = CHUNK
    cw = GROUP_W
    hb = rows // HALO

    def body(cur_ref, prev_ref, w_ref, b_ref, o_ref):
        i = pl.program_id(1)
        cur = cur_ref[...].astype(F32)
        prev = jnp.where(i > 0, prev_ref[...].astype(F32), 0.0)
        ext = jnp.concatenate([prev, cur], axis=0)
        pre = _conv_taps(ext, w_ref[...], 4, rows) + b_ref[...]
        o_ref[...] = (pre * _sig(pre)).astype(o_ref.dtype)

    return _pcall(body, grid=(c // cw, t // rows),
                  in_specs=[pl.BlockSpec((rows, cw), lambda j, i: (i, j)),
                            pl.BlockSpec((HALO, cw), lambda j, i: (jnp.maximum(i * hb - 1, 0), j)),
                            pl.BlockSpec((4, cw), lambda j, i: (0, j)),
                            pl.BlockSpec((1, cw), lambda j, i: (0, j))],
                  out_specs=pl.BlockSpec((rows, cw), lambda j, i: (i, j)), out_shape=_sds((t, c), BF16),
                  compiler_params=_params(dimension_semantics=("parallel", "parallel")), name=name)(xbc, xbc, w, b)


def _conv_bwd(dxc, xbc, w, b, *, name):
    t, c = xbc.shape
    rows = CHUNK
    cw = GROUP_W
    hb = rows // HALO
    nblk = t // rows
    nhalo = t // HALO

    def body(d_ref, dn_ref, cur_ref, prev_ref, next_ref, w_ref, b_ref, dx_ref, dw_ref, db_ref):
        i = pl.program_id(1)
        last = i == nblk - 1
        wv = w_ref[...]
        xe = jnp.concatenate([jnp.where(i > 0, prev_ref[...].astype(F32), 0.0), cur_ref[...].astype(F32),
                              jnp.where(last, 0.0, next_ref[...].astype(F32))], axis=0)
        n = rows + 2 * HALO
        sh = [xe] + [pltpu.roll(xe, k, axis=0) for k in range(1, 4)]
        pre = wv[3:4, :] * sh[0] + wv[2:3, :] * sh[1] + wv[1:2, :] * sh[2] + wv[0:1, :] * sh[3] + b_ref[...]
        de = jnp.concatenate([jnp.zeros((HALO, cw), F32), d_ref[...].astype(F32),
                              jnp.where(last, 0.0, dn_ref[...].astype(F32))], axis=0)
        s = _sig(pre)
        dpre = de * (s * (1.0 + pre * (1.0 - s)))
        dx = wv[3:4, :] * dpre
        for m in range(1, 4):
            dx = dx + wv[3 - m:4 - m, :] * pltpu.roll(dpre, n - m, axis=0)
        dx_ref[...] = dx[HALO:HALO + rows].astype(dx_ref.dtype)
        dcur = dpre[HALO:HALO + rows]
        dwv = jnp.concatenate([jnp.sum(dcur * sh[3 - j][HALO:HALO + rows], axis=0, keepdims=True) for j in range(4)], axis=0)
        dbv = jnp.sum(dcur, axis=0, keepdims=True)

        @pl.when(i == 0)
        def _():
            dw_ref[...] = dwv
            db_ref[...] = dbv

        @pl.when(i > 0)
        def _():
            dw_ref[...] += dwv
            db_ref[...] += dbv

    cur = pl.BlockSpec((rows, cw), lambda j, i: (i, j))
    prev = pl.BlockSpec((HALO, cw), lambda j, i: (jnp.maximum(i * hb - 1, 0), j))
    nxt = pl.BlockSpec((HALO, cw), lambda j, i: (jnp.minimum((i + 1) * hb, nhalo - 1), j))
    return _pcall(body, grid=(c // cw, nblk),
                  in_specs=[cur, nxt, cur, prev, nxt, pl.BlockSpec((4, cw), lambda j, i: (0, j)),
                            pl.BlockSpec((1, cw), lambda j, i: (0, j))],
                  out_specs=[cur, pl.BlockSpec((4, cw), lambda j, i: (0, j)), pl.BlockSpec((1, cw), lambda j, i: (0, j))],
                  out_shape=[_sds((t, c), BF16), _sds((4, c), F32), _sds((1, c), F32)],
                  compiler_params=_params(dimension_semantics=("parallel", "arbitrary")), name=name)(dxc, dxc, xbc, xbc, xbc, w, b)


def _neg_exp_alog(alog):
    lane = lax.broadcasted_iota(jnp.int32, alog.shape, 1)
    return jnp.where(lane < HEADS, -jnp.exp(alog), 0.0)


def _dt_prep(dtr, bias, alog, *, name):
    t = dtr.shape[0]
    rows = CHUNK

    def body(r_ref, b_ref, a_ref, dt_ref, dtg_ref, acsg_ref, acst_ref):
        raw = r_ref[...] + b_ref[...]
        dt = jnp.maximum(raw, 0.0) + jnp.log(1.0 + jnp.exp(-jnp.abs(raw)))
        a = _neg_exp_alog(a_ref[...])
        adt = dt * a
        ri = lax.broadcasted_iota(jnp.int32, (rows, rows), 0)
        ci = lax.broadcasted_iota(jnp.int32, (rows, rows), 1)
        tri = (ri >= ci).astype(F32)
        acs = jnp.dot(tri, adt, precision=lax.Precision.HIGHEST, preferred_element_type=F32)
        dt_ref[...] = dt
        acst_ref[...] = acs.T
        for g in range(N_GROUPS):
            sh = (128 - 8 * g) % 128
            dtg_ref[g] = dt if sh == 0 else pltpu.roll(dt, sh, axis=1)
            acsg_ref[g] = acs if sh == 0 else pltpu.roll(acs, sh, axis=1)

    row = pl.BlockSpec((rows, DT_PAD), lambda i: (i, 0))
    vec = pl.BlockSpec((1, DT_PAD), lambda i: (0, 0))
    grp = pl.BlockSpec((N_GROUPS, rows, DT_PAD), lambda i: (0, i, 0))
    return _pcall(body, grid=(t // rows,), in_specs=[row, vec, vec],
                  out_specs=[row, grp, grp, pl.BlockSpec((DT_PAD, rows), lambda i: (0, i))],
                  out_shape=[_sds((t, DT_PAD), F32), _sds((N_GROUPS, t, DT_PAD), F32), _sds((N_GROUPS, t, DT_PAD), F32),
                             _sds((DT_PAD, t), F32)],
                  compiler_params=_params(dimension_semantics=("parallel",)), name=name)(dtr, bias, alog)


def _dt_bwd(ddtg, dacg, dart, dt, dtr, bias, alog, *, name):
    t = dtr.shape[0]
    rows = CHUNK

    def body(ddtg_ref, dacg_ref, dart_ref, dt_ref, r_ref, b_ref, a_ref, dr_ref, db_ref, da_ref):
        i = pl.program_id(0)
        lane = lax.broadcasted_iota(jnp.int32, (rows, DT_PAD), 1)
        ddt = jnp.zeros((rows, DT_PAD), F32)
        dacs = jnp.concatenate([dart_ref[...], jnp.zeros((DT_PAD - HEADS, rows), F32)], axis=0).T
        for g in range(N_GROUPS):
            sel = (lane >= 8 * g) & (lane < 8 * g + 8)
            dd = ddtg_ref[g]
            da = dacg_ref[g]
            if g:
                dd = pltpu.roll(dd, 8 * g, axis=1)
                da = pltpu.roll(da, 8 * g, axis=1)
            ddt = ddt + jnp.where(sel, dd, 0.0)
            dacs = dacs + jnp.where(sel, da, 0.0)
        ri = lax.broadcasted_iota(jnp.int32, (rows, rows), 0)
        ci = lax.broadcasted_iota(jnp.int32, (rows, rows), 1)
        triu = (ci >= ri).astype(F32)
        rev = jnp.dot(triu, dacs, precision=lax.Precision.HIGHEST, preferred_element_type=F32)
        a = _neg_exp_alog(a_ref[...])
        dtv = dt_ref[...]
        raw = r_ref[...] + b_ref[...]
        draw = (ddt + a * rev) * _sig(raw)
        dr_ref[...] = draw
        dbv = jnp.sum(draw, axis=0, keepdims=True)
        dav = jnp.sum(dtv * rev, axis=0, keepdims=True) * a

        @pl.when(i == 0)
        def _():
            db_ref[...] = dbv
            da_ref[...] = dav

        @pl.when(i > 0)
        def _():
            db_ref[...] += dbv
            da_ref[...] += dav

    row = pl.BlockSpec((rows, DT_PAD), lambda i: (i, 0))
    vec = pl.BlockSpec((1, DT_PAD), lambda i: (0, 0))
    grp = pl.BlockSpec((N_GROUPS, rows, DT_PAD), lambda i: (0, i, 0))
    return _pcall(body, grid=(t // rows,),
                  in_specs=[grp, grp, pl.BlockSpec((HEADS, rows), lambda i: (0, i)), row, row, vec, vec],
                  out_specs=[row, vec, vec], out_shape=[_sds((t, DT_PAD), F32), _sds((1, DT_PAD), F32), _sds((1, DT_PAD), F32)],
                  compiler_params=_params(dimension_semantics=("arbitrary",)), name=name)(ddtg, dacg, dart, dt, dtr, bias, alog)


def _pair_cols(col_ref_val, p, lo):
    return jnp.where(lo, col_ref_val[:, 2 * p:2 * p + 1], col_ref_val[:, 2 * p + 1:2 * p + 2])


def _ssd_fwd(xc, dtg, acsg, acst, z, dskip, nw, *, name):
    t = xc.shape[0]
    L = CHUNK
    nc = t // L

    def body(xc_ref, dtg_ref, acsg_ref, acst_ref, z_ref, dsk_ref, nw_ref, y_ref, st_ref, mix_ref, s_ref):
        c = pl.program_id(1)

        @pl.when(c == 0)
        def _():
            s_ref[...] = jnp.zeros_like(s_ref)

        blk = xc_ref[...]
        bm = blk[:, 512:640]
        cm = blk[:, 640:768]
        cb = _dot(cm, bm, NT)
        dtv = dtg_ref[...]
        acs = acsg_ref[...]
        acst_v = acst_ref[...]
        ri = lax.broadcasted_iota(jnp.int32, (L, L), 0)
        ci = lax.broadcasted_iota(jnp.int32, (L, L), 1)
        causal = ri >= ci
        lo = lax.broadcasted_iota(jnp.int32, (1, 128), 1) < 64
        lo_rows = lax.broadcasted_iota(jnp.int32, (128, 1), 0) < 64
        dskv = dsk_ref[...]
        ys = []
        for p in range(PAIRS_PER_GROUP):
            xp = blk[:, 128 * p:128 * p + 128].astype(F32)
            dt_p = _pair_cols(dtv, p, lo)
            a_p = _pair_cols(acs, p, lo)
            alast = acs[L - 1:L, :]
            al_p = _pair_cols(alast, p, lo)
            xdt = xp * dt_p
            xdt_b = xdt.astype(BF16)
            yd = []
            for hh in range(2):
                j = 2 * p + hh
                seg = acs[:, j:j + 1] - acst_v[j:j + 1, :]
                lam = jnp.exp(jnp.where(causal, seg, -1e30))
                w = (cb * lam).astype(BF16)
                yd.append(_dot(w, xdt_b))
            y = jnp.where(lo, yd[0], yd[1])
            sp = s_ref[p]
            st_ref[p] = sp
            y = y + _dot(cm, sp.astype(BF16), NT) * jnp.exp(a_p)
            dsc = jnp.exp(al_p - a_p)
            snew = _dot((xdt * dsc).astype(BF16), bm, TN)
            al_rows = jnp.where(lo_rows, alast[:, 2 * p:2 * p + 1], alast[:, 2 * p + 1:2 * p + 2])
            s_ref[p] = sp * jnp.exp(al_rows) + snew
            ys.append(y + xp * dskv[:, 128 * p:128 * p + 128])
        yfull = jnp.concatenate(ys, axis=1)
        y_ref[...] = yfull.astype(y_ref.dtype)
        zz = z_ref[...].astype(F32)
        yg = yfull * (zz * _sig(zz))
        r = lax.rsqrt(jnp.mean(yg * yg, axis=-1, keepdims=True) + EPS)
        mix_ref[...] = (yg * r * nw_ref[...]).astype(mix_ref.dtype)

    grp = pl.BlockSpec((None, L, DT_PAD), lambda g, c: (g, c, 0))
    return _pcall(body, grid=(N_GROUPS, nc),
                  in_specs=[pl.BlockSpec((L, GROUP_W), lambda g, c: (c, g)), grp, grp,
                            pl.BlockSpec((8, L), lambda g, c: (g, c)),
                            pl.BlockSpec((L, 512), lambda g, c: (c, g)),
                            pl.BlockSpec((1, 512), lambda g, c: (0, g)), pl.BlockSpec((1, 512), lambda g, c: (0, g))],
                  out_specs=[pl.BlockSpec((L, 512), lambda g, c: (c, g)),
                             pl.BlockSpec((None, PAIRS_PER_GROUP, 128, N_STATE), lambda g, c: (c, g, 0, 0)),
                             pl.BlockSpec((L, 512), lambda g, c: (c, g))],
                  out_shape=[_sds((t, D_SSD), BF16), _sds((nc, N_GROUPS * PAIRS_PER_GROUP, 128, N_STATE), F32),
                             _sds((t, D_SSD + D_SC), BF16)],
                  scratch_shapes=[pltpu.VMEM((PAIRS_PER_GROUP, 128, N_STATE), F32)],
                  compiler_params=_params(dimension_semantics=("parallel", "arbitrary")), name=name)(
                      xc, dtg, acsg, acst, z, dskip, nw)


def _ssd_bwd(xc, dtg, acsg, acst, z, y, states, dmix, dskip, nw, *, name):
    t = xc.shape[0]
    L = CHUNK
    nc = t // L

    def body(xc_ref, dtg_ref, acsg_ref, acst_ref, z_ref, y_ref, st_ref, dm_ref, dsk_ref, nw_ref,
             dxc_ref, dz_ref, ddt_ref, dac_ref, dar_ref, ddsk_ref, dnw_ref, ds_ref):
        c = pl.program_id(1)

        @pl.when(c == 0)
        def _():
            ds_ref[...] = jnp.zeros_like(ds_ref)
            ddsk_ref[...] = jnp.zeros_like(ddsk_ref)
            dnw_ref[...] = jnp.zeros_like(dnw_ref)

        blk = xc_ref[...]
        xs = blk[:, :512].astype(F32)
        bm = blk[:, 512:640]
        cm = blk[:, 640:768]
        bmf = bm.astype(F32)
        yv = y_ref[...].astype(F32)
        zz = z_ref[...].astype(F32)
        nwv = nw_ref[...]
        dout = dm_ref[...].astype(F32)
        sz = _sig(zz)
        silu = zz * sz
        yg = yv * silu
        r = lax.rsqrt(jnp.mean(yg * yg, axis=-1, keepdims=True) + EPS)
        xh = yg * r
        dnw_ref[...] += jnp.sum(dout * xh, axis=0, keepdims=True)
        dyn = dout * nwv
        dyg = r * (dyn - xh * jnp.mean(dyn * xh, axis=-1, keepdims=True))
        dy = dyg * silu
        dz_ref[...] = (dyg * yv * (sz * (1.0 + zz * (1.0 - sz)))).astype(dz_ref.dtype)
        ddsk_ref[...] += jnp.sum(dy * xs, axis=0, keepdims=True)

        cb = _dot(cm, bm, NT)
        dtv = dtg_ref[...]
        acs = acsg_ref[...]
        acst_v = acst_ref[...]
        alast = acs[L - 1:L, :]
        ri = lax.broadcasted_iota(jnp.int32, (L, L), 0)
        ci = lax.broadcasted_iota(jnp.int32, (L, L), 1)
        causal = ri >= ci
        lane = lax.broadcasted_iota(jnp.int32, (1, 128), 1)
        lo = lane < 64
        lo_rows = lax.broadcasted_iota(jnp.int32, (128, 1), 0) < 64
        lane_l = lax.broadcasted_iota(jnp.int32, (L, DT_PAD), 1)
        row_l = lax.broadcasted_iota(jnp.int32, (L, 1), 0)
        sub8 = lax.broadcasted_iota(jnp.int32, (8, L), 0)
        dskv = dsk_ref[...]
        dm_acc = jnp.zeros((L, L), F32)
        db_acc = jnp.zeros((L, N_STATE), F32)
        dc_acc = jnp.zeros((L, N_STATE), F32)
        ddt_out = jnp.zeros((L, DT_PAD), F32)
        dac_out = jnp.zeros((L, DT_PAD), F32)
        dar_out = jnp.zeros((8, L), F32)
        dxs = []
        for p in range(PAIRS_PER_GROUP):
            xp = xs[:, 128 * p:128 * p + 128]
            dyp = dy[:, 128 * p:128 * p + 128]
            dt_p = _pair_cols(dtv, p, lo)
            a_p = _pair_cols(acs, p, lo)
            al_p = _pair_cols(alast, p, lo)
            xdt = xp * dt_p
            xdt_b = xdt.astype(BF16)
            ea_p = jnp.exp(a_p)
            dsc_p = jnp.exp(al_p - a_p)
            sp = st_ref[p]
            sp_b = sp.astype(BF16)
            dsp = ds_ref[p]
            dsp_b = dsp.astype(BF16)
            cs = _dot(cm, sp_b, NT)
            dye_b = (dyp * ea_p).astype(BF16)
            dc_acc = dc_acc + _dot(dye_b, sp_b)
            ds_prev = _dot(dye_b, cm, TN)
            bds = _dot(bm, dsp_b, NT)
            al_rows = jnp.where(lo_rows, alast[:, 2 * p:2 * p + 1], alast[:, 2 * p + 1:2 * p + 2])
            ds_prev = ds_prev + jnp.exp(al_rows) * dsp
            prod_off = dyp * cs
            prod_st = dsp * sp
            dxdt_h = []
            for hh in range(2):
                j = 2 * p + hh
                hm = lo if hh == 0 else jnp.logical_not(lo)
                hm_rows = lo_rows if hh == 0 else jnp.logical_not(lo_rows)
                a_col = acs[:, j:j + 1]
                seg = a_col - acst_v[j:j + 1, :]
                lam = jnp.exp(jnp.where(causal, seg, -1e30))
                wf = cb * lam
                w = wf.astype(BF16)
                dy_h = jnp.where(hm, dyp, 0.0).astype(BF16)
                dw = _dot(dy_h, xdt_b, NT)
                dxdt_h.append(_dot(w, dyp.astype(BF16), TN))
                dm_acc = dm_acc + dw * lam
                e = dw * wf
                dac = jnp.sum(e, axis=1, keepdims=True)
                dar = -jnp.sum(e, axis=0, keepdims=True)
                ea_col = jnp.exp(a_col)
                dac = dac + ea_col * jnp.sum(jnp.where(hm, prod_off, 0.0), axis=1, keepdims=True)
                al_h = alast[:, j:j + 1]
                dal = jnp.exp(al_h) * jnp.sum(jnp.sum(jnp.where(hm_rows, prod_st, 0.0), axis=1, keepdims=True), axis=0, keepdims=True)
                xds_h = _dot(jnp.where(hm, xdt, 0.0).astype(BF16), dsp_b)
                dsc_col = jnp.exp(al_h - a_col)
                db_acc = db_acc + dsc_col * xds_h
                tt = jnp.sum(xds_h * bmf, axis=1, keepdims=True) * dsc_col
                dal = dal + jnp.sum(tt, axis=0, keepdims=True)
                dac = dac - tt + jnp.where(row_l == L - 1, dal, 0.0)
                dac_out = jnp.where(lane_l == j, dac, dac_out)
                dar_out = jnp.where(sub8 == j, dar, dar_out)
            dxdt = jnp.where(lo, dxdt_h[0], dxdt_h[1]) + dsc_p * bds
            dxs.append(dxdt * dt_p + dyp * dskv[:, 128 * p:128 * p + 128])
            prod_dt = dxdt * xp
            for hh in range(2):
                j = 2 * p + hh
                hm = lo if hh == 0 else jnp.logical_not(lo)
                ddt_col = jnp.sum(jnp.where(hm, prod_dt, 0.0), axis=1, keepdims=True)
                ddt_out = jnp.where(lane_l == j, ddt_col, ddt_out)
            ds_ref[p] = ds_prev
        dm_b = dm_acc.astype(BF16)
        dc_acc = dc_acc + _dot(dm_b, bm)
        db_acc = db_acc + _dot(dm_b, cm, TN)
        dxc_ref[...] = jnp.concatenate(dxs + [db_acc, dc_acc], axis=1).astype(dxc_ref.dtype)
        ddt_ref[...] = ddt_out
        dac_ref[...] = dac_out
        dar_ref[...] = dar_out

    rc = lambda g, c: (nc - 1 - c, g)
    grp = pl.BlockSpec((None, L, DT_PAD), lambda g, c: (g, nc - 1 - c, 0))
    vec = pl.BlockSpec((1, 512), lambda g, c: (0, g))
    return _pcall(body, grid=(N_GROUPS, nc),
                  in_specs=[pl.BlockSpec((L, GROUP_W), rc), grp, grp,
                            pl.BlockSpec((8, L), lambda g, c: (g, nc - 1 - c)),
                            pl.BlockSpec((L, 512), rc), pl.BlockSpec((L, 512), rc),
                            pl.BlockSpec((None, PAIRS_PER_GROUP, 128, N_STATE), lambda g, c: (nc - 1 - c, g, 0, 0)),
                            pl.BlockSpec((L, 512), rc), vec, vec],
                  out_specs=[pl.BlockSpec((L, GROUP_W), rc), pl.BlockSpec((L, 512), rc), grp, grp,
                             pl.BlockSpec((8, L), lambda g, c: (g, nc - 1 - c)), vec, vec],
                  out_shape=[_sds((t, D_XBC), BF16), _sds((t, D_SSD), BF16), _sds((N_GROUPS, t, DT_PAD), F32),
                             _sds((N_GROUPS, t, DT_PAD), F32), _sds((HEADS, t), F32), _sds((1, D_SSD), F32),
                             _sds((1, D_SSD), F32)],
                  scratch_shapes=[pltpu.VMEM((PAIRS_PER_GROUP, 128, N_STATE), F32)],
                  compiler_params=_params(dimension_semantics=("parallel", "arbitrary")), name=name)(
                      xc, dtg, acsg, acst, z, y, states, dmix, dskip, nw)


SC_CW = 1024


def _group_rstd(v):
    outs = []
    for q in range(v.shape[1] // SC_GROUP):
        vq = v[:, SC_GROUP * q:SC_GROUP * (q + 1)]
        outs.append(jnp.broadcast_to(lax.rsqrt(jnp.mean(vq * vq, axis=-1, keepdims=True) + EPS), vq.shape))
    return jnp.concatenate(outs, axis=1)


def _group_mean(v):
    outs = []
    for q in range(v.shape[1] // SC_GROUP):
        vq = v[:, SC_GROUP * q:SC_GROUP * (q + 1)]
        outs.append(jnp.broadcast_to(jnp.mean(vq, axis=-1, keepdims=True), vq.shape))
    return jnp.concatenate(outs, axis=1)


def _sc_fwd(ps, w, nw, mix, *, name):
    t = ps.shape[1]
    rows = CHUNK
    hb = rows // HALO
    cw = SC_CW
    off = D_SSD // cw

    def body(cur_ref, prev_ref, w_ref, nw_ref, mix_in_ref, o_ref):
        del mix_in_ref
        i = pl.program_id(1)
        u = cur_ref[0].astype(F32)
        gb = cur_ref[1].astype(F32)
        gc = cur_ref[2].astype(F32)
        cu_prev = jnp.where(i > 0, prev_ref[2].astype(F32) * prev_ref[0].astype(F32), 0.0)
        ext = jnp.concatenate([cu_prev, gc * u], axis=0)
        v = gb * _conv_taps(ext, w_ref[...], 3, rows)
        o_ref[...] = (v * _group_rstd(v) * nw_ref[...]).astype(o_ref.dtype)

    return _pcall(body, grid=(D_SC // cw, t // rows),
                  in_specs=[pl.BlockSpec((3, rows, cw), lambda j, i: (0, i, j)),
                            pl.BlockSpec((3, HALO, cw), lambda j, i: (0, jnp.maximum(i * hb - 1, 0), j)),
                            pl.BlockSpec((3, cw), lambda j, i: (0, j)), pl.BlockSpec((1, cw), lambda j, i: (0, j)),
                            pl.BlockSpec(memory_space=pl.ANY)],
                  out_specs=pl.BlockSpec((rows, cw), lambda j, i: (i, off + j)),
                  out_shape=_sds(mix.shape, mix.dtype), input_output_aliases={4: 0},
                  compiler_params=_params(dimension_semantics=("parallel", "parallel")), name=name)(ps, ps, w, nw, mix)


def _sc_bwd(ps, w, nw, dmix, *, name):
    t = ps.shape[1]
    rows = CHUNK
    hb = rows // HALO
    cw = SC_CW
    off = D_SSD // cw
    nblk = t // rows
    nhalo = t // HALO
    n = rows + 2 * HALO

    def body(cur_ref, prev_ref, next_ref, w_ref, nw_ref, d_ref, dn_ref, dps_ref, dw_ref, dnw_ref):
        i = pl.program_id(1)
        first = i == 0
        last = i == nblk - 1

        def ext(k):
            return jnp.concatenate([jnp.where(first, 0.0, prev_ref[k].astype(F32)), cur_ref[k].astype(F32),
                                    jnp.where(last, 0.0, next_ref[k].astype(F32))], axis=0)

        ue, gbe, gce = ext(0), ext(1), ext(2)
        wv = w_ref[...]
        nwv = nw_ref[...]
        cue = gce * ue
        cu1 = pltpu.roll(cue, 1, axis=0)
        cu2 = pltpu.roll(cue, 2, axis=0)
        conv = wv[2:3, :] * cue + wv[1:2, :] * cu1 + wv[0:1, :] * cu2
        ve = gbe * conv
        doe = jnp.concatenate([jnp.zeros((HALO, cw), F32), d_ref[...].astype(F32),
                               jnp.where(last, 0.0, dn_ref[...].astype(F32))], axis=0)
        r = _group_rstd(ve)
        xh = ve * r
        dvn = doe * nwv
        dv = r * (dvn - xh * _group_mean(dvn * xh))
        dconv = dv * gbe
        dcu = wv[2:3, :] * dconv + wv[1:2, :] * pltpu.roll(dconv, n - 1, axis=0) + wv[0:1, :] * pltpu.roll(dconv, n - 2, axis=0)
        sl = slice(HALO, HALO + rows)
        dps_ref[0] = (dcu * gce)[sl].astype(dps_ref.dtype)
        dps_ref[1] = (dv * conv)[sl].astype(dps_ref.dtype)
        dps_ref[2] = (dcu * ue)[sl].astype(dps_ref.dtype)
        dc = dconv[sl]
        dwv = jnp.concatenate([jnp.sum(dc * cu2[sl], axis=0, keepdims=True), jnp.sum(dc * cu1[sl], axis=0, keepdims=True),
                               jnp.sum(dc * cue[sl], axis=0, keepdims=True)], axis=0)
        dnv = jnp.sum((doe * xh)[sl], axis=0, keepdims=True)

        @pl.when(first)
        def _():
            dw_ref[...] = dwv
            dnw_ref[...] = dnv

        @pl.when(i > 0)
        def _():
            dw_ref[...] += dwv
            dnw_ref[...] += dnv

    cur = pl.BlockSpec((3, rows, cw), lambda j, i: (0, i, j))
    prev = pl.BlockSpec((3, HALO, cw), lambda j, i: (0, jnp.maximum(i * hb - 1, 0), j))
    nxt = pl.BlockSpec((3, HALO, cw), lambda j, i: (0, jnp.minimum((i + 1) * hb, nhalo - 1), j))
    return _pcall(body, grid=(D_SC // cw, nblk),
                  in_specs=[cur, prev, nxt, pl.BlockSpec((3, cw), lambda j, i: (0, j)), pl.BlockSpec((1, cw), lambda j, i: (0, j)),
                            pl.BlockSpec((rows, cw), lambda j, i: (i, off + j)),
                            pl.BlockSpec((HALO, cw), lambda j, i: (jnp.minimum((i + 1) * hb, nhalo - 1), off + j))],
                  out_specs=[cur, pl.BlockSpec((3, cw), lambda j, i: (0, j)), pl.BlockSpec((1, cw), lambda j, i: (0, j))],
                  out_shape=[_sds(ps.shape, BF16), _sds((3, D_SC), F32), _sds((1, D_SC), F32)],
                  compiler_params=_params(dimension_semantics=("parallel", "arbitrary")), name=name)(ps, ps, ps, w, nw, dmix, dmix)


XA_SCALE = XA_HD ** -0.5


def _softmax(s):
    m = jnp.max(s, axis=-1, keepdims=True)
    e = jnp.exp(s - m)
    return e / jnp.sum(e, axis=-1, keepdims=True)


def _xa_fwd(q, k, v, *, name):
    t = q.shape[0]
    nm = k.shape[0]
    tq = _tile(t, 512)

    def body(q_ref, k_ref, v_ref, o_ref):
        outs = []
        for h in range(XA_HEADS):
            sl = slice(XA_HD * h, XA_HD * (h + 1))
            s = _dot(q_ref[:, sl], k_ref[:, sl], NT) * XA_SCALE
            outs.append(_dot(_softmax(s).astype(BF16), v_ref[:, sl]))
        o_ref[...] = jnp.concatenate(outs, axis=1).astype(o_ref.dtype)

    row = pl.BlockSpec((tq, D_XA), lambda i: (i, 0))
    kv = pl.BlockSpec((nm, D_XA), lambda i: (0, 0))
    return _pcall(body, grid=(t // tq,), in_specs=[row, kv, kv], out_specs=row, out_shape=_sds((t, D_XA), BF16),
                  compiler_params=_params(dimension_semantics=("parallel",)), name=name)(q, k, v)


def _xa_bwd(q, k, v, do, *, name):
    t = q.shape[0]
    nm = k.shape[0]
    tq = _tile(t, 512)

    def body(q_ref, k_ref, v_ref, do_ref, dq_ref, dk_ref, dv_ref):
        i = pl.program_id(0)
        dqs, dks, dvs = [], [], []
        for h in range(XA_HEADS):
            sl = slice(XA_HD * h, XA_HD * (h + 1))
            qh, kh, vh, doh = q_ref[:, sl], k_ref[:, sl], v_ref[:, sl], do_ref[:, sl]
            p = _softmax(_dot(qh, kh, NT) * XA_SCALE)
            dvs.append(_dot(p.astype(BF16), doh, TN))
            dp = _dot(doh, vh, NT)
            ds = (p * (dp - jnp.sum(dp * p, axis=-1, keepdims=True)) * XA_SCALE).astype(BF16)
            dqs.append(_dot(ds, kh))
            dks.append(_dot(ds, qh, TN))
        dq_ref[...] = jnp.concatenate(dqs, axis=1).astype(dq_ref.dtype)
        dkv = jnp.concatenate(dks, axis=1)
        dvv = jnp.concatenate(dvs, axis=1)

        @pl.when(i == 0)
        def _():
            dk_ref[...] = dkv
            dv_ref[...] = dvv

        @pl.when(i > 0)
        def _():
            dk_ref[...] += dkv
            dv_ref[...] += dvv

    row = pl.BlockSpec((tq, D_XA), lambda i: (i, 0))
    kv = pl.BlockSpec((nm, D_XA), lambda i: (0, 0))
    return _pcall(body, grid=(t // tq,), in_specs=[row, kv, kv, row], out_specs=[row, kv, kv],
                  out_shape=[_sds((t, D_XA), BF16), _sds((nm, D_XA), F32), _sds((nm, D_XA), F32)],
                  compiler_params=_params(dimension_semantics=("arbitrary",)), name=name)(q, k, v, do)


FF_CW = 1408


def _swiglu_fwd(g, u, *, name):
    t, f = g.shape
    tr = _tile(t, 512)

    def body(g_ref, u_ref, o_ref):
        gv = g_ref[...].astype(F32)
        o_ref[...] = (gv * _sig(gv) * u_ref[...].astype(F32)).astype(o_ref.dtype)

    blk = pl.BlockSpec((tr, FF_CW), lambda i, j: (i, j))
    return _pcall(body, grid=(t // tr, f // FF_CW), in_specs=[blk, blk], out_specs=blk, out_shape=_sds((t, f), BF16),
                  compiler_params=_params(dimension_semantics=("parallel", "parallel")), name=name)(g, u)


def _swiglu_bwd(g, u, dact, *, name):
    t, f = g.shape
    tr = _tile(t, 512)

    def body(g_ref, u_ref, d_ref, dg_ref, du_ref):
        gv = g_ref[...].astype(F32)
        uv = u_ref[...].astype(F32)
        dv = d_ref[...].astype(F32)
        s = _sig(gv)
        dg_ref[...] = (dv * uv * (s * (1.0 + gv * (1.0 - s)))).astype(dg_ref.dtype)
        du_ref[...] = (dv * gv * s).astype(du_ref.dtype)

    blk = pl.BlockSpec((tr, FF_CW), lambda i, j: (i, j))
    return _pcall(body, grid=(t // tr, f // FF_CW), in_specs=[blk, blk, blk], out_specs=[blk, blk],
                  out_shape=[_sds((t, f), BF16), _sds((t, f), BF16)],
                  compiler_params=_params(dimension_semantics=("parallel", "parallel")), name=name)(g, u, dact)


def _row_tile(n):
    for cand in (128, 64, 32, 16):
        if n % cand == 0:
            return cand
    raise ValueError(n)


def _add_halves(g4, rb, c_arr, *, name):
    _, r, cdim = g4.shape
    hr = r // 2
    rt = _row_tile(hr)
    nb = hr // rt

    def body(c_ref, g_ref, rb_ref, o_ref):
        del c_ref
        o_ref[...] = (g_ref[...].astype(F32) + rb_ref[...].astype(F32)).astype(o_ref.dtype)

    gs = pltpu.PrefetchScalarGridSpec(
        num_scalar_prefetch=1, grid=(4, nb),
        in_specs=[pl.BlockSpec((None, rt, cdim), lambda q, i, c: (q, c[0] * nb + i, 0)),
                  pl.BlockSpec((None, rt, cdim), lambda q, i, c: (q, i, 0))],
        out_specs=pl.BlockSpec((None, rt, cdim), lambda q, i, c: (q, i, 0)))
    return _pcall(body, grid_spec=gs, out_shape=_sds((4, hr, cdim), BF16),
                  compiler_params=_params(dimension_semantics=("parallel", "parallel")), name=name)(c_arr, g4, rb)


def _sum_chips(p4, rc, q_arr, c_arr, *, name):
    _, hr, cdim = p4.shape
    rt = _row_tile(hr)
    nb = hr // rt

    def body(q_ref, c_ref, p_ref, rc_ref, o_ref):
        del q_ref, c_ref
        o_ref[...] = ((p_ref[...].astype(F32) + rc_ref[0].astype(F32)) + rc_ref[1].astype(F32)) + rc_ref[2].astype(F32)

    gs = pltpu.PrefetchScalarGridSpec(
        num_scalar_prefetch=2, grid=(nb,),
        in_specs=[pl.BlockSpec((None, rt, cdim), lambda i, q, c: (q[0], i, 0)),
                  pl.BlockSpec((3, rt, cdim), lambda i, q, c: (0, i, 0))],
        out_specs=pl.BlockSpec((rt, cdim), lambda i, q, c: (c[0] * nb + i, 0)))
    return _pcall(body, grid_spec=gs, out_shape=_sds((2 * hr, cdim), F32),
                  compiler_params=_params(dimension_semantics=("parallel",)), name=name)(q_arr, c_arr, p4, rc)


def _adam_math(w, g, m, v):
    m = ADAM_B1 * m + (1.0 - ADAM_B1) * g
    v = ADAM_B2 * v + (1.0 - ADAM_B2) * (g * g)
    m_hat = m / (1.0 - ADAM_B1 ** ADAM_STEP)
    v_hat = v / (1.0 - ADAM_B2 ** ADAM_STEP)
    delta = -ADAM_LR * (m_hat / (jnp.sqrt(v_hat) + ADAM_EPS) + ADAM_WD * w)
    return delta, m, v


def _adamw_layer(layer, w, m, v, g, prev, *, name):
    depth, r, cdim = w.shape
    rt = _row_tile(r)
    n_prev = 0 if prev is None else 4

    def body(*refs):
        w_ref, m_ref, v_ref, g_ref = refs[:4]
        go_ref, d_ref, mo_ref, vo_ref = refs[4 + n_prev:]
        gv = g_ref[...]
        delta, mn, vn = _adam_math(w_ref[...], gv, m_ref[...], v_ref[...])
        go_ref[...] = gv
        d_ref[...] = delta
        mo_ref[...] = mn
        vo_ref[...] = vn

    st = pl.BlockSpec((None, rt, cdim), lambda i: (layer, i, 0))
    in_specs = [st, st, st, pl.BlockSpec((rt, cdim), lambda i: (i, 0))] + [pl.BlockSpec(memory_space=pl.ANY)] * n_prev
    args = (w, m, v, g) + (tuple(prev) if prev is not None else ())
    return _pcall(body, grid=(r // rt,), in_specs=in_specs, out_specs=[st] * 4,
                  out_shape=[_sds((depth, r, cdim), F32)] * 4,
                  input_output_aliases={4 + i: i for i in range(n_prev)},
                  compiler_params=_params(dimension_semantics=("parallel",)), name=name)(*args)


def _adamw_flat(w, g, m, v, *, name):
    def body(w_ref, g_ref, m_ref, v_ref, d_ref, mo_ref, vo_ref):
        delta, mn, vn = _adam_math(w_ref[...], g_ref[...], m_ref[...], v_ref[...])
        d_ref[...] = delta
        mo_ref[...] = mn
        vo_ref[...] = vn

    return _pcall(body, out_shape=[_sds(w.shape, F32)] * 3, compiler_params=_params(), name=name)(w, g, m, v)


def _place():
    x, y, c = lax.axis_index("x"), lax.axis_index("y"), lax.axis_index("c")
    chips = [(1 - x, y), (x, 1 - y), (1 - x, 1 - y)]
    return x, y, c, chips


def _cast_place(w, q_arr, *, dep=None, name):
    r, cdim = w.shape
    rt = _row_tile(r)
    deps = () if dep is None else (dep,)

    def body(q_ref, w_ref, *rest):
        del q_ref
        o_ref = rest[-1]
        o_ref[...] = w_ref[...].astype(o_ref.dtype)

    gs = pltpu.PrefetchScalarGridSpec(
        num_scalar_prefetch=1, grid=(r // rt,),
        in_specs=[pl.BlockSpec((rt, cdim), lambda i, q: (i, 0))] + [pl.BlockSpec(memory_space=pl.ANY)] * len(deps),
        out_specs=pl.BlockSpec((None, rt, cdim), lambda i, q: (q[0], i, 0)))
    return _pcall(body, grid_spec=gs, out_shape=_sds((4, r, cdim), BF16),
                  compiler_params=_params(dimension_semantics=("parallel",)), name=name)(q_arr, w, *deps)


HBM_SPEC = pl.BlockSpec(memory_space=pltpu.HBM)
SEM_SPEC = pl.BlockSpec(memory_space=pltpu.SEMAPHORE)
ANY_SPEC = pl.BlockSpec(memory_space=pl.ANY)
SPLIT_PARAMS = pltpu.CompilerParams(has_side_effects=pltpu.SideEffectType.DATAFLOW_SIDE_EFFECTING)


def _gather_copies(bufs, sems):
    n = len(bufs)
    x, y, c, chips = _place()
    qme = 2 * x + y
    cps = []
    for t in range(n):
        hr = bufs[t].shape[1] // 2
        mine = bufs[t].at[qme, pl.ds(c * hr, hr)]
        for k, (px, py) in enumerate(chips):
            landed = bufs[t].at[2 * px + py, pl.ds(c * hr, hr)]
            peer = dict(device_id=(px, py, c), device_id_type=MESH)
            cps.append((pltpu.make_async_remote_copy(src_ref=mine, dst_ref=mine, send_sem=sems[3 * t + k],
                                                     recv_sem=sems[3 * n + 3 * t + k], **peer),
                        pltpu.make_async_remote_copy(src_ref=mine, dst_ref=landed, send_sem=sems[3 * t + k],
                                                     recv_sem=sems[3 * n + 3 * t + k], **peer)))
    return cps


def _gather_start(bufs, after, *, name):
    n = len(bufs)

    def body(*refs):
        ins = refs[:n]
        sems = refs[n + 1:7 * n + 1]
        token = refs[8 * n + 1]
        for send, _ in _gather_copies(ins, sems):
            send.start()
        token[...] = jnp.zeros_like(token)

    res = _pcall(body, in_specs=[HBM_SPEC] * n + [ANY_SPEC],
                 out_specs=(SEM_SPEC,) * (6 * n) + (HBM_SPEC,) * n + (pl.BlockSpec(memory_space=pltpu.VMEM),),
                 out_shape=(pltpu.SemaphoreType.DMA(()),) * (6 * n) + tuple(pltpu.HBM(b.shape, b.dtype) for b in bufs)
                 + (_sds((8, 128), F32),),
                 input_output_aliases={t: 6 * n + t for t in range(n)}, compiler_params=SPLIT_PARAMS, name=name)(*bufs, after)
    return list(res[:6 * n]), list(res[6 * n:7 * n]), res[7 * n]


def _gather_wait(bufs, sems, after, *, name):
    n = len(bufs)

    def body(*refs):
        ins = refs[:n]
        sem_refs = refs[n:7 * n]
        for send, arrive in _gather_copies(ins, sem_refs):
            send.wait_send()
            arrive.wait_recv()

    return list(_pcall(body, in_specs=[HBM_SPEC] * n + [SEM_SPEC] * (6 * n) + [ANY_SPEC], out_specs=(HBM_SPEC,) * n,
                       out_shape=tuple(pltpu.HBM(b.shape, b.dtype) for b in bufs),
                       input_output_aliases={t: t for t in range(n)}, compiler_params=SPLIT_PARAMS, name=name)(*bufs, *sems, after))


def _gather_forward(bufs, *, name):
    n = len(bufs)

    def body(*refs):
        outs = refs[n:2 * n]
        send, recv = refs[2 * n:]
        x, y, c, chips = _place()
        sib = dict(device_id=(x, y, 1 - c), device_id_type=MESH)
        cps = []
        for t in range(n):
            hr = outs[t].shape[1] // 2
            for k, (px, py) in enumerate(chips):
                landed = outs[t].at[2 * px + py, pl.ds(c * hr, hr)]
                cp = pltpu.make_async_remote_copy(src_ref=landed, dst_ref=landed, send_sem=send.at[t, k],
                                                  recv_sem=recv.at[t, k], **sib)
                cp.start()
                cps.append(cp)
        for t in range(n):
            hr = outs[t].shape[1] // 2
            for k, (px, py) in enumerate(chips):
                other = outs[t].at[2 * px + py, pl.ds((1 - c) * hr, hr)]
                pltpu.make_async_remote_copy(src_ref=other, dst_ref=other, send_sem=send.at[t, k], recv_sem=recv.at[t, k],
                                             **sib).wait_recv()
        for cp in cps:
            cp.wait_send()

    return _pcall(body, in_specs=[ANY_SPEC] * n, out_specs=[ANY_SPEC] * n,
                  out_shape=[_sds(b.shape, b.dtype) for b in bufs], input_output_aliases={t: t for t in range(n)},
                  scratch_shapes=[pltpu.SemaphoreType.DMA((n, 3))] * 2,
                  compiler_params=pltpu.CompilerParams(has_side_effects=True), name=name)(*bufs)


def _owner_copies(parts, lands, sems):
    n = len(parts)
    _, _, c, chips = _place()
    cps = []
    for t in range(n):
        for k, (px, py) in enumerate(chips):
            cps.append(pltpu.make_async_remote_copy(src_ref=parts[t].at[2 * px + py], dst_ref=lands[t].at[k],
                                                    send_sem=sems[3 * t + k], recv_sem=sems[3 * n + 3 * t + k],
                                                    device_id=(px, py, c), device_id_type=MESH))
    return cps


def _owners_start(parts, *, name):
    n = len(parts)
    lands = [pltpu.with_memory_space_constraint(lax.empty((3,) + p.shape[1:], p.dtype), pltpu.HBM) for p in parts]

    def body(*refs):
        ins, lnd = refs[:n], refs[n:2 * n]
        sems = refs[2 * n:8 * n]
        token = refs[10 * n]
        for cp in _owner_copies(ins, lnd, sems):
            cp.start()
        token[...] = jnp.zeros_like(token)

    res = _pcall(body, in_specs=[HBM_SPEC] * (2 * n),
                 out_specs=(SEM_SPEC,) * (6 * n) + (HBM_SPEC,) * (2 * n) + (pl.BlockSpec(memory_space=pltpu.VMEM),),
                 out_shape=(pltpu.SemaphoreType.DMA(()),) * (6 * n)
                 + tuple(pltpu.HBM(b.shape, b.dtype) for b in list(parts) + lands) + (_sds((8, 128), F32),),
                 input_output_aliases={t: 6 * n + t for t in range(2 * n)}, compiler_params=SPLIT_PARAMS, name=name)(*parts, *lands)
    return list(res[:6 * n]), list(res[6 * n:7 * n]), list(res[7 * n:8 * n]), res[8 * n]


def _owners_wait(parts, lands, sems, after, *, name):
    n = len(parts)

    def body(*refs):
        ins, lnd = refs[:n], refs[n:2 * n]
        sem_refs = refs[2 * n:8 * n]
        for cp in _owner_copies(ins, lnd, sem_refs):
            cp.wait_send()
            cp.wait_recv()

    res = _pcall(body, in_specs=[HBM_SPEC] * (2 * n) + [SEM_SPEC] * (6 * n) + [ANY_SPEC], out_specs=(HBM_SPEC,) * (2 * n),
                 out_shape=tuple(pltpu.HBM(b.shape, b.dtype) for b in list(parts) + list(lands)),
                 input_output_aliases={t: t for t in range(2 * n)}, compiler_params=SPLIT_PARAMS, name=name)(
                     *parts, *lands, *sems, after)
    return list(res[:n]), list(res[n:])


def _swap_halves(grads, *, name):
    n = len(grads)

    def body(*refs):
        ins, outs = refs[:n], refs[n:2 * n]
        send, recv = refs[2 * n:]
        x, y, c, _ = _place()
        cps = []
        for t in range(n):
            hr = ins[t].shape[1] // 2
            cp = pltpu.make_async_remote_copy(src_ref=ins[t].at[:, pl.ds((1 - c) * hr, hr), :], dst_ref=outs[t],
                                              send_sem=send.at[t], recv_sem=recv.at[t],
                                              device_id=(x, y, 1 - c), device_id_type=MESH)
            cp.start()
            cps.append(cp)
        for cp in cps:
            cp.wait()

    anyspec = pl.BlockSpec(memory_space=pl.ANY)
    return _pcall(body, in_specs=[anyspec] * n, out_specs=[anyspec] * n,
                  out_shape=[_sds((4, g.shape[1] // 2, g.shape[2]), g.dtype) for g in grads],
                  scratch_shapes=[pltpu.SemaphoreType.DMA((n,))] * 2,
                  compiler_params=pltpu.CompilerParams(has_side_effects=True), name=name)(*grads)


def _join_halves(bufs, *, name):
    n = len(bufs)

    def body(*refs):
        outs = refs[n:2 * n]
        send, recv = refs[2 * n:]
        x, y, c, _ = _place()
        cps = []
        for t in range(n):
            hr = outs[t].shape[0] // 2
            mine = outs[t].at[pl.ds(c * hr, hr)]
            cp = pltpu.make_async_remote_copy(src_ref=mine, dst_ref=mine, send_sem=send.at[t], recv_sem=recv.at[t],
                                              device_id=(x, y, 1 - c), device_id_type=MESH)
            cp.start()
            cps.append(cp)
        for t in range(n):
            hr = outs[t].shape[0] // 2
            theirs = outs[t].at[pl.ds((1 - c) * hr, hr)]
            pltpu.make_async_remote_copy(src_ref=theirs, dst_ref=theirs, send_sem=send.at[t], recv_sem=recv.at[t],
                                         device_id=(x, y, 1 - c), device_id_type=MESH).wait_recv()
        for cp in cps:
            cp.wait_send()

    anyspec = pl.BlockSpec(memory_space=pl.ANY)
    return _pcall(body, in_specs=[anyspec] * n, out_specs=[anyspec] * n,
                  out_shape=[_sds(b.shape, b.dtype) for b in bufs], input_output_aliases={t: t for t in range(n)},
                  scratch_shapes=[pltpu.SemaphoreType.DMA((n,))] * 2,
                  compiler_params=pltpu.CompilerParams(has_side_effects=True), name=name)(*bufs)


def _allreduce_small(buf, *, name):
    rows = buf.shape[0]
    rels = [(dx, dy, dc) for dx in (0, 1) for dy in (0, 1) for dc in (0, 1)][1:]

    def body(in_ref, out_ref, gbuf, send, recv):
        x, y, c = lax.axis_index("x"), lax.axis_index("y"), lax.axis_index("c")
        me = 4 * x + 2 * y + c
        gbuf[me] = in_ref[...]
        cps = []
        for k, (dx, dy, dc) in enumerate(rels):
            peer = (x + dx - 2 * x * dx, y + dy - 2 * y * dy, c + dc - 2 * c * dc)
            cp = pltpu.make_async_remote_copy(src_ref=in_ref, dst_ref=gbuf.at[me], send_sem=send.at[k], recv_sem=recv.at[k],
                                              device_id=peer, device_id_type=MESH)
            cp.start()
            cps.append(cp)
        for k, (dx, dy, dc) in enumerate(rels):
            px, py, pc = x + dx - 2 * x * dx, y + dy - 2 * y * dy, c + dc - 2 * c * dc
            pltpu.make_async_remote_copy(src_ref=in_ref, dst_ref=gbuf.at[4 * px + 2 * py + pc], send_sem=send.at[k],
                                         recv_sem=recv.at[k], device_id=(px, py, pc), device_id_type=MESH).wait_recv()
        for cp in cps:
            cp.wait_send()
        acc = gbuf[0]
        for d in range(1, 8):
            acc = acc + gbuf[d]
        out_ref[...] = acc

    vm = pl.BlockSpec(memory_space=pltpu.VMEM)
    return _pcall(body, in_specs=[vm], out_specs=vm, out_shape=_sds(buf.shape, F32),
                  scratch_shapes=[pltpu.VMEM((8, rows, 128), F32), pltpu.SemaphoreType.DMA((7,)), pltpu.SemaphoreType.DMA((7,))],
                  compiler_params=_params(has_side_effects=True), name=name)(buf)


def _pack(arrs):
    flat = jnp.concatenate([a.reshape(-1).astype(F32) for a in arrs])
    n = flat.shape[0]
    rows = -(-n // 1024) * 8
    return jnp.pad(flat, (0, rows * 128 - n)).reshape(rows, 128)


def _unpack(buf, shapes):
    flat = buf.reshape(-1)
    out, o = [], 0
    for s in shapes:
        n = 1
        for d in s:
            n *= d
        out.append(flat[o:o + n].reshape(s))
        o += n
    return out


def _perm_xbc(a):
    parts = []
    for g in range(N_GROUPS):
        parts += [a[..., 512 * g:512 * (g + 1)], a[..., 2048 + 128 * g:2048 + 128 * (g + 1)],
                  a[..., 2560 + 128 * g:2560 + 128 * (g + 1)]]
    return jnp.concatenate(parts, axis=-1)


def _unperm_xbc(a):
    xs = [a[..., GROUP_W * g:GROUP_W * g + 512] for g in range(N_GROUPS)]
    bs = [a[..., GROUP_W * g + 512:GROUP_W * g + 640] for g in range(N_GROUPS)]
    cs = [a[..., GROUP_W * g + 640:GROUP_W * (g + 1)] for g in range(N_GROUPS)]
    return jnp.concatenate(xs + bs + cs, axis=-1)


def _split_w_in(w4):
    k = w4.shape[1]
    nat = jnp.transpose(w4, (1, 0, 2)).reshape(k, -1)
    w_z = nat[:, :2048]
    w_xbc = _perm_xbc(nat[:, 2048:5120])
    w_dt = jnp.pad(nat[:, 5120:5152], ((0, 0), (0, DT_PAD - HEADS)))
    w_s = nat[:, 5152:]
    return w_z, w_xbc, w_dt, w_s


def _join_dw_in(dw_z, dw_xbc, dw_dt, dw_s):
    k = dw_z.shape[0]
    nat = jnp.concatenate([dw_z, _unperm_xbc(dw_xbc), dw_dt[:, :HEADS], dw_s], axis=1)
    return jnp.transpose(nat.reshape(k, 4, -1), (1, 0, 2))


def kernel(x, mem, norm_mix, w_in, ssd_conv_w, ssd_conv_b, dt_bias, a_log, d_skip, ssd_norm, sc_conv_w, sc_norm, w_out, mem_norm, norm_xa, w_q, w_k, w_v, w_o, norm_ffn, w_gate, w_up, w_down, norm_final, loss_target, m_norm_mix, m_w_in, m_ssd_conv_w, m_ssd_conv_b, m_dt_bias, m_a_log, m_d_skip, m_ssd_norm, m_sc_conv_w, m_sc_norm, m_w_out, m_mem_norm, m_norm_xa, m_w_q, m_w_k, m_w_v, m_w_o, m_norm_ffn, m_w_gate, m_w_up, m_w_down, m_norm_final, v_norm_mix, v_w_in, v_ssd_conv_w, v_ssd_conv_b, v_dt_bias, v_a_log, v_d_skip, v_ssd_norm, v_sc_conv_w, v_sc_norm, v_w_out, v_mem_norm, v_norm_xa, v_w_q, v_w_k, v_w_v, v_w_o, v_norm_ffn, v_w_gate, v_w_up, v_w_down, v_norm_final):
    depth = w_in.shape[0]
    ix, iy, ic = lax.axis_index("x"), lax.axis_index("y"), lax.axis_index("c")
    qme = 2 * ix + iy
    c_arr = jnp.reshape(ic, (1,)).astype(jnp.int32)
    q_arr = jnp.reshape(qme, (1,)).astype(jnp.int32)
    h = x[0]
    tgt = loss_target[0]

    big = dict(w_in=w_in, w_out=w_out, w_q=w_q, w_k=w_k, w_v=w_v, w_o=w_o, w_gate=w_gate, w_up=w_up, w_down=w_down)
    big_m = dict(w_in=m_w_in, w_out=m_w_out, w_q=m_w_q, w_k=m_w_k, w_v=m_w_v, w_o=m_w_o, w_gate=m_w_gate, w_up=m_w_up, w_down=m_w_down)
    big_v = dict(w_in=v_w_in, w_out=v_w_out, w_q=v_w_q, w_k=v_w_k, w_v=v_w_v, w_o=v_w_o, w_gate=v_w_gate, w_up=v_w_up, w_down=v_w_down)
    names = list(big)

    conv_full = jnp.zeros((depth, 4, D_XBC), F32)
    conv_full = lax.dynamic_update_slice(conv_full, jnp.where(ic == 0, ssd_conv_w, 0.0), (0, 0, qme * (D_XBC // 4)))
    sc_full = jnp.zeros((depth, 3, D_SC), F32)
    sc_full = lax.dynamic_update_slice(sc_full, jnp.where(ic == 0, sc_conv_w, 0.0), (0, 0, qme * (D_SC // 4)))
    conv_full, sc_full = _unpack(_allreduce_small(_pack([conv_full, sc_full]), name="gather_conv_w"),
                                 [conv_full.shape, sc_full.shape])
    conv_p = _perm_xbc(conv_full)
    convb_p = _perm_xbc(ssd_conv_b)

    pad_h = lambda a: jnp.pad(a, ((0, 0), (0, DT_PAD - HEADS)))
    dt_bias_p, a_log_p = pad_h(dt_bias), pad_h(a_log)
    dskip_ch = jnp.repeat(d_skip, D_SSD // HEADS, axis=1)

    def finish_gather(tag, group, sems, flying, after):
        landed = _gather_wait(flying, sems, after, name=f"gather_wait{tag}")
        return dict(zip(group, _gather_forward(landed, name=f"gather_fwd{tag}")))

    placed0 = [_cast_place(big[n][0], q_arr, name=f"cast_{n}0") for n in names]
    sems_a, fly_a, tok_a = _gather_start(placed0[:1], q_arr, name="gather_start0a")
    sems_b, fly_b, tok_b = _gather_start(placed0[1:], tok_a, name="gather_start0b")
    placed = [placed0] + [[_cast_place(big[n][l], q_arr, dep=tok_b, name=f"cast_{n}{l}") for n in names]
                          for l in range(1, depth)]
    memn = _rms_fwd(mem[0], mem_norm.reshape(1, D) + tok_b[:1, :1], name="memn")
    gathered = [finish_gather("0a", names[:1], sems_a, fly_a, placed[-1][-1])]

    saved = []
    for l in range(depth):
        gw = gathered[l]
        gain = norm_mix[l:l + 1]
        if l + 1 < depth:
            sems, flying, token = _gather_start(placed[l + 1], gw["w_in"], name=f"gather_start{l + 1}")
            gain = gain + token[:1, :1]
        w_z, w_xbc, w_dt, w_s = _split_w_in(gw["w_in"])
        hn1 = _rms_fwd(h, gain, name=f"rms_mix{l}")
        pz = _mm_nn(hn1, w_z, tn=1024, name=f"proj_z{l}")
        pxbc = _mm_nn(hn1, w_xbc, tn=1024, name=f"proj_xbc{l}")
        dtr = _mm_nn(hn1, w_dt, tn=DT_PAD, out_dtype=F32, name=f"proj_dt{l}")
        t = h.shape[0]
        tm = _tile(t, 1024)
        ps = _mm(hn1, w_s, mode="nn", grid=(t // tm, 6, 1),
                 a_spec=pl.BlockSpec((tm, D), lambda i, j, kk: (i, 0)),
                 b_spec=pl.BlockSpec((D, 1024), lambda i, j, kk: (0, j)),
                 o_spec=pl.BlockSpec((None, tm, 1024), lambda i, j, kk: (j // 2, i, j % 2)), o_tile=(tm, 1024),
                 out_sds=_sds((3, t, D_SC), BF16), name=f"proj_s{l}")
        xc = _conv_fwd(pxbc, conv_p[l], convb_p[l:l + 1], name=f"conv{l}")
        dt, dtg, acsg, acst = _dt_prep(dtr, dt_bias_p[l:l + 1], a_log_p[l:l + 1], name=f"dt_prep{l}")
        y, states, mix = _ssd_fwd(xc, dtg, acsg, acst, pz, dskip_ch[l:l + 1], ssd_norm[l:l + 1], name=f"ssd{l}")
        mix = _sc_fwd(ps, sc_full[l], sc_norm[l:l + 1], mix, name=f"sc{l}")
        if l == 0:
            gw.update(finish_gather("0b", names[1:], sems_b, fly_b, mix))
        wo2 = gw["w_out"].reshape(-1, D)
        wq2, wk2, wv2 = (gw[n].reshape(D, D_XA) for n in ("w_q", "w_k", "w_v"))
        wd2 = gw["w_down"].reshape(D_FF, D)
        s = dict(h0=h, w_z=w_z, w_xbc=w_xbc, w_dt=w_dt, w_s=w_s, wo2=wo2, wq2=wq2, wk2=wk2, wv2=wv2, wd2=wd2)
        h1 = _mm_nn(mix, wo2, tn=1024, tm=512, out_dtype=F32, add=h, name=f"out_proj{l}")
        hn2 = _rms_fwd(h1, norm_xa[l:l + 1], name=f"rms_xa{l}")
        q = _mm_nn(hn2, wq2, tn=D_XA, name=f"q{l}")
        k = _mm_nn(memn, wk2, tn=D_XA, name=f"k{l}")
        v = _mm_nn(memn, wv2, tn=D_XA, name=f"v{l}")
        o = _xa_fwd(q, k, v, name=f"xa{l}")
        h2 = _mm_nn_sm(o, gw["w_o"], out_dtype=F32, add=h1, name=f"xa_out{l}")
        hn3 = _rms_fwd(h2, norm_ffn[l:l + 1], name=f"rms_ffn{l}")
        fg = _mm_nn_sm(hn3, gw["w_gate"], name=f"ff_gate{l}")
        fu = _mm_nn_sm(hn3, gw["w_up"], name=f"ff_up{l}")
        act = _swiglu_fwd(fg, fu, name=f"swiglu{l}")
        h3 = _mm_nn(act, wd2, tn=1024, tm=512, out_dtype=F32, add=h2, name=f"ff_down{l}")
        s.update(hn1=hn1, pz=pz, pxbc=pxbc, dtr=dtr, ps=ps, xc=xc, dt=dt, dtg=dtg, acsg=acsg, acst=acst, y=y,
                 states=states, mix=mix, h1=h1, hn2=hn2, q=q, k=k, v=v, o=o, h2=h2, hn3=hn3, fg=fg, fu=fu, act=act)
        saved.append(s)
        h = h3
        if l + 1 < depth:
            gathered.append(finish_gather(str(l + 1), names, sems, flying, h3))

    loss_vec, dh, dhb, d_norm_final = _final(h, norm_final.reshape(1, D), tgt, name="final")
    loss = lax.psum(loss_vec[0, 0], ("x", "y", "c"))

    small = dict(norm_mix=[], ssd_conv_w=[], ssd_conv_b=[], dt_bias=[], a_log=[], d_skip=[], ssd_norm=[], sc_conv_w=[],
                 sc_norm=[], norm_xa=[], norm_ffn=[])
    dmemn = None
    carried = {n: None for n in names}
    pending = None
    first_groups = [["w_gate", "w_up", "w_down"], ["w_out", "w_q", "w_k", "w_v", "w_o"], ["w_in"]]

    def start_reduce(lyr, tag, group, grads):
        g_list = [grads[n] for n in group]
        recv_sib = _swap_halves(g_list, name=f"swap_halves{tag}")
        parts = [_add_halves(g, rb, c_arr, name=f"add_halves_{n}{lyr}") for n, g, rb in zip(group, g_list, recv_sib)]
        return (lyr, tag, group) + _owners_start(parts, name=f"owners_start{tag}")

    def finish_reduce(pend, after):
        lyr, tag, group, sems_r, parts_r, lands_r, _ = pend
        parts_r, lands_r = _owners_wait(parts_r, lands_r, sems_r, after, name=f"owners_wait{tag}")
        halves = [_sum_chips(p, rc, q_arr, c_arr, name=f"sum_chips_{n}{lyr}") for n, p, rc in zip(group, parts_r, lands_r)]
        full = _join_halves(halves, name=f"join_halves{tag}")
        for n, g in zip(group, full):
            carried[n] = _adamw_layer(lyr, big[n], big_m[n], big_v[n], g, carried[n], name=f"adamw_{n}{lyr}")

    for l in reversed(range(depth)):
        s, gw = saved[l], gathered[l]
        t = dh.shape[0]
        tm = _tile(t, 1024)
        early = []
        dact = _mm_nt(dhb, s["wd2"], tn=FF_CW, dep=None if pending is None else pending[-1], name=f"d_act{l}")
        dw_down = _mm_tn(s["act"], dhb, tm=FF_CW, tn=1024, name=f"dw_down{l}")
        dg, du = _swiglu_bwd(s["fg"], s["fu"], dact, name=f"d_swiglu{l}")
        dw_gate = _mm_tn_sm(s["hn3"], dg, tm=1024, name=f"dw_gate{l}")
        dw_up = _mm_tn_sm(s["hn3"], du, tm=1024, name=f"dw_up{l}")
        dhn = _mm_nt_sm(dg, gw["w_gate"], tn=512, out_dtype=F32, name=f"d_hn3a{l}")
        dhn = _mm_nt_sm(du, gw["w_up"], tn=512, out_dtype=F32, add=dhn, name=f"d_hn3b{l}")
        dh, dhb, dn = _rms_bwd(s["h2"], norm_ffn[l:l + 1], dhn, dh, name=f"d_rms_ffn{l}")
        small["norm_ffn"].append(dn)
        dep = None
        if l == 0:
            early.append(start_reduce(0, "0a", first_groups[0],
                                      dict(w_gate=dw_gate, w_up=dw_up, w_down=dw_down.reshape(4, -1, D))))
            dep = early[-1][-1]
        do = _mm_nt_sm(dhb, gw["w_o"], tn=D_XA, dep=dep, name=f"d_o{l}")
        dw_o = _mm_tn_sm(s["o"], dhb, tm=D_XA, name=f"dw_o{l}")
        dq, dk, dv = _xa_bwd(s["q"], s["k"], s["v"], do, name=f"d_xa{l}")
        dw_q = _mm_tn(s["hn2"], dq, tm=1024, tn=D_XA, name=f"dw_q{l}")
        dw_k = _mm_tn(memn, dk, tm=1024, tn=D_XA, name=f"dw_k{l}")
        dw_v = _mm_tn(memn, dv, tm=1024, tn=D_XA, name=f"dw_v{l}")
        dhn = _mm_nt(dq, s["wq2"], tn=1024, out_dtype=F32, name=f"d_hn2{l}")
        dmemn = _mm_nt(dk, s["wk2"], tn=1024, out_dtype=F32, add=dmemn, name=f"d_memn_k{l}")
        dmemn = _mm_nt(dv, s["wv2"], tn=1024, out_dtype=F32, add=dmemn, name=f"d_memn_v{l}")
        dh, dhb, dn = _rms_bwd(s["h1"], norm_xa[l:l + 1], dhn, dh, name=f"d_rms_xa{l}")
        small["norm_xa"].append(dn)
        dw_out = _mm_tn(s["mix"], dhb, tm=1024, tn=1024, name=f"dw_out{l}")
        dep = None
        if l == 0:
            early.append(start_reduce(0, "0b", first_groups[1],
                                      dict(w_out=dw_out.reshape(4, -1, D), w_q=dw_q.reshape(4, -1, D_XA),
                                           w_k=dw_k.reshape(4, -1, D_XA), w_v=dw_v.reshape(4, -1, D_XA), w_o=dw_o)))
            dep = early[-1][-1]
        dmix = _mm_nt(dhb, s["wo2"], tn=1024, dep=dep, name=f"d_mix{l}")
        dps, d_scw, d_scn = _sc_bwd(s["ps"], sc_full[l], sc_norm[l:l + 1], dmix, name=f"d_sc{l}")
        dxc, dz, ddtg, dacg, dart, ddsk, d_ssdn = _ssd_bwd(s["xc"], s["dtg"], s["acsg"], s["acst"], s["pz"], s["y"], s["states"],
                                                           dmix, dskip_ch[l:l + 1], ssd_norm[l:l + 1], name=f"d_ssd{l}")
        dxbc, d_cw, d_cb = _conv_bwd(dxc, s["pxbc"], conv_p[l], convb_p[l:l + 1], name=f"d_conv{l}")
        ddtr, d_dtb, d_alog = _dt_bwd(ddtg, dacg, dart, s["dt"], s["dtr"], dt_bias_p[l:l + 1], a_log_p[l:l + 1], name=f"d_dt{l}")
        small["sc_conv_w"].append(d_scw)
        small["sc_norm"].append(d_scn)
        small["ssd_norm"].append(d_ssdn)
        small["d_skip"].append(jnp.sum(ddsk.reshape(HEADS, D_SSD // HEADS), axis=1).reshape(1, HEADS))
        small["ssd_conv_w"].append(_unperm_xbc(d_cw))
        small["ssd_conv_b"].append(_unperm_xbc(d_cb))
        small["dt_bias"].append(d_dtb[:, :HEADS])
        small["a_log"].append(d_alog[:, :HEADS])
        hn1 = s["hn1"]
        dw_z = _mm_tn(hn1, dz, tm=1024, tn=1024, name=f"dw_z{l}")
        dw_xbc = _mm_tn(hn1, dxbc, tm=1024, tn=1024, name=f"dw_xbc{l}")
        dw_dt = _mm_tn(hn1, ddtr, tm=1024, tn=DT_PAD, name=f"dw_dt{l}")
        tk = _tile(t, TN_TK)
        dw_s = _mm(hn1, dps, mode="tn", grid=(2, 6, t // tk),
                   a_spec=pl.BlockSpec((tk, 1024), lambda i, j, kk: (kk, i)),
                   b_spec=pl.BlockSpec((None, tk, 1024), lambda i, j, kk: (j // 2, kk, j % 2)),
                   o_spec=pl.BlockSpec((1024, 1024), lambda i, j, kk: (i, j)), o_tile=(1024, 1024),
                   out_sds=_sds((D, 3 * D_SC), BF16), name=f"dw_s{l}")
        dhn = _mm_nt(dz, s["w_z"], tn=1024, out_dtype=F32, name=f"d_hn1z{l}")
        dhn = _mm_nt(dxbc, s["w_xbc"], tn=1024, out_dtype=F32, add=dhn, name=f"d_hn1x{l}")
        dhn = _mm_nt(ddtr, s["w_dt"], tn=1024, out_dtype=F32, add=dhn, name=f"d_hn1d{l}")
        dhn = _mm(dps, s["w_s"], mode="nt", grid=(t // tm, 2, 3),
                  a_spec=pl.BlockSpec((None, tm, D_SC), lambda i, j, kk: (kk, i, 0)),
                  b_spec=pl.BlockSpec((1024, D_SC), lambda i, j, kk: (j, kk)),
                  o_spec=pl.BlockSpec((tm, 1024), lambda i, j, kk: (i, j)), o_tile=(tm, 1024),
                  out_sds=_sds((t, D), F32), add=dhn, name=f"d_hn1s{l}")
        dh, dhb, dn = _rms_bwd(s["h0"], norm_mix[l:l + 1], dhn, dh, name=f"d_rms_mix{l}")
        small["norm_mix"].append(dn)

        grads = dict(w_in=_join_dw_in(dw_z, dw_xbc, dw_dt, dw_s), w_out=dw_out.reshape(4, -1, D),
                     w_q=dw_q.reshape(4, -1, D_XA), w_k=dw_k.reshape(4, -1, D_XA), w_v=dw_v.reshape(4, -1, D_XA),
                     w_o=dw_o, w_gate=dw_gate, w_up=dw_up, w_down=dw_down.reshape(4, -1, D))
        if pending is not None:
            finish_reduce(pending, dh)
        if l == 0:
            pending = start_reduce(0, "0c", first_groups[2], grads)
            for pend in early:
                finish_reduce(pend, pending[-1])
        else:
            pending = start_reduce(l, str(l), names, grads)

    finish_reduce(pending, carried[first_groups[1][0]][0])
    grad_x = dh[None]

    _, _, d_mem_norm = _rms_bwd(mem[0], mem_norm.reshape(1, D), dmemn, jnp.zeros_like(dmemn), name="d_mem_norm")
    stack = lambda n: jnp.concatenate(small[n][::-1], axis=0) if small[n][0].ndim == 2 and small[n][0].shape[0] == 1 \
        else jnp.stack(small[n][::-1], axis=0)
    small_names = ["norm_mix", "ssd_conv_w", "ssd_conv_b", "dt_bias", "a_log", "d_skip", "ssd_norm", "sc_conv_w", "sc_norm",
                   "mem_norm", "norm_xa", "norm_ffn", "norm_final"]
    local_g = dict(mem_norm=d_mem_norm.reshape(D), norm_final=d_norm_final.reshape(D))
    for n in small:
        local_g[n] = stack(n)
    shapes = [local_g[n].shape for n in small_names]
    red = dict(zip(small_names, _unpack(_allreduce_small(_pack([local_g[n] for n in small_names]), name="allreduce_small"), shapes)))
    red["ssd_conv_w"] = lax.dynamic_slice(red["ssd_conv_w"], (0, 0, qme * (D_XBC // 4)), ssd_conv_w.shape)
    red["sc_conv_w"] = lax.dynamic_slice(red["sc_conv_w"], (0, 0, qme * (D_SC // 4)), sc_conv_w.shape)
    sw = dict(norm_mix=norm_mix, ssd_conv_w=ssd_conv_w, ssd_conv_b=ssd_conv_b, dt_bias=dt_bias, a_log=a_log, d_skip=d_skip,
              ssd_norm=ssd_norm, sc_conv_w=sc_conv_w, sc_norm=sc_norm, mem_norm=mem_norm, norm_xa=norm_xa, norm_ffn=norm_ffn,
              norm_final=norm_final)
    sm = dict(norm_mix=m_norm_mix, ssd_conv_w=m_ssd_conv_w, ssd_conv_b=m_ssd_conv_b, dt_bias=m_dt_bias, a_log=m_a_log,
              d_skip=m_d_skip, ssd_norm=m_ssd_norm, sc_conv_w=m_sc_conv_w, sc_norm=m_sc_norm, mem_norm=m_mem_norm,
              norm_xa=m_norm_xa, norm_ffn=m_norm_ffn, norm_final=m_norm_final)
    sv = dict(norm_mix=v_norm_mix, ssd_conv_w=v_ssd_conv_w, ssd_conv_b=v_ssd_conv_b, dt_bias=v_dt_bias, a_log=v_a_log,
              d_skip=v_d_skip, ssd_norm=v_ssd_norm, sc_conv_w=v_sc_conv_w, sc_norm=v_sc_norm, mem_norm=v_mem_norm,
              norm_xa=v_norm_xa, norm_ffn=v_norm_ffn, norm_final=v_norm_final)
    shard_shapes = [sw[n].shape for n in small_names]
    pk = lambda d: _pack([d[n] for n in small_names])
    sd, snm, snv = _adamw_flat(pk(sw), pk(red), pk(sm), pk(sv), name="adamw_small")
    s_delta = dict(zip(small_names, _unpack(sd, shard_shapes)))
    s_newm = dict(zip(small_names, _unpack(snm, shard_shapes)))
    s_newv = dict(zip(small_names, _unpack(snv, shard_shapes)))

    order = ["norm_mix", "w_in", "ssd_conv_w", "ssd_conv_b", "dt_bias", "a_log", "d_skip", "ssd_norm", "sc_conv_w", "sc_norm",
             "w_out", "mem_norm", "norm_xa", "w_q", "w_k", "w_v", "w_o", "norm_ffn", "w_gate", "w_up", "w_down", "norm_final"]

    def pick(kind):
        out = []
        for n in order:
            if n in carried:
                out.append(carried[n][kind])
            else:
                out.append([red, s_delta, s_newm, s_newv][kind][n])
        return out

    return (loss, grad_x, *pick(0), *pick(1), *pick(2), *pick(3))
```

```python
import functools

import jax
import jax.numpy as jnp
from jax import lax
from jax.experimental import pallas as pl
from jax.experimental.pallas import tpu as pltpu

F32 = jnp.float32
BF16 = jnp.bfloat16
MESH = pl.DeviceIdType.MESH

D = 2048
D_SSD = 2048
N_GROUPS = 4
GROUP_W = 768
D_XBC = 3072
N_STATE = 128
HEADS = 32
PAIRS_PER_GROUP = 4
CHUNK = 256
DT_PAD = 128
D_SC = 2048
SC_GROUP = 128
XA_HEADS = 4
XA_HD = 128
D_XA = 512
D_FF = 5632
EPS = 1e-5
HALO = 16
TN_TK = 2048
VMEM_LIMIT = 56 * 1024 * 1024

ADAM_LR, ADAM_B1, ADAM_B2, ADAM_EPS, ADAM_WD, ADAM_STEP = 0.001, 0.9, 0.999, 1e-08, 0.01, 10

NT = (((1,), (1,)), ((), ()))
TN = (((0,), (0,)), ((), ()))
NN = (((1,), (0,)), ((), ()))


def _pcall(body, **kw):
    return pl.pallas_call(body, **kw)


def _params(**kw):
    return pltpu.CompilerParams(vmem_limit_bytes=VMEM_LIMIT, **kw)


def _sds(shape, dtype):
    return jax.ShapeDtypeStruct(shape, dtype)


def _sig(x):
    return 1.0 / (1.0 + jnp.exp(-x))


def _dot(a, b, dims=NN):
    return lax.dot_general(a, b, dims, preferred_element_type=F32)


def _mm(a, b, *, mode, grid, a_spec, b_spec, o_spec, o_tile, out_sds, add=None, dep=None, name):
    gk = grid[2]
    dims = {"nn": NN, "nt": NT, "tn": TN, "nt4": NT}[mode]
    has_add = add is not None
    n_dep = 0 if dep is None else 1

    def body(*refs):
        a_ref, b_ref = refs[0], refs[1]
        add_ref = refs[2] if has_add else None
        refs = refs[:2 + has_add] + refs[2 + has_add + n_dep:]
        o_ref = refs[2 + has_add]
        if mode == "nt4":
            bv = jnp.concatenate([b_ref[s] for s in range(4)], axis=1)
        else:
            bv = b_ref[...].astype(BF16)
        p = _dot(a_ref[...].astype(BF16), bv, dims)

        def finish(acc):
            if has_add:
                acc = acc + add_ref[...]
            o_ref[...] = acc.astype(o_ref.dtype)

        if gk == 1:
            finish(p)
        else:
            acc_ref = refs[3 + has_add]
            k = pl.program_id(2)

            @pl.when(k == 0)
            def _():
                acc_ref[...] = p

            @pl.when(k > 0)
            def _():
                acc_ref[...] += p

            @pl.when(k == gk - 1)
            def _():
                finish(acc_ref[...])

    in_specs = [a_spec, b_spec] + ([o_spec] if has_add else []) + [pl.BlockSpec(memory_space=pl.ANY)] * n_dep
    args = (a, b) + ((add,) if has_add else ()) + ((dep,) if n_dep else ())
    scratch = [pltpu.VMEM(o_tile, F32)] if gk > 1 else []
    return _pcall(body, grid=grid, in_specs=in_specs, out_specs=o_spec, out_shape=out_sds, scratch_shapes=scratch,
                  compiler_params=_params(dimension_semantics=("parallel", "parallel", "arbitrary")), name=name)(*args)


def _tile(n, pref):
    t = min(n, pref)
    assert n % t == 0, (n, pref)
    return t


def _mm_nn(a, w, *, tn, tk=None, tm=1024, out_dtype=BF16, add=None, dep=None, name):
    m, k = a.shape
    n = w.shape[1]
    tm = _tile(m, tm)
    tk = k if tk is None else tk
    grid = (m // tm, n // tn, k // tk)
    return _mm(a, w, mode="nn", grid=grid,
               a_spec=pl.BlockSpec((tm, tk), lambda i, j, kk: (i, kk)),
               b_spec=pl.BlockSpec((tk, tn), lambda i, j, kk: (kk, j)),
               o_spec=pl.BlockSpec((tm, tn), lambda i, j, kk: (i, j)), o_tile=(tm, tn),
               out_sds=_sds((m, n), out_dtype), add=add, dep=dep, name=name)


def _mm_nn_sm(a, w4, *, out_dtype=BF16, add=None, name):
    m, k = a.shape
    n = w4.shape[2]
    tm = _tile(m, 1024)
    return _mm(a, w4, mode="nn", grid=(m // tm, 4, 1),
               a_spec=pl.BlockSpec((tm, k), lambda i, j, kk: (i, 0)),
               b_spec=pl.BlockSpec((None, k, n), lambda i, j, kk: (j, 0, 0)),
               o_spec=pl.BlockSpec((tm, n), lambda i, j, kk: (i, j)), o_tile=(tm, n),
               out_sds=_sds((m, 4 * n), out_dtype), add=add, name=name)


def _mm_nt(a, w, *, tn, tk=None, out_dtype=BF16, add=None, dep=None, name):
    m, k = a.shape
    n = w.shape[0]
    tm = _tile(m, 1024)
    tk = k if tk is None else tk
    grid = (m // tm, n // tn, k // tk)
    return _mm(a, w, mode="nt", grid=grid,
               a_spec=pl.BlockSpec((tm, tk), lambda i, j, kk: (i, kk)),
               b_spec=pl.BlockSpec((tn, tk), lambda i, j, kk: (j, kk)),
               o_spec=pl.BlockSpec((tm, tn), lambda i, j, kk: (i, j)), o_tile=(tm, tn),
               out_sds=_sds((m, n), out_dtype), add=add, dep=dep, name=name)


def _mm_nt_sm(a, w4, *, tn, out_dtype=BF16, add=None, dep=None, name):
    m = a.shape[0]
    _, k, n = w4.shape
    tm = _tile(m, 512)
    tn = _tile(k, tn)
    return _mm(a, w4, mode="nt4", grid=(m // tm, k // tn, 1),
               a_spec=pl.BlockSpec((tm, 4 * n), lambda i, j, kk: (i, 0)),
               b_spec=pl.BlockSpec((4, tn, n), lambda i, j, kk: (0, j, 0)),
               o_spec=pl.BlockSpec((tm, tn), lambda i, j, kk: (i, j)), o_tile=(tm, tn),
               out_sds=_sds((m, k), out_dtype), add=add, dep=dep, name=name)


def _mm_tn(a, g, *, tm, tn, out_dtype=BF16, name):
    t, m = a.shape
    n = g.shape[1]
    tk = _tile(t, TN_TK)
    return _mm(a, g, mode="tn", grid=(m // tm, n // tn, t // tk),
               a_spec=pl.BlockSpec((tk, tm), lambda i, j, kk: (kk, i)),
               b_spec=pl.BlockSpec((tk, tn), lambda i, j, kk: (kk, j)),
               o_spec=pl.BlockSpec((tm, tn), lambda i, j, kk: (i, j)), o_tile=(tm, tn),
               out_sds=_sds((m, n), out_dtype), name=name)


def _mm_tn_sm(a, g, *, tm, out_dtype=BF16, name):
    t, m = a.shape
    n = g.shape[1] // 4
    tk = _tile(t, TN_TK)
    return _mm(a, g, mode="tn", grid=(m // tm, 4, t // tk),
               a_spec=pl.BlockSpec((tk, tm), lambda i, j, kk: (kk, i)),
               b_spec=pl.BlockSpec((tk, n), lambda i, j, kk: (kk, j)),
               o_spec=pl.BlockSpec((None, tm, n), lambda i, j, kk: (j, i, 0)), o_tile=(tm, n),
               out_sds=_sds((4, m, n), out_dtype), name=name)


def _rms_fwd(h, g, *, name):
    t, d = h.shape
    tr = _tile(t, 512)

    def body(h_ref, g_ref, o_ref):
        x = h_ref[...]
        r = lax.rsqrt(jnp.mean(x * x, axis=-1, keepdims=True) + EPS)
        o_ref[...] = (x * r * g_ref[...]).astype(o_ref.dtype)

    return _pcall(body, grid=(t // tr,),
                  in_specs=[pl.BlockSpec((tr, d), lambda i: (i, 0)), pl.BlockSpec((1, d), lambda i: (0, 0))],
                  out_specs=pl.BlockSpec((tr, d), lambda i: (i, 0)), out_shape=_sds((t, d), BF16),
                  compiler_params=_params(dimension_semantics=("parallel",)), name=name)(h, g)


def _rms_bwd(h, g, dy, dres, *, name):
    t, d = h.shape
    tr = _tile(t, 256)

    def body(h_ref, g_ref, dy_ref, dres_ref, dh_ref, dhb_ref, dg_ref):
        i = pl.program_id(0)
        x = h_ref[...]
        r = lax.rsqrt(jnp.mean(x * x, axis=-1, keepdims=True) + EPS)
        xh = x * r
        dyv = dy_ref[...].astype(F32)
        dxh = dyv * g_ref[...]
        dh = dres_ref[...] + r * (dxh - xh * jnp.mean(dxh * xh, axis=-1, keepdims=True))
        dh_ref[...] = dh
        dhb_ref[...] = dh.astype(BF16)
        part = jnp.sum(dyv * xh, axis=0, keepdims=True)

        @pl.when(i == 0)
        def _():
            dg_ref[...] = part

        @pl.when(i > 0)
        def _():
            dg_ref[...] += part

    row = pl.BlockSpec((tr, d), lambda i: (i, 0))
    vec = pl.BlockSpec((1, d), lambda i: (0, 0))
    return _pcall(body, grid=(t // tr,), in_specs=[row, vec, row, row], out_specs=[row, row, vec],
                  out_shape=[_sds((t, d), F32), _sds((t, d), BF16), _sds((1, d), F32)],
                  compiler_params=_params(dimension_semantics=("arbitrary",)), name=name)(h, g, dy, dres)


def _final(h, g, tgt, *, name):
    t, d = h.shape
    tr = _tile(t, 256)

    def body(h_ref, g_ref, t_ref, loss_ref, dh_ref, dhb_ref, dg_ref):
        i = pl.program_id(0)
        x = h_ref[...]
        gv = g_ref[...]
        r = lax.rsqrt(jnp.mean(x * x, axis=-1, keepdims=True) + EPS)
        xh = x * r
        e = xh * gv - t_ref[...]
        lpart = jnp.zeros((1, 128), F32) + 0.5 * jnp.sum(jnp.mean(e * e, axis=-1, keepdims=True))
        dyv = e * (1.0 / d)
        dxh = dyv * gv
        dh = r * (dxh - xh * jnp.mean(dxh * xh, axis=-1, keepdims=True))
        dh_ref[...] = dh
        dhb_ref[...] = dh.astype(BF16)
        part = jnp.sum(dyv * xh, axis=0, keepdims=True)

        @pl.when(i == 0)
        def _():
            dg_ref[...] = part
            loss_ref[...] = lpart

        @pl.when(i > 0)
        def _():
            dg_ref[...] += part
            loss_ref[...] += lpart

    row = pl.BlockSpec((tr, d), lambda i: (i, 0))
    vec = pl.BlockSpec((1, d), lambda i: (0, 0))
    return _pcall(body, grid=(t // tr,), in_specs=[row, vec, row],
                  out_specs=[pl.BlockSpec((1, 128), lambda i: (0, 0)), row, row, vec],
                  out_shape=[_sds((1, 128), F32), _sds((t, d), F32), _sds((t, d), BF16), _sds((1, d), F32)],
                  compiler_params=_params(dimension_semantics=("arbitrary",)), name=name)(h, g, tgt)


def _conv_taps(ext, w, ntap, rows):
    n = ext.shape[0]
    acc = w[ntap - 1:ntap, :] * ext[HALO:HALO + rows]
    for k in range(1, ntap):
        acc = acc + w[ntap - 1 - k:ntap - k, :] * pltpu.roll(ext, k, axis=0)[HALO:HALO + rows]
    del n
    return acc


def _conv_fwd(xbc, w, b, *, name):
    t, c = xbc.shape
    rows = CHUNK
    cw = GROUP_W
    hb = rows // HALO

    def body(cur_ref, prev_ref, w_ref, b_ref, o_ref):
        i = pl.program_id(1)
        cur = cur_ref[...].astype(F32)
        prev = jnp.where(i > 0, prev_ref[...].astype(F32), 0.0)
        ext = jnp.concatenate([prev, cur], axis=0)
        pre = _conv_taps(ext, w_ref[...], 4, rows) + b_ref[...]
        o_ref[...] = (pre * _sig(pre)).astype(o_ref.dtype)

    return _pcall(body, grid=(c // cw, t // rows),
                  in_specs=[pl.BlockSpec((rows, cw), lambda j, i: (i, j)),
                            pl.BlockSpec((HALO, cw), lambda j, i: (jnp.maximum(i * hb - 1, 0), j)),
                            pl.BlockSpec((4, cw), lambda j, i: (0, j)),
                            pl.BlockSpec((1, cw), lambda j, i: (0, j))],
                  out_specs=pl.BlockSpec((rows, cw), lambda j, i: (i, j)), out_shape=_sds((t, c), BF16),
                  compiler_params=_params(dimension_semantics=("parallel", "parallel")), name=name)(xbc, xbc, w, b)


def _conv_bwd(dxc, xbc, w, b, *, name):
    t, c = xbc.shape
    rows = CHUNK
    cw = GROUP_W
    hb = rows // HALO
    nblk = t // rows
    nhalo = t // HALO

    def body(d_ref, dn_ref, cur_ref, prev_ref, next_ref, w_ref, b_ref, dx_ref, dw_ref, db_ref):
        i = pl.program_id(1)
        last = i == nblk - 1
        wv = w_ref[...]
        xe = jnp.concatenate([jnp.where(i > 0, prev_ref[...].astype(F32), 0.0), cur_ref[...].astype(F32),
                              jnp.where(last, 0.0, next_ref[...].astype(F32))], axis=0)
        n = rows + 2 * HALO
        sh = [xe] + [pltpu.roll(xe, k, axis=0) for k in range(1, 4)]
        pre = wv[3:4, :] * sh[0] + wv[2:3, :] * sh[1] + wv[1:2, :] * sh[2] + wv[0:1, :] * sh[3] + b_ref[...]
        de = jnp.concatenate([jnp.zeros((HALO, cw), F32), d_ref[...].astype(F32),
                              jnp.where(last, 0.0, dn_ref[...].astype(F32))], axis=0)
        s = _sig(pre)
        dpre = de * (s * (1.0 + pre * (1.0 - s)))
        dx = wv[3:4, :] * dpre
        for m in range(1, 4):
            dx = dx + wv[3 - m:4 - m, :] * pltpu.roll(dpre, n - m, axis=0)
        dx_ref[...] = dx[HALO:HALO + rows].astype(dx_ref.dtype)
        dcur = dpre[HALO:HALO + rows]
        dwv = jnp.concatenate([jnp.sum(dcur * sh[3 - j][HALO:HALO + rows], axis=0, keepdims=True) for j in range(4)], axis=0)
        dbv = jnp.sum(dcur, axis=0, keepdims=True)

        @pl.when(i == 0)
        def _():
            dw_ref[...] = dwv
            db_ref[...] = dbv

        @pl.when(i > 0)
        def _():
            dw_ref[...] += dwv
            db_ref[...] += dbv

    cur = pl.BlockSpec((rows, cw), lambda j, i: (i, j))
    prev = pl.BlockSpec((HALO, cw), lambda j, i: (jnp.maximum(i * hb - 1, 0), j))
    nxt = pl.BlockSpec((HALO, cw), lambda j, i: (jnp.minimum((i + 1) * hb, nhalo - 1), j))
    return _pcall(body, grid=(c // cw, nblk),
                  in_specs=[cur, nxt, cur, prev, nxt, pl.BlockSpec((4, cw), lambda j, i: (0, j)),
                            pl.BlockSpec((1, cw), lambda j, i: (0, j))],
                  out_specs=[cur, pl.BlockSpec((4, cw), lambda j, i: (0, j)), pl.BlockSpec((1, cw), lambda j, i: (0, j))],
                  out_shape=[_sds((t, c), BF16), _sds((4, c), F32), _sds((1, c), F32)],
                  compiler_params=_params(dimension_semantics=("parallel", "arbitrary")), name=name)(dxc, dxc, xbc, xbc, xbc, w, b)


def _neg_exp_alog(alog):
    lane = lax.broadcasted_iota(jnp.int32, alog.shape, 1)
    return jnp.where(lane < HEADS, -jnp.exp(alog), 0.0)


def _dt_prep(dtr, bias, alog, *, name):
    t = dtr.shape[0]
    rows = CHUNK

    def body(r_ref, b_ref, a_ref, dt_ref, dtg_ref, acsg_ref, acst_ref):
        raw = r_ref[...] + b_ref[...]
        dt = jnp.maximum(raw, 0.0) + jnp.log(1.0 + jnp.exp(-jnp.abs(raw)))
        a = _neg_exp_alog(a_ref[...])
        adt = dt * a
        ri = lax.broadcasted_iota(jnp.int32, (rows, rows), 0)
        ci = lax.broadcasted_iota(jnp.int32, (rows, rows), 1)
        tri = (ri >= ci).astype(F32)
        acs = jnp.dot(tri, adt, precision=lax.Precision.HIGHEST, preferred_element_type=F32)
        dt_ref[...] = dt
        acst_ref[...] = acs.T
        for g in range(N_GROUPS):
            sh = (128 - 8 * g) % 128
            dtg_ref[g] = dt if sh == 0 else pltpu.roll(dt, sh, axis=1)
            acsg_ref[g] = acs if sh == 0 else pltpu.roll(acs, sh, axis=1)

    row = pl.BlockSpec((rows, DT_PAD), lambda i: (i, 0))
    vec = pl.BlockSpec((1, DT_PAD), lambda i: (0, 0))
    grp = pl.BlockSpec((N_GROUPS, rows, DT_PAD), lambda i: (0, i, 0))
    return _pcall(body, grid=(t // rows,), in_specs=[row, vec, vec],
                  out_specs=[row, grp, grp, pl.BlockSpec((DT_PAD, rows), lambda i: (0, i))],
                  out_shape=[_sds((t, DT_PAD), F32), _sds((N_GROUPS, t, DT_PAD), F32), _sds((N_GROUPS, t, DT_PAD), F32),
                             _sds((DT_PAD, t), F32)],
                  compiler_params=_params(dimension_semantics=("parallel",)), name=name)(dtr, bias, alog)


def _dt_bwd(ddtg, dacg, dart, dt, dtr, bias, alog, *, name):
    t = dtr.shape[0]
    rows = CHUNK

    def body(ddtg_ref, dacg_ref, dart_ref, dt_ref, r_ref, b_ref, a_ref, dr_ref, db_ref, da_ref):
        i = pl.program_id(0)
        lane = lax.broadcasted_iota(jnp.int32, (rows, DT_PAD), 1)
        ddt = jnp.zeros((rows, DT_PAD), F32)
        dacs = jnp.concatenate([dart_ref[...], jnp.zeros((DT_PAD - HEADS, rows), F32)], axis=0).T
        for g in range(N_GROUPS):
            sel = (lane >= 8 * g) & (lane < 8 * g + 8)
            dd = ddtg_ref[g]
            da = dacg_ref[g]
            if g:
                dd = pltpu.roll(dd, 8 * g, axis=1)
                da = pltpu.roll(da, 8 * g, axis=1)
            ddt = ddt + jnp.where(sel, dd, 0.0)
            dacs = dacs + jnp.where(sel, da, 0.0)
        ri = lax.broadcasted_iota(jnp.int32, (rows, rows), 0)
        ci = lax.broadcasted_iota(jnp.int32, (rows, rows), 1)
        triu = (ci >= ri).astype(F32)
        rev = jnp.dot(triu, dacs, precision=lax.Precision.HIGHEST, preferred_element_type=F32)
        a = _neg_exp_alog(a_ref[...])
        dtv = dt_ref[...]
        raw = r_ref[...] + b_ref[...]
        draw = (ddt + a * rev) * _sig(raw)
        dr_ref[...] = draw
        dbv = jnp.sum(draw, axis=0, keepdims=True)
        dav = jnp.sum(dtv * rev, axis=0, keepdims=True) * a

        @pl.when(i == 0)
        def _():
            db_ref[...] = dbv
            da_ref[...] = dav

        @pl.when(i > 0)
        def _():
            db_ref[...] += dbv
            da_ref[...] += dav

    row = pl.BlockSpec((rows, DT_PAD), lambda i: (i, 0))
    vec = pl.BlockSpec((1, DT_PAD), lambda i: (0, 0))
    grp = pl.BlockSpec((N_GROUPS, rows, DT_PAD), lambda i: (0, i, 0))
    return _pcall(body, grid=(t // rows,),
                  in_specs=[grp, grp, pl.BlockSpec((HEADS, rows), lambda i: (0, i)), row, row, vec, vec],
                  out_specs=[row, vec, vec], out_shape=[_sds((t, DT_PAD), F32), _sds((1, DT_PAD), F32), _sds((1, DT_PAD), F32)],
                  compiler_params=_params(dimension_semantics=("arbitrary",)), name=name)(ddtg, dacg, dart, dt, dtr, bias, alog)


def _pair_cols(col_ref_val, p, lo):
    return jnp.where(lo, col_ref_val[:, 2 * p:2 * p + 1], col_ref_val[:, 2 * p + 1:2 * p + 2])


def _ssd_fwd(xc, dtg, acsg, acst, z, dskip, nw, *, name):
    t = xc.shape[0]
    L = CHUNK
    nc = t // L

    def body(xc_ref, dtg_ref, acsg_ref, acst_ref, z_ref, dsk_ref, nw_ref, y_ref, st_ref, mix_ref, s_ref):
        c = pl.program_id(1)

        @pl.when(c == 0)
        def _():
            s_ref[...] = jnp.zeros_like(s_ref)

        blk = xc_ref[...]
        bm = blk[:, 512:640]
        cm = blk[:, 640:768]
        cb = _dot(cm, bm, NT)
        dtv = dtg_ref[...]
        acs = acsg_ref[...]
        acst_v = acst_ref[...]
        ri = lax.broadcasted_iota(jnp.int32, (L, L), 0)
        ci = lax.broadcasted_iota(jnp.int32, (L, L), 1)
        causal = ri >= ci
        lo = lax.broadcasted_iota(jnp.int32, (1, 128), 1) < 64
        lo_rows = lax.broadcasted_iota(jnp.int32, (128, 1), 0) < 64
        dskv = dsk_ref[...]
        ys = []
        for p in range(PAIRS_PER_GROUP):
            xp = blk[:, 128 * p:128 * p + 128].astype(F32)
            dt_p = _pair_cols(dtv, p, lo)
            a_p = _pair_cols(acs, p, lo)
            alast = acs[L - 1:L, :]
            al_p = _pair_cols(alast, p, lo)
            xdt = xp * dt_p
            xdt_b = xdt.astype(BF16)
            yd = []
            for hh in range(2):
                j = 2 * p + hh
                seg = acs[:, j:j + 1] - acst_v[j:j + 1, :]
                lam = jnp.exp(jnp.where(causal, seg, -1e30))
                w = (cb * lam).astype(BF16)
                yd.append(_dot(w, xdt_b))
            y = jnp.where(lo, yd[0], yd[1])
            sp = s_ref[p]
            st_ref[p] = sp
            y = y + _dot(cm, sp.astype(BF16), NT) * jnp.exp(a_p)
            dsc = jnp.exp(al_p - a_p)
            snew = _dot((xdt * dsc).astype(BF16), bm, TN)
            al_rows = jnp.where(lo_rows, alast[:, 2 * p:2 * p + 1], alast[:, 2 * p + 1:2 * p + 2])
            s_ref[p] = sp * jnp.exp(al_rows) + snew
            ys.append(y + xp * dskv[:, 128 * p:128 * p + 128])
        yfull = jnp.concatenate(ys, axis=1)
        y_ref[...] = yfull.astype(y_ref.dtype)
        zz = z_ref[...].astype(F32)
        yg = yfull * (zz * _sig(zz))
        r = lax.rsqrt(jnp.mean(yg * yg, axis=-1, keepdims=True) + EPS)
        mix_ref[...] = (yg * r * nw_ref[...]).astype(mix_ref.dtype)

    grp = pl.BlockSpec((None, L, DT_PAD), lambda g, c: (g, c, 0))
    return _pcall(body, grid=(N_GROUPS, nc),
                  in_specs=[pl.BlockSpec((L, GROUP_W), lambda g, c: (c, g)), grp, grp,
                            pl.BlockSpec((8, L), lambda g, c: (g, c)),
                            pl.BlockSpec((L, 512), lambda g, c: (c, g)),
                            pl.BlockSpec((1, 512), lambda g, c: (0, g)), pl.BlockSpec((1, 512), lambda g, c: (0, g))],
                  out_specs=[pl.BlockSpec((L, 512), lambda g, c: (c, g)),
                             pl.BlockSpec((None, PAIRS_PER_GROUP, 128, N_STATE), lambda g, c: (c, g, 0, 0)),
                             pl.BlockSpec((L, 512), lambda g, c: (c, g))],
                  out_shape=[_sds((t, D_SSD), BF16), _sds((nc, N_GROUPS * PAIRS_PER_GROUP, 128, N_STATE), F32),
                             _sds((t, D_SSD + D_SC), BF16)],
                  scratch_shapes=[pltpu.VMEM((PAIRS_PER_GROUP, 128, N_STATE), F32)],
                  compiler_params=_params(dimension_semantics=("parallel", "arbitrary")), name=name)(
                      xc, dtg, acsg, acst, z, dskip, nw)


def _ssd_bwd(xc, dtg, acsg, acst, z, y, states, dmix, dskip, nw, *, name):
    t = xc.shape[0]
    L = CHUNK
    nc = t // L

    def body(xc_ref, dtg_ref, acsg_ref, acst_ref, z_ref, y_ref, st_ref, dm_ref, dsk_ref, nw_ref,
             dxc_ref, dz_ref, ddt_ref, dac_ref, dar_ref, ddsk_ref, dnw_ref, ds_ref):
        c = pl.program_id(1)

        @pl.when(c == 0)
        def _():
            ds_ref[...] = jnp.zeros_like(ds_ref)
            ddsk_ref[...] = jnp.zeros_like(ddsk_ref)
            dnw_ref[...] = jnp.zeros_like(dnw_ref)

        blk = xc_ref[...]
        xs = blk[:, :512].astype(F32)
        bm = blk[:, 512:640]
        cm = blk[:, 640:768]
        bmf = bm.astype(F32)
        yv = y_ref[...].astype(F32)
        zz = z_ref[...].astype(F32)
        nwv = nw_ref[...]
        dout = dm_ref[...].astype(F32)
        sz = _sig(zz)
        silu = zz * sz
        yg = yv * silu
        r = lax.rsqrt(jnp.mean(yg * yg, axis=-1, keepdims=True) + EPS)
        xh = yg * r
        dnw_ref[...] += jnp.sum(dout * xh, axis=0, keepdims=True)
        dyn = dout * nwv
        dyg = r * (dyn - xh * jnp.mean(dyn * xh, axis=-1, keepdims=True))
        dy = dyg * silu
        dz_ref[...] = (dyg * yv * (sz * (1.0 + zz * (1.0 - sz)))).astype(dz_ref.dtype)
        ddsk_ref[...] += jnp.sum(dy * xs, axis=0, keepdims=True)

        cb = _dot(cm, bm, NT)
        dtv = dtg_ref[...]
        acs = acsg_ref[...]
        acst_v = acst_ref[...]
        alast = acs[L - 1:L, :]
        ri = lax.broadcasted_iota(jnp.int32, (L, L), 0)
        ci = lax.broadcasted_iota(jnp.int32, (L, L), 1)
        causal = ri >= ci
        lane = lax.broadcasted_iota(jnp.int32, (1, 128), 1)
        lo = lane < 64
        lo_rows = lax.broadcasted_iota(jnp.int32, (128, 1), 0) < 64
        lane_l = lax.broadcasted_iota(jnp.int32, (L, DT_PAD), 1)
        row_l = lax.broadcasted_iota(jnp.int32, (L, 1), 0)
        sub8 = lax.broadcasted_iota(jnp.int32, (8, L), 0)
        dskv = dsk_ref[...]
        dm_acc = jnp.zeros((L, L), F32)
        db_acc = jnp.zeros((L, N_STATE), F32)
        dc_acc = jnp.zeros((L, N_STATE), F32)
        ddt_out = jnp.zeros((L, DT_PAD), F32)
        dac_out = jnp.zeros((L, DT_PAD), F32)
        dar_out = jnp.zeros((8, L), F32)
        dxs = []
        for p in range(PAIRS_PER_GROUP):
            xp = xs[:, 128 * p:128 * p + 128]
            dyp = dy[:, 128 * p:128 * p + 128]
            dt_p = _pair_cols(dtv, p, lo)
            a_p = _pair_cols(acs, p, lo)
            al_p = _pair_cols(alast, p, lo)
            xdt = xp * dt_p
            xdt_b = xdt.astype(BF16)
            ea_p = jnp.exp(a_p)
            dsc_p = jnp.exp(al_p - a_p)
            sp = st_ref[p]
            sp_b = sp.astype(BF16)
            dsp = ds_ref[p]
            dsp_b = dsp.astype(BF16)
            cs = _dot(cm, sp_b, NT)
            dye_b = (dyp * ea_p).astype(BF16)
            dc_acc = dc_acc + _dot(dye_b, sp_b)
            ds_prev = _dot(dye_b, cm, TN)
            bds = _dot(bm, dsp_b, NT)
            al_rows = jnp.where(lo_rows, alast[:, 2 * p:2 * p + 1], alast[:, 2 * p + 1:2 * p + 2])
            ds_prev = ds_prev + jnp.exp(al_rows) * dsp
            prod_off = dyp * cs
            prod_st = dsp * sp
            dxdt_h = []
            for hh in range(2):
                j = 2 * p + hh
                hm = lo if hh == 0 else jnp.logical_not(lo)
                hm_rows = lo_rows if hh == 0 else jnp.logical_not(lo_rows)
                a_col = acs[:, j:j + 1]
                seg = a_col - acst_v[j:j + 1, :]
                lam = jnp.exp(jnp.where(causal, seg, -1e30))
                wf = cb * lam
                w = wf.astype(BF16)
                dy_h = jnp.where(hm, dyp, 0.0).astype(BF16)
                dw = _dot(dy_h, xdt_b, NT)
                dxdt_h.append(_dot(w, dyp.astype(BF16), TN))
                dm_acc = dm_acc + dw * lam
                e = dw * wf
                dac = jnp.sum(e, axis=1, keepdims=True)
                dar = -jnp.sum(e, axis=0, keepdims=True)
                ea_col = jnp.exp(a_col)
                dac = dac + ea_col * jnp.sum(jnp.where(hm, prod_off, 0.0), axis=1, keepdims=True)
                al_h = alast[:, j:j + 1]
                dal = jnp.exp(al_h) * jnp.sum(jnp.sum(jnp.where(hm_rows, prod_st, 0.0), axis=1, keepdims=True), axis=0, keepdims=True)
                xds_h = _dot(jnp.where(hm, xdt, 0.0).astype(BF16), dsp_b)
                dsc_col = jnp.exp(al_h - a_col)
                db_acc = db_acc + dsc_col * xds_h
                tt = jnp.sum(xds_h * bmf, axis=1, keepdims=True) * dsc_col
                dal = dal + jnp.sum(tt, axis=0, keepdims=True)
                dac = dac - tt + jnp.where(row_l == L - 1, dal, 0.0)
                dac_out = jnp.where(lane_l == j, dac, dac_out)
                dar_out = jnp.where(sub8 == j, dar, dar_out)
            dxdt = jnp.where(lo, dxdt_h[0], dxdt_h[1]) + dsc_p * bds
            dxs.append(dxdt * dt_p + dyp * dskv[:, 128 * p:128 * p + 128])
            prod_dt = dxdt * xp
            for hh in range(2):
                j = 2 * p + hh
                hm = lo if hh == 0 else jnp.logical_not(lo)
                ddt_col = jnp.sum(jnp.where(hm, prod_dt, 0.0), axis=1, keepdims=True)
                ddt_out = jnp.where(lane_l == j, ddt_col, ddt_out)
            ds_ref[p] = ds_prev
        dm_b = dm_acc.astype(BF16)
        dc_acc = dc_acc + _dot(dm_b, bm)
        db_acc = db_acc + _dot(dm_b, cm, TN)
        dxc_ref[...] = jnp.concatenate(dxs + [db_acc, dc_acc], axis=1).astype(dxc_ref.dtype)
        ddt_ref[...] = ddt_out
        dac_ref[...] = dac_out
        dar_ref[...] = dar_out

    rc = lambda g, c: (nc - 1 - c, g)
    grp = pl.BlockSpec((None, L, DT_PAD), lambda g, c: (g, nc - 1 - c, 0))
    vec = pl.BlockSpec((1, 512), lambda g, c: (0, g))
    return _pcall(body, grid=(N_GROUPS, nc),
                  in_specs=[pl.BlockSpec((L, GROUP_W), rc), grp, grp,
                            pl.BlockSpec((8, L), lambda g, c: (g, nc - 1 - c)),
                            pl.BlockSpec((L, 512), rc), pl.BlockSpec((L, 512), rc),
                            pl.BlockSpec((None, PAIRS_PER_GROUP, 128, N_STATE), lambda g, c: (nc - 1 - c, g, 0, 0)),
                            pl.BlockSpec((L, 512), rc), vec, vec],
                  out_specs=[pl.BlockSpec((L, GROUP_W), rc), pl.BlockSpec((L, 512), rc), grp, grp,
                             pl.BlockSpec((8, L), lambda g, c: (g, nc - 1 - c)), vec, vec],
                  out_shape=[_sds((t, D_XBC), BF16), _sds((t, D_SSD), BF16), _sds((N_GROUPS, t, DT_PAD), F32),
                             _sds((N_GROUPS, t, DT_PAD), F32), _sds((HEADS, t), F32), _sds((1, D_SSD), F32),
                             _sds((1, D_SSD), F32)],
                  scratch_shapes=[pltpu.VMEM((PAIRS_PER_GROUP, 128, N_STATE), F32)],
                  compiler_params=_params(dimension_semantics=("parallel", "arbitrary")), name=name)(
                      xc, dtg, acsg, acst, z, y, states, dmix, dskip, nw)


SC_CW = 1024


def _group_rstd(v):
    outs = []
    for q in range(v.shape[1] // SC_GROUP):
        vq = v[:, SC_GROUP * q:SC_GROUP * (q + 1)]
        outs.append(jnp.broadcast_to(lax.rsqrt(jnp.mean(vq * vq, axis=-1, keepdims=True) + EPS), vq.shape))
    return jnp.concatenate(outs, axis=1)


def _group_mean(v):
    outs = []
    for q in range(v.shape[1] // SC_GROUP):
        vq = v[:, SC_GROUP * q:SC_GROUP * (q + 1)]
        outs.append(jnp.broadcast_to(jnp.mean(vq, axis=-1, keepdims=True), vq.shape))
    return jnp.concatenate(outs, axis=1)


def _sc_fwd(ps, w, nw, mix, *, name):
    t = ps.shape[1]
    rows = CHUNK
    hb = rows // HALO
    cw = SC_CW
    off = D_SSD // cw

    def body(cur_ref, prev_ref, w_ref, nw_ref, mix_in_ref, o_ref):
        del mix_in_ref
        i = pl.program_id(1)
        u = cur_ref[0].astype(F32)
        gb = cur_ref[1].astype(F32)
        gc = cur_ref[2].astype(F32)
        cu_prev = jnp.where(i > 0, prev_ref[2].astype(F32) * prev_ref[0].astype(F32), 0.0)
        ext = jnp.concatenate([cu_prev, gc * u], axis=0)
        v = gb * _conv_taps(ext, w_ref[...], 3, rows)
        o_ref[...] = (v * _group_rstd(v) * nw_ref[...]).astype(o_ref.dtype)

    return _pcall(body, grid=(D_SC // cw, t // rows),
                  in_specs=[pl.BlockSpec((3, rows, cw), lambda j, i: (0, i, j)),
                            pl.BlockSpec((3, HALO, cw), lambda j, i: (0, jnp.maximum(i * hb - 1, 0), j)),
                            pl.BlockSpec((3, cw), lambda j, i: (0, j)), pl.BlockSpec((1, cw), lambda j, i: (0, j)),
                            pl.BlockSpec(memory_space=pl.ANY)],
                  out_specs=pl.BlockSpec((rows, cw), lambda j, i: (i, off + j)),
                  out_shape=_sds(mix.shape, mix.dtype), input_output_aliases={4: 0},
                  compiler_params=_params(dimension_semantics=("parallel", "parallel")), name=name)(ps, ps, w, nw, mix)


def _sc_bwd(ps, w, nw, dmix, *, name):
    t = ps.shape[1]
    rows = CHUNK
    hb = rows // HALO
    cw = SC_CW
    off = D_SSD // cw
    nblk = t // rows
    nhalo = t // HALO
    n = rows + 2 * HALO

    def body(cur_ref, prev_ref, next_ref, w_ref, nw_ref, d_ref, dn_ref, dps_ref, dw_ref, dnw_ref):
        i = pl.program_id(1)
        first = i == 0
        last = i == nblk - 1

        def ext(k):
            return jnp.concatenate([jnp.where(first, 0.0, prev_ref[k].astype(F32)), cur_ref[k].astype(F32),
                                    jnp.where(last, 0.0, next_ref[k].astype(F32))], axis=0)

        ue, gbe, gce = ext(0), ext(1), ext(2)
        wv = w_ref[...]
        nwv = nw_ref[...]
        cue = gce * ue
        cu1 = pltpu.roll(cue, 1, axis=0)
        cu2 = pltpu.roll(cue, 2, axis=0)
        conv = wv[2:3, :] * cue + wv[1:2, :] * cu1 + wv[0:1, :] * cu2
        ve = gbe * conv
        doe = jnp.concatenate([jnp.zeros((HALO, cw), F32), d_ref[...].astype(F32),
                               jnp.where(last, 0.0, dn_ref[...].astype(F32))], axis=0)
        r = _group_rstd(ve)
        xh = ve * r
        dvn = doe * nwv
        dv = r * (dvn - xh * _group_mean(dvn * xh))
        dconv = dv * gbe
        dcu = wv[2:3, :] * dconv + wv[1:2, :] * pltpu.roll(dconv, n - 1, axis=0) + wv[0:1, :] * pltpu.roll(dconv, n - 2, axis=0)
        sl = slice(HALO, HALO + rows)
        dps_ref[0] = (dcu * gce)[sl].astype(dps_ref.dtype)
        dps_ref[1] = (dv * conv)[sl].astype(dps_ref.dtype)
        dps_ref[2] = (dcu * ue)[sl].astype(dps_ref.dtype)
        dc = dconv[sl]
        dwv = jnp.concatenate([jnp.sum(dc * cu2[sl], axis=0, keepdims=True), jnp.sum(dc * cu1[sl], axis=0, keepdims=True),
                               jnp.sum(dc * cue[sl], axis=0, keepdims=True)], axis=0)
        dnv = jnp.sum((doe * xh)[sl], axis=0, keepdims=True)

        @pl.when(first)
        def _():
            dw_ref[...] = dwv
            dnw_ref[...] = dnv

        @pl.when(i > 0)
        def _():
            dw_ref[...] += dwv
            dnw_ref[...] += dnv

    cur = pl.BlockSpec((3, rows, cw), lambda j, i: (0, i, j))
    prev = pl.BlockSpec((3, HALO, cw), lambda j, i: (0, jnp.maximum(i * hb - 1, 0), j))
    nxt = pl.BlockSpec((3, HALO, cw), lambda j, i: (0, jnp.minimum((i + 1) * hb, nhalo - 1), j))
    return _pcall(body, grid=(D_SC // cw, nblk),
                  in_specs=[cur, prev, nxt, pl.BlockSpec((3, cw), lambda j, i: (0, j)), pl.BlockSpec((1, cw), lambda j, i: (0, j)),
                            pl.BlockSpec((rows, cw), lambda j, i: (i, off + j)),
                            pl.BlockSpec((HALO, cw), lambda j, i: (jnp.minimum((i + 1) * hb, nhalo - 1), off + j))],
                  out_specs=[cur, pl.BlockSpec((3, cw), lambda j, i: (0, j)), pl.BlockSpec((1, cw), lambda j, i: (0, j))],
                  out_shape=[_sds(ps.shape, BF16), _sds((3, D_SC), F32), _sds((1, D_SC), F32)],
                  compiler_params=_params(dimension_semantics=("parallel", "arbitrary")), name=name)(ps, ps, ps, w, nw, dmix, dmix)


XA_SCALE = XA_HD ** -0.5


def _softmax(s):
    m = jnp.max(s, axis=-1, keepdims=True)
    e = jnp.exp(s - m)
    return e / jnp.sum(e, axis=-1, keepdims=True)


def _xa_fwd(q, k, v, *, name):
    t = q.shape[0]
    nm = k.shape[0]
    tq = _tile(t, 512)

    def body(q_ref, k_ref, v_ref, o_ref):
        outs = []
        for h in range(XA_HEADS):
            sl = slice(XA_HD * h, XA_HD * (h + 1))
            s = _dot(q_ref[:, sl], k_ref[:, sl], NT) * XA_SCALE
            outs.append(_dot(_softmax(s).astype(BF16), v_ref[:, sl]))
        o_ref[...] = jnp.concatenate(outs, axis=1).astype(o_ref.dtype)

    row = pl.BlockSpec((tq, D_XA), lambda i: (i, 0))
    kv = pl.BlockSpec((nm, D_XA), lambda i: (0, 0))
    return _pcall(body, grid=(t // tq,), in_specs=[row, kv, kv], out_specs=row, out_shape=_sds((t, D_XA), BF16),
                  compiler_params=_params(dimension_semantics=("parallel",)), name=name)(q, k, v)


def _xa_bwd(q, k, v, do, *, name):
    t = q.shape[0]
    nm = k.shape[0]
    tq = _tile(t, 512)

    def body(q_ref, k_ref, v_ref, do_ref, dq_ref, dk_ref, dv_ref):
        i = pl.program_id(0)
        dqs, dks, dvs = [], [], []
        for h in range(XA_HEADS):
            sl = slice(XA_HD * h, XA_HD * (h + 1))
            qh, kh, vh, doh = q_ref[:, sl], k_ref[:, sl], v_ref[:, sl], do_ref[:, sl]
            p = _softmax(_dot(qh, kh, NT) * XA_SCALE)
            dvs.append(_dot(p.astype(BF16), doh, TN))
            dp = _dot(doh, vh, NT)
            ds = (p * (dp - jnp.sum(dp * p, axis=-1, keepdims=True)) * XA_SCALE).astype(BF16)
            dqs.append(_dot(ds, kh))
            dks.append(_dot(ds, qh, TN))
        dq_ref[...] = jnp.concatenate(dqs, axis=1).astype(dq_ref.dtype)
        dkv = jnp.concatenate(dks, axis=1)
        dvv = jnp.concatenate(dvs, axis=1)

        @pl.when(i == 0)
        def _():
            dk_ref[...] = dkv
            dv_ref[...] = dvv

        @pl.when(i > 0)
        def _():
            dk_ref[...] += dkv
            dv_ref[...] += dvv

    row = pl.BlockSpec((tq, D_XA), lambda i: (i, 0))
    kv = pl.BlockSpec((nm, D_XA), lambda i: (0, 0))
    return _pcall(body, grid=(t // tq,), in_specs=[row, kv, kv, row], out_specs=[row, kv, kv],
                  out_shape=[_sds((t, D_XA), BF16), _sds((nm, D_XA), F32), _sds((nm, D_XA), F32)],
                  compiler_params=_params(dimension_semantics=("arbitrary",)), name=name)(q, k, v, do)


FF_CW = 1408


def _swiglu_fwd(g, u, *, name):
    t, f = g.shape
    tr = _tile(t, 512)

    def body(g_ref, u_ref, o_ref):
        gv = g_ref[...].astype(F32)
        o_ref[...] = (gv * _sig(gv) * u_ref[...].astype(F32)).astype(o_ref.dtype)

    blk = pl.BlockSpec((tr, FF_CW), lambda i, j: (i, j))
    return _pcall(body, grid=(t // tr, f // FF_CW), in_specs=[blk, blk], out_specs=blk, out_shape=_sds((t, f), BF16),
                  compiler_params=_params(dimension_semantics=("parallel", "parallel")), name=name)(g, u)


def _swiglu_bwd(g, u, dact, *, name):
    t, f = g.shape
    tr = _tile(t, 512)

    def body(g_ref, u_ref, d_ref, dg_ref, du_ref):
        gv = g_ref[...].astype(F32)
        uv = u_ref[...].astype(F32)
        dv = d_ref[...].astype(F32)
        s = _sig(gv)
        dg_ref[...] = (dv * uv * (s * (1.0 + gv * (1.0 - s)))).astype(dg_ref.dtype)
        du_ref[...] = (dv * gv * s).astype(du_ref.dtype)

    blk = pl.BlockSpec((tr, FF_CW), lambda i, j: (i, j))
    return _pcall(body, grid=(t // tr, f // FF_CW), in_specs=[blk, blk, blk], out_specs=[blk, blk],
                  out_shape=[_sds((t, f), BF16), _sds((t, f), BF16)],
                  compiler_params=_params(dimension_semantics=("parallel", "parallel")), name=name)(g, u, dact)


def _row_tile(n):
    for cand in (128, 64, 32, 16):
        if n % cand == 0:
            return cand
    raise ValueError(n)


def _add_halves(g4, rb, c_arr, *, name):
    _, r, cdim = g4.shape
    hr = r // 2
    rt = _row_tile(hr)
    nb = hr // rt

    def body(c_ref, g_ref, rb_ref, o_ref):
        del c_ref
        o_ref[...] = (g_ref[...].astype(F32) + rb_ref[...].astype(F32)).astype(o_ref.dtype)

    gs = pltpu.PrefetchScalarGridSpec(
        num_scalar_prefetch=1, grid=(4, nb),
        in_specs=[pl.BlockSpec((None, rt, cdim), lambda q, i, c: (q, c[0] * nb + i, 0)),
                  pl.BlockSpec((None, rt, cdim), lambda q, i, c: (q, i, 0))],
        out_specs=pl.BlockSpec((None, rt, cdim), lambda q, i, c: (q, i, 0)))
    return _pcall(body, grid_spec=gs, out_shape=_sds((4, hr, cdim), BF16),
                  compiler_params=_params(dimension_semantics=("parallel", "parallel")), name=name)(c_arr, g4, rb)


def _sum_chips(p4, rc, q_arr, c_arr, *, name):
    _, hr, cdim = p4.shape
    rt = _row_tile(hr)
    nb = hr // rt

    def body(q_ref, c_ref, p_ref, rc_ref, o_ref):
        del q_ref, c_ref
        o_ref[...] = ((p_ref[...].astype(F32) + rc_ref[0].astype(F32)) + rc_ref[1].astype(F32)) + rc_ref[2].astype(F32)

    gs = pltpu.PrefetchScalarGridSpec(
        num_scalar_prefetch=2, grid=(nb,),
        in_specs=[pl.BlockSpec((None, rt, cdim), lambda i, q, c: (q[0], i, 0)),
                  pl.BlockSpec((3, rt, cdim), lambda i, q, c: (0, i, 0))],
        out_specs=pl.BlockSpec((rt, cdim), lambda i, q, c: (c[0] * nb + i, 0)))
    return _pcall(body, grid_spec=gs, out_shape=_sds((2 * hr, cdim), F32),
                  compiler_params=_params(dimension_semantics=("parallel",)), name=name)(q_arr, c_arr, p4, rc)


def _adam_math(w, g, m, v):
    m = ADAM_B1 * m + (1.0 - ADAM_B1) * g
    v = ADAM_B2 * v + (1.0 - ADAM_B2) * (g * g)
    m_hat = m / (1.0 - ADAM_B1 ** ADAM_STEP)
    v_hat = v / (1.0 - ADAM_B2 ** ADAM_STEP)
    delta = -ADAM_LR * (m_hat / (jnp.sqrt(v_hat) + ADAM_EPS) + ADAM_WD * w)
    return delta, m, v


def _adamw_layer(layer, w, m, v, g, prev, *, name):
    depth, r, cdim = w.shape
    rt = _row_tile(r)
    n_prev = 0 if prev is None else 4

    def body(*refs):
        w_ref, m_ref, v_ref, g_ref = refs[:4]
        go_ref, d_ref, mo_ref, vo_ref = refs[4 + n_prev:]
        gv = g_ref[...]
        delta, mn, vn = _adam_math(w_ref[...], gv, m_ref[...], v_ref[...])
        go_ref[...] = gv
        d_ref[...] = delta
        mo_ref[...] = mn
        vo_ref[...] = vn

    st = pl.BlockSpec((None, rt, cdim), lambda i: (layer, i, 0))
    in_specs = [st, st, st, pl.BlockSpec((rt, cdim), lambda i: (i, 0))] + [pl.BlockSpec(memory_space=pl.ANY)] * n_prev
    args = (w, m, v, g) + (tuple(prev) if prev is not None else ())
    return _pcall(body, grid=(r // rt,), in_specs=in_specs, out_specs=[st] * 4,
                  out_shape=[_sds((depth, r, cdim), F32)] * 4,
                  input_output_aliases={4 + i: i for i in range(n_prev)},
                  compiler_params=_params(dimension_semantics=("parallel",)), name=name)(*args)


def _adamw_flat(w, g, m, v, *, name):
    def body(w_ref, g_ref, m_ref, v_ref, d_ref, mo_ref, vo_ref):
        delta, mn, vn = _adam_math(w_ref[...], g_ref[...], m_ref[...], v_ref[...])
        d_ref[...] = delta
        mo_ref[...] = mn
        vo_ref[...] = vn

    return _pcall(body, out_shape=[_sds(w.shape, F32)] * 3, compiler_params=_params(), name=name)(w, g, m, v)


def _place():
    x, y, c = lax.axis_index("x"), lax.axis_index("y"), lax.axis_index("c")
    chips = [(1 - x, y), (x, 1 - y), (1 - x, 1 - y)]
    return x, y, c, chips


def _cast_place(w, q_arr, *, dep=None, name):
    r, cdim = w.shape
    rt = _row_tile(r)
    deps = () if dep is None else (dep,)

    def body(q_ref, w_ref, *rest):
        del q_ref
        o_ref = rest[-1]
        o_ref[...] = w_ref[...].astype(o_ref.dtype)

    gs = pltpu.PrefetchScalarGridSpec(
        num_scalar_prefetch=1, grid=(r // rt,),
        in_specs=[pl.BlockSpec((rt, cdim), lambda i, q: (i, 0))] + [pl.BlockSpec(memory_space=pl.ANY)] * len(deps),
        out_specs=pl.BlockSpec((None, rt, cdim), lambda i, q: (q[0], i, 0)))
    return _pcall(body, grid_spec=gs, out_shape=_sds((4, r, cdim), BF16),
                  compiler_params=_params(dimension_semantics=("parallel",)), name=name)(q_arr, w, *deps)


HBM_SPEC = pl.BlockSpec(memory_space=pltpu.HBM)
SEM_SPEC = pl.BlockSpec(memory_space=pltpu.SEMAPHORE)
ANY_SPEC = pl.BlockSpec(memory_space=pl.ANY)
SPLIT_PARAMS = pltpu.CompilerParams(has_side_effects=pltpu.SideEffectType.DATAFLOW_SIDE_EFFECTING)


def _gather_copies(bufs, sems):
    n = len(bufs)
    x, y, c, chips = _place()
    qme = 2 * x + y
    cps = []
    for t in range(n):
        hr = bufs[t].shape[1] // 2
        mine = bufs[t].at[qme, pl.ds(c * hr, hr)]
        for k, (px, py) in enumerate(chips):
            landed = bufs[t].at[2 * px + py, pl.ds(c * hr, hr)]
            peer = dict(device_id=(px, py, c), device_id_type=MESH)
            cps.append((pltpu.make_async_remote_copy(src_ref=mine, dst_ref=mine, send_sem=sems[3 * t + k],
                                                     recv_sem=sems[3 * n + 3 * t + k], **peer),
                        pltpu.make_async_remote_copy(src_ref=mine, dst_ref=landed, send_sem=sems[3 * t + k],
                                                     recv_sem=sems[3 * n + 3 * t + k], **peer)))
    return cps


def _forward_copies(bufs, sems):
    n = len(bufs)
    x, y, c, chips = _place()
    sib = dict(device_id=(x, y, 1 - c), device_id_type=MESH)
    cps = []
    for t in range(n):
        hr = bufs[t].shape[1] // 2
        for k, (px, py) in enumerate(chips):
            landed = bufs[t].at[2 * px + py, pl.ds(c * hr, hr)]
            other = bufs[t].at[2 * px + py, pl.ds((1 - c) * hr, hr)]
            cps.append((pltpu.make_async_remote_copy(src_ref=landed, dst_ref=landed, send_sem=sems[3 * t + k],
                                                     recv_sem=sems[3 * n + 3 * t + k], **sib),
                        pltpu.make_async_remote_copy(src_ref=landed, dst_ref=other, send_sem=sems[3 * t + k],
                                                     recv_sem=sems[3 * n + 3 * t + k], **sib)))
    return cps


def _gather_start(bufs, after, *, copies=_gather_copies, name):
    n = len(bufs)

    def body(*refs):
        ins = refs[:n]
        sems = refs[n + 1:7 * n + 1]
        token = refs[8 * n + 1]
        for send, _ in copies(ins, sems):
            send.start()
        token[...] = jnp.zeros_like(token)

    res = _pcall(body, in_specs=[HBM_SPEC] * n + [ANY_SPEC],
                 out_specs=(SEM_SPEC,) * (6 * n) + (HBM_SPEC,) * n + (pl.BlockSpec(memory_space=pltpu.VMEM),),
                 out_shape=(pltpu.SemaphoreType.DMA(()),) * (6 * n) + tuple(pltpu.HBM(b.shape, b.dtype) for b in bufs)
                 + (_sds((8, 128), F32),),
                 input_output_aliases={t: 6 * n + t for t in range(n)}, compiler_params=SPLIT_PARAMS, name=name)(*bufs, after)
    return list(res[:6 * n]), list(res[6 * n:7 * n]), res[7 * n]


def _gather_wait(bufs, sems, after, *, copies=_gather_copies, name):
    n = len(bufs)

    def body(*refs):
        ins = refs[:n]
        sem_refs = refs[n:7 * n]
        for send, arrive in copies(ins, sem_refs):
            send.wait_send()
            arrive.wait_recv()

    return list(_pcall(body, in_specs=[HBM_SPEC] * n + [SEM_SPEC] * (6 * n) + [ANY_SPEC], out_specs=(HBM_SPEC,) * n,
                       out_shape=tuple(pltpu.HBM(b.shape, b.dtype) for b in bufs),
                       input_output_aliases={t: t for t in range(n)}, compiler_params=SPLIT_PARAMS, name=name)(*bufs, *sems, after))


def _gather_forward(bufs, *, name):
    n = len(bufs)

    def body(*refs):
        outs = refs[n:2 * n]
        send, recv = refs[2 * n:]
        x, y, c, chips = _place()
        sib = dict(device_id=(x, y, 1 - c), device_id_type=MESH)
        cps = []
        for t in range(n):
            hr = outs[t].shape[1] // 2
            for k, (px, py) in enumerate(chips):
                landed = outs[t].at[2 * px + py, pl.ds(c * hr, hr)]
                cp = pltpu.make_async_remote_copy(src_ref=landed, dst_ref=landed, send_sem=send.at[t, k],
                                                  recv_sem=recv.at[t, k], **sib)
                cp.start()
                cps.append(cp)
        for t in range(n):
            hr = outs[t].shape[1] // 2
            for k, (px, py) in enumerate(chips):
                other = outs[t].at[2 * px + py, pl.ds((1 - c) * hr, hr)]
                pltpu.make_async_remote_copy(src_ref=other, dst_ref=other, send_sem=send.at[t, k], recv_sem=recv.at[t, k],
                                             **sib).wait_recv()
        for cp in cps:
            cp.wait_send()

    return _pcall(body, in_specs=[ANY_SPEC] * n, out_specs=[ANY_SPEC] * n,
                  out_shape=[_sds(b.shape, b.dtype) for b in bufs], input_output_aliases={t: t for t in range(n)},
                  scratch_shapes=[pltpu.SemaphoreType.DMA((n, 3))] * 2,
                  compiler_params=pltpu.CompilerParams(has_side_effects=True), name=name)(*bufs)


def _owner_copies(parts, lands, sems):
    n = len(parts)
    _, _, c, chips = _place()
    cps = []
    for t in range(n):
        for k, (px, py) in enumerate(chips):
            cps.append(pltpu.make_async_remote_copy(src_ref=parts[t].at[2 * px + py], dst_ref=lands[t].at[k],
                                                    send_sem=sems[3 * t + k], recv_sem=sems[3 * n + 3 * t + k],
                                                    device_id=(px, py, c), device_id_type=MESH))
    return cps


def _swap_copies(grads, lands, sems):
    n = len(grads)
    x, y, c, _ = _place()
    cps = []
    for t in range(n):
        hr = grads[t].shape[1] // 2
        cps.append(pltpu.make_async_remote_copy(src_ref=grads[t].at[:, pl.ds((1 - c) * hr, hr), :], dst_ref=lands[t],
                                                send_sem=sems[t], recv_sem=sems[n + t],
                                                device_id=(x, y, 1 - c), device_id_type=MESH))
    return cps


def _exchange_start(srcs, land_shapes, copies, n_copies, *, name):
    n = len(srcs)
    ns = 2 * n_copies
    lands = [pltpu.with_memory_space_constraint(lax.empty(shape, s.dtype), pltpu.HBM) for shape, s in zip(land_shapes, srcs)]

    def body(*refs):
        ins, lnd = refs[:n], refs[n:2 * n]
        sems = refs[2 * n:2 * n + ns]
        token = refs[4 * n + ns]
        for cp in copies(ins, lnd, sems):
            cp.start()
        token[...] = jnp.zeros_like(token)

    res = _pcall(body, in_specs=[HBM_SPEC] * (2 * n),
                 out_specs=(SEM_SPEC,) * ns + (HBM_SPEC,) * (2 * n) + (pl.BlockSpec(memory_space=pltpu.VMEM),),
                 out_shape=(pltpu.SemaphoreType.DMA(()),) * ns
                 + tuple(pltpu.HBM(b.shape, b.dtype) for b in list(srcs) + lands) + (_sds((8, 128), F32),),
                 input_output_aliases={t: ns + t for t in range(2 * n)}, compiler_params=SPLIT_PARAMS, name=name)(*srcs, *lands)
    return list(res[:ns]), list(res[ns:ns + n]), list(res[ns + n:ns + 2 * n]), res[ns + 2 * n]


def _exchange_wait(srcs, lands, sems, after, copies, *, name):
    n = len(srcs)
    ns = len(sems)

    def body(*refs):
        ins, lnd = refs[:n], refs[n:2 * n]
        for cp in copies(ins, lnd, refs[2 * n:2 * n + ns]):
            cp.wait_send()
            cp.wait_recv()

    res = _pcall(body, in_specs=[HBM_SPEC] * (2 * n) + [SEM_SPEC] * ns + [ANY_SPEC], out_specs=(HBM_SPEC,) * (2 * n),
                 out_shape=tuple(pltpu.HBM(b.shape, b.dtype) for b in list(srcs) + list(lands)),
                 input_output_aliases={t: t for t in range(2 * n)}, compiler_params=SPLIT_PARAMS, name=name)(
                     *srcs, *lands, *sems, after)
    return list(res[:n]), list(res[n:])


def _owners_start(parts, *, name):
    return _exchange_start(parts, [(3,) + p.shape[1:] for p in parts], _owner_copies, 3 * len(parts), name=name)


def _owners_wait(parts, lands, sems, after, *, name):
    return _exchange_wait(parts, lands, sems, after, _owner_copies, name=name)


def _swap_start(grads, *, name):
    return _exchange_start(grads, [(4, g.shape[1] // 2, g.shape[2]) for g in grads], _swap_copies, len(grads), name=name)


def _swap_wait(grads, lands, sems, after, *, name):
    return _exchange_wait(grads, lands, sems, after, _swap_copies, name=name)


def _swap_halves(grads, *, name):
    n = len(grads)

    def body(*refs):
        ins, outs = refs[:n], refs[n:2 * n]
        send, recv = refs[2 * n:]
        x, y, c, _ = _place()
        cps = []
        for t in range(n):
            hr = ins[t].shape[1] // 2
            cp = pltpu.make_async_remote_copy(src_ref=ins[t].at[:, pl.ds((1 - c) * hr, hr), :], dst_ref=outs[t],
                                              send_sem=send.at[t], recv_sem=recv.at[t],
                                              device_id=(x, y, 1 - c), device_id_type=MESH)
            cp.start()
            cps.append(cp)
        for cp in cps:
            cp.wait()

    anyspec = pl.BlockSpec(memory_space=pl.ANY)
    return _pcall(body, in_specs=[anyspec] * n, out_specs=[anyspec] * n,
                  out_shape=[_sds((4, g.shape[1] // 2, g.shape[2]), g.dtype) for g in grads],
                  scratch_shapes=[pltpu.SemaphoreType.DMA((n,))] * 2,
                  compiler_params=pltpu.CompilerParams(has_side_effects=True), name=name)(*grads)


def _join_halves(bufs, *, name):
    n = len(bufs)

    def body(*refs):
        outs = refs[n:2 * n]
        send, recv = refs[2 * n:]
        x, y, c, _ = _place()
        cps = []
        for t in range(n):
            hr = outs[t].shape[0] // 2
            mine = outs[t].at[pl.ds(c * hr, hr)]
            cp = pltpu.make_async_remote_copy(src_ref=mine, dst_ref=mine, send_sem=send.at[t], recv_sem=recv.at[t],
                                              device_id=(x, y, 1 - c), device_id_type=MESH)
            cp.start()
            cps.append(cp)
        for t in range(n):
            hr = outs[t].shape[0] // 2
            theirs = outs[t].at[pl.ds((1 - c) * hr, hr)]
            pltpu.make_async_remote_copy(src_ref=theirs, dst_ref=theirs, send_sem=send.at[t], recv_sem=recv.at[t],
                                         device_id=(x, y, 1 - c), device_id_type=MESH).wait_recv()
        for cp in cps:
            cp.wait_send()

    anyspec = pl.BlockSpec(memory_space=pl.ANY)
    return _pcall(body, in_specs=[anyspec] * n, out_specs=[anyspec] * n,
                  out_shape=[_sds(b.shape, b.dtype) for b in bufs], input_output_aliases={t: t for t in range(n)},
                  scratch_shapes=[pltpu.SemaphoreType.DMA((n,))] * 2,
                  compiler_params=pltpu.CompilerParams(has_side_effects=True), name=name)(*bufs)


def _allreduce_small(buf, *, name):
    rows = buf.shape[0]
    rels = [(dx, dy, dc) for dx in (0, 1) for dy in (0, 1) for dc in (0, 1)][1:]

    def body(in_ref, out_ref, gbuf, send, recv):
        x, y, c = lax.axis_index("x"), lax.axis_index("y"), lax.axis_index("c")
        me = 4 * x + 2 * y + c
        gbuf[me] = in_ref[...]
        cps = []
        for k, (dx, dy, dc) in enumerate(rels):
            peer = (x + dx - 2 * x * dx, y + dy - 2 * y * dy, c + dc - 2 * c * dc)
            cp = pltpu.make_async_remote_copy(src_ref=in_ref, dst_ref=gbuf.at[me], send_sem=send.at[k], recv_sem=recv.at[k],
                                              device_id=peer, device_id_type=MESH)
            cp.start()
            cps.append(cp)
        for k, (dx, dy, dc) in enumerate(rels):
            px, py, pc = x + dx - 2 * x * dx, y + dy - 2 * y * dy, c + dc - 2 * c * dc
            pltpu.make_async_remote_copy(src_ref=in_ref, dst_ref=gbuf.at[4 * px + 2 * py + pc], send_sem=send.at[k],
                                         recv_sem=recv.at[k], device_id=(px, py, pc), device_id_type=MESH).wait_recv()
        for cp in cps:
            cp.wait_send()
        acc = gbuf[0]
        for d in range(1, 8):
            acc = acc + gbuf[d]
        out_ref[...] = acc

    vm = pl.BlockSpec(memory_space=pltpu.VMEM)
    return _pcall(body, in_specs=[vm], out_specs=vm, out_shape=_sds(buf.shape, F32),
                  scratch_shapes=[pltpu.VMEM((8, rows, 128), F32), pltpu.SemaphoreType.DMA((7,)), pltpu.SemaphoreType.DMA((7,))],
                  compiler_params=_params(has_side_effects=True), name=name)(buf)


def _pack(arrs):
    flat = jnp.concatenate([a.reshape(-1).astype(F32) for a in arrs])
    n = flat.shape[0]
    rows = -(-n // 1024) * 8
    return jnp.pad(flat, (0, rows * 128 - n)).reshape(rows, 128)


def _unpack(buf, shapes):
    flat = buf.reshape(-1)
    out, o = [], 0
    for s in shapes:
        n = 1
        for d in s:
            n *= d
        out.append(flat[o:o + n].reshape(s))
        o += n
    return out


def _perm_xbc(a):
    parts = []
    for g in range(N_GROUPS):
        parts += [a[..., 512 * g:512 * (g + 1)], a[..., 2048 + 128 * g:2048 + 128 * (g + 1)],
                  a[..., 2560 + 128 * g:2560 + 128 * (g + 1)]]
    return jnp.concatenate(parts, axis=-1)


def _unperm_xbc(a):
    xs = [a[..., GROUP_W * g:GROUP_W * g + 512] for g in range(N_GROUPS)]
    bs = [a[..., GROUP_W * g + 512:GROUP_W * g + 640] for g in range(N_GROUPS)]
    cs = [a[..., GROUP_W * g + 640:GROUP_W * (g + 1)] for g in range(N_GROUPS)]
    return jnp.concatenate(xs + bs + cs, axis=-1)


def _split_w_in(w4):
    k = w4.shape[1]
    nat = jnp.transpose(w4, (1, 0, 2)).reshape(k, -1)
    w_z = nat[:, :2048]
    w_xbc = _perm_xbc(nat[:, 2048:5120])
    w_dt = jnp.pad(nat[:, 5120:5152], ((0, 0), (0, DT_PAD - HEADS)))
    w_s = nat[:, 5152:]
    return w_z, w_xbc, w_dt, w_s


def _join_dw_in(dw_z, dw_xbc, dw_dt, dw_s):
    k = dw_z.shape[0]
    nat = jnp.concatenate([dw_z, _unperm_xbc(dw_xbc), dw_dt[:, :HEADS], dw_s], axis=1)
    return jnp.transpose(nat.reshape(k, 4, -1), (1, 0, 2))


def kernel(x, mem, norm_mix, w_in, ssd_conv_w, ssd_conv_b, dt_bias, a_log, d_skip, ssd_norm, sc_conv_w, sc_norm, w_out, mem_norm, norm_xa, w_q, w_k, w_v, w_o, norm_ffn, w_gate, w_up, w_down, norm_final, loss_target, m_norm_mix, m_w_in, m_ssd_conv_w, m_ssd_conv_b, m_dt_bias, m_a_log, m_d_skip, m_ssd_norm, m_sc_conv_w, m_sc_norm, m_w_out, m_mem_norm, m_norm_xa, m_w_q, m_w_k, m_w_v, m_w_o, m_norm_ffn, m_w_gate, m_w_up, m_w_down, m_norm_final, v_norm_mix, v_w_in, v_ssd_conv_w, v_ssd_conv_b, v_dt_bias, v_a_log, v_d_skip, v_ssd_norm, v_sc_conv_w, v_sc_norm, v_w_out, v_mem_norm, v_norm_xa, v_w_q, v_w_k, v_w_v, v_w_o, v_norm_ffn, v_w_gate, v_w_up, v_w_down, v_norm_final):
    depth = w_in.shape[0]
    ix, iy, ic = lax.axis_index("x"), lax.axis_index("y"), lax.axis_index("c")
    qme = 2 * ix + iy
    c_arr = jnp.reshape(ic, (1,)).astype(jnp.int32)
    q_arr = jnp.reshape(qme, (1,)).astype(jnp.int32)
    h = x[0]
    tgt = loss_target[0]

    big = dict(w_in=w_in, w_out=w_out, w_q=w_q, w_k=w_k, w_v=w_v, w_o=w_o, w_gate=w_gate, w_up=w_up, w_down=w_down)
    big_m = dict(w_in=m_w_in, w_out=m_w_out, w_q=m_w_q, w_k=m_w_k, w_v=m_w_v, w_o=m_w_o, w_gate=m_w_gate, w_up=m_w_up, w_down=m_w_down)
    big_v = dict(w_in=v_w_in, w_out=v_w_out, w_q=v_w_q, w_k=v_w_k, w_v=v_w_v, w_o=v_w_o, w_gate=v_w_gate, w_up=v_w_up, w_down=v_w_down)
    names = list(big)

    conv_full = jnp.zeros((depth, 4, D_XBC), F32)
    conv_full = lax.dynamic_update_slice(conv_full, jnp.where(ic == 0, ssd_conv_w, 0.0), (0, 0, qme * (D_XBC // 4)))
    sc_full = jnp.zeros((depth, 3, D_SC), F32)
    sc_full = lax.dynamic_update_slice(sc_full, jnp.where(ic == 0, sc_conv_w, 0.0), (0, 0, qme * (D_SC // 4)))
    conv_buf = _allreduce_small(_pack([conv_full, sc_full]), name="gather_conv_w")
    conv_full, sc_full = _unpack(conv_buf, [conv_full.shape, sc_full.shape])
    conv_p = _perm_xbc(conv_full)
    convb_p = _perm_xbc(ssd_conv_b)

    pad_h = lambda a: jnp.pad(a, ((0, 0), (0, DT_PAD - HEADS)))
    dt_bias_p, a_log_p = pad_h(dt_bias), pad_h(a_log)
    dskip_ch = jnp.repeat(d_skip, D_SSD // HEADS, axis=1)

    def finish_gather(tag, group, sems, flying, after):
        landed = _gather_wait(flying, sems, after, name=f"gather_wait{tag}")
        return dict(zip(group, _gather_forward(landed, name=f"gather_fwd{tag}")))

    placed0 = [_cast_place(big[n][0], q_arr, name=f"cast_{n}0") for n in names]
    sems_a, fly_a, tok_a = _gather_start(placed0[:1], conv_buf, name="gather_start0a")
    sems_b, fly_b, tok_b = _gather_start(placed0[1:], tok_a, name="gather_start0b")
    placed, last = [placed0], tok_b
    for l in range(1, depth):
        row = []
        for n in names:
            last = _cast_place(big[n][l], q_arr, dep=last, name=f"cast_{n}{l}")
            row.append(last)
        placed.append(row)
    memn = _rms_fwd(mem[0], mem_norm.reshape(1, D) + tok_b[:1, :1], name="memn")
    gathered = [finish_gather("0a", names[:1], sems_a, fly_a, last)]

    saved = []
    for l in range(depth):
        gw = gathered[l]
        gain = norm_mix[l:l + 1]
        if l + 1 < depth:
            sems, flying, token = _gather_start(placed[l + 1], gw["w_in"], name=f"gather_start{l + 1}")
            gain = gain + token[:1, :1]
        w_z, w_xbc, w_dt, w_s = _split_w_in(gw["w_in"])
        hn1 = _rms_fwd(h, gain, name=f"rms_mix{l}")
        pz = _mm_nn(hn1, w_z, tn=1024, name=f"proj_z{l}")
        pxbc = _mm_nn(hn1, w_xbc, tn=1024, name=f"proj_xbc{l}")
        dtr = _mm_nn(hn1, w_dt, tn=DT_PAD, out_dtype=F32, name=f"proj_dt{l}")
        t = h.shape[0]
        tm = _tile(t, 1024)
        ps = _mm(hn1, w_s, mode="nn", grid=(t // tm, 6, 1),
                 a_spec=pl.BlockSpec((tm, D), lambda i, j, kk: (i, 0)),
                 b_spec=pl.BlockSpec((D, 1024), lambda i, j, kk: (0, j)),
                 o_spec=pl.BlockSpec((None, tm, 1024), lambda i, j, kk: (j // 2, i, j % 2)), o_tile=(tm, 1024),
                 out_sds=_sds((3, t, D_SC), BF16), name=f"proj_s{l}")
        xc = _conv_fwd(pxbc, conv_p[l], convb_p[l:l + 1], name=f"conv{l}")
        dt, dtg, acsg, acst = _dt_prep(dtr, dt_bias_p[l:l + 1], a_log_p[l:l + 1], name=f"dt_prep{l}")
        y, states, mix = _ssd_fwd(xc, dtg, acsg, acst, pz, dskip_ch[l:l + 1], ssd_norm[l:l + 1], name=f"ssd{l}")
        mix = _sc_fwd(ps, sc_full[l], sc_norm[l:l + 1], mix, name=f"sc{l}")
        if l == 0:
            gw.update(finish_gather("0b", names[1:], sems_b, fly_b, mix))
        dep = None
        if l + 1 < depth:
            landed = _gather_wait(flying, sems, mix, name=f"gather_wait{l + 1}")
            sems, flying, dep = _gather_start(landed, mix, copies=_forward_copies, name=f"gather_fstart{l + 1}")
        wo2 = gw["w_out"].reshape(-1, D)
        wq2, wk2, wv2 = (gw[n].reshape(D, D_XA) for n in ("w_q", "w_k", "w_v"))
        wd2 = gw["w_down"].reshape(D_FF, D)
        s = dict(h0=h, w_z=w_z, w_xbc=w_xbc, w_dt=w_dt, w_s=w_s, wo2=wo2, wq2=wq2, wk2=wk2, wv2=wv2, wd2=wd2)
        h1 = _mm_nn(mix, wo2, tn=1024, tm=512, out_dtype=F32, add=h, dep=dep, name=f"out_proj{l}")
        hn2 = _rms_fwd(h1, norm_xa[l:l + 1], name=f"rms_xa{l}")
        q = _mm_nn(hn2, wq2, tn=D_XA, name=f"q{l}")
        k = _mm_nn(memn, wk2, tn=D_XA, name=f"k{l}")
        v = _mm_nn(memn, wv2, tn=D_XA, name=f"v{l}")
        o = _xa_fwd(q, k, v, name=f"xa{l}")
        h2 = _mm_nn_sm(o, gw["w_o"], out_dtype=F32, add=h1, name=f"xa_out{l}")
        hn3 = _rms_fwd(h2, norm_ffn[l:l + 1], name=f"rms_ffn{l}")
        fg = _mm_nn_sm(hn3, gw["w_gate"], name=f"ff_gate{l}")
        fu = _mm_nn_sm(hn3, gw["w_up"], name=f"ff_up{l}")
        act = _swiglu_fwd(fg, fu, name=f"swiglu{l}")
        h3 = _mm_nn(act, wd2, tn=1024, tm=512, out_dtype=F32, add=h2, name=f"ff_down{l}")
        s.update(hn1=hn1, pz=pz, pxbc=pxbc, dtr=dtr, ps=ps, xc=xc, dt=dt, dtg=dtg, acsg=acsg, acst=acst, y=y,
                 states=states, mix=mix, h1=h1, hn2=hn2, q=q, k=k, v=v, o=o, h2=h2, hn3=hn3, fg=fg, fu=fu, act=act)
        saved.append(s)
        h = h3
        if l + 1 < depth:
            gathered.append(dict(zip(names, _gather_wait(flying, sems, h3, copies=_forward_copies,
                                                         name=f"gather_fwait{l + 1}"))))

    loss_vec, dh, dhb, d_norm_final = _final(h, norm_final.reshape(1, D), tgt, name="final")
    loss = lax.psum(loss_vec[0, 0], ("x", "y", "c"))

    small = dict(norm_mix=[], ssd_conv_w=[], ssd_conv_b=[], dt_bias=[], a_log=[], d_skip=[], ssd_norm=[], sc_conv_w=[],
                 sc_norm=[], norm_xa=[], norm_ffn=[])
    dmemn = None
    carried = {n: None for n in names}
    pending = None
    first_groups = [["w_gate", "w_up", "w_down"], ["w_out", "w_q", "w_k", "w_v", "w_o"], ["w_in"]]

    def start_reduce(lyr, tag, group, grads, behind=None):
        g_list = [grads[n] for n in group]
        if behind is None:
            recv_sib = _swap_halves(g_list, name=f"swap_halves{tag}")
        else:
            sems_s, g_fly, lands_s, tok_s = _swap_start(g_list, name=f"swap_start{tag}")
            finish_reduce(behind, tok_s)
            g_list, recv_sib = _swap_wait(g_fly, lands_s, sems_s, carried["w_in"][0], name=f"swap_wait{tag}")
        parts = [_add_halves(g, rb, c_arr, name=f"add_halves_{n}{lyr}") for n, g, rb in zip(group, g_list, recv_sib)]
        return (lyr, tag, group) + _owners_start(parts, name=f"owners_start{tag}")

    def finish_reduce(pend, after):
        lyr, tag, group, sems_r, parts_r, lands_r, _ = pend
        parts_r, lands_r = _owners_wait(parts_r, lands_r, sems_r, after, name=f"owners_wait{tag}")
        halves = [_sum_chips(p, rc, q_arr, c_arr, name=f"sum_chips_{n}{lyr}") for n, p, rc in zip(group, parts_r, lands_r)]
        full = _join_halves(halves, name=f"join_halves{tag}")
        for n, g in zip(group, full):
            carried[n] = _adamw_layer(lyr, big[n], big_m[n], big_v[n], g, carried[n], name=f"adamw_{n}{lyr}")

    for l in reversed(range(depth)):
        s, gw = saved[l], gathered[l]
        t = dh.shape[0]
        tm = _tile(t, 1024)
        early = []
        dact = _mm_nt(dhb, s["wd2"], tn=FF_CW, dep=None if pending is None else pending[-1], name=f"d_act{l}")
        dw_down = _mm_tn(s["act"], dhb, tm=FF_CW, tn=1024, name=f"dw_down{l}")
        dg, du = _swiglu_bwd(s["fg"], s["fu"], dact, name=f"d_swiglu{l}")
        dw_gate = _mm_tn_sm(s["hn3"], dg, tm=1024, name=f"dw_gate{l}")
        dw_up = _mm_tn_sm(s["hn3"], du, tm=1024, name=f"dw_up{l}")
        dhn = _mm_nt_sm(dg, gw["w_gate"], tn=512, out_dtype=F32, name=f"d_hn3a{l}")
        dhn = _mm_nt_sm(du, gw["w_up"], tn=512, out_dtype=F32, add=dhn, name=f"d_hn3b{l}")
        dh, dhb, dn = _rms_bwd(s["h2"], norm_ffn[l:l + 1], dhn, dh, name=f"d_rms_ffn{l}")
        small["norm_ffn"].append(dn)
        dep = None
        if l == 0:
            early.append(start_reduce(0, "0a", first_groups[0],
                                      dict(w_gate=dw_gate, w_up=dw_up, w_down=dw_down.reshape(4, -1, D))))
            dep = early[-1][-1]
        do = _mm_nt_sm(dhb, gw["w_o"], tn=D_XA, dep=dep, name=f"d_o{l}")
        dw_o = _mm_tn_sm(s["o"], dhb, tm=D_XA, name=f"dw_o{l}")
        dq, dk, dv = _xa_bwd(s["q"], s["k"], s["v"], do, name=f"d_xa{l}")
        dw_q = _mm_tn(s["hn2"], dq, tm=1024, tn=D_XA, name=f"dw_q{l}")
        dw_k = _mm_tn(memn, dk, tm=1024, tn=D_XA, name=f"dw_k{l}")
        dw_v = _mm_tn(memn, dv, tm=1024, tn=D_XA, name=f"dw_v{l}")
        dhn = _mm_nt(dq, s["wq2"], tn=1024, out_dtype=F32, name=f"d_hn2{l}")
        dmemn = _mm_nt(dk, s["wk2"], tn=1024, out_dtype=F32, add=dmemn, name=f"d_memn_k{l}")
        dmemn = _mm_nt(dv, s["wv2"], tn=1024, out_dtype=F32, add=dmemn, name=f"d_memn_v{l}")
        dh, dhb, dn = _rms_bwd(s["h1"], norm_xa[l:l + 1], dhn, dh, name=f"d_rms_xa{l}")
        small["norm_xa"].append(dn)
        dw_out = _mm_tn(s["mix"], dhb, tm=1024, tn=1024, name=f"dw_out{l}")
        dep = None
        if l == 0:
            early.append(start_reduce(0, "0b", first_groups[1],
                                      dict(w_out=dw_out.reshape(4, -1, D), w_q=dw_q.reshape(4, -1, D_XA),
                                           w_k=dw_k.reshape(4, -1, D_XA), w_v=dw_v.reshape(4, -1, D_XA), w_o=dw_o)))
            dep = early[-1][-1]
        dmix = _mm_nt(dhb, s["wo2"], tn=1024, dep=dep, name=f"d_mix{l}")
        dps, d_scw, d_scn = _sc_bwd(s["ps"], sc_full[l], sc_norm[l:l + 1], dmix, name=f"d_sc{l}")
        dxc, dz, ddtg, dacg, dart, ddsk, d_ssdn = _ssd_bwd(s["xc"], s["dtg"], s["acsg"], s["acst"], s["pz"], s["y"], s["states"],
                                                           dmix, dskip_ch[l:l + 1], ssd_norm[l:l + 1], name=f"d_ssd{l}")
        dxbc, d_cw, d_cb = _conv_bwd(dxc, s["pxbc"], conv_p[l], convb_p[l:l + 1], name=f"d_conv{l}")
        ddtr, d_dtb, d_alog = _dt_bwd(ddtg, dacg, dart, s["dt"], s["dtr"], dt_bias_p[l:l + 1], a_log_p[l:l + 1], name=f"d_dt{l}")
        small["sc_conv_w"].append(d_scw)
        small["sc_norm"].append(d_scn)
        small["ssd_norm"].append(d_ssdn)
        small["d_skip"].append(jnp.sum(ddsk.reshape(HEADS, D_SSD // HEADS), axis=1).reshape(1, HEADS))
        small["ssd_conv_w"].append(_unperm_xbc(d_cw))
        small["ssd_conv_b"].append(_unperm_xbc(d_cb))
        small["dt_bias"].append(d_dtb[:, :HEADS])
        small["a_log"].append(d_alog[:, :HEADS])
        hn1 = s["hn1"]
        dw_z = _mm_tn(hn1, dz, tm=1024, tn=1024, name=f"dw_z{l}")
        dw_xbc = _mm_tn(hn1, dxbc, tm=1024, tn=1024, name=f"dw_xbc{l}")
        dw_dt = _mm_tn(hn1, ddtr, tm=1024, tn=DT_PAD, name=f"dw_dt{l}")
        tk = _tile(t, TN_TK)
        dw_s = _mm(hn1, dps, mode="tn", grid=(2, 6, t // tk),
                   a_spec=pl.BlockSpec((tk, 1024), lambda i, j, kk: (kk, i)),
                   b_spec=pl.BlockSpec((None, tk, 1024), lambda i, j, kk: (j // 2, kk, j % 2)),
                   o_spec=pl.BlockSpec((1024, 1024), lambda i, j, kk: (i, j)), o_tile=(1024, 1024),
                   out_sds=_sds((D, 3 * D_SC), BF16), name=f"dw_s{l}")
        dhn = _mm_nt(dz, s["w_z"], tn=1024, out_dtype=F32, name=f"d_hn1z{l}")
        dhn = _mm_nt(dxbc, s["w_xbc"], tn=1024, out_dtype=F32, add=dhn, name=f"d_hn1x{l}")
        dhn = _mm_nt(ddtr, s["w_dt"], tn=1024, out_dtype=F32, add=dhn, name=f"d_hn1d{l}")
        dhn = _mm(dps, s["w_s"], mode="nt", grid=(t // tm, 2, 3),
                  a_spec=pl.BlockSpec((None, tm, D_SC), lambda i, j, kk: (kk, i, 0)),
                  b_spec=pl.BlockSpec((1024, D_SC), lambda i, j, kk: (j, kk)),
                  o_spec=pl.BlockSpec((tm, 1024), lambda i, j, kk: (i, j)), o_tile=(tm, 1024),
                  out_sds=_sds((t, D), F32), add=dhn, name=f"d_hn1s{l}")
        dh, dhb, dn = _rms_bwd(s["h0"], norm_mix[l:l + 1], dhn, dh, name=f"d_rms_mix{l}")
        small["norm_mix"].append(dn)

        grads = dict(w_in=_join_dw_in(dw_z, dw_xbc, dw_dt, dw_s), w_out=dw_out.reshape(4, -1, D),
                     w_q=dw_q.reshape(4, -1, D_XA), w_k=dw_k.reshape(4, -1, D_XA), w_v=dw_v.reshape(4, -1, D_XA),
                     w_o=dw_o, w_gate=dw_gate, w_up=dw_up, w_down=dw_down.reshape(4, -1, D))
        if l == 0:
            if pending is not None:
                finish_reduce(pending, dh)
            pending = start_reduce(0, "0c", first_groups[2], grads)
            finish_reduce(early[0], pending[-1])
            finish_reduce(early[1], carried["w_down"][0])
        else:
            pending = start_reduce(l, str(l), names, grads, behind=pending)

    finish_reduce(pending, carried["w_out"][0])
    grad_x = dh[None]

    _, _, d_mem_norm = _rms_bwd(mem[0], mem_norm.reshape(1, D), dmemn, jnp.zeros_like(dmemn), name="d_mem_norm")
    stack = lambda n: jnp.concatenate(small[n][::-1], axis=0) if small[n][0].ndim == 2 and small[n][0].shape[0] == 1 \
        else jnp.stack(small[n][::-1], axis=0)
    small_names = ["norm_mix", "ssd_conv_w", "ssd_conv_b", "dt_bias", "a_log", "d_skip", "ssd_norm", "sc_conv_w", "sc_norm",
                   "mem_norm", "norm_xa", "norm_ffn", "norm_final"]
    local_g = dict(mem_norm=d_mem_norm.reshape(D), norm_final=d_norm_final.reshape(D))
    for n in small:
        local_g[n] = stack(n)
    shapes = [local_g[n].shape for n in small_names]
    red = dict(zip(small_names, _unpack(_allreduce_small(_pack([local_g[n] for n in small_names]), name="allreduce_small"), shapes)))
    red["ssd_conv_w"] = lax.dynamic_slice(red["ssd_conv_w"], (0, 0, qme * (D_XBC // 4)), ssd_conv_w.shape)
    red["sc_conv_w"] = lax.dynamic_slice(red["sc_conv_w"], (0, 0, qme * (D_SC // 4)), sc_conv_w.shape)
    sw = dict(norm_mix=norm_mix, ssd_conv_w=ssd_conv_w, ssd_conv_b=ssd_conv_b, dt_bias=dt_bias, a_log=a_log, d_skip=d_skip,
              ssd_norm=ssd_norm, sc_conv_w=sc_conv_w, sc_norm=sc_norm, mem_norm=mem_norm, norm_xa=norm_xa, norm_ffn=norm_ffn,
              norm_final=norm_final)
    sm = dict(norm_mix=m_norm_mix, ssd_conv_w=m_ssd_conv_w, ssd_conv_b=m_ssd_conv_b, dt_bias=m_dt_bias, a_log=m_a_log,
              d_skip=m_d_skip, ssd_norm=m_ssd_norm, sc_conv_w=m_sc_conv_w, sc_norm=m_sc_norm, mem_norm=m_mem_norm,
              norm_xa=m_norm_xa, norm_ffn=m_norm_ffn, norm_final=m_norm_final)
    sv = dict(norm_mix=v_norm_mix, ssd_conv_w=v_ssd_conv_w, ssd_conv_b=v_ssd_conv_b, dt_bias=v_dt_bias, a_log=v_a_log,
              d_skip=v_d_skip, ssd_norm=v_ssd_norm, sc_conv_w=v_sc_conv_w, sc_norm=v_sc_norm, mem_norm=v_mem_norm,
              norm_xa=v_norm_xa, norm_ffn=v_norm_ffn, norm_final=v_norm_final)
    shard_shapes = [sw[n].shape for n in small_names]
    pk = lambda d: _pack([d[n] for n in small_names])
    sd, snm, snv = _adamw_flat(pk(sw), pk(red), pk(sm), pk(sv), name="adamw_small")
    s_delta = dict(zip(small_names, _unpack(sd, shard_shapes)))
    s_newm = dict(zip(small_names, _unpack(snm, shard_shapes)))
    s_newv = dict(zip(small_names, _unpack(snv, shard_shapes)))

    order = ["norm_mix", "w_in", "ssd_conv_w", "ssd_conv_b", "dt_bias", "a_log", "d_skip", "ssd_norm", "sc_conv_w", "sc_norm",
             "w_out", "mem_norm", "norm_xa", "w_q", "w_k", "w_v", "w_o", "norm_ffn", "w_gate", "w_up", "w_down", "norm_final"]

    def pick(kind):
        out = []
        for n in order:
            if n in carried:
                out.append(carried[n][kind])
            else:
                out.append([red, s_delta, s_newm, s_newv][kind][n])
        return out

    return (loss, grad_x, *pick(0), *pick(1), *pick(2), *pick(3))
```

```python
import functools

import jax
import jax.numpy as jnp
from jax import lax
from jax.experimental import pallas as pl
from jax.experimental.pallas import tpu as pltpu

F32 = jnp.float32
BF16 = jnp.bfloat16
MESH = pl.DeviceIdType.MESH

D = 2048
D_SSD = 2048
N_GROUPS = 4
GROUP_W = 768
D_XBC = 3072
N_STATE = 128
HEADS = 32
PAIRS_PER_GROUP = 4
CHUNK = 256
DT_PAD = 128
D_SC = 2048
SC_GROUP = 128
XA_HEADS = 4
XA_HD = 128
D_XA = 512
D_FF = 5632
EPS = 1e-5
HALO = 16
TN_TK = 2048
VMEM_LIMIT = 56 * 1024 * 1024

ADAM_LR, ADAM_B1, ADAM_B2, ADAM_EPS, ADAM_WD, ADAM_STEP = 0.001, 0.9, 0.999, 1e-08, 0.01, 10

NT = (((1,), (1,)), ((), ()))
TN = (((0,), (0,)), ((), ()))
NN = (((1,), (0,)), ((), ()))


def _pcall(body, **kw):
    return pl.pallas_call(body, **kw)


def _params(**kw):
    return pltpu.CompilerParams(vmem_limit_bytes=VMEM_LIMIT, **kw)


def _sds(shape, dtype):
    return jax.ShapeDtypeStruct(shape, dtype)


def _sig(x):
    return 0.5 * jnp.tanh(0.5 * x) + 0.5


def _dot(a, b, dims=NN):
    return lax.dot_general(a, b, dims, preferred_element_type=F32)


def _mm(a, b, *, mode, grid, a_spec, b_spec, o_spec, o_tile, out_sds, add=None, dep=None, name):
    gk = grid[2]
    dims = {"nn": NN, "nt": NT, "tn": TN, "nt4": NT}[mode]
    has_add = add is not None
    n_dep = 0 if dep is None else 1

    def body(*refs):
        a_ref, b_ref = refs[0], refs[1]
        add_ref = refs[2] if has_add else None
        refs = refs[:2 + has_add] + refs[2 + has_add + n_dep:]
        o_ref = refs[2 + has_add]
        if mode == "nt4":
            bv = jnp.concatenate([b_ref[s] for s in range(4)], axis=1)
        else:
            bv = b_ref[...].astype(BF16)
        p = _dot(a_ref[...].astype(BF16), bv, dims)

        def finish(acc):
            if has_add:
                acc = acc + add_ref[...]
            o_ref[...] = acc.astype(o_ref.dtype)

        if gk == 1:
            finish(p)
        else:
            acc_ref = refs[3 + has_add]
            k = pl.program_id(2)

            @pl.when(k == 0)
            def _():
                acc_ref[...] = p

            @pl.when(k > 0)
            def _():
                acc_ref[...] += p

            @pl.when(k == gk - 1)
            def _():
                finish(acc_ref[...])

    in_specs = [a_spec, b_spec] + ([o_spec] if has_add else []) + [pl.BlockSpec(memory_space=pl.ANY)] * n_dep
    args = (a, b) + ((add,) if has_add else ()) + ((dep,) if n_dep else ())
    scratch = [pltpu.VMEM(o_tile, F32)] if gk > 1 else []
    return _pcall(body, grid=grid, in_specs=in_specs, out_specs=o_spec, out_shape=out_sds, scratch_shapes=scratch,
                  compiler_params=_params(dimension_semantics=("parallel", "parallel", "arbitrary")), name=name)(*args)


def _tile(n, pref):
    t = min(n, pref)
    assert n % t == 0, (n, pref)
    return t


def _mm_nn(a, w, *, tn, tk=None, tm=1024, out_dtype=BF16, add=None, dep=None, name):
    m, k = a.shape
    n = w.shape[1]
    tm = _tile(m, tm)
    tk = k if tk is None else tk
    grid = (m // tm, n // tn, k // tk)
    return _mm(a, w, mode="nn", grid=grid,
               a_spec=pl.BlockSpec((tm, tk), lambda i, j, kk: (i, kk)),
               b_spec=pl.BlockSpec((tk, tn), lambda i, j, kk: (kk, j)),
               o_spec=pl.BlockSpec((tm, tn), lambda i, j, kk: (i, j)), o_tile=(tm, tn),
               out_sds=_sds((m, n), out_dtype), add=add, dep=dep, name=name)


def _mm_nn_sm(a, w4, *, out_dtype=BF16, add=None, name):
    m, k = a.shape
    n = w4.shape[2]
    tm = _tile(m, 1024)
    return _mm(a, w4, mode="nn", grid=(m // tm, 4, 1),
               a_spec=pl.BlockSpec((tm, k), lambda i, j, kk: (i, 0)),
               b_spec=pl.BlockSpec((None, k, n), lambda i, j, kk: (j, 0, 0)),
               o_spec=pl.BlockSpec((tm, n), lambda i, j, kk: (i, j)), o_tile=(tm, n),
               out_sds=_sds((m, 4 * n), out_dtype), add=add, name=name)


def _mm_nt(a, w, *, tn, tk=None, out_dtype=BF16, add=None, dep=None, name):
    m, k = a.shape
    n = w.shape[0]
    tm = _tile(m, 1024)
    tk = k if tk is None else tk
    grid = (m // tm, n // tn, k // tk)
    return _mm(a, w, mode="nt", grid=grid,
               a_spec=pl.BlockSpec((tm, tk), lambda i, j, kk: (i, kk)),
               b_spec=pl.BlockSpec((tn, tk), lambda i, j, kk: (j, kk)),
               o_spec=pl.BlockSpec((tm, tn), lambda i, j, kk: (i, j)), o_tile=(tm, tn),
               out_sds=_sds((m, n), out_dtype), add=add, dep=dep, name=name)


def _mm_nt_sm(a, w4, *, tn, out_dtype=BF16, add=None, dep=None, name):
    m = a.shape[0]
    _, k, n = w4.shape
    tm = _tile(m, 512)
    tn = _tile(k, tn)
    return _mm(a, w4, mode="nt4", grid=(m // tm, k // tn, 1),
               a_spec=pl.BlockSpec((tm, 4 * n), lambda i, j, kk: (i, 0)),
               b_spec=pl.BlockSpec((4, tn, n), lambda i, j, kk: (0, j, 0)),
               o_spec=pl.BlockSpec((tm, tn), lambda i, j, kk: (i, j)), o_tile=(tm, tn),
               out_sds=_sds((m, k), out_dtype), add=add, dep=dep, name=name)


def _mm_tn(a, g, *, tm, tn, out_dtype=BF16, name):
    t, m = a.shape
    n = g.shape[1]
    tk = _tile(t, TN_TK)
    return _mm(a, g, mode="tn", grid=(m // tm, n // tn, t // tk),
               a_spec=pl.BlockSpec((tk, tm), lambda i, j, kk: (kk, i)),
               b_spec=pl.BlockSpec((tk, tn), lambda i, j, kk: (kk, j)),
               o_spec=pl.BlockSpec((tm, tn), lambda i, j, kk: (i, j)), o_tile=(tm, tn),
               out_sds=_sds((m, n), out_dtype), name=name)


def _mm_tn_sm(a, g, *, tm, out_dtype=BF16, name):
    t, m = a.shape
    n = g.shape[1] // 4
    tk = _tile(t, TN_TK)
    return _mm(a, g, mode="tn", grid=(m // tm, 4, t // tk),
               a_spec=pl.BlockSpec((tk, tm), lambda i, j, kk: (kk, i)),
               b_spec=pl.BlockSpec((tk, n), lambda i, j, kk: (kk, j)),
               o_spec=pl.BlockSpec((None, tm, n), lambda i, j, kk: (j, i, 0)), o_tile=(tm, n),
               out_sds=_sds((4, m, n), out_dtype), name=name)


def _rms_fwd(h, g, *, name):
    t, d = h.shape
    tr = _tile(t, 512)

    def body(h_ref, g_ref, o_ref):
        x = h_ref[...]
        r = lax.rsqrt(jnp.mean(x * x, axis=-1, keepdims=True) + EPS)
        o_ref[...] = (x * r * g_ref[...]).astype(o_ref.dtype)

    return _pcall(body, grid=(t // tr,),
                  in_specs=[pl.BlockSpec((tr, d), lambda i: (i, 0)), pl.BlockSpec((1, d), lambda i: (0, 0))],
                  out_specs=pl.BlockSpec((tr, d), lambda i: (i, 0)), out_shape=_sds((t, d), BF16),
                  compiler_params=_params(dimension_semantics=("parallel",)), name=name)(h, g)


def _rms_bwd(h, g, dy, dres, *, name):
    t, d = h.shape
    tr = _tile(t, 256)

    def body(h_ref, g_ref, dy_ref, dres_ref, dh_ref, dhb_ref, dg_ref):
        i = pl.program_id(0)
        x = h_ref[...]
        r = lax.rsqrt(jnp.mean(x * x, axis=-1, keepdims=True) + EPS)
        xh = x * r
        dyv = dy_ref[...].astype(F32)
        dxh = dyv * g_ref[...]
        dh = dres_ref[...] + r * (dxh - xh * jnp.mean(dxh * xh, axis=-1, keepdims=True))
        dh_ref[...] = dh
        dhb_ref[...] = dh.astype(BF16)
        part = jnp.sum(dyv * xh, axis=0, keepdims=True)

        @pl.when(i == 0)
        def _():
            dg_ref[...] = part

        @pl.when(i > 0)
        def _():
            dg_ref[...] += part

    row = pl.BlockSpec((tr, d), lambda i: (i, 0))
    vec = pl.BlockSpec((1, d), lambda i: (0, 0))
    return _pcall(body, grid=(t // tr,), in_specs=[row, vec, row, row], out_specs=[row, row, vec],
                  out_shape=[_sds((t, d), F32), _sds((t, d), BF16), _sds((1, d), F32)],
                  compiler_params=_params(dimension_semantics=("arbitrary",)), name=name)(h, g, dy, dres)


def _final(h, g, tgt, *, name):
    t, d = h.shape
    tr = _tile(t, 256)

    def body(h_ref, g_ref, t_ref, loss_ref, dh_ref, dhb_ref, dg_ref):
        i = pl.program_id(0)
        x = h_ref[...]
        gv = g_ref[...]
        r = lax.rsqrt(jnp.mean(x * x, axis=-1, keepdims=True) + EPS)
        xh = x * r
        e = xh * gv - t_ref[...]
        lpart = jnp.zeros((1, 128), F32) + 0.5 * jnp.sum(jnp.mean(e * e, axis=-1, keepdims=True))
        dyv = e * (1.0 / d)
        dxh = dyv * gv
        dh = r * (dxh - xh * jnp.mean(dxh * xh, axis=-1, keepdims=True))
        dh_ref[...] = dh
        dhb_ref[...] = dh.astype(BF16)
        part = jnp.sum(dyv * xh, axis=0, keepdims=True)

        @pl.when(i == 0)
        def _():
            dg_ref[...] = part
            loss_ref[...] = lpart

        @pl.when(i > 0)
        def _():
            dg_ref[...] += part
            loss_ref[...] += lpart

    row = pl.BlockSpec((tr, d), lambda i: (i, 0))
    vec = pl.BlockSpec((1, d), lambda i: (0, 0))
    return _pcall(body, grid=(t // tr,), in_specs=[row, vec, row],
                  out_specs=[pl.BlockSpec((1, 128), lambda i: (0, 0)), row, row, vec],
                  out_shape=[_sds((1, 128), F32), _sds((t, d), F32), _sds((t, d), BF16), _sds((1, d), F32)],
                  compiler_params=_params(dimension_semantics=("arbitrary",)), name=name)(h, g, tgt)


def _conv_taps(ext, w, ntap, rows):
    n = ext.shape[0]
    acc = w[ntap - 1:ntap, :] * ext[HALO:HALO + rows]
    for k in range(1, ntap):
        acc = acc + w[ntap - 1 - k:ntap - k, :] * pltpu.roll(ext, k, axis=0)[HALO:HALO + rows]
    del n
    return acc


def _conv_fwd(xbc, w, b, *, name):
    t, c = xbc.shape
    rows = CHUNK
    cw = GROUP_W
    hb = rows // HALO

    def body(cur_ref, prev_ref, w_ref, b_ref, o_ref):
        i = pl.program_id(1)
        cur = cur_ref[...].astype(F32)
        prev = jnp.where(i > 0, prev_ref[...].astype(F32), 0.0)
        ext = jnp.concatenate([prev, cur], axis=0)
        pre = _conv_taps(ext, w_ref[...], 4, rows) + b_ref[...]
        o_ref[...] = (pre * _sig(pre)).astype(o_ref.dtype)

    return _pcall(body, grid=(c // cw, t // rows),
                  in_specs=[pl.BlockSpec((rows, cw), lambda j, i: (i, j)),
                            pl.BlockSpec((HALO, cw), lambda j, i: (jnp.maximum(i * hb - 1, 0), j)),
                            pl.BlockSpec((4, cw), lambda j, i: (0, j)),
                            pl.BlockSpec((1, cw), lambda j, i: (0, j))],
                  out_specs=pl.BlockSpec((rows, cw), lambda j, i: (i, j)), out_shape=_sds((t, c), BF16),
                  compiler_params=_params(dimension_semantics=("parallel", "parallel")), name=name)(xbc, xbc, w, b)


def _conv_bwd(dxc, xbc, w, b, *, name):
    t, c = xbc.shape
    rows = CHUNK
    cw = GROUP_W
    hb = rows // HALO
    nblk = t // rows
    nhalo = t // HALO

    def body(d_ref, dn_ref, cur_ref, prev_ref, next_ref, w_ref, b_ref, dx_ref, dw_ref, db_ref):
        i = pl.program_id(1)
        last = i == nblk - 1
        wv = w_ref[...]
        xe = jnp.concatenate([jnp.where(i > 0, prev_ref[...].astype(F32), 0.0), cur_ref[...].astype(F32),
                              jnp.where(last, 0.0, next_ref[...].astype(F32))], axis=0)
        n = rows + 2 * HALO
        sh = [xe] + [pltpu.roll(xe, k, axis=0) for k in range(1, 4)]
        pre = wv[3:4, :] * sh[0] + wv[2:3, :] * sh[1] + wv[1:2, :] * sh[2] + wv[0:1, :] * sh[3] + b_ref[...]
        de = jnp.concatenate([jnp.zeros((HALO, cw), F32), d_ref[...].astype(F32),
                              jnp.where(last, 0.0, dn_ref[...].astype(F32))], axis=0)
        s = _sig(pre)
        dpre = de * (s * (1.0 + pre * (1.0 - s)))
        dx = wv[3:4, :] * dpre
        for m in range(1, 4):
            dx = dx + wv[3 - m:4 - m, :] * pltpu.roll(dpre, n - m, axis=0)
        dx_ref[...] = dx[HALO:HALO + rows].astype(dx_ref.dtype)
        dcur = dpre[HALO:HALO + rows]
        dwv = jnp.concatenate([jnp.sum(dcur * sh[3 - j][HALO:HALO + rows], axis=0, keepdims=True) for j in range(4)], axis=0)
        dbv = jnp.sum(dcur, axis=0, keepdims=True)

        @pl.when(i == 0)
        def _():
            dw_ref[...] = dwv
            db_ref[...] = dbv

        @pl.when(i > 0)
        def _():
            dw_ref[...] += dwv
            db_ref[...] += dbv

    cur = pl.BlockSpec((rows, cw), lambda j, i: (i, j))
    prev = pl.BlockSpec((HALO, cw), lambda j, i: (jnp.maximum(i * hb - 1, 0), j))
    nxt = pl.BlockSpec((HALO, cw), lambda j, i: (jnp.minimum((i + 1) * hb, nhalo - 1), j))
    return _pcall(body, grid=(c // cw, nblk),
                  in_specs=[cur, nxt, cur, prev, nxt, pl.BlockSpec((4, cw), lambda j, i: (0, j)),
                            pl.BlockSpec((1, cw), lambda j, i: (0, j))],
                  out_specs=[cur, pl.BlockSpec((4, cw), lambda j, i: (0, j)), pl.BlockSpec((1, cw), lambda j, i: (0, j))],
                  out_shape=[_sds((t, c), BF16), _sds((4, c), F32), _sds((1, c), F32)],
                  compiler_params=_params(dimension_semantics=("parallel", "arbitrary")), name=name)(dxc, dxc, xbc, xbc, xbc, w, b)


def _neg_exp_alog(alog):
    lane = lax.broadcasted_iota(jnp.int32, alog.shape, 1)
    return jnp.where(lane < HEADS, -jnp.exp(alog), 0.0)


def _dt_prep(dtr, bias, alog, *, name):
    t = dtr.shape[0]
    rows = CHUNK

    def body(r_ref, b_ref, a_ref, dt_ref, dtg_ref, acsg_ref, acst_ref):
        raw = r_ref[...] + b_ref[...]
        dt = jnp.maximum(raw, 0.0) + jnp.log(1.0 + jnp.exp(-jnp.abs(raw)))
        a = _neg_exp_alog(a_ref[...])
        adt = dt * a
        ri = lax.broadcasted_iota(jnp.int32, (rows, rows), 0)
        ci = lax.broadcasted_iota(jnp.int32, (rows, rows), 1)
        tri = (ri >= ci).astype(F32)
        acs = jnp.dot(tri, adt, precision=lax.Precision.HIGHEST, preferred_element_type=F32)
        dt_ref[...] = dt
        acst_ref[...] = acs.T
        for g in range(N_GROUPS):
            sh = (128 - 8 * g) % 128
            dtg_ref[g] = dt if sh == 0 else pltpu.roll(dt, sh, axis=1)
            acsg_ref[g] = acs if sh == 0 else pltpu.roll(acs, sh, axis=1)

    row = pl.BlockSpec((rows, DT_PAD), lambda i: (i, 0))
    vec = pl.BlockSpec((1, DT_PAD), lambda i: (0, 0))
    grp = pl.BlockSpec((N_GROUPS, rows, DT_PAD), lambda i: (0, i, 0))
    return _pcall(body, grid=(t // rows,), in_specs=[row, vec, vec],
                  out_specs=[row, grp, grp, pl.BlockSpec((DT_PAD, rows), lambda i: (0, i))],
                  out_shape=[_sds((t, DT_PAD), F32), _sds((N_GROUPS, t, DT_PAD), F32), _sds((N_GROUPS, t, DT_PAD), F32),
                             _sds((DT_PAD, t), F32)],
                  compiler_params=_params(dimension_semantics=("parallel",)), name=name)(dtr, bias, alog)


def _dt_bwd(ddtg, dacg, dt, dtr, bias, alog, *, name):
    t = dtr.shape[0]
    rows = CHUNK

    def body(ddtg_ref, dacg_ref, dt_ref, r_ref, b_ref, a_ref, dr_ref, db_ref, da_ref):
        i = pl.program_id(0)
        lane = lax.broadcasted_iota(jnp.int32, (rows, DT_PAD), 1)
        ddt = jnp.zeros((rows, DT_PAD), F32)
        dacs = jnp.zeros((rows, DT_PAD), F32)
        for g in range(N_GROUPS):
            sel = (lane >= 8 * g) & (lane < 8 * g + 8)
            dd = ddtg_ref[g]
            da = dacg_ref[g]
            if g:
                dd = pltpu.roll(dd, 8 * g, axis=1)
                da = pltpu.roll(da, 8 * g, axis=1)
            ddt = ddt + jnp.where(sel, dd, 0.0)
            dacs = dacs + jnp.where(sel, da, 0.0)
        ri = lax.broadcasted_iota(jnp.int32, (rows, rows), 0)
        ci = lax.broadcasted_iota(jnp.int32, (rows, rows), 1)
        triu = (ci >= ri).astype(F32)
        rev = jnp.dot(triu, dacs, precision=lax.Precision.HIGHEST, preferred_element_type=F32)
        a = _neg_exp_alog(a_ref[...])
        dtv = dt_ref[...]
        raw = r_ref[...] + b_ref[...]
        draw = (ddt + a * rev) * (1.0 / (1.0 + jnp.exp(-raw)))
        dr_ref[...] = draw
        dbv = jnp.sum(draw, axis=0, keepdims=True)
        dav = jnp.sum(dtv * rev, axis=0, keepdims=True) * a

        @pl.when(i == 0)
        def _():
            db_ref[...] = dbv
            da_ref[...] = dav

        @pl.when(i > 0)
        def _():
            db_ref[...] += dbv
            da_ref[...] += dav

    row = pl.BlockSpec((rows, DT_PAD), lambda i: (i, 0))
    vec = pl.BlockSpec((1, DT_PAD), lambda i: (0, 0))
    grp = pl.BlockSpec((N_GROUPS, rows, DT_PAD), lambda i: (0, i, 0))
    return _pcall(body, grid=(t // rows,),
                  in_specs=[grp, grp, row, row, vec, vec],
                  out_specs=[row, vec, vec], out_shape=[_sds((t, DT_PAD), F32), _sds((1, DT_PAD), F32), _sds((1, DT_PAD), F32)],
                  compiler_params=_params(dimension_semantics=("arbitrary",)), name=name)(ddtg, dacg, dt, dtr, bias, alog)


def _pair_cols(col_ref_val, p, lo):
    return jnp.where(lo, col_ref_val[:, 2 * p:2 * p + 1], col_ref_val[:, 2 * p + 1:2 * p + 2])


def _ssd_fwd(xc, dtg, acsg, acst, z, dskip, nw, *, name):
    t = xc.shape[0]
    L = CHUNK
    nc = t // L

    def body(xc_ref, dtg_ref, acsg_ref, acst_ref, z_ref, dsk_ref, nw_ref, y_ref, st_ref, mix_ref, s_ref):
        c = pl.program_id(1)

        @pl.when(c == 0)
        def _():
            s_ref[...] = jnp.zeros_like(s_ref)

        blk = xc_ref[...]
        bm = blk[:, 512:640]
        cm = blk[:, 640:768]
        cb = _dot(cm, bm, NT)
        dtv = dtg_ref[...]
        acs = acsg_ref[...]
        acst_v = acst_ref[...]
        ri = lax.broadcasted_iota(jnp.int32, (L, L), 0)
        ci = lax.broadcasted_iota(jnp.int32, (L, L), 1)
        causal = ri >= ci
        lo = lax.broadcasted_iota(jnp.int32, (1, 128), 1) < 64
        lo_rows = lax.broadcasted_iota(jnp.int32, (128, 1), 0) < 64
        dskv = dsk_ref[...]
        ys = []
        for p in range(PAIRS_PER_GROUP):
            xp = blk[:, 128 * p:128 * p + 128].astype(F32)
            dt_p = _pair_cols(dtv, p, lo)
            a_p = _pair_cols(acs, p, lo)
            alast = acs[L - 1:L, :]
            al_p = _pair_cols(alast, p, lo)
            xdt = xp * dt_p
            xdt_b = xdt.astype(BF16)
            yd = []
            for hh in range(2):
                j = 2 * p + hh
                seg = acs[:, j:j + 1] - acst_v[j:j + 1, :]
                lam = jnp.exp(jnp.where(causal, seg, -1e30))
                w = (cb * lam).astype(BF16)
                yd.append(_dot(w, xdt_b))
            y = jnp.where(lo, yd[0], yd[1])
            sp = s_ref[p]
            st_ref[p] = sp
            y = y + _dot(cm, sp.astype(BF16), NT) * jnp.exp(a_p)
            dsc = jnp.exp(al_p - a_p)
            snew = _dot((xdt * dsc).astype(BF16), bm, TN)
            al_rows = jnp.where(lo_rows, alast[:, 2 * p:2 * p + 1], alast[:, 2 * p + 1:2 * p + 2])
            s_ref[p] = sp * jnp.exp(al_rows) + snew
            ys.append(y + xp * dskv[:, 128 * p:128 * p + 128])
        yfull = jnp.concatenate(ys, axis=1)
        y_ref[...] = yfull.astype(y_ref.dtype)
        zz = z_ref[...].astype(F32)
        yg = yfull * (zz * _sig(zz))
        r = lax.rsqrt(jnp.mean(yg * yg, axis=-1, keepdims=True) + EPS)
        mix_ref[...] = (yg * r * nw_ref[...]).astype(mix_ref.dtype)

    grp = pl.BlockSpec((None, L, DT_PAD), lambda g, c: (g, c, 0))
    return _pcall(body, grid=(N_GROUPS, nc),
                  in_specs=[pl.BlockSpec((L, GROUP_W), lambda g, c: (c, g)), grp, grp,
                            pl.BlockSpec((8, L), lambda g, c: (g, c)),
                            pl.BlockSpec((L, 512), lambda g, c: (c, g)),
                            pl.BlockSpec((1, 512), lambda g, c: (0, g)), pl.BlockSpec((1, 512), lambda g, c: (0, g))],
                  out_specs=[pl.BlockSpec((L, 512), lambda g, c: (c, g)),
                             pl.BlockSpec((None, PAIRS_PER_GROUP, 128, N_STATE), lambda g, c: (c, g, 0, 0)),
                             pl.BlockSpec((L, 512), lambda g, c: (c, g))],
                  out_shape=[_sds((t, D_SSD), BF16), _sds((nc, N_GROUPS * PAIRS_PER_GROUP, 128, N_STATE), F32),
                             _sds((t, D_SSD + D_SC), BF16)],
                  scratch_shapes=[pltpu.VMEM((PAIRS_PER_GROUP, 128, N_STATE), F32)],
                  compiler_params=_params(dimension_semantics=("parallel", "arbitrary")), name=name)(
                      xc, dtg, acsg, acst, z, dskip, nw)


def _ssd_bwd(xc, dtg, acsg, acst, z, y, states, dmix, dskip, nw, *, name):
    t = xc.shape[0]
    L = CHUNK
    nc = t // L

    def body(xc_ref, dtg_ref, acsg_ref, acst_ref, z_ref, y_ref, st_ref, dm_ref, dsk_ref, nw_ref,
             dxc_ref, dz_ref, ddt_ref, dac_ref, ddsk_ref, dnw_ref, ds_ref):
        c = pl.program_id(1)

        @pl.when(c == 0)
        def _():
            ds_ref[...] = jnp.zeros_like(ds_ref)
            ddsk_ref[...] = jnp.zeros_like(ddsk_ref)
            dnw_ref[...] = jnp.zeros_like(dnw_ref)

        blk = xc_ref[...]
        xs = blk[:, :512].astype(F32)
        bm = blk[:, 512:640]
        cm = blk[:, 640:768]
        bmf = bm.astype(F32)
        yv = y_ref[...].astype(F32)
        zz = z_ref[...].astype(F32)
        nwv = nw_ref[...]
        dout = dm_ref[...].astype(F32)
        sz = _sig(zz)
        silu = zz * sz
        yg = yv * silu
        r = lax.rsqrt(jnp.mean(yg * yg, axis=-1, keepdims=True) + EPS)
        xh = yg * r
        dnw_ref[...] += jnp.sum(dout * xh, axis=0, keepdims=True)
        dyn = dout * nwv
        dyg = r * (dyn - xh * jnp.mean(dyn * xh, axis=-1, keepdims=True))
        dy = dyg * silu
        dz_ref[...] = (dyg * yv * (sz * (1.0 + zz * (1.0 - sz)))).astype(dz_ref.dtype)
        ddsk_ref[...] += jnp.sum(dy * xs, axis=0, keepdims=True)

        cb = _dot(cm, bm, NT)
        dtv = dtg_ref[...]
        acs = acsg_ref[...]
        acst_v = acst_ref[...]
        alast = acs[L - 1:L, :]
        ri = lax.broadcasted_iota(jnp.int32, (L, L), 0)
        ci = lax.broadcasted_iota(jnp.int32, (L, L), 1)
        causal = ri >= ci
        lane = lax.broadcasted_iota(jnp.int32, (1, 128), 1)
        lo = lane < 64
        lo_rows = lax.broadcasted_iota(jnp.int32, (128, 1), 0) < 64
        lane_l = lax.broadcasted_iota(jnp.int32, (L, DT_PAD), 1)
        row_l = lax.broadcasted_iota(jnp.int32, (L, 1), 0)
        dskv = dsk_ref[...]
        dm_acc = jnp.zeros((L, L), F32)
        db_acc = jnp.zeros((L, N_STATE), F32)
        dc_acc = jnp.zeros((L, N_STATE), F32)
        ddt_out = jnp.zeros((L, DT_PAD), F32)
        dac_out = jnp.zeros((L, DT_PAD), F32)
        dxs = []
        for p in range(PAIRS_PER_GROUP):
            xp = xs[:, 128 * p:128 * p + 128]
            dyp = dy[:, 128 * p:128 * p + 128]
            dyp_b = dyp.astype(BF16)
            dyp_r = dyp_b.astype(F32)
            dt_p = _pair_cols(dtv, p, lo)
            a_p = _pair_cols(acs, p, lo)
            al_p = _pair_cols(alast, p, lo)
            xdt = xp * dt_p
            xdt_b = xdt.astype(BF16)
            xdt_r = xdt_b.astype(F32)
            ea_p = jnp.exp(a_p)
            dsc_p = jnp.exp(al_p - a_p)
            sp = st_ref[p]
            sp_b = sp.astype(BF16)
            dsp = ds_ref[p]
            dsp_b = dsp.astype(BF16)
            cs = _dot(cm, sp_b, NT)
            dye_b = (dyp * ea_p).astype(BF16)
            dc_acc = dc_acc + _dot(dye_b, sp_b)
            ds_prev = _dot(dye_b, cm, TN)
            bds = _dot(bm, dsp_b, NT)
            al_rows = jnp.where(lo_rows, alast[:, 2 * p:2 * p + 1], alast[:, 2 * p + 1:2 * p + 2])
            ds_prev = ds_prev + jnp.exp(al_rows) * dsp
            prod_st = dsp * sp
            dxdt_h = []
            for hh in range(2):
                j = 2 * p + hh
                hm = lo if hh == 0 else jnp.logical_not(lo)
                hm_rows = lo_rows if hh == 0 else jnp.logical_not(lo_rows)
                a_col = acs[:, j:j + 1]
                seg = a_col - acst_v[j:j + 1, :]
                lam = jnp.exp(jnp.where(causal, seg, -1e30))
                w = (cb * lam).astype(BF16)
                dy_h = jnp.where(hm, dyp, 0.0).astype(BF16)
                dw = _dot(dy_h, xdt_b, NT)
                dxd = _dot(w, dyp_b, TN)
                dxdt_h.append(dxd)
                dm_acc = dm_acc + dw * lam
                diag = dyp_r * _dot(w, xdt_b) - xdt_r * dxd
                dac = jnp.sum(jnp.where(hm, diag + dyp * cs * jnp.exp(a_col), 0.0), axis=1, keepdims=True)
                al_h = alast[:, j:j + 1]
                dal = jnp.exp(al_h) * jnp.sum(jnp.sum(jnp.where(hm_rows, prod_st, 0.0), axis=1, keepdims=True), axis=0, keepdims=True)
                xds_h = _dot(jnp.where(hm, xdt, 0.0).astype(BF16), dsp_b)
                dsc_col = jnp.exp(al_h - a_col)
                db_acc = db_acc + dsc_col * xds_h
                tt = jnp.sum(xds_h * bmf, axis=1, keepdims=True) * dsc_col
                dal = dal + jnp.sum(tt, axis=0, keepdims=True)
                dac = dac - tt + jnp.where(row_l == L - 1, dal, 0.0)
                dac_out = jnp.where(lane_l == j, dac, dac_out)
            dxdt = jnp.where(lo, dxdt_h[0], dxdt_h[1]) + dsc_p * bds
            dxs.append(dxdt * dt_p + dyp * dskv[:, 128 * p:128 * p + 128])
            prod_dt = dxdt * xp
            for hh in range(2):
                j = 2 * p + hh
                hm = lo if hh == 0 else jnp.logical_not(lo)
                ddt_col = jnp.sum(jnp.where(hm, prod_dt, 0.0), axis=1, keepdims=True)
                ddt_out = jnp.where(lane_l == j, ddt_col, ddt_out)
            ds_ref[p] = ds_prev
        dm_b = dm_acc.astype(BF16)
        dc_acc = dc_acc + _dot(dm_b, bm)
        db_acc = db_acc + _dot(dm_b, cm, TN)
        dxc_ref[...] = jnp.concatenate(dxs + [db_acc, dc_acc], axis=1).astype(dxc_ref.dtype)
        ddt_ref[...] = ddt_out
        dac_ref[...] = dac_out

    rc = lambda g, c: (nc - 1 - c, g)
    grp = pl.BlockSpec((None, L, DT_PAD), lambda g, c: (g, nc - 1 - c, 0))
    vec = pl.BlockSpec((1, 512), lambda g, c: (0, g))
    return _pcall(body, grid=(N_GROUPS, nc),
                  in_specs=[pl.BlockSpec((L, GROUP_W), rc), grp, grp,
                            pl.BlockSpec((8, L), lambda g, c: (g, nc - 1 - c)),
                            pl.BlockSpec((L, 512), rc), pl.BlockSpec((L, 512), rc),
                            pl.BlockSpec((None, PAIRS_PER_GROUP, 128, N_STATE), lambda g, c: (nc - 1 - c, g, 0, 0)),
                            pl.BlockSpec((L, 512), rc), vec, vec],
                  out_specs=[pl.BlockSpec((L, GROUP_W), rc), pl.BlockSpec((L, 512), rc), grp, grp, vec, vec],
                  out_shape=[_sds((t, D_XBC), BF16), _sds((t, D_SSD), BF16), _sds((N_GROUPS, t, DT_PAD), F32),
                             _sds((N_GROUPS, t, DT_PAD), F32), _sds((1, D_SSD), F32), _sds((1, D_SSD), F32)],
                  scratch_shapes=[pltpu.VMEM((PAIRS_PER_GROUP, 128, N_STATE), F32)],
                  compiler_params=_params(dimension_semantics=("parallel", "arbitrary")), name=name)(
                      xc, dtg, acsg, acst, z, y, states, dmix, dskip, nw)


SC_CW = 1024


def _group_rstd(v):
    outs = []
    for q in range(v.shape[1] // SC_GROUP):
        vq = v[:, SC_GROUP * q:SC_GROUP * (q + 1)]
        outs.append(jnp.broadcast_to(lax.rsqrt(jnp.mean(vq * vq, axis=-1, keepdims=True) + EPS), vq.shape))
    return jnp.concatenate(outs, axis=1)


def _group_mean(v):
    outs = []
    for q in range(v.shape[1] // SC_GROUP):
        vq = v[:, SC_GROUP * q:SC_GROUP * (q + 1)]
        outs.append(jnp.broadcast_to(jnp.mean(vq, axis=-1, keepdims=True), vq.shape))
    return jnp.concatenate(outs, axis=1)


def _sc_fwd(ps, w, nw, mix, *, name):
    t = ps.shape[1]
    rows = CHUNK
    hb = rows // HALO
    cw = SC_CW
    off = D_SSD // cw

    def body(cur_ref, prev_ref, w_ref, nw_ref, mix_in_ref, o_ref):
        del mix_in_ref
        i = pl.program_id(1)
        u = cur_ref[0].astype(F32)
        gb = cur_ref[1].astype(F32)
        gc = cur_ref[2].astype(F32)
        cu_prev = jnp.where(i > 0, prev_ref[2].astype(F32) * prev_ref[0].astype(F32), 0.0)
        ext = jnp.concatenate([cu_prev, gc * u], axis=0)
        v = gb * _conv_taps(ext, w_ref[...], 3, rows)
        o_ref[...] = (v * _group_rstd(v) * nw_ref[...]).astype(o_ref.dtype)

    return _pcall(body, grid=(D_SC // cw, t // rows),
                  in_specs=[pl.BlockSpec((3, rows, cw), lambda j, i: (0, i, j)),
                            pl.BlockSpec((3, HALO, cw), lambda j, i: (0, jnp.maximum(i * hb - 1, 0), j)),
                            pl.BlockSpec((3, cw), lambda j, i: (0, j)), pl.BlockSpec((1, cw), lambda j, i: (0, j)),
                            pl.BlockSpec(memory_space=pl.ANY)],
                  out_specs=pl.BlockSpec((rows, cw), lambda j, i: (i, off + j)),
                  out_shape=_sds(mix.shape, mix.dtype), input_output_aliases={4: 0},
                  compiler_params=_params(dimension_semantics=("parallel", "parallel")), name=name)(ps, ps, w, nw, mix)


def _sc_bwd(ps, w, nw, dmix, *, name):
    t = ps.shape[1]
    rows = CHUNK
    hb = rows // HALO
    cw = SC_CW
    off = D_SSD // cw
    nblk = t // rows
    nhalo = t // HALO
    n = rows + 2 * HALO

    def body(cur_ref, prev_ref, next_ref, w_ref, nw_ref, d_ref, dn_ref, dps_ref, dw_ref, dnw_ref):
        i = pl.program_id(1)
        first = i == 0
        last = i == nblk - 1

        def ext(k):
            return jnp.concatenate([jnp.where(first, 0.0, prev_ref[k].astype(F32)), cur_ref[k].astype(F32),
                                    jnp.where(last, 0.0, next_ref[k].astype(F32))], axis=0)

        ue, gbe, gce = ext(0), ext(1), ext(2)
        wv = w_ref[...]
        nwv = nw_ref[...]
        cue = gce * ue
        cu1 = pltpu.roll(cue, 1, axis=0)
        cu2 = pltpu.roll(cue, 2, axis=0)
        conv = wv[2:3, :] * cue + wv[1:2, :] * cu1 + wv[0:1, :] * cu2
        ve = gbe * conv
        doe = jnp.concatenate([jnp.zeros((HALO, cw), F32), d_ref[...].astype(F32),
                               jnp.where(last, 0.0, dn_ref[...].astype(F32))], axis=0)
        r = _group_rstd(ve)
        xh = ve * r
        dvn = doe * nwv
        dv = r * (dvn - xh * _group_mean(dvn * xh))
        dconv = dv * gbe
        dcu = wv[2:3, :] * dconv + wv[1:2, :] * pltpu.roll(dconv, n - 1, axis=0) + wv[0:1, :] * pltpu.roll(dconv, n - 2, axis=0)
        sl = slice(HALO, HALO + rows)
        dps_ref[0] = (dcu * gce)[sl].astype(dps_ref.dtype)
        dps_ref[1] = (dv * conv)[sl].astype(dps_ref.dtype)
        dps_ref[2] = (dcu * ue)[sl].astype(dps_ref.dtype)
        dc = dconv[sl]
        dwv = jnp.concatenate([jnp.sum(dc * cu2[sl], axis=0, keepdims=True), jnp.sum(dc * cu1[sl], axis=0, keepdims=True),
                               jnp.sum(dc * cue[sl], axis=0, keepdims=True)], axis=0)
        dnv = jnp.sum((doe * xh)[sl], axis=0, keepdims=True)

        @pl.when(first)
        def _():
            dw_ref[...] = dwv
            dnw_ref[...] = dnv

        @pl.when(i > 0)
        def _():
            dw_ref[...] += dwv
            dnw_ref[...] += dnv

    cur = pl.BlockSpec((3, rows, cw), lambda j, i: (0, i, j))
    prev = pl.BlockSpec((3, HALO, cw), lambda j, i: (0, jnp.maximum(i * hb - 1, 0), j))
    nxt = pl.BlockSpec((3, HALO, cw), lambda j, i: (0, jnp.minimum((i + 1) * hb, nhalo - 1), j))
    return _pcall(body, grid=(D_SC // cw, nblk),
                  in_specs=[cur, prev, nxt, pl.BlockSpec((3, cw), lambda j, i: (0, j)), pl.BlockSpec((1, cw), lambda j, i: (0, j)),
                            pl.BlockSpec((rows, cw), lambda j, i: (i, off + j)),
                            pl.BlockSpec((HALO, cw), lambda j, i: (jnp.minimum((i + 1) * hb, nhalo - 1), off + j))],
                  out_specs=[cur, pl.BlockSpec((3, cw), lambda j, i: (0, j)), pl.BlockSpec((1, cw), lambda j, i: (0, j))],
                  out_shape=[_sds(ps.shape, BF16), _sds((3, D_SC), F32), _sds((1, D_SC), F32)],
                  compiler_params=_params(dimension_semantics=("parallel", "arbitrary")), name=name)(ps, ps, ps, w, nw, dmix, dmix)


XA_SCALE = XA_HD ** -0.5


def _softmax(s):
    m = jnp.max(s, axis=-1, keepdims=True)
    e = jnp.exp(s - m)
    return e * (1.0 / jnp.sum(e, axis=-1, keepdims=True))


def _xa_fwd(q, k, v, *, name):
    t = q.shape[0]
    nm = k.shape[0]
    tq = _tile(t, 512)

    def body(q_ref, k_ref, v_ref, o_ref):
        outs = []
        for h in range(XA_HEADS):
            sl = slice(XA_HD * h, XA_HD * (h + 1))
            s = _dot(q_ref[:, sl], k_ref[:, sl], NT) * XA_SCALE
            outs.append(_dot(_softmax(s).astype(BF16), v_ref[:, sl]))
        o_ref[...] = jnp.concatenate(outs, axis=1).astype(o_ref.dtype)

    row = pl.BlockSpec((tq, D_XA), lambda i: (i, 0))
    kv = pl.BlockSpec((nm, D_XA), lambda i: (0, 0))
    return _pcall(body, grid=(t // tq,), in_specs=[row, kv, kv], out_specs=row, out_shape=_sds((t, D_XA), BF16),
                  compiler_params=_params(dimension_semantics=("parallel",)), name=name)(q, k, v)


def _xa_bwd(q, k, v, do, *, name):
    t = q.shape[0]
    nm = k.shape[0]
    tq = _tile(t, 512)

    def body(q_ref, k_ref, v_ref, do_ref, dq_ref, dk_ref, dv_ref):
        i = pl.program_id(0)
        dqs, dks, dvs = [], [], []
        for h in range(XA_HEADS):
            sl = slice(XA_HD * h, XA_HD * (h + 1))
            qh, kh, vh, doh = q_ref[:, sl], k_ref[:, sl], v_ref[:, sl], do_ref[:, sl]
            p = _softmax(_dot(qh, kh, NT) * XA_SCALE)
            dvs.append(_dot(p.astype(BF16), doh, TN))
            dp = _dot(doh, vh, NT)
            ds = (p * (dp - jnp.sum(dp * p, axis=-1, keepdims=True)) * XA_SCALE).astype(BF16)
            dqs.append(_dot(ds, kh))
            dks.append(_dot(ds, qh, TN))
        dq_ref[...] = jnp.concatenate(dqs, axis=1).astype(dq_ref.dtype)
        dkv = jnp.concatenate(dks, axis=1)
        dvv = jnp.concatenate(dvs, axis=1)

        @pl.when(i == 0)
        def _():
            dk_ref[...] = dkv
            dv_ref[...] = dvv

        @pl.when(i > 0)
        def _():
            dk_ref[...] += dkv
            dv_ref[...] += dvv

    row = pl.BlockSpec((tq, D_XA), lambda i: (i, 0))
    kv = pl.BlockSpec((nm, D_XA), lambda i: (0, 0))
    return _pcall(body, grid=(t // tq,), in_specs=[row, kv, kv, row], out_specs=[row, kv, kv],
                  out_shape=[_sds((t, D_XA), BF16), _sds((nm, D_XA), F32), _sds((nm, D_XA), F32)],
                  compiler_params=_params(dimension_semantics=("arbitrary",)), name=name)(q, k, v, do)


FF_CW = 1408


def _swiglu_fwd(g, u, *, name):
    t, f = g.shape
    tr = _tile(t, 512)

    def body(g_ref, u_ref, o_ref):
        gv = g_ref[...].astype(F32)
        o_ref[...] = (gv * _sig(gv) * u_ref[...].astype(F32)).astype(o_ref.dtype)

    blk = pl.BlockSpec((tr, FF_CW), lambda i, j: (i, j))
    return _pcall(body, grid=(t // tr, f // FF_CW), in_specs=[blk, blk], out_specs=blk, out_shape=_sds((t, f), BF16),
                  compiler_params=_params(dimension_semantics=("parallel", "parallel")), name=name)(g, u)


def _swiglu_bwd(g, u, dact, *, name):
    t, f = g.shape
    tr = _tile(t, 512)

    def body(g_ref, u_ref, d_ref, dg_ref, du_ref):
        gv = g_ref[...].astype(F32)
        uv = u_ref[...].astype(F32)
        dv = d_ref[...].astype(F32)
        s = _sig(gv)
        dg_ref[...] = (dv * uv * (s * (1.0 + gv * (1.0 - s)))).astype(dg_ref.dtype)
        du_ref[...] = (dv * gv * s).astype(du_ref.dtype)

    blk = pl.BlockSpec((tr, FF_CW), lambda i, j: (i, j))
    return _pcall(body, grid=(t // tr, f // FF_CW), in_specs=[blk, blk, blk], out_specs=[blk, blk],
                  out_shape=[_sds((t, f), BF16), _sds((t, f), BF16)],
                  compiler_params=_params(dimension_semantics=("parallel", "parallel")), name=name)(g, u, dact)


def _row_tile(n):
    for cand in (128, 64, 32, 16):
        if n % cand == 0:
            return cand
    raise ValueError(n)


def _add_halves(g4, rb, c_arr, *, name):
    _, r, cdim = g4.shape
    hr = r // 2
    rt = _row_tile(hr)
    nb = hr // rt

    def body(c_ref, g_ref, rb_ref, o_ref):
        del c_ref
        o_ref[...] = (g_ref[...].astype(F32) + rb_ref[...].astype(F32)).astype(o_ref.dtype)

    gs = pltpu.PrefetchScalarGridSpec(
        num_scalar_prefetch=1, grid=(4, nb),
        in_specs=[pl.BlockSpec((None, rt, cdim), lambda q, i, c: (q, c[0] * nb + i, 0)),
                  pl.BlockSpec((None, rt, cdim), lambda q, i, c: (q, i, 0))],
        out_specs=pl.BlockSpec((None, rt, cdim), lambda q, i, c: (q, i, 0)))
    return _pcall(body, grid_spec=gs, out_shape=_sds((4, hr, cdim), BF16),
                  compiler_params=_params(dimension_semantics=("parallel", "parallel")), name=name)(c_arr, g4, rb)


def _sum_chips(p4, rc, q_arr, c_arr, *, name):
    _, hr, cdim = p4.shape
    rt = _row_tile(hr)
    nb = hr // rt

    def body(q_ref, c_ref, p_ref, rc_ref, o_ref):
        del q_ref, c_ref
        o_ref[...] = ((p_ref[...].astype(F32) + rc_ref[0].astype(F32)) + rc_ref[1].astype(F32)) + rc_ref[2].astype(F32)

    gs = pltpu.PrefetchScalarGridSpec(
        num_scalar_prefetch=2, grid=(nb,),
        in_specs=[pl.BlockSpec((None, rt, cdim), lambda i, q, c: (q[0], i, 0)),
                  pl.BlockSpec((3, rt, cdim), lambda i, q, c: (0, i, 0))],
        out_specs=pl.BlockSpec((rt, cdim), lambda i, q, c: (c[0] * nb + i, 0)))
    return _pcall(body, grid_spec=gs, out_shape=_sds((2 * hr, cdim), F32),
                  compiler_params=_params(dimension_semantics=("parallel",)), name=name)(q_arr, c_arr, p4, rc)


def _adam_math(w, g, m, v):
    m = ADAM_B1 * m + (1.0 - ADAM_B1) * g
    v = ADAM_B2 * v + (1.0 - ADAM_B2) * (g * g)
    m_hat = m / (1.0 - ADAM_B1 ** ADAM_STEP)
    v_hat = v / (1.0 - ADAM_B2 ** ADAM_STEP)
    delta = -ADAM_LR * (m_hat / (jnp.sqrt(v_hat) + ADAM_EPS) + ADAM_WD * w)
    return delta, m, v


def _adamw_layer(layer, w, m, v, g, prev, *, name):
    depth, r, cdim = w.shape
    rt = _row_tile(r)
    n_prev = 0 if prev is None else 4

    def body(*refs):
        w_ref, m_ref, v_ref, g_ref = refs[:4]
        go_ref, d_ref, mo_ref, vo_ref = refs[4 + n_prev:]
        gv = g_ref[...]
        delta, mn, vn = _adam_math(w_ref[...], gv, m_ref[...], v_ref[...])
        go_ref[...] = gv
        d_ref[...] = delta
        mo_ref[...] = mn
        vo_ref[...] = vn

    st = pl.BlockSpec((None, rt, cdim), lambda i: (layer, i, 0))
    in_specs = [st, st, st, pl.BlockSpec((rt, cdim), lambda i: (i, 0))] + [pl.BlockSpec(memory_space=pl.ANY)] * n_prev
    args = (w, m, v, g) + (tuple(prev) if prev is not None else ())
    return _pcall(body, grid=(r // rt,), in_specs=in_specs, out_specs=[st] * 4,
                  out_shape=[_sds((depth, r, cdim), F32)] * 4,
                  input_output_aliases={4 + i: i for i in range(n_prev)},
                  compiler_params=_params(dimension_semantics=("parallel",)), name=name)(*args)


def _adamw_flat(w, g, m, v, *, name):
    def body(w_ref, g_ref, m_ref, v_ref, d_ref, mo_ref, vo_ref):
        delta, mn, vn = _adam_math(w_ref[...], g_ref[...], m_ref[...], v_ref[...])
        d_ref[...] = delta
        mo_ref[...] = mn
        vo_ref[...] = vn

    return _pcall(body, out_shape=[_sds(w.shape, F32)] * 3, compiler_params=_params(), name=name)(w, g, m, v)


def _place():
    x, y, c = lax.axis_index("x"), lax.axis_index("y"), lax.axis_index("c")
    chips = [(1 - x, y), (x, 1 - y), (1 - x, 1 - y)]
    return x, y, c, chips


def _cast_place(w, q_arr, *, dep=None, name):
    r, cdim = w.shape
    rt = _row_tile(r)
    deps = () if dep is None else (dep,)

    def body(q_ref, w_ref, *rest):
        del q_ref
        o_ref = rest[-1]
        o_ref[...] = w_ref[...].astype(o_ref.dtype)

    gs = pltpu.PrefetchScalarGridSpec(
        num_scalar_prefetch=1, grid=(r // rt,),
        in_specs=[pl.BlockSpec((rt, cdim), lambda i, q: (i, 0))] + [pl.BlockSpec(memory_space=pl.ANY)] * len(deps),
        out_specs=pl.BlockSpec((None, rt, cdim), lambda i, q: (q[0], i, 0)))
    return _pcall(body, grid_spec=gs, out_shape=_sds((4, r, cdim), BF16),
                  compiler_params=_params(dimension_semantics=("parallel",)), name=name)(q_arr, w, *deps)


HBM_SPEC = pl.BlockSpec(memory_space=pltpu.HBM)
SEM_SPEC = pl.BlockSpec(memory_space=pltpu.SEMAPHORE)
ANY_SPEC = pl.BlockSpec(memory_space=pl.ANY)
SPLIT_PARAMS = pltpu.CompilerParams(has_side_effects=pltpu.SideEffectType.DATAFLOW_SIDE_EFFECTING)


def _gather_copies(bufs, sems):
    n = len(bufs)
    x, y, c, chips = _place()
    qme = 2 * x + y
    cps = []
    for t in range(n):
        hr = bufs[t].shape[1] // 2
        mine = bufs[t].at[qme, pl.ds(c * hr, hr)]
        for k, (px, py) in enumerate(chips):
            landed = bufs[t].at[2 * px + py, pl.ds(c * hr, hr)]
            peer = dict(device_id=(px, py, c), device_id_type=MESH)
            cps.append((pltpu.make_async_remote_copy(src_ref=mine, dst_ref=mine, send_sem=sems[3 * t + k],
                                                     recv_sem=sems[3 * n + 3 * t + k], **peer),
                        pltpu.make_async_remote_copy(src_ref=mine, dst_ref=landed, send_sem=sems[3 * t + k],
                                                     recv_sem=sems[3 * n + 3 * t + k], **peer)))
    return cps


def _forward_copies(bufs, sems):
    n = len(bufs)
    x, y, c, chips = _place()
    sib = dict(device_id=(x, y, 1 - c), device_id_type=MESH)
    cps = []
    for t in range(n):
        hr = bufs[t].shape[1] // 2
        for k, (px, py) in enumerate(chips):
            landed = bufs[t].at[2 * px + py, pl.ds(c * hr, hr)]
            other = bufs[t].at[2 * px + py, pl.ds((1 - c) * hr, hr)]
            cps.append((pltpu.make_async_remote_copy(src_ref=landed, dst_ref=landed, send_sem=sems[3 * t + k],
                                                     recv_sem=sems[3 * n + 3 * t + k], **sib),
                        pltpu.make_async_remote_copy(src_ref=landed, dst_ref=other, send_sem=sems[3 * t + k],
                                                     recv_sem=sems[3 * n + 3 * t + k], **sib)))
    return cps


def _gather_start(bufs, after, *, copies=_gather_copies, name):
    n = len(bufs)

    def body(*refs):
        ins = refs[:n]
        sems = refs[n + 1:7 * n + 1]
        token = refs[8 * n + 1]
        for send, _ in copies(ins, sems):
            send.start()
        token[...] = jnp.zeros_like(token)

    res = _pcall(body, in_specs=[HBM_SPEC] * n + [ANY_SPEC],
                 out_specs=(SEM_SPEC,) * (6 * n) + (HBM_SPEC,) * n + (pl.BlockSpec(memory_space=pltpu.VMEM),),
                 out_shape=(pltpu.SemaphoreType.DMA(()),) * (6 * n) + tuple(pltpu.HBM(b.shape, b.dtype) for b in bufs)
                 + (_sds((8, 128), F32),),
                 input_output_aliases={t: 6 * n + t for t in range(n)}, compiler_params=SPLIT_PARAMS, name=name)(*bufs, after)
    return list(res[:6 * n]), list(res[6 * n:7 * n]), res[7 * n]


def _gather_wait(bufs, sems, after, *, copies=_gather_copies, name):
    n = len(bufs)

    def body(*refs):
        ins = refs[:n]
        sem_refs = refs[n:7 * n]
        for send, arrive in copies(ins, sem_refs):
            send.wait_send()
            arrive.wait_recv()

    return list(_pcall(body, in_specs=[HBM_SPEC] * n + [SEM_SPEC] * (6 * n) + [ANY_SPEC], out_specs=(HBM_SPEC,) * n,
                       out_shape=tuple(pltpu.HBM(b.shape, b.dtype) for b in bufs),
                       input_output_aliases={t: t for t in range(n)}, compiler_params=SPLIT_PARAMS, name=name)(*bufs, *sems, after))


def _gather_forward(bufs, *, name):
    n = len(bufs)

    def body(*refs):
        outs = refs[n:2 * n]
        send, recv = refs[2 * n:]
        x, y, c, chips = _place()
        sib = dict(device_id=(x, y, 1 - c), device_id_type=MESH)
        cps = []
        for t in range(n):
            hr = outs[t].shape[1] // 2
            for k, (px, py) in enumerate(chips):
                landed = outs[t].at[2 * px + py, pl.ds(c * hr, hr)]
                cp = pltpu.make_async_remote_copy(src_ref=landed, dst_ref=landed, send_sem=send.at[t, k],
                                                  recv_sem=recv.at[t, k], **sib)
                cp.start()
                cps.append(cp)
        for t in range(n):
            hr = outs[t].shape[1] // 2
            for k, (px, py) in enumerate(chips):
                other = outs[t].at[2 * px + py, pl.ds((1 - c) * hr, hr)]
                pltpu.make_async_remote_copy(src_ref=other, dst_ref=other, send_sem=send.at[t, k], recv_sem=recv.at[t, k],
                                             **sib).wait_recv()
        for cp in cps:
            cp.wait_send()

    return _pcall(body, in_specs=[ANY_SPEC] * n, out_specs=[ANY_SPEC] * n,
                  out_shape=[_sds(b.shape, b.dtype) for b in bufs], input_output_aliases={t: t for t in range(n)},
                  scratch_shapes=[pltpu.SemaphoreType.DMA((n, 3))] * 2,
                  compiler_params=pltpu.CompilerParams(has_side_effects=True), name=name)(*bufs)


def _owner_copies(parts, lands, sems):
    n = len(parts)
    _, _, c, chips = _place()
    cps = []
    for t in range(n):
        for k, (px, py) in enumerate(chips):
            cps.append(pltpu.make_async_remote_copy(src_ref=parts[t].at[2 * px + py], dst_ref=lands[t].at[k],
                                                    send_sem=sems[3 * t + k], recv_sem=sems[3 * n + 3 * t + k],
                                                    device_id=(px, py, c), device_id_type=MESH))
    return cps


def _swap_copies(grads, lands, sems):
    n = len(grads)
    x, y, c, _ = _place()
    cps = []
    for t in range(n):
        hr = grads[t].shape[1] // 2
        cps.append(pltpu.make_async_remote_copy(src_ref=grads[t].at[:, pl.ds((1 - c) * hr, hr), :], dst_ref=lands[t],
                                                send_sem=sems[t], recv_sem=sems[n + t],
                                                device_id=(x, y, 1 - c), device_id_type=MESH))
    return cps


def _exchange_start(srcs, land_shapes, copies, n_copies, *, name):
    n = len(srcs)
    ns = 2 * n_copies
    lands = [pltpu.with_memory_space_constraint(lax.empty(shape, s.dtype), pltpu.HBM) for shape, s in zip(land_shapes, srcs)]

    def body(*refs):
        ins, lnd = refs[:n], refs[n:2 * n]
        sems = refs[2 * n:2 * n + ns]
        token = refs[4 * n + ns]
        for cp in copies(ins, lnd, sems):
            cp.start()
        token[...] = jnp.zeros_like(token)

    res = _pcall(body, in_specs=[HBM_SPEC] * (2 * n),
                 out_specs=(SEM_SPEC,) * ns + (HBM_SPEC,) * (2 * n) + (pl.BlockSpec(memory_space=pltpu.VMEM),),
                 out_shape=(pltpu.SemaphoreType.DMA(()),) * ns
                 + tuple(pltpu.HBM(b.shape, b.dtype) for b in list(srcs) + lands) + (_sds((8, 128), F32),),
                 input_output_aliases={t: ns + t for t in range(2 * n)}, compiler_params=SPLIT_PARAMS, name=name)(*srcs, *lands)
    return list(res[:ns]), list(res[ns:ns + n]), list(res[ns + n:ns + 2 * n]), res[ns + 2 * n]


def _exchange_wait(srcs, lands, sems, after, copies, *, name):
    n = len(srcs)
    ns = len(sems)

    def body(*refs):
        ins, lnd = refs[:n], refs[n:2 * n]
        for cp in copies(ins, lnd, refs[2 * n:2 * n + ns]):
            cp.wait_send()
            cp.wait_recv()

    res = _pcall(body, in_specs=[HBM_SPEC] * (2 * n) + [SEM_SPEC] * ns + [ANY_SPEC], out_specs=(HBM_SPEC,) * (2 * n),
                 out_shape=tuple(pltpu.HBM(b.shape, b.dtype) for b in list(srcs) + list(lands)),
                 input_output_aliases={t: t for t in range(2 * n)}, compiler_params=SPLIT_PARAMS, name=name)(
                     *srcs, *lands, *sems, after)
    return list(res[:n]), list(res[n:])


def _owners_start(parts, *, name):
    return _exchange_start(parts, [(3,) + p.shape[1:] for p in parts], _owner_copies, 3 * len(parts), name=name)


def _owners_wait(parts, lands, sems, after, *, name):
    return _exchange_wait(parts, lands, sems, after, _owner_copies, name=name)


def _swap_start(grads, *, name):
    return _exchange_start(grads, [(4, g.shape[1] // 2, g.shape[2]) for g in grads], _swap_copies, len(grads), name=name)


def _swap_wait(grads, lands, sems, after, *, name):
    return _exchange_wait(grads, lands, sems, after, _swap_copies, name=name)


def _swap_halves(grads, *, name):
    n = len(grads)

    def body(*refs):
        ins, outs = refs[:n], refs[n:2 * n]
        send, recv = refs[2 * n:]
        x, y, c, _ = _place()
        cps = []
        for t in range(n):
            hr = ins[t].shape[1] // 2
            cp = pltpu.make_async_remote_copy(src_ref=ins[t].at[:, pl.ds((1 - c) * hr, hr), :], dst_ref=outs[t],
                                              send_sem=send.at[t], recv_sem=recv.at[t],
                                              device_id=(x, y, 1 - c), device_id_type=MESH)
            cp.start()
            cps.append(cp)
        for cp in cps:
            cp.wait()

    anyspec = pl.BlockSpec(memory_space=pl.ANY)
    return _pcall(body, in_specs=[anyspec] * n, out_specs=[anyspec] * n,
                  out_shape=[_sds((4, g.shape[1] // 2, g.shape[2]), g.dtype) for g in grads],
                  scratch_shapes=[pltpu.SemaphoreType.DMA((n,))] * 2,
                  compiler_params=pltpu.CompilerParams(has_side_effects=True), name=name)(*grads)


def _join_halves(bufs, *, name):
    n = len(bufs)

    def body(*refs):
        outs = refs[n:2 * n]
        send, recv = refs[2 * n:]
        x, y, c, _ = _place()
        cps = []
        for t in range(n):
            hr = outs[t].shape[0] // 2
            mine = outs[t].at[pl.ds(c * hr, hr)]
            cp = pltpu.make_async_remote_copy(src_ref=mine, dst_ref=mine, send_sem=send.at[t], recv_sem=recv.at[t],
                                              device_id=(x, y, 1 - c), device_id_type=MESH)
            cp.start()
            cps.append(cp)
        for t in range(n):
            hr = outs[t].shape[0] // 2
            theirs = outs[t].at[pl.ds((1 - c) * hr, hr)]
            pltpu.make_async_remote_copy(src_ref=theirs, dst_ref=theirs, send_sem=send.at[t], recv_sem=recv.at[t],
                                         device_id=(x, y, 1 - c), device_id_type=MESH).wait_recv()
        for cp in cps:
            cp.wait_send()

    anyspec = pl.BlockSpec(memory_space=pl.ANY)
    return _pcall(body, in_specs=[anyspec] * n, out_specs=[anyspec] * n,
                  out_shape=[_sds(b.shape, b.dtype) for b in bufs], input_output_aliases={t: t for t in range(n)},
                  scratch_shapes=[pltpu.SemaphoreType.DMA((n,))] * 2,
                  compiler_params=pltpu.CompilerParams(has_side_effects=True), name=name)(*bufs)


def _allreduce_small(buf, *, name):
    rows = buf.shape[0]
    rels = [(dx, dy, dc) for dx in (0, 1) for dy in (0, 1) for dc in (0, 1)][1:]

    def body(in_ref, out_ref, gbuf, send, recv):
        x, y, c = lax.axis_index("x"), lax.axis_index("y"), lax.axis_index("c")
        me = 4 * x + 2 * y + c
        gbuf[me] = in_ref[...]
        cps = []
        for k, (dx, dy, dc) in enumerate(rels):
            peer = (x + dx - 2 * x * dx, y + dy - 2 * y * dy, c + dc - 2 * c * dc)
            cp = pltpu.make_async_remote_copy(src_ref=in_ref, dst_ref=gbuf.at[me], send_sem=send.at[k], recv_sem=recv.at[k],
                                              device_id=peer, device_id_type=MESH)
            cp.start()
            cps.append(cp)
        for k, (dx, dy, dc) in enumerate(rels):
            px, py, pc = x + dx - 2 * x * dx, y + dy - 2 * y * dy, c + dc - 2 * c * dc
            pltpu.make_async_remote_copy(src_ref=in_ref, dst_ref=gbuf.at[4 * px + 2 * py + pc], send_sem=send.at[k],
                                         recv_sem=recv.at[k], device_id=(px, py, pc), device_id_type=MESH).wait_recv()
        for cp in cps:
            cp.wait_send()
        acc = gbuf[0]
        for d in range(1, 8):
            acc = acc + gbuf[d]
        out_ref[...] = acc

    vm = pl.BlockSpec(memory_space=pltpu.VMEM)
    return _pcall(body, in_specs=[vm], out_specs=vm, out_shape=_sds(buf.shape, F32),
                  scratch_shapes=[pltpu.VMEM((8, rows, 128), F32), pltpu.SemaphoreType.DMA((7,)), pltpu.SemaphoreType.DMA((7,))],
                  compiler_params=_params(has_side_effects=True), name=name)(buf)


def _pack(arrs):
    flat = jnp.concatenate([a.reshape(-1).astype(F32) for a in arrs])
    n = flat.shape[0]
    rows = -(-n // 1024) * 8
    return jnp.pad(flat, (0, rows * 128 - n)).reshape(rows, 128)


def _unpack(buf, shapes):
    flat = buf.reshape(-1)
    out, o = [], 0
    for s in shapes:
        n = 1
        for d in s:
            n *= d
        out.append(flat[o:o + n].reshape(s))
        o += n
    return out


def _perm_xbc(a):
    parts = []
    for g in range(N_GROUPS):
        parts += [a[..., 512 * g:512 * (g + 1)], a[..., 2048 + 128 * g:2048 + 128 * (g + 1)],
                  a[..., 2560 + 128 * g:2560 + 128 * (g + 1)]]
    return jnp.concatenate(parts, axis=-1)


def _unperm_xbc(a):
    xs = [a[..., GROUP_W * g:GROUP_W * g + 512] for g in range(N_GROUPS)]
    bs = [a[..., GROUP_W * g + 512:GROUP_W * g + 640] for g in range(N_GROUPS)]
    cs = [a[..., GROUP_W * g + 640:GROUP_W * (g + 1)] for g in range(N_GROUPS)]
    return jnp.concatenate(xs + bs + cs, axis=-1)


def _split_w_in(w4):
    k = w4.shape[1]
    nat = jnp.transpose(w4, (1, 0, 2)).reshape(k, -1)
    w_z = nat[:, :2048]
    w_xbc = _perm_xbc(nat[:, 2048:5120])
    w_dt = jnp.pad(nat[:, 5120:5152], ((0, 0), (0, DT_PAD - HEADS)))
    w_s = nat[:, 5152:]
    return w_z, w_xbc, w_dt, w_s


def _join_dw_in(dw_z, dw_xbc, dw_dt, dw_s):
    k = dw_z.shape[0]
    nat = jnp.concatenate([dw_z, _unperm_xbc(dw_xbc), dw_dt[:, :HEADS], dw_s], axis=1)
    return jnp.transpose(nat.reshape(k, 4, -1), (1, 0, 2))


def kernel(x, mem, norm_mix, w_in, ssd_conv_w, ssd_conv_b, dt_bias, a_log, d_skip, ssd_norm, sc_conv_w, sc_norm, w_out, mem_norm, norm_xa, w_q, w_k, w_v, w_o, norm_ffn, w_gate, w_up, w_down, norm_final, loss_target, m_norm_mix, m_w_in, m_ssd_conv_w, m_ssd_conv_b, m_dt_bias, m_a_log, m_d_skip, m_ssd_norm, m_sc_conv_w, m_sc_norm, m_w_out, m_mem_norm, m_norm_xa, m_w_q, m_w_k, m_w_v, m_w_o, m_norm_ffn, m_w_gate, m_w_up, m_w_down, m_norm_final, v_norm_mix, v_w_in, v_ssd_conv_w, v_ssd_conv_b, v_dt_bias, v_a_log, v_d_skip, v_ssd_norm, v_sc_conv_w, v_sc_norm, v_w_out, v_mem_norm, v_norm_xa, v_w_q, v_w_k, v_w_v, v_w_o, v_norm_ffn, v_w_gate, v_w_up, v_w_down, v_norm_final):
    depth = w_in.shape[0]
    ix, iy, ic = lax.axis_index("x"), lax.axis_index("y"), lax.axis_index("c")
    qme = 2 * ix + iy
    c_arr = jnp.reshape(ic, (1,)).astype(jnp.int32)
    q_arr = jnp.reshape(qme, (1,)).astype(jnp.int32)
    h = x[0]
    tgt = loss_target[0]

    big = dict(w_in=w_in, w_out=w_out, w_q=w_q, w_k=w_k, w_v=w_v, w_o=w_o, w_gate=w_gate, w_up=w_up, w_down=w_down)
    big_m = dict(w_in=m_w_in, w_out=m_w_out, w_q=m_w_q, w_k=m_w_k, w_v=m_w_v, w_o=m_w_o, w_gate=m_w_gate, w_up=m_w_up, w_down=m_w_down)
    big_v = dict(w_in=v_w_in, w_out=v_w_out, w_q=v_w_q, w_k=v_w_k, w_v=v_w_v, w_o=v_w_o, w_gate=v_w_gate, w_up=v_w_up, w_down=v_w_down)
    names = list(big)

    conv_full = jnp.zeros((depth, 4, D_XBC), F32)
    conv_full = lax.dynamic_update_slice(conv_full, jnp.where(ic == 0, ssd_conv_w, 0.0), (0, 0, qme * (D_XBC // 4)))
    sc_full = jnp.zeros((depth, 3, D_SC), F32)
    sc_full = lax.dynamic_update_slice(sc_full, jnp.where(ic == 0, sc_conv_w, 0.0), (0, 0, qme * (D_SC // 4)))
    conv_buf = _allreduce_small(_pack([conv_full, sc_full]), name="gather_conv_w")
    conv_full, sc_full = _unpack(conv_buf, [conv_full.shape, sc_full.shape])
    conv_p = _perm_xbc(conv_full)
    convb_p = _perm_xbc(ssd_conv_b)

    pad_h = lambda a: jnp.pad(a, ((0, 0), (0, DT_PAD - HEADS)))
    dt_bias_p, a_log_p = pad_h(dt_bias), pad_h(a_log)
    dskip_ch = jnp.repeat(d_skip, D_SSD // HEADS, axis=1)

    def finish_gather(tag, group, sems, flying, after):
        landed = _gather_wait(flying, sems, after, name=f"gather_wait{tag}")
        return dict(zip(group, _gather_forward(landed, name=f"gather_fwd{tag}")))

    placed0 = [_cast_place(big[n][0], q_arr, name=f"cast_{n}0") for n in names]
    sems_a, fly_a, tok_a = _gather_start(placed0[:1], conv_buf, name="gather_start0a")
    sems_b, fly_b, tok_b = _gather_start(placed0[1:], tok_a, name="gather_start0b")
    placed, last = [placed0], tok_b
    for l in range(1, depth):
        row = []
        for n in names:
            last = _cast_place(big[n][l], q_arr, dep=last, name=f"cast_{n}{l}")
            row.append(last)
        placed.append(row)
    memn = _rms_fwd(mem[0], mem_norm.reshape(1, D) + tok_b[:1, :1], name="memn")
    gathered = [finish_gather("0a", names[:1], sems_a, fly_a, last)]

    saved = []
    for l in range(depth):
        gw = gathered[l]
        gain = norm_mix[l:l + 1]
        if l + 1 < depth:
            sems, flying, token = _gather_start(placed[l + 1], gw["w_in"], name=f"gather_start{l + 1}")
            gain = gain + token[:1, :1]
        w_z, w_xbc, w_dt, w_s = _split_w_in(gw["w_in"])
        hn1 = _rms_fwd(h, gain, name=f"rms_mix{l}")
        pz = _mm_nn(hn1, w_z, tn=1024, name=f"proj_z{l}")
        pxbc = _mm_nn(hn1, w_xbc, tn=1024, name=f"proj_xbc{l}")
        dtr = _mm_nn(hn1, w_dt, tn=DT_PAD, out_dtype=F32, name=f"proj_dt{l}")
        t = h.shape[0]
        tm = _tile(t, 1024)
        ps = _mm(hn1, w_s, mode="nn", grid=(t // tm, 6, 1),
                 a_spec=pl.BlockSpec((tm, D), lambda i, j, kk: (i, 0)),
                 b_spec=pl.BlockSpec((D, 1024), lambda i, j, kk: (0, j)),
                 o_spec=pl.BlockSpec((None, tm, 1024), lambda i, j, kk: (j // 2, i, j % 2)), o_tile=(tm, 1024),
                 out_sds=_sds((3, t, D_SC), BF16), name=f"proj_s{l}")
        xc = _conv_fwd(pxbc, conv_p[l], convb_p[l:l + 1], name=f"conv{l}")
        dt, dtg, acsg, acst = _dt_prep(dtr, dt_bias_p[l:l + 1], a_log_p[l:l + 1], name=f"dt_prep{l}")
        y, states, mix = _ssd_fwd(xc, dtg, acsg, acst, pz, dskip_ch[l:l + 1], ssd_norm[l:l + 1], name=f"ssd{l}")
        mix = _sc_fwd(ps, sc_full[l], sc_norm[l:l + 1], mix, name=f"sc{l}")
        if l == 0:
            gw.update(finish_gather("0b", names[1:], sems_b, fly_b, mix))
        dep = None
        if l + 1 < depth:
            landed = _gather_wait(flying, sems, mix, name=f"gather_wait{l + 1}")
            sems, flying, dep = _gather_start(landed, mix, copies=_forward_copies, name=f"gather_fstart{l + 1}")
        wo2 = gw["w_out"].reshape(-1, D)
        wq2, wk2, wv2 = (gw[n].reshape(D, D_XA) for n in ("w_q", "w_k", "w_v"))
        wd2 = gw["w_down"].reshape(D_FF, D)
        s = dict(h0=h, w_z=w_z, w_xbc=w_xbc, w_dt=w_dt, w_s=w_s, wo2=wo2, wq2=wq2, wk2=wk2, wv2=wv2, wd2=wd2)
        h1 = _mm_nn(mix, wo2, tn=1024, tm=512, out_dtype=F32, add=h, dep=dep, name=f"out_proj{l}")
        hn2 = _rms_fwd(h1, norm_xa[l:l + 1], name=f"rms_xa{l}")
        q = _mm_nn(hn2, wq2, tn=D_XA, name=f"q{l}")
        k = _mm_nn(memn, wk2, tn=D_XA, name=f"k{l}")
        v = _mm_nn(memn, wv2, tn=D_XA, name=f"v{l}")
        o = _xa_fwd(q, k, v, name=f"xa{l}")
        h2 = _mm_nn_sm(o, gw["w_o"], out_dtype=F32, add=h1, name=f"xa_out{l}")
        hn3 = _rms_fwd(h2, norm_ffn[l:l + 1], name=f"rms_ffn{l}")
        fg = _mm_nn_sm(hn3, gw["w_gate"], name=f"ff_gate{l}")
        fu = _mm_nn_sm(hn3, gw["w_up"], name=f"ff_up{l}")
        act = _swiglu_fwd(fg, fu, name=f"swiglu{l}")
        h3 = _mm_nn(act, wd2, tn=1024, tm=512, out_dtype=F32, add=h2, name=f"ff_down{l}")
        s.update(hn1=hn1, pz=pz, pxbc=pxbc, dtr=dtr, ps=ps, xc=xc, dt=dt, dtg=dtg, acsg=acsg, acst=acst, y=y,
                 states=states, mix=mix, h1=h1, hn2=hn2, q=q, k=k, v=v, o=o, h2=h2, hn3=hn3, fg=fg, fu=fu, act=act)
        saved.append(s)
        h = h3
        if l + 1 < depth:
            gathered.append(dict(zip(names, _gather_wait(flying, sems, h3, copies=_forward_copies,
                                                         name=f"gather_fwait{l + 1}"))))

    loss_vec, dh, dhb, d_norm_final = _final(h, norm_final.reshape(1, D), tgt, name="final")
    loss = lax.psum(loss_vec[0, 0], ("x", "y", "c"))

    small = dict(norm_mix=[], ssd_conv_w=[], ssd_conv_b=[], dt_bias=[], a_log=[], d_skip=[], ssd_norm=[], sc_conv_w=[],
                 sc_norm=[], norm_xa=[], norm_ffn=[])
    dmemn = None
    carried = {n: None for n in names}
    pending = None
    first_groups = [["w_gate", "w_up", "w_down"], ["w_out", "w_q", "w_k", "w_v", "w_o"], ["w_in"]]

    def start_reduce(lyr, tag, group, grads, behind=None):
        g_list = [grads[n] for n in group]
        if behind is None:
            recv_sib = _swap_halves(g_list, name=f"swap_halves{tag}")
        else:
            sems_s, g_fly, lands_s, tok_s = _swap_start(g_list, name=f"swap_start{tag}")
            finish_reduce(behind, tok_s)
            g_list, recv_sib = _swap_wait(g_fly, lands_s, sems_s, carried["w_in"][0], name=f"swap_wait{tag}")
        parts = [_add_halves(g, rb, c_arr, name=f"add_halves_{n}{lyr}") for n, g, rb in zip(group, g_list, recv_sib)]
        return (lyr, tag, group) + _owners_start(parts, name=f"owners_start{tag}")

    def finish_reduce(pend, after):
        lyr, tag, group, sems_r, parts_r, lands_r, _ = pend
        parts_r, lands_r = _owners_wait(parts_r, lands_r, sems_r, after, name=f"owners_wait{tag}")
        halves = [_sum_chips(p, rc, q_arr, c_arr, name=f"sum_chips_{n}{lyr}") for n, p, rc in zip(group, parts_r, lands_r)]
        full = _join_halves(halves, name=f"join_halves{tag}")
        for n, g in zip(group, full):
            carried[n] = _adamw_layer(lyr, big[n], big_m[n], big_v[n], g, carried[n], name=f"adamw_{n}{lyr}")

    for l in reversed(range(depth)):
        s, gw = saved[l], gathered[l]
        t = dh.shape[0]
        tm = _tile(t, 1024)
        early = []
        dact = _mm_nt(dhb, s["wd2"], tn=FF_CW, dep=None if pending is None else pending[-1], name=f"d_act{l}")
        dw_down = _mm_tn(s["act"], dhb, tm=FF_CW, tn=1024, name=f"dw_down{l}")
        dg, du = _swiglu_bwd(s["fg"], s["fu"], dact, name=f"d_swiglu{l}")
        dw_gate = _mm_tn_sm(s["hn3"], dg, tm=1024, name=f"dw_gate{l}")
        dw_up = _mm_tn_sm(s["hn3"], du, tm=1024, name=f"dw_up{l}")
        dhn = _mm_nt_sm(dg, gw["w_gate"], tn=512, out_dtype=F32, name=f"d_hn3a{l}")
        dhn = _mm_nt_sm(du, gw["w_up"], tn=512, out_dtype=F32, add=dhn, name=f"d_hn3b{l}")
        dh, dhb, dn = _rms_bwd(s["h2"], norm_ffn[l:l + 1], dhn, dh, name=f"d_rms_ffn{l}")
        small["norm_ffn"].append(dn)
        dep = None
        if l == 0:
            early.append(start_reduce(0, "0a", first_groups[0],
                                      dict(w_gate=dw_gate, w_up=dw_up, w_down=dw_down.reshape(4, -1, D))))
            dep = early[-1][-1]
        do = _mm_nt_sm(dhb, gw["w_o"], tn=D_XA, dep=dep, name=f"d_o{l}")
        dw_o = _mm_tn_sm(s["o"], dhb, tm=D_XA, name=f"dw_o{l}")
        dq, dk, dv = _xa_bwd(s["q"], s["k"], s["v"], do, name=f"d_xa{l}")
        dw_q = _mm_tn(s["hn2"], dq, tm=1024, tn=D_XA, name=f"dw_q{l}")
        dw_k = _mm_tn(memn, dk, tm=1024, tn=D_XA, name=f"dw_k{l}")
        dw_v = _mm_tn(memn, dv, tm=1024, tn=D_XA, name=f"dw_v{l}")
        dhn = _mm_nt(dq, s["wq2"], tn=1024, out_dtype=F32, name=f"d_hn2{l}")
        dmemn = _mm_nt(dk, s["wk2"], tn=1024, out_dtype=F32, add=dmemn, name=f"d_memn_k{l}")
        dmemn = _mm_nt(dv, s["wv2"], tn=1024, out_dtype=F32, add=dmemn, name=f"d_memn_v{l}")
        dh, dhb, dn = _rms_bwd(s["h1"], norm_xa[l:l + 1], dhn, dh, name=f"d_rms_xa{l}")
        small["norm_xa"].append(dn)
        dw_out = _mm_tn(s["mix"], dhb, tm=1024, tn=1024, name=f"dw_out{l}")
        dep = None
        if l == 0:
            early.append(start_reduce(0, "0b", first_groups[1],
                                      dict(w_out=dw_out.reshape(4, -1, D), w_q=dw_q.reshape(4, -1, D_XA),
                                           w_k=dw_k.reshape(4, -1, D_XA), w_v=dw_v.reshape(4, -1, D_XA), w_o=dw_o)))
            dep = early[-1][-1]
        dmix = _mm_nt(dhb, s["wo2"], tn=1024, dep=dep, name=f"d_mix{l}")
        dps, d_scw, d_scn = _sc_bwd(s["ps"], sc_full[l], sc_norm[l:l + 1], dmix, name=f"d_sc{l}")
        dxc, dz, ddtg, dacg, ddsk, d_ssdn = _ssd_bwd(s["xc"], s["dtg"], s["acsg"], s["acst"], s["pz"], s["y"], s["states"],
                                                           dmix, dskip_ch[l:l + 1], ssd_norm[l:l + 1], name=f"d_ssd{l}")
        dxbc, d_cw, d_cb = _conv_bwd(dxc, s["pxbc"], conv_p[l], convb_p[l:l + 1], name=f"d_conv{l}")
        ddtr, d_dtb, d_alog = _dt_bwd(ddtg, dacg, s["dt"], s["dtr"], dt_bias_p[l:l + 1], a_log_p[l:l + 1], name=f"d_dt{l}")
        small["sc_conv_w"].append(d_scw)
        small["sc_norm"].append(d_scn)
        small["ssd_norm"].append(d_ssdn)
        small["d_skip"].append(jnp.sum(ddsk.reshape(HEADS, D_SSD // HEADS), axis=1).reshape(1, HEADS))
        small["ssd_conv_w"].append(_unperm_xbc(d_cw))
        small["ssd_conv_b"].append(_unperm_xbc(d_cb))
        small["dt_bias"].append(d_dtb[:, :HEADS])
        small["a_log"].append(d_alog[:, :HEADS])
        hn1 = s["hn1"]
        dw_z = _mm_tn(hn1, dz, tm=1024, tn=1024, name=f"dw_z{l}")
        dw_xbc = _mm_tn(hn1, dxbc, tm=1024, tn=1024, name=f"dw_xbc{l}")
        dw_dt = _mm_tn(hn1, ddtr, tm=1024, tn=DT_PAD, name=f"dw_dt{l}")
        tk = _tile(t, TN_TK)
        dw_s = _mm(hn1, dps, mode="tn", grid=(2, 6, t // tk),
                   a_spec=pl.BlockSpec((tk, 1024), lambda i, j, kk: (kk, i)),
                   b_spec=pl.BlockSpec((None, tk, 1024), lambda i, j, kk: (j // 2, kk, j % 2)),
                   o_spec=pl.BlockSpec((1024, 1024), lambda i, j, kk: (i, j)), o_tile=(1024, 1024),
                   out_sds=_sds((D, 3 * D_SC), BF16), name=f"dw_s{l}")
        dhn = _mm_nt(dz, s["w_z"], tn=1024, out_dtype=F32, name=f"d_hn1z{l}")
        dhn = _mm_nt(dxbc, s["w_xbc"], tn=1024, out_dtype=F32, add=dhn, name=f"d_hn1x{l}")
        dhn = _mm_nt(ddtr, s["w_dt"], tn=1024, out_dtype=F32, add=dhn, name=f"d_hn1d{l}")
        dhn = _mm(dps, s["w_s"], mode="nt", grid=(t // tm, 2, 3),
                  a_spec=pl.BlockSpec((None, tm, D_SC), lambda i, j, kk: (kk, i, 0)),
                  b_spec=pl.BlockSpec((1024, D_SC), lambda i, j, kk: (j, kk)),
                  o_spec=pl.BlockSpec((tm, 1024), lambda i, j, kk: (i, j)), o_tile=(tm, 1024),
                  out_sds=_sds((t, D), F32), add=dhn, name=f"d_hn1s{l}")
        dh, dhb, dn = _rms_bwd(s["h0"], norm_mix[l:l + 1], dhn, dh, name=f"d_rms_mix{l}")
        small["norm_mix"].append(dn)

        grads = dict(w_in=_join_dw_in(dw_z, dw_xbc, dw_dt, dw_s), w_out=dw_out.reshape(4, -1, D),
                     w_q=dw_q.reshape(4, -1, D_XA), w_k=dw_k.reshape(4, -1, D_XA), w_v=dw_v.reshape(4, -1, D_XA),
                     w_o=dw_o, w_gate=dw_gate, w_up=dw_up, w_down=dw_down.reshape(4, -1, D))
        if l == 0:
            if pending is not None:
                finish_reduce(pending, dh)
            pending = start_reduce(0, "0c", first_groups[2], grads)
            finish_reduce(early[0], pending[-1])
            finish_reduce(early[1], carried["w_down"][0])
        else:
            pending = start_reduce(l, str(l), names, grads, behind=pending)

    finish_reduce(pending, carried["w_out"][0])
    grad_x = dh[None]

    _, _, d_mem_norm = _rms_bwd(mem[0], mem_norm.reshape(1, D), dmemn, jnp.zeros_like(dmemn), name="d_mem_norm")
    stack = lambda n: jnp.concatenate(small[n][::-1], axis=0) if small[n][0].ndim == 2 and small[n][0].shape[0] == 1 \
        else jnp.stack(small[n][::-1], axis=0)
    small_names = ["norm_mix", "ssd_conv_w", "ssd_conv_b", "dt_bias", "a_log", "d_skip", "ssd_norm", "sc_conv_w", "sc_norm",
                   "mem_norm", "norm_xa", "norm_ffn", "norm_final"]
    local_g = dict(mem_norm=d_mem_norm.reshape(D), norm_final=d_norm_final.reshape(D))
    for n in small:
        local_g[n] = stack(n)
    shapes = [local_g[n].shape for n in small_names]
    red = dict(zip(small_names, _unpack(_allreduce_small(_pack([local_g[n] for n in small_names]), name="allreduce_small"), shapes)))
    red["ssd_conv_w"] = lax.dynamic_slice(red["ssd_conv_w"], (0, 0, qme * (D_XBC // 4)), ssd_conv_w.shape)
    red["sc_conv_w"] = lax.dynamic_slice(red["sc_conv_w"], (0, 0, qme * (D_SC // 4)), sc_conv_w.shape)
    sw = dict(norm_mix=norm_mix, ssd_conv_w=ssd_conv_w, ssd_conv_b=ssd_conv_b, dt_bias=dt_bias, a_log=a_log, d_skip=d_skip,
              ssd_norm=ssd_norm, sc_conv_w=sc_conv_w, sc_norm=sc_norm, mem_norm=mem_norm, norm_xa=norm_xa, norm_ffn=norm_ffn,
              norm_final=norm_final)
    sm = dict(norm_mix=m_norm_mix, ssd_conv_w=m_ssd_conv_w, ssd_conv_b=m_ssd_conv_b, dt_bias=m_dt_bias, a_log=m_a_log,
              d_skip=m_d_skip, ssd_norm=m_ssd_norm, sc_conv_w=m_sc_conv_w, sc_norm=m_sc_norm, mem_norm=m_mem_norm,
              norm_xa=m_norm_xa, norm_ffn=m_norm_ffn, norm_final=m_norm_final)
    sv = dict(norm_mix=v_norm_mix, ssd_conv_w=v_ssd_conv_w, ssd_conv_b=v_ssd_conv_b, dt_bias=v_dt_bias, a_log=v_a_log,
              d_skip=v_d_skip, ssd_norm=v_ssd_norm, sc_conv_w=v_sc_conv_w, sc_norm=v_sc_norm, mem_norm=v_mem_norm,
              norm_xa=v_norm_xa, norm_ffn=v_norm_ffn, norm_final=v_norm_final)
    shard_shapes = [sw[n].shape for n in small_names]
    pk = lambda d: _pack([d[n] for n in small_names])
    sd, snm, snv = _adamw_flat(pk(sw), pk(red), pk(sm), pk(sv), name="adamw_small")
    s_delta = dict(zip(small_names, _unpack(sd, shard_shapes)))
    s_newm = dict(zip(small_names, _unpack(snm, shard_shapes)))
    s_newv = dict(zip(small_names, _unpack(snv, shard_shapes)))

    order = ["norm_mix", "w_in", "ssd_conv_w", "ssd_conv_b", "dt_bias", "a_log", "d_skip", "ssd_norm", "sc_conv_w", "sc_norm",
             "w_out", "mem_norm", "norm_xa", "w_q", "w_k", "w_v", "w_o", "norm_ffn", "w_gate", "w_up", "w_down", "norm_final"]

    def pick(kind):
        out = []
        for n in order:
            if n in carried:
                out.append(carried[n][kind])
            else:
                out.append([red, s_delta, s_newm, s_newv][kind][n])
        return out

    return (loss, grad_x, *pick(0), *pick(1), *pick(2), *pick(3))
```

```python
import functools

import jax
import jax.numpy as jnp
from jax import lax
from jax.experimental import pallas as pl
from jax.experimental.pallas import tpu as pltpu

F32 = jnp.float32
BF16 = jnp.bfloat16
MESH = pl.DeviceIdType.MESH

D = 2048
D_SSD = 2048
N_GROUPS = 4
GROUP_W = 768
D_XBC = 3072
N_STATE = 128
HEADS = 32
PAIRS_PER_GROUP = 4
CHUNK = 256
DT_PAD = 128
D_SC = 2048
SC_GROUP = 128
XA_HEADS = 4
XA_HD = 128
D_XA = 512
D_FF = 5632
EPS = 1e-5
HALO = 16
TN_TK = 2048
VMEM_LIMIT = 56 * 1024 * 1024

ADAM_LR, ADAM_B1, ADAM_B2, ADAM_EPS, ADAM_WD, ADAM_STEP = 0.001, 0.9, 0.999, 1e-08, 0.01, 10

NT = (((1,), (1,)), ((), ()))
TN = (((0,), (0,)), ((), ()))
NN = (((1,), (0,)), ((), ()))


def _pcall(body, **kw):
    return pl.pallas_call(body, **kw)


def _params(**kw):
    return pltpu.CompilerParams(vmem_limit_bytes=VMEM_LIMIT, **kw)


def _sds(shape, dtype):
    return jax.ShapeDtypeStruct(shape, dtype)


def _sig(x):
    return 0.5 * jnp.tanh(0.5 * x) + 0.5


def _dot(a, b, dims=NN):
    return lax.dot_general(a, b, dims, preferred_element_type=F32)


def _mm(a, b, *, mode, grid, a_spec, b_spec, o_spec, o_tile, out_sds, add=None, dep=None, name):
    gk = grid[2]
    dims = {"nn": NN, "nt": NT, "tn": TN, "nt4": NT}[mode]
    has_add = add is not None
    n_dep = 0 if dep is None else 1

    def body(*refs):
        a_ref, b_ref = refs[0], refs[1]
        add_ref = refs[2] if has_add else None
        refs = refs[:2 + has_add] + refs[2 + has_add + n_dep:]
        o_ref = refs[2 + has_add]
        if mode == "nt4":
            bv = jnp.concatenate([b_ref[s] for s in range(4)], axis=1)
        else:
            bv = b_ref[...].astype(BF16)
        p = _dot(a_ref[...].astype(BF16), bv, dims)

        def finish(acc):
            if has_add:
                acc = acc + add_ref[...]
            o_ref[...] = acc.astype(o_ref.dtype)

        if gk == 1:
            finish(p)
        else:
            acc_ref = refs[3 + has_add]
            k = pl.program_id(2)

            @pl.when(k == 0)
            def _():
                acc_ref[...] = p

            @pl.when(k > 0)
            def _():
                acc_ref[...] += p

            @pl.when(k == gk - 1)
            def _():
                finish(acc_ref[...])

    in_specs = [a_spec, b_spec] + ([o_spec] if has_add else []) + [pl.BlockSpec(memory_space=pl.ANY)] * n_dep
    args = (a, b) + ((add,) if has_add else ()) + ((dep,) if n_dep else ())
    scratch = [pltpu.VMEM(o_tile, F32)] if gk > 1 else []
    return _pcall(body, grid=grid, in_specs=in_specs, out_specs=o_spec, out_shape=out_sds, scratch_shapes=scratch,
                  compiler_params=_params(dimension_semantics=("parallel", "parallel", "arbitrary")), name=name)(*args)


def _tile(n, pref):
    t = min(n, pref)
    assert n % t == 0, (n, pref)
    return t


def _mm_nn(a, w, *, tn, tk=None, tm=1024, out_dtype=BF16, add=None, dep=None, name):
    m, k = a.shape
    n = w.shape[1]
    tm = _tile(m, tm)
    tk = k if tk is None else tk
    grid = (m // tm, n // tn, k // tk)
    return _mm(a, w, mode="nn", grid=grid,
               a_spec=pl.BlockSpec((tm, tk), lambda i, j, kk: (i, kk)),
               b_spec=pl.BlockSpec((tk, tn), lambda i, j, kk: (kk, j)),
               o_spec=pl.BlockSpec((tm, tn), lambda i, j, kk: (i, j)), o_tile=(tm, tn),
               out_sds=_sds((m, n), out_dtype), add=add, dep=dep, name=name)


def _mm_nn_sm(a, w4, *, out_dtype=BF16, add=None, name):
    m, k = a.shape
    n = w4.shape[2]
    tm = _tile(m, 1024)
    return _mm(a, w4, mode="nn", grid=(m // tm, 4, 1),
               a_spec=pl.BlockSpec((tm, k), lambda i, j, kk: (i, 0)),
               b_spec=pl.BlockSpec((None, k, n), lambda i, j, kk: (j, 0, 0)),
               o_spec=pl.BlockSpec((tm, n), lambda i, j, kk: (i, j)), o_tile=(tm, n),
               out_sds=_sds((m, 4 * n), out_dtype), add=add, name=name)


def _mm_nt(a, w, *, tn, tk=None, out_dtype=BF16, add=None, dep=None, name):
    m, k = a.shape
    n = w.shape[0]
    tm = _tile(m, 1024)
    tk = k if tk is None else tk
    grid = (m // tm, n // tn, k // tk)
    return _mm(a, w, mode="nt", grid=grid,
               a_spec=pl.BlockSpec((tm, tk), lambda i, j, kk: (i, kk)),
               b_spec=pl.BlockSpec((tn, tk), lambda i, j, kk: (j, kk)),
               o_spec=pl.BlockSpec((tm, tn), lambda i, j, kk: (i, j)), o_tile=(tm, tn),
               out_sds=_sds((m, n), out_dtype), add=add, dep=dep, name=name)


def _mm_nt_sm(a, w4, *, tn, out_dtype=BF16, add=None, dep=None, name):
    m = a.shape[0]
    _, k, n = w4.shape
    tm = _tile(m, 512)
    tn = _tile(k, tn)
    return _mm(a, w4, mode="nt4", grid=(m // tm, k // tn, 1),
               a_spec=pl.BlockSpec((tm, 4 * n), lambda i, j, kk: (i, 0)),
               b_spec=pl.BlockSpec((4, tn, n), lambda i, j, kk: (0, j, 0)),
               o_spec=pl.BlockSpec((tm, tn), lambda i, j, kk: (i, j)), o_tile=(tm, tn),
               out_sds=_sds((m, k), out_dtype), add=add, dep=dep, name=name)


def _mm_tn(a, g, *, tm, tn, out_dtype=BF16, name):
    t, m = a.shape
    n = g.shape[1]
    tk = _tile(t, TN_TK)
    return _mm(a, g, mode="tn", grid=(m // tm, n // tn, t // tk),
               a_spec=pl.BlockSpec((tk, tm), lambda i, j, kk: (kk, i)),
               b_spec=pl.BlockSpec((tk, tn), lambda i, j, kk: (kk, j)),
               o_spec=pl.BlockSpec((tm, tn), lambda i, j, kk: (i, j)), o_tile=(tm, tn),
               out_sds=_sds((m, n), out_dtype), name=name)


def _mm_tn_sm(a, g, *, tm, out_dtype=BF16, name):
    t, m = a.shape
    n = g.shape[1] // 4
    tk = _tile(t, TN_TK)
    return _mm(a, g, mode="tn", grid=(m // tm, 4, t // tk),
               a_spec=pl.BlockSpec((tk, tm), lambda i, j, kk: (kk, i)),
               b_spec=pl.BlockSpec((tk, n), lambda i, j, kk: (kk, j)),
               o_spec=pl.BlockSpec((None, tm, n), lambda i, j, kk: (j, i, 0)), o_tile=(tm, n),
               out_sds=_sds((4, m, n), out_dtype), name=name)


def _rms_fwd(h, g, *, name):
    t, d = h.shape
    tr = _tile(t, 512)

    def body(h_ref, g_ref, o_ref):
        x = h_ref[...]
        r = lax.rsqrt(jnp.mean(x * x, axis=-1, keepdims=True) + EPS)
        o_ref[...] = (x * r * g_ref[...]).astype(o_ref.dtype)

    return _pcall(body, grid=(t // tr,),
                  in_specs=[pl.BlockSpec((tr, d), lambda i: (i, 0)), pl.BlockSpec((1, d), lambda i: (0, 0))],
                  out_specs=pl.BlockSpec((tr, d), lambda i: (i, 0)), out_shape=_sds((t, d), BF16),
                  compiler_params=_params(dimension_semantics=("parallel",)), name=name)(h, g)


def _rms_bwd(h, g, dy, dres, *, name):
    t, d = h.shape
    tr = _tile(t, 256)

    def body(h_ref, g_ref, dy_ref, dres_ref, dh_ref, dhb_ref, dg_ref):
        i = pl.program_id(0)
        x = h_ref[...]
        r = lax.rsqrt(jnp.mean(x * x, axis=-1, keepdims=True) + EPS)
        xh = x * r
        dyv = dy_ref[...].astype(F32)
        dxh = dyv * g_ref[...]
        dh = dres_ref[...] + r * (dxh - xh * jnp.mean(dxh * xh, axis=-1, keepdims=True))
        dh_ref[...] = dh
        dhb_ref[...] = dh.astype(BF16)
        part = jnp.sum(dyv * xh, axis=0, keepdims=True)

        @pl.when(i == 0)
        def _():
            dg_ref[...] = part

        @pl.when(i > 0)
        def _():
            dg_ref[...] += part

    row = pl.BlockSpec((tr, d), lambda i: (i, 0))
    vec = pl.BlockSpec((1, d), lambda i: (0, 0))
    return _pcall(body, grid=(t // tr,), in_specs=[row, vec, row, row], out_specs=[row, row, vec],
                  out_shape=[_sds((t, d), F32), _sds((t, d), BF16), _sds((1, d), F32)],
                  compiler_params=_params(dimension_semantics=("arbitrary",)), name=name)(h, g, dy, dres)


def _final(h, g, tgt, *, name):
    t, d = h.shape
    tr = _tile(t, 256)

    def body(h_ref, g_ref, t_ref, loss_ref, dh_ref, dhb_ref, dg_ref):
        i = pl.program_id(0)
        x = h_ref[...]
        gv = g_ref[...]
        r = lax.rsqrt(jnp.mean(x * x, axis=-1, keepdims=True) + EPS)
        xh = x * r
        e = xh * gv - t_ref[...]
        lpart = jnp.zeros((1, 128), F32) + 0.5 * jnp.sum(jnp.mean(e * e, axis=-1, keepdims=True))
        dyv = e * (1.0 / d)
        dxh = dyv * gv
        dh = r * (dxh - xh * jnp.mean(dxh * xh, axis=-1, keepdims=True))
        dh_ref[...] = dh
        dhb_ref[...] = dh.astype(BF16)
        part = jnp.sum(dyv * xh, axis=0, keepdims=True)

        @pl.when(i == 0)
        def _():
            dg_ref[...] = part
            loss_ref[...] = lpart

        @pl.when(i > 0)
        def _():
            dg_ref[...] += part
            loss_ref[...] += lpart

    row = pl.BlockSpec((tr, d), lambda i: (i, 0))
    vec = pl.BlockSpec((1, d), lambda i: (0, 0))
    return _pcall(body, grid=(t // tr,), in_specs=[row, vec, row],
                  out_specs=[pl.BlockSpec((1, 128), lambda i: (0, 0)), row, row, vec],
                  out_shape=[_sds((1, 128), F32), _sds((t, d), F32), _sds((t, d), BF16), _sds((1, d), F32)],
                  compiler_params=_params(dimension_semantics=("arbitrary",)), name=name)(h, g, tgt)


def _conv_taps(ext, w, ntap, rows):
    n = ext.shape[0]
    acc = w[ntap - 1:ntap, :] * ext[HALO:HALO + rows]
    for k in range(1, ntap):
        acc = acc + w[ntap - 1 - k:ntap - k, :] * pltpu.roll(ext, k, axis=0)[HALO:HALO + rows]
    del n
    return acc


def _conv_fwd(xbc, w, b, *, name):
    t, c = xbc.shape
    rows = CHUNK
    cw = GROUP_W
    hb = rows // HALO

    def body(cur_ref, prev_ref, w_ref, b_ref, o_ref):
        i = pl.program_id(1)
        cur = cur_ref[...].astype(F32)
        prev = jnp.where(i > 0, prev_ref[...].astype(F32), 0.0)
        ext = jnp.concatenate([prev, cur], axis=0)
        pre = _conv_taps(ext, w_ref[...], 4, rows) + b_ref[...]
        o_ref[...] = (pre * _sig(pre)).astype(o_ref.dtype)

    return _pcall(body, grid=(c // cw, t // rows),
                  in_specs=[pl.BlockSpec((rows, cw), lambda j, i: (i, j)),
                            pl.BlockSpec((HALO, cw), lambda j, i: (jnp.maximum(i * hb - 1, 0), j)),
                            pl.BlockSpec((4, cw), lambda j, i: (0, j)),
                            pl.BlockSpec((1, cw), lambda j, i: (0, j))],
                  out_specs=pl.BlockSpec((rows, cw), lambda j, i: (i, j)), out_shape=_sds((t, c), BF16),
                  compiler_params=_params(dimension_semantics=("parallel", "parallel")), name=name)(xbc, xbc, w, b)


def _conv_bwd(dxc, xbc, w, b, *, name):
    t, c = xbc.shape
    rows = CHUNK
    cw = GROUP_W
    hb = rows // HALO
    nblk = t // rows
    nhalo = t // HALO

    def body(d_ref, dn_ref, cur_ref, prev_ref, next_ref, w_ref, b_ref, dx_ref, dw_ref, db_ref):
        i = pl.program_id(1)
        last = i == nblk - 1
        wv = w_ref[...]
        xe = jnp.concatenate([jnp.where(i > 0, prev_ref[...].astype(F32), 0.0), cur_ref[...].astype(F32),
                              jnp.where(last, 0.0, next_ref[...].astype(F32))], axis=0)
        n = rows + 2 * HALO
        sh = [xe] + [pltpu.roll(xe, k, axis=0) for k in range(1, 4)]
        pre = wv[3:4, :] * sh[0] + wv[2:3, :] * sh[1] + wv[1:2, :] * sh[2] + wv[0:1, :] * sh[3] + b_ref[...]
        de = jnp.concatenate([jnp.zeros((HALO, cw), F32), d_ref[...].astype(F32),
                              jnp.where(last, 0.0, dn_ref[...].astype(F32))], axis=0)
        s = _sig(pre)
        dpre = de * (s * (1.0 + pre * (1.0 - s)))
        dx = wv[3:4, :] * dpre
        for m in range(1, 4):
            dx = dx + wv[3 - m:4 - m, :] * pltpu.roll(dpre, n - m, axis=0)
        dx_ref[...] = dx[HALO:HALO + rows].astype(dx_ref.dtype)
        dcur = dpre[HALO:HALO + rows]
        dwv = jnp.concatenate([jnp.sum(dcur * sh[3 - j][HALO:HALO + rows], axis=0, keepdims=True) for j in range(4)], axis=0)
        dbv = jnp.sum(dcur, axis=0, keepdims=True)

        @pl.when(i == 0)
        def _():
            dw_ref[...] = dwv
            db_ref[...] = dbv

        @pl.when(i > 0)
        def _():
            dw_ref[...] += dwv
            db_ref[...] += dbv

    cur = pl.BlockSpec((rows, cw), lambda j, i: (i, j))
    prev = pl.BlockSpec((HALO, cw), lambda j, i: (jnp.maximum(i * hb - 1, 0), j))
    nxt = pl.BlockSpec((HALO, cw), lambda j, i: (jnp.minimum((i + 1) * hb, nhalo - 1), j))
    return _pcall(body, grid=(c // cw, nblk),
                  in_specs=[cur, nxt, cur, prev, nxt, pl.BlockSpec((4, cw), lambda j, i: (0, j)),
                            pl.BlockSpec((1, cw), lambda j, i: (0, j))],
                  out_specs=[cur, pl.BlockSpec((4, cw), lambda j, i: (0, j)), pl.BlockSpec((1, cw), lambda j, i: (0, j))],
                  out_shape=[_sds((t, c), BF16), _sds((4, c), F32), _sds((1, c), F32)],
                  compiler_params=_params(dimension_semantics=("parallel", "arbitrary")), name=name)(dxc, dxc, xbc, xbc, xbc, w, b)


def _neg_exp_alog(alog):
    lane = lax.broadcasted_iota(jnp.int32, alog.shape, 1)
    return jnp.where(lane < HEADS, -jnp.exp(alog), 0.0)


def _dt_prep(dtr, bias, alog, *, name):
    t = dtr.shape[0]
    rows = CHUNK

    def body(r_ref, b_ref, a_ref, dt_ref, dtg_ref, acsg_ref, acst_ref):
        raw = r_ref[...] + b_ref[...]
        dt = jnp.maximum(raw, 0.0) + jnp.log(1.0 + jnp.exp(-jnp.abs(raw)))
        a = _neg_exp_alog(a_ref[...])
        adt = dt * a
        ri = lax.broadcasted_iota(jnp.int32, (rows, rows), 0)
        ci = lax.broadcasted_iota(jnp.int32, (rows, rows), 1)
        tri = (ri >= ci).astype(F32)
        acs = jnp.dot(tri, adt, precision=lax.Precision.HIGHEST, preferred_element_type=F32)
        dt_ref[...] = dt
        acst_ref[...] = acs.T
        for g in range(N_GROUPS):
            sh = (128 - 8 * g) % 128
            dtg_ref[g] = dt if sh == 0 else pltpu.roll(dt, sh, axis=1)
            acsg_ref[g] = acs if sh == 0 else pltpu.roll(acs, sh, axis=1)

    row = pl.BlockSpec((rows, DT_PAD), lambda i: (i, 0))
    vec = pl.BlockSpec((1, DT_PAD), lambda i: (0, 0))
    grp = pl.BlockSpec((N_GROUPS, rows, DT_PAD), lambda i: (0, i, 0))
    return _pcall(body, grid=(t // rows,), in_specs=[row, vec, vec],
                  out_specs=[row, grp, grp, pl.BlockSpec((DT_PAD, rows), lambda i: (0, i))],
                  out_shape=[_sds((t, DT_PAD), F32), _sds((N_GROUPS, t, DT_PAD), F32), _sds((N_GROUPS, t, DT_PAD), F32),
                             _sds((DT_PAD, t), F32)],
                  compiler_params=_params(dimension_semantics=("parallel",)), name=name)(dtr, bias, alog)


def _dt_bwd(ddtg, dacg, dt, dtr, bias, alog, *, name):
    t = dtr.shape[0]
    rows = CHUNK

    def body(ddtg_ref, dacg_ref, dt_ref, r_ref, b_ref, a_ref, dr_ref, db_ref, da_ref):
        i = pl.program_id(0)
        lane = lax.broadcasted_iota(jnp.int32, (rows, DT_PAD), 1)
        ddt = jnp.zeros((rows, DT_PAD), F32)
        dacs = jnp.zeros((rows, DT_PAD), F32)
        for g in range(N_GROUPS):
            sel = (lane >= 8 * g) & (lane < 8 * g + 8)
            dd = ddtg_ref[g]
            da = dacg_ref[g]
            if g:
                dd = pltpu.roll(dd, 8 * g, axis=1)
                da = pltpu.roll(da, 8 * g, axis=1)
            ddt = ddt + jnp.where(sel, dd, 0.0)
            dacs = dacs + jnp.where(sel, da, 0.0)
        ri = lax.broadcasted_iota(jnp.int32, (rows, rows), 0)
        ci = lax.broadcasted_iota(jnp.int32, (rows, rows), 1)
        triu = (ci >= ri).astype(F32)
        rev = jnp.dot(triu, dacs, precision=lax.Precision.HIGHEST, preferred_element_type=F32)
        a = _neg_exp_alog(a_ref[...])
        dtv = dt_ref[...]
        raw = r_ref[...] + b_ref[...]
        draw = (ddt + a * rev) * (1.0 / (1.0 + jnp.exp(-raw)))
        dr_ref[...] = draw
        dbv = jnp.sum(draw, axis=0, keepdims=True)
        dav = jnp.sum(dtv * rev, axis=0, keepdims=True) * a

        @pl.when(i == 0)
        def _():
            db_ref[...] = dbv
            da_ref[...] = dav

        @pl.when(i > 0)
        def _():
            db_ref[...] += dbv
            da_ref[...] += dav

    row = pl.BlockSpec((rows, DT_PAD), lambda i: (i, 0))
    vec = pl.BlockSpec((1, DT_PAD), lambda i: (0, 0))
    grp = pl.BlockSpec((N_GROUPS, rows, DT_PAD), lambda i: (0, i, 0))
    return _pcall(body, grid=(t // rows,),
                  in_specs=[grp, grp, row, row, vec, vec],
                  out_specs=[row, vec, vec], out_shape=[_sds((t, DT_PAD), F32), _sds((1, DT_PAD), F32), _sds((1, DT_PAD), F32)],
                  compiler_params=_params(dimension_semantics=("arbitrary",)), name=name)(ddtg, dacg, dt, dtr, bias, alog)


def _pair_cols(col_ref_val, p, lo):
    return jnp.where(lo, col_ref_val[:, 2 * p:2 * p + 1], col_ref_val[:, 2 * p + 1:2 * p + 2])


def _ssd_fwd(xc, dtg, acsg, acst, z, dskip, nw, *, name):
    t = xc.shape[0]
    L = CHUNK
    nc = t // L

    def body(xc_ref, dtg_ref, acsg_ref, acst_ref, z_ref, dsk_ref, nw_ref, y_ref, st_ref, mix_ref, s_ref):
        c = pl.program_id(1)

        @pl.when(c == 0)
        def _():
            s_ref[...] = jnp.zeros_like(s_ref)

        blk = xc_ref[...]
        bm = blk[:, 512:640]
        cm = blk[:, 640:768]
        cb = _dot(cm, bm, NT)
        dtv = dtg_ref[...]
        acs = acsg_ref[...]
        acst_v = acst_ref[...]
        ri = lax.broadcasted_iota(jnp.int32, (L, L), 0)
        ci = lax.broadcasted_iota(jnp.int32, (L, L), 1)
        causal = ri >= ci
        lo = lax.broadcasted_iota(jnp.int32, (1, 128), 1) < 64
        lo_rows = lax.broadcasted_iota(jnp.int32, (128, 1), 0) < 64
        dskv = dsk_ref[...]
        ys = []
        for p in range(PAIRS_PER_GROUP):
            xp = blk[:, 128 * p:128 * p + 128].astype(F32)
            dt_p = _pair_cols(dtv, p, lo)
            a_p = _pair_cols(acs, p, lo)
            alast = acs[L - 1:L, :]
            al_p = _pair_cols(alast, p, lo)
            xdt = xp * dt_p
            xdt_b = xdt.astype(BF16)
            yd = []
            for hh in range(2):
                j = 2 * p + hh
                seg = acs[:, j:j + 1] - acst_v[j:j + 1, :]
                lam = jnp.exp(jnp.where(causal, seg, -1e30))
                w = (cb * lam).astype(BF16)
                yd.append(_dot(w, xdt_b))
            y = jnp.where(lo, yd[0], yd[1])
            sp = s_ref[p]
            st_ref[p] = sp
            y = y + _dot(cm, sp.astype(BF16), NT) * jnp.exp(a_p)
            dsc = jnp.exp(al_p - a_p)
            snew = _dot((xdt * dsc).astype(BF16), bm, TN)
            al_rows = jnp.where(lo_rows, alast[:, 2 * p:2 * p + 1], alast[:, 2 * p + 1:2 * p + 2])
            s_ref[p] = sp * jnp.exp(al_rows) + snew
            ys.append(y + xp * dskv[:, 128 * p:128 * p + 128])
        yfull = jnp.concatenate(ys, axis=1)
        y_ref[...] = yfull.astype(y_ref.dtype)
        zz = z_ref[...].astype(F32)
        yg = yfull * (zz * _sig(zz))
        r = lax.rsqrt(jnp.mean(yg * yg, axis=-1, keepdims=True) + EPS)
        mix_ref[...] = (yg * r * nw_ref[...]).astype(mix_ref.dtype)

    grp = pl.BlockSpec((None, L, DT_PAD), lambda g, c: (g, c, 0))
    return _pcall(body, grid=(N_GROUPS, nc),
                  in_specs=[pl.BlockSpec((L, GROUP_W), lambda g, c: (c, g)), grp, grp,
                            pl.BlockSpec((8, L), lambda g, c: (g, c)),
                            pl.BlockSpec((L, 512), lambda g, c: (c, g)),
                            pl.BlockSpec((1, 512), lambda g, c: (0, g)), pl.BlockSpec((1, 512), lambda g, c: (0, g))],
                  out_specs=[pl.BlockSpec((L, 512), lambda g, c: (c, g)),
                             pl.BlockSpec((None, PAIRS_PER_GROUP, 128, N_STATE), lambda g, c: (c, g, 0, 0)),
                             pl.BlockSpec((L, 512), lambda g, c: (c, g))],
                  out_shape=[_sds((t, D_SSD), BF16), _sds((nc, N_GROUPS * PAIRS_PER_GROUP, 128, N_STATE), F32),
                             _sds((t, D_SSD + D_SC), BF16)],
                  scratch_shapes=[pltpu.VMEM((PAIRS_PER_GROUP, 128, N_STATE), F32)],
                  compiler_params=_params(dimension_semantics=("parallel", "arbitrary")), name=name)(
                      xc, dtg, acsg, acst, z, dskip, nw)


def _ssd_bwd(xc, dtg, acsg, acst, z, y, states, dmix, dskip, nw, *, name):
    t = xc.shape[0]
    L = CHUNK
    nc = t // L

    def body(xc_ref, dtg_ref, acsg_ref, acst_ref, z_ref, y_ref, st_ref, dm_ref, dsk_ref, nw_ref,
             dxc_ref, dz_ref, ddt_ref, dac_ref, ddsk_ref, dnw_ref, ds_ref):
        c = pl.program_id(1)

        @pl.when(c == 0)
        def _():
            ds_ref[...] = jnp.zeros_like(ds_ref)
            ddsk_ref[...] = jnp.zeros_like(ddsk_ref)
            dnw_ref[...] = jnp.zeros_like(dnw_ref)

        blk = xc_ref[...]
        xs = blk[:, :512].astype(F32)
        bm = blk[:, 512:640]
        cm = blk[:, 640:768]
        bmf = bm.astype(F32)
        yv = y_ref[...].astype(F32)
        zz = z_ref[...].astype(F32)
        nwv = nw_ref[...]
        dout = dm_ref[...].astype(F32)
        sz = _sig(zz)
        silu = zz * sz
        yg = yv * silu
        r = lax.rsqrt(jnp.mean(yg * yg, axis=-1, keepdims=True) + EPS)
        xh = yg * r
        dnw_ref[...] += jnp.sum(dout * xh, axis=0, keepdims=True)
        dyn = dout * nwv
        dyg = r * (dyn - xh * jnp.mean(dyn * xh, axis=-1, keepdims=True))
        dy = dyg * silu
        dz_ref[...] = (dyg * yv * (sz * (1.0 + zz * (1.0 - sz)))).astype(dz_ref.dtype)
        ddsk_ref[...] += jnp.sum(dy * xs, axis=0, keepdims=True)

        cb = _dot(cm, bm, NT)
        dtv = dtg_ref[...]
        acs = acsg_ref[...]
        acst_v = acst_ref[...]
        alast = acs[L - 1:L, :]
        ri = lax.broadcasted_iota(jnp.int32, (L, L), 0)
        ci = lax.broadcasted_iota(jnp.int32, (L, L), 1)
        causal = ri >= ci
        lane = lax.broadcasted_iota(jnp.int32, (1, 128), 1)
        lo = lane < 64
        lo_rows = lax.broadcasted_iota(jnp.int32, (128, 1), 0) < 64
        lane_l = lax.broadcasted_iota(jnp.int32, (L, DT_PAD), 1)
        row_l = lax.broadcasted_iota(jnp.int32, (L, 1), 0)
        dskv = dsk_ref[...]
        dm_acc = jnp.zeros((L, L), F32)
        db_acc = jnp.zeros((L, N_STATE), F32)
        dc_acc = jnp.zeros((L, N_STATE), F32)
        ddt_out = jnp.zeros((L, DT_PAD), F32)
        dac_out = jnp.zeros((L, DT_PAD), F32)
        dxs = []
        for p in range(PAIRS_PER_GROUP):
            xp = xs[:, 128 * p:128 * p + 128]
            dyp = dy[:, 128 * p:128 * p + 128]
            dyp_b = dyp.astype(BF16)
            dyp_r = dyp_b.astype(F32)
            dt_p = _pair_cols(dtv, p, lo)
            a_p = _pair_cols(acs, p, lo)
            al_p = _pair_cols(alast, p, lo)
            xdt = xp * dt_p
            xdt_b = xdt.astype(BF16)
            xdt_r = xdt_b.astype(F32)
            ea_p = jnp.exp(a_p)
            dsc_p = jnp.exp(al_p - a_p)
            sp = st_ref[p]
            sp_b = sp.astype(BF16)
            dsp = ds_ref[p]
            dsp_b = dsp.astype(BF16)
            cs = _dot(cm, sp_b, NT)
            dye_b = (dyp * ea_p).astype(BF16)
            dc_acc = dc_acc + _dot(dye_b, sp_b)
            ds_prev = _dot(dye_b, cm, TN)
            bds = _dot(bm, dsp_b, NT)
            al_rows = jnp.where(lo_rows, alast[:, 2 * p:2 * p + 1], alast[:, 2 * p + 1:2 * p + 2])
            ds_prev = ds_prev + jnp.exp(al_rows) * dsp
            prod_st = dsp * sp
            dxdt_h = []
            for hh in range(2):
                j = 2 * p + hh
                hm = lo if hh == 0 else jnp.logical_not(lo)
                hm_rows = lo_rows if hh == 0 else jnp.logical_not(lo_rows)
                a_col = acs[:, j:j + 1]
                seg = a_col - acst_v[j:j + 1, :]
                lam = jnp.exp(jnp.where(causal, seg, -1e30))
                w = (cb * lam).astype(BF16)
                dy_h = jnp.where(hm, dyp, 0.0).astype(BF16)
                dw = _dot(dy_h, xdt_b, NT)
                dxd = _dot(w, dyp_b, TN)
                dxdt_h.append(dxd)
                dm_acc = dm_acc + dw * lam
                diag = dyp_r * _dot(w, xdt_b) - xdt_r * dxd
                dac = jnp.sum(jnp.where(hm, diag + dyp * cs * jnp.exp(a_col), 0.0), axis=1, keepdims=True)
                al_h = alast[:, j:j + 1]
                dal = jnp.exp(al_h) * jnp.sum(jnp.sum(jnp.where(hm_rows, prod_st, 0.0), axis=1, keepdims=True), axis=0, keepdims=True)
                xds_h = _dot(jnp.where(hm, xdt, 0.0).astype(BF16), dsp_b)
                dsc_col = jnp.exp(al_h - a_col)
                db_acc = db_acc + dsc_col * xds_h
                tt = jnp.sum(xds_h * bmf, axis=1, keepdims=True) * dsc_col
                dal = dal + jnp.sum(tt, axis=0, keepdims=True)
                dac = dac - tt + jnp.where(row_l == L - 1, dal, 0.0)
                dac_out = jnp.where(lane_l == j, dac, dac_out)
            dxdt = jnp.where(lo, dxdt_h[0], dxdt_h[1]) + dsc_p * bds
            dxs.append(dxdt * dt_p + dyp * dskv[:, 128 * p:128 * p + 128])
            prod_dt = dxdt * xp
            for hh in range(2):
                j = 2 * p + hh
                hm = lo if hh == 0 else jnp.logical_not(lo)
                ddt_col = jnp.sum(jnp.where(hm, prod_dt, 0.0), axis=1, keepdims=True)
                ddt_out = jnp.where(lane_l == j, ddt_col, ddt_out)
            ds_ref[p] = ds_prev
        dm_b = dm_acc.astype(BF16)
        dc_acc = dc_acc + _dot(dm_b, bm)
        db_acc = db_acc + _dot(dm_b, cm, TN)
        dxc_ref[...] = jnp.concatenate(dxs + [db_acc, dc_acc], axis=1).astype(dxc_ref.dtype)
        ddt_ref[...] = ddt_out
        dac_ref[...] = dac_out

    rc = lambda g, c: (nc - 1 - c, g)
    grp = pl.BlockSpec((None, L, DT_PAD), lambda g, c: (g, nc - 1 - c, 0))
    vec = pl.BlockSpec((1, 512), lambda g, c: (0, g))
    return _pcall(body, grid=(N_GROUPS, nc),
                  in_specs=[pl.BlockSpec((L, GROUP_W), rc), grp, grp,
                            pl.BlockSpec((8, L), lambda g, c: (g, nc - 1 - c)),
                            pl.BlockSpec((L, 512), rc), pl.BlockSpec((L, 512), rc),
                            pl.BlockSpec((None, PAIRS_PER_GROUP, 128, N_STATE), lambda g, c: (nc - 1 - c, g, 0, 0)),
                            pl.BlockSpec((L, 512), rc), vec, vec],
                  out_specs=[pl.BlockSpec((L, GROUP_W), rc), pl.BlockSpec((L, 512), rc), grp, grp, vec, vec],
                  out_shape=[_sds((t, D_XBC), BF16), _sds((t, D_SSD), BF16), _sds((N_GROUPS, t, DT_PAD), F32),
                             _sds((N_GROUPS, t, DT_PAD), F32), _sds((1, D_SSD), F32), _sds((1, D_SSD), F32)],
                  scratch_shapes=[pltpu.VMEM((PAIRS_PER_GROUP, 128, N_STATE), F32)],
                  compiler_params=_params(dimension_semantics=("parallel", "arbitrary")), name=name)(
                      xc, dtg, acsg, acst, z, y, states, dmix, dskip, nw)


SC_CW = 1024


def _group_rstd(v):
    outs = []
    for q in range(v.shape[1] // SC_GROUP):
        vq = v[:, SC_GROUP * q:SC_GROUP * (q + 1)]
        outs.append(jnp.broadcast_to(lax.rsqrt(jnp.mean(vq * vq, axis=-1, keepdims=True) + EPS), vq.shape))
    return jnp.concatenate(outs, axis=1)


def _group_mean(v):
    outs = []
    for q in range(v.shape[1] // SC_GROUP):
        vq = v[:, SC_GROUP * q:SC_GROUP * (q + 1)]
        outs.append(jnp.broadcast_to(jnp.mean(vq, axis=-1, keepdims=True), vq.shape))
    return jnp.concatenate(outs, axis=1)


def _sc_fwd(ps, w, nw, mix, *, name):
    t = ps.shape[1]
    rows = CHUNK
    hb = rows // HALO
    cw = SC_CW
    off = D_SSD // cw

    def body(cur_ref, prev_ref, w_ref, nw_ref, mix_in_ref, o_ref):
        del mix_in_ref
        i = pl.program_id(1)
        u = cur_ref[0].astype(F32)
        gb = cur_ref[1].astype(F32)
        gc = cur_ref[2].astype(F32)
        cu_prev = jnp.where(i > 0, prev_ref[2].astype(F32) * prev_ref[0].astype(F32), 0.0)
        ext = jnp.concatenate([cu_prev, gc * u], axis=0)
        v = gb * _conv_taps(ext, w_ref[...], 3, rows)
        o_ref[...] = (v * _group_rstd(v) * nw_ref[...]).astype(o_ref.dtype)

    return _pcall(body, grid=(D_SC // cw, t // rows),
                  in_specs=[pl.BlockSpec((3, rows, cw), lambda j, i: (0, i, j)),
                            pl.BlockSpec((3, HALO, cw), lambda j, i: (0, jnp.maximum(i * hb - 1, 0), j)),
                            pl.BlockSpec((3, cw), lambda j, i: (0, j)), pl.BlockSpec((1, cw), lambda j, i: (0, j)),
                            pl.BlockSpec(memory_space=pl.ANY)],
                  out_specs=pl.BlockSpec((rows, cw), lambda j, i: (i, off + j)),
                  out_shape=_sds(mix.shape, mix.dtype), input_output_aliases={4: 0},
                  compiler_params=_params(dimension_semantics=("parallel", "parallel")), name=name)(ps, ps, w, nw, mix)


def _sc_bwd(ps, w, nw, dmix, *, name):
    t = ps.shape[1]
    rows = CHUNK
    hb = rows // HALO
    cw = SC_CW
    off = D_SSD // cw
    nblk = t // rows
    nhalo = t // HALO
    n = rows + 2 * HALO

    def body(cur_ref, prev_ref, next_ref, w_ref, nw_ref, d_ref, dn_ref, dps_ref, dw_ref, dnw_ref):
        i = pl.program_id(1)
        first = i == 0
        last = i == nblk - 1

        def ext(k):
            return jnp.concatenate([jnp.where(first, 0.0, prev_ref[k].astype(F32)), cur_ref[k].astype(F32),
                                    jnp.where(last, 0.0, next_ref[k].astype(F32))], axis=0)

        ue, gbe, gce = ext(0), ext(1), ext(2)
        wv = w_ref[...]
        nwv = nw_ref[...]
        cue = gce * ue
        cu1 = pltpu.roll(cue, 1, axis=0)
        cu2 = pltpu.roll(cue, 2, axis=0)
        conv = wv[2:3, :] * cue + wv[1:2, :] * cu1 + wv[0:1, :] * cu2
        ve = gbe * conv
        doe = jnp.concatenate([jnp.zeros((HALO, cw), F32), d_ref[...].astype(F32),
                               jnp.where(last, 0.0, dn_ref[...].astype(F32))], axis=0)
        r = _group_rstd(ve)
        xh = ve * r
        dvn = doe * nwv
        dv = r * (dvn - xh * _group_mean(dvn * xh))
        dconv = dv * gbe
        dcu = wv[2:3, :] * dconv + wv[1:2, :] * pltpu.roll(dconv, n - 1, axis=0) + wv[0:1, :] * pltpu.roll(dconv, n - 2, axis=0)
        sl = slice(HALO, HALO + rows)
        dps_ref[0] = (dcu * gce)[sl].astype(dps_ref.dtype)
        dps_ref[1] = (dv * conv)[sl].astype(dps_ref.dtype)
        dps_ref[2] = (dcu * ue)[sl].astype(dps_ref.dtype)
        dc = dconv[sl]
        dwv = jnp.concatenate([jnp.sum(dc * cu2[sl], axis=0, keepdims=True), jnp.sum(dc * cu1[sl], axis=0, keepdims=True),
                               jnp.sum(dc * cue[sl], axis=0, keepdims=True)], axis=0)
        dnv = jnp.sum((doe * xh)[sl], axis=0, keepdims=True)

        @pl.when(first)
        def _():
            dw_ref[...] = dwv
            dnw_ref[...] = dnv

        @pl.when(i > 0)
        def _():
            dw_ref[...] += dwv
            dnw_ref[...] += dnv

    cur = pl.BlockSpec((3, rows, cw), lambda j, i: (0, i, j))
    prev = pl.BlockSpec((3, HALO, cw), lambda j, i: (0, jnp.maximum(i * hb - 1, 0), j))
    nxt = pl.BlockSpec((3, HALO, cw), lambda j, i: (0, jnp.minimum((i + 1) * hb, nhalo - 1), j))
    return _pcall(body, grid=(D_SC // cw, nblk),
                  in_specs=[cur, prev, nxt, pl.BlockSpec((3, cw), lambda j, i: (0, j)), pl.BlockSpec((1, cw), lambda j, i: (0, j)),
                            pl.BlockSpec((rows, cw), lambda j, i: (i, off + j)),
                            pl.BlockSpec((HALO, cw), lambda j, i: (jnp.minimum((i + 1) * hb, nhalo - 1), off + j))],
                  out_specs=[cur, pl.BlockSpec((3, cw), lambda j, i: (0, j)), pl.BlockSpec((1, cw), lambda j, i: (0, j))],
                  out_shape=[_sds(ps.shape, BF16), _sds((3, D_SC), F32), _sds((1, D_SC), F32)],
                  compiler_params=_params(dimension_semantics=("parallel", "arbitrary")), name=name)(ps, ps, ps, w, nw, dmix, dmix)


XA_SCALE = XA_HD ** -0.5


def _softmax(s):
    m = jnp.max(s, axis=-1, keepdims=True)
    e = jnp.exp(s - m)
    return e * (1.0 / jnp.sum(e, axis=-1, keepdims=True))


def _xa_fwd(q, k, v, *, name):
    t = q.shape[0]
    nm = k.shape[0]
    tq = _tile(t, 512)

    def body(q_ref, k_ref, v_ref, o_ref):
        outs = []
        for h in range(XA_HEADS):
            sl = slice(XA_HD * h, XA_HD * (h + 1))
            s = _dot(q_ref[:, sl], k_ref[:, sl], NT) * XA_SCALE
            outs.append(_dot(_softmax(s).astype(BF16), v_ref[:, sl]))
        o_ref[...] = jnp.concatenate(outs, axis=1).astype(o_ref.dtype)

    row = pl.BlockSpec((tq, D_XA), lambda i: (i, 0))
    kv = pl.BlockSpec((nm, D_XA), lambda i: (0, 0))
    return _pcall(body, grid=(t // tq,), in_specs=[row, kv, kv], out_specs=row, out_shape=_sds((t, D_XA), BF16),
                  compiler_params=_params(dimension_semantics=("parallel",)), name=name)(q, k, v)


def _xa_bwd(q, k, v, do, *, name):
    t = q.shape[0]
    nm = k.shape[0]
    tq = _tile(t, 512)

    def body(q_ref, k_ref, v_ref, do_ref, dq_ref, dk_ref, dv_ref):
        i = pl.program_id(0)
        dqs, dks, dvs = [], [], []
        for h in range(XA_HEADS):
            sl = slice(XA_HD * h, XA_HD * (h + 1))
            qh, kh, vh, doh = q_ref[:, sl], k_ref[:, sl], v_ref[:, sl], do_ref[:, sl]
            p = _softmax(_dot(qh, kh, NT) * XA_SCALE)
            dvs.append(_dot(p.astype(BF16), doh, TN))
            dp = _dot(doh, vh, NT)
            ds = (p * (dp - jnp.sum(dp * p, axis=-1, keepdims=True)) * XA_SCALE).astype(BF16)
            dqs.append(_dot(ds, kh))
            dks.append(_dot(ds, qh, TN))
        dq_ref[...] = jnp.concatenate(dqs, axis=1).astype(dq_ref.dtype)
        dkv = jnp.concatenate(dks, axis=1)
        dvv = jnp.concatenate(dvs, axis=1)

        @pl.when(i == 0)
        def _():
            dk_ref[...] = dkv
            dv_ref[...] = dvv

        @pl.when(i > 0)
        def _():
            dk_ref[...] += dkv
            dv_ref[...] += dvv

    row = pl.BlockSpec((tq, D_XA), lambda i: (i, 0))
    kv = pl.BlockSpec((nm, D_XA), lambda i: (0, 0))
    return _pcall(body, grid=(t // tq,), in_specs=[row, kv, kv, row], out_specs=[row, kv, kv],
                  out_shape=[_sds((t, D_XA), BF16), _sds((nm, D_XA), F32), _sds((nm, D_XA), F32)],
                  compiler_params=_params(dimension_semantics=("arbitrary",)), name=name)(q, k, v, do)


FF_CW = 1408


def _swiglu_fwd(g, u, *, name):
    t, f = g.shape
    tr = _tile(t, 512)

    def body(g_ref, u_ref, o_ref):
        gv = g_ref[...].astype(F32)
        o_ref[...] = (gv * _sig(gv) * u_ref[...].astype(F32)).astype(o_ref.dtype)

    blk = pl.BlockSpec((tr, FF_CW), lambda i, j: (i, j))
    return _pcall(body, grid=(t // tr, f // FF_CW), in_specs=[blk, blk], out_specs=blk, out_shape=_sds((t, f), BF16),
                  compiler_params=_params(dimension_semantics=("parallel", "parallel")), name=name)(g, u)


def _swiglu_bwd(g, u, dact, *, name):
    t, f = g.shape
    tr = _tile(t, 512)

    def body(g_ref, u_ref, d_ref, dg_ref, du_ref):
        gv = g_ref[...].astype(F32)
        uv = u_ref[...].astype(F32)
        dv = d_ref[...].astype(F32)
        s = _sig(gv)
        dg_ref[...] = (dv * uv * (s * (1.0 + gv * (1.0 - s)))).astype(dg_ref.dtype)
        du_ref[...] = (dv * gv * s).astype(du_ref.dtype)

    blk = pl.BlockSpec((tr, FF_CW), lambda i, j: (i, j))
    return _pcall(body, grid=(t // tr, f // FF_CW), in_specs=[blk, blk, blk], out_specs=[blk, blk],
                  out_shape=[_sds((t, f), BF16), _sds((t, f), BF16)],
                  compiler_params=_params(dimension_semantics=("parallel", "parallel")), name=name)(g, u, dact)


def _row_tile(n):
    for cand in (128, 64, 32, 16):
        if n % cand == 0:
            return cand
    raise ValueError(n)


def _add_halves(g4, rb, c_arr, *, name):
    _, r, cdim = g4.shape
    hr = r // 2
    rt = _row_tile(hr)
    nb = hr // rt

    def body(c_ref, g_ref, rb_ref, o_ref):
        del c_ref
        o_ref[...] = (g_ref[...].astype(F32) + rb_ref[...].astype(F32)).astype(o_ref.dtype)

    gs = pltpu.PrefetchScalarGridSpec(
        num_scalar_prefetch=1, grid=(4, nb),
        in_specs=[pl.BlockSpec((None, rt, cdim), lambda q, i, c: (q, c[0] * nb + i, 0)),
                  pl.BlockSpec((None, rt, cdim), lambda q, i, c: (q, i, 0))],
        out_specs=pl.BlockSpec((None, rt, cdim), lambda q, i, c: (q, i, 0)))
    return _pcall(body, grid_spec=gs, out_shape=_sds((4, hr, cdim), BF16),
                  compiler_params=_params(dimension_semantics=("parallel", "parallel")), name=name)(c_arr, g4, rb)


def _sum_chips(p4, rc, q_arr, c_arr, *, name):
    _, hr, cdim = p4.shape
    rt = _row_tile(hr)
    nb = hr // rt

    def body(q_ref, c_ref, p_ref, rc_ref, o_ref):
        del q_ref, c_ref
        o_ref[...] = ((p_ref[...].astype(F32) + rc_ref[0].astype(F32)) + rc_ref[1].astype(F32)) + rc_ref[2].astype(F32)

    gs = pltpu.PrefetchScalarGridSpec(
        num_scalar_prefetch=2, grid=(nb,),
        in_specs=[pl.BlockSpec((None, rt, cdim), lambda i, q, c: (q[0], i, 0)),
                  pl.BlockSpec((3, rt, cdim), lambda i, q, c: (0, i, 0))],
        out_specs=pl.BlockSpec((rt, cdim), lambda i, q, c: (c[0] * nb + i, 0)))
    return _pcall(body, grid_spec=gs, out_shape=_sds((2 * hr, cdim), F32),
                  compiler_params=_params(dimension_semantics=("parallel",)), name=name)(q_arr, c_arr, p4, rc)


def _adam_math(w, g, m, v):
    m = ADAM_B1 * m + (1.0 - ADAM_B1) * g
    v = ADAM_B2 * v + (1.0 - ADAM_B2) * (g * g)
    m_hat = m / (1.0 - ADAM_B1 ** ADAM_STEP)
    v_hat = v / (1.0 - ADAM_B2 ** ADAM_STEP)
    delta = -ADAM_LR * (m_hat / (jnp.sqrt(v_hat) + ADAM_EPS) + ADAM_WD * w)
    return delta, m, v


def _adamw_layer(layer, w, m, v, g, prev, *, name):
    depth, r, cdim = w.shape
    rt = _row_tile(r)
    n_prev = 0 if prev is None else 4

    def body(*refs):
        w_ref, m_ref, v_ref, g_ref = refs[:4]
        go_ref, d_ref, mo_ref, vo_ref = refs[4 + n_prev:]
        gv = g_ref[...]
        delta, mn, vn = _adam_math(w_ref[...], gv, m_ref[...], v_ref[...])
        go_ref[...] = gv
        d_ref[...] = delta
        mo_ref[...] = mn
        vo_ref[...] = vn

    st = pl.BlockSpec((None, rt, cdim), lambda i: (layer, i, 0))
    in_specs = [st, st, st, pl.BlockSpec((rt, cdim), lambda i: (i, 0))] + [pl.BlockSpec(memory_space=pl.ANY)] * n_prev
    args = (w, m, v, g) + (tuple(prev) if prev is not None else ())
    return _pcall(body, grid=(r // rt,), in_specs=in_specs, out_specs=[st] * 4,
                  out_shape=[_sds((depth, r, cdim), F32)] * 4,
                  input_output_aliases={4 + i: i for i in range(n_prev)},
                  compiler_params=_params(dimension_semantics=("parallel",)), name=name)(*args)


def _adamw_flat(w, g, m, v, *, name):
    def body(w_ref, g_ref, m_ref, v_ref, d_ref, mo_ref, vo_ref):
        delta, mn, vn = _adam_math(w_ref[...], g_ref[...], m_ref[...], v_ref[...])
        d_ref[...] = delta
        mo_ref[...] = mn
        vo_ref[...] = vn

    return _pcall(body, out_shape=[_sds(w.shape, F32)] * 3, compiler_params=_params(), name=name)(w, g, m, v)


def _place():
    x, y, c = lax.axis_index("x"), lax.axis_index("y"), lax.axis_index("c")
    chips = [(1 - x, y), (x, 1 - y), (1 - x, 1 - y)]
    return x, y, c, chips


def _cast_place(w, q_arr, *, dep=None, name):
    r, cdim = w.shape
    rt = _row_tile(r)
    deps = () if dep is None else (dep,)

    def body(q_ref, w_ref, *rest):
        del q_ref
        o_ref = rest[-1]
        o_ref[...] = w_ref[...].astype(o_ref.dtype)

    gs = pltpu.PrefetchScalarGridSpec(
        num_scalar_prefetch=1, grid=(r // rt,),
        in_specs=[pl.BlockSpec((rt, cdim), lambda i, q: (i, 0))] + [pl.BlockSpec(memory_space=pl.ANY)] * len(deps),
        out_specs=pl.BlockSpec((None, rt, cdim), lambda i, q: (q[0], i, 0)))
    return _pcall(body, grid_spec=gs, out_shape=_sds((4, r, cdim), BF16),
                  compiler_params=_params(dimension_semantics=("parallel",)), name=name)(q_arr, w, *deps)


HBM_SPEC = pl.BlockSpec(memory_space=pltpu.HBM)
SEM_SPEC = pl.BlockSpec(memory_space=pltpu.SEMAPHORE)
ANY_SPEC = pl.BlockSpec(memory_space=pl.ANY)
SPLIT_PARAMS = pltpu.CompilerParams(has_side_effects=pltpu.SideEffectType.DATAFLOW_SIDE_EFFECTING)


def _gather_copies(bufs, sems):
    n = len(bufs)
    x, y, c, chips = _place()
    qme = 2 * x + y
    cps = []
    for t in range(n):
        hr = bufs[t].shape[1] // 2
        mine = bufs[t].at[qme, pl.ds(c * hr, hr)]
        for k, (px, py) in enumerate(chips):
            landed = bufs[t].at[2 * px + py, pl.ds(c * hr, hr)]
            peer = dict(device_id=(px, py, c), device_id_type=MESH)
            cps.append((pltpu.make_async_remote_copy(src_ref=mine, dst_ref=mine, send_sem=sems[3 * t + k],
                                                     recv_sem=sems[3 * n + 3 * t + k], **peer),
                        pltpu.make_async_remote_copy(src_ref=mine, dst_ref=landed, send_sem=sems[3 * t + k],
                                                     recv_sem=sems[3 * n + 3 * t + k], **peer)))
    return cps


def _forward_copies(bufs, sems):
    n = len(bufs)
    x, y, c, chips = _place()
    sib = dict(device_id=(x, y, 1 - c), device_id_type=MESH)
    cps = []
    for t in range(n):
        hr = bufs[t].shape[1] // 2
        for k, (px, py) in enumerate(chips):
            landed = bufs[t].at[2 * px + py, pl.ds(c * hr, hr)]
            other = bufs[t].at[2 * px + py, pl.ds((1 - c) * hr, hr)]
            cps.append((pltpu.make_async_remote_copy(src_ref=landed, dst_ref=landed, send_sem=sems[3 * t + k],
                                                     recv_sem=sems[3 * n + 3 * t + k], **sib),
                        pltpu.make_async_remote_copy(src_ref=landed, dst_ref=other, send_sem=sems[3 * t + k],
                                                     recv_sem=sems[3 * n + 3 * t + k], **sib)))
    return cps


def _join_copies(bufs, sems):
    n = len(bufs)
    x, y, c, _ = _place()
    sib = dict(device_id=(x, y, 1 - c), device_id_type=MESH)
    cps = []
    for t in range(n):
        hr = bufs[t].shape[0] // 2
        mine = bufs[t].at[pl.ds(c * hr, hr)]
        theirs = bufs[t].at[pl.ds((1 - c) * hr, hr)]
        cps.append((pltpu.make_async_remote_copy(src_ref=mine, dst_ref=mine, send_sem=sems[t], recv_sem=sems[n + t], **sib),
                    pltpu.make_async_remote_copy(src_ref=mine, dst_ref=theirs, send_sem=sems[t], recv_sem=sems[n + t], **sib)))
    return cps


def _gather_start(bufs, after, *, copies=_gather_copies, per=3, name):
    n = len(bufs)
    ns = 2 * per * n

    def body(*refs):
        ins = refs[:n]
        sems = refs[n + 1:n + 1 + ns]
        token = refs[2 * n + 1 + ns]
        for send, _ in copies(ins, sems):
            send.start()
        token[...] = jnp.zeros_like(token)

    res = _pcall(body, in_specs=[HBM_SPEC] * n + [ANY_SPEC],
                 out_specs=(SEM_SPEC,) * ns + (HBM_SPEC,) * n + (pl.BlockSpec(memory_space=pltpu.VMEM),),
                 out_shape=(pltpu.SemaphoreType.DMA(()),) * ns + tuple(pltpu.HBM(b.shape, b.dtype) for b in bufs)
                 + (_sds((8, 128), F32),),
                 input_output_aliases={t: ns + t for t in range(n)}, compiler_params=SPLIT_PARAMS, name=name)(*bufs, after)
    return list(res[:ns]), list(res[ns:ns + n]), res[ns + n]


def _gather_wait(bufs, sems, after, *, copies=_gather_copies, name):
    n = len(bufs)
    ns = len(sems)

    def body(*refs):
        ins = refs[:n]
        for send, arrive in copies(ins, refs[n:n + ns]):
            send.wait_send()
            arrive.wait_recv()

    return list(_pcall(body, in_specs=[HBM_SPEC] * n + [SEM_SPEC] * ns + [ANY_SPEC], out_specs=(HBM_SPEC,) * n,
                       out_shape=tuple(pltpu.HBM(b.shape, b.dtype) for b in bufs),
                       input_output_aliases={t: t for t in range(n)}, compiler_params=SPLIT_PARAMS, name=name)(*bufs, *sems, after))


def _gather_forward(bufs, *, name):
    n = len(bufs)

    def body(*refs):
        outs = refs[n:2 * n]
        send, recv = refs[2 * n:]
        x, y, c, chips = _place()
        sib = dict(device_id=(x, y, 1 - c), device_id_type=MESH)
        cps = []
        for t in range(n):
            hr = outs[t].shape[1] // 2
            for k, (px, py) in enumerate(chips):
                landed = outs[t].at[2 * px + py, pl.ds(c * hr, hr)]
                cp = pltpu.make_async_remote_copy(src_ref=landed, dst_ref=landed, send_sem=send.at[t, k],
                                                  recv_sem=recv.at[t, k], **sib)
                cp.start()
                cps.append(cp)
        for t in range(n):
            hr = outs[t].shape[1] // 2
            for k, (px, py) in enumerate(chips):
                other = outs[t].at[2 * px + py, pl.ds((1 - c) * hr, hr)]
                pltpu.make_async_remote_copy(src_ref=other, dst_ref=other, send_sem=send.at[t, k], recv_sem=recv.at[t, k],
                                             **sib).wait_recv()
        for cp in cps:
            cp.wait_send()

    return _pcall(body, in_specs=[ANY_SPEC] * n, out_specs=[ANY_SPEC] * n,
                  out_shape=[_sds(b.shape, b.dtype) for b in bufs], input_output_aliases={t: t for t in range(n)},
                  scratch_shapes=[pltpu.SemaphoreType.DMA((n, 3))] * 2,
                  compiler_params=pltpu.CompilerParams(has_side_effects=True), name=name)(*bufs)


def _owner_copies(parts, lands, sems):
    n = len(parts)
    _, _, c, chips = _place()
    cps = []
    for t in range(n):
        for k, (px, py) in enumerate(chips):
            cps.append(pltpu.make_async_remote_copy(src_ref=parts[t].at[2 * px + py], dst_ref=lands[t].at[k],
                                                    send_sem=sems[3 * t + k], recv_sem=sems[3 * n + 3 * t + k],
                                                    device_id=(px, py, c), device_id_type=MESH))
    return cps


def _swap_copies(grads, lands, sems):
    n = len(grads)
    x, y, c, _ = _place()
    cps = []
    for t in range(n):
        hr = grads[t].shape[1] // 2
        cps.append(pltpu.make_async_remote_copy(src_ref=grads[t].at[:, pl.ds((1 - c) * hr, hr), :], dst_ref=lands[t],
                                                send_sem=sems[t], recv_sem=sems[n + t],
                                                device_id=(x, y, 1 - c), device_id_type=MESH))
    return cps


def _exchange_start(srcs, land_shapes, copies, n_copies, *, name):
    n = len(srcs)
    ns = 2 * n_copies
    lands = [pltpu.with_memory_space_constraint(lax.empty(shape, s.dtype), pltpu.HBM) for shape, s in zip(land_shapes, srcs)]

    def body(*refs):
        ins, lnd = refs[:n], refs[n:2 * n]
        sems = refs[2 * n:2 * n + ns]
        token = refs[4 * n + ns]
        for cp in copies(ins, lnd, sems):
            cp.start()
        token[...] = jnp.zeros_like(token)

    res = _pcall(body, in_specs=[HBM_SPEC] * (2 * n),
                 out_specs=(SEM_SPEC,) * ns + (HBM_SPEC,) * (2 * n) + (pl.BlockSpec(memory_space=pltpu.VMEM),),
                 out_shape=(pltpu.SemaphoreType.DMA(()),) * ns
                 + tuple(pltpu.HBM(b.shape, b.dtype) for b in list(srcs) + lands) + (_sds((8, 128), F32),),
                 input_output_aliases={t: ns + t for t in range(2 * n)}, compiler_params=SPLIT_PARAMS, name=name)(*srcs, *lands)
    return list(res[:ns]), list(res[ns:ns + n]), list(res[ns + n:ns + 2 * n]), res[ns + 2 * n]


def _exchange_wait(srcs, lands, sems, after, copies, *, name):
    n = len(srcs)
    ns = len(sems)

    def body(*refs):
        ins, lnd = refs[:n], refs[n:2 * n]
        for cp in copies(ins, lnd, refs[2 * n:2 * n + ns]):
            cp.wait_send()
            cp.wait_recv()

    res = _pcall(body, in_specs=[HBM_SPEC] * (2 * n) + [SEM_SPEC] * ns + [ANY_SPEC], out_specs=(HBM_SPEC,) * (2 * n),
                 out_shape=tuple(pltpu.HBM(b.shape, b.dtype) for b in list(srcs) + list(lands)),
                 input_output_aliases={t: t for t in range(2 * n)}, compiler_params=SPLIT_PARAMS, name=name)(
                     *srcs, *lands, *sems, after)
    return list(res[:n]), list(res[n:])


def _owners_start(parts, *, name):
    return _exchange_start(parts, [(3,) + p.shape[1:] for p in parts], _owner_copies, 3 * len(parts), name=name)


def _owners_wait(parts, lands, sems, after, *, name):
    return _exchange_wait(parts, lands, sems, after, _owner_copies, name=name)


def _swap_start(grads, *, name):
    return _exchange_start(grads, [(4, g.shape[1] // 2, g.shape[2]) for g in grads], _swap_copies, len(grads), name=name)


def _swap_wait(grads, lands, sems, after, *, name):
    return _exchange_wait(grads, lands, sems, after, _swap_copies, name=name)


def _swap_halves(grads, *, name):
    n = len(grads)

    def body(*refs):
        ins, outs = refs[:n], refs[n:2 * n]
        send, recv = refs[2 * n:]
        x, y, c, _ = _place()
        cps = []
        for t in range(n):
            hr = ins[t].shape[1] // 2
            cp = pltpu.make_async_remote_copy(src_ref=ins[t].at[:, pl.ds((1 - c) * hr, hr), :], dst_ref=outs[t],
                                              send_sem=send.at[t], recv_sem=recv.at[t],
                                              device_id=(x, y, 1 - c), device_id_type=MESH)
            cp.start()
            cps.append(cp)
        for cp in cps:
            cp.wait()

    anyspec = pl.BlockSpec(memory_space=pl.ANY)
    return _pcall(body, in_specs=[anyspec] * n, out_specs=[anyspec] * n,
                  out_shape=[_sds((4, g.shape[1] // 2, g.shape[2]), g.dtype) for g in grads],
                  scratch_shapes=[pltpu.SemaphoreType.DMA((n,))] * 2,
                  compiler_params=pltpu.CompilerParams(has_side_effects=True), name=name)(*grads)


def _join_halves(bufs, *, name):
    n = len(bufs)

    def body(*refs):
        outs = refs[n:2 * n]
        send, recv = refs[2 * n:]
        x, y, c, _ = _place()
        cps = []
        for t in range(n):
            hr = outs[t].shape[0] // 2
            mine = outs[t].at[pl.ds(c * hr, hr)]
            cp = pltpu.make_async_remote_copy(src_ref=mine, dst_ref=mine, send_sem=send.at[t], recv_sem=recv.at[t],
                                              device_id=(x, y, 1 - c), device_id_type=MESH)
            cp.start()
            cps.append(cp)
        for t in range(n):
            hr = outs[t].shape[0] // 2
            theirs = outs[t].at[pl.ds((1 - c) * hr, hr)]
            pltpu.make_async_remote_copy(src_ref=theirs, dst_ref=theirs, send_sem=send.at[t], recv_sem=recv.at[t],
                                         device_id=(x, y, 1 - c), device_id_type=MESH).wait_recv()
        for cp in cps:
            cp.wait_send()

    anyspec = pl.BlockSpec(memory_space=pl.ANY)
    return _pcall(body, in_specs=[anyspec] * n, out_specs=[anyspec] * n,
                  out_shape=[_sds(b.shape, b.dtype) for b in bufs], input_output_aliases={t: t for t in range(n)},
                  scratch_shapes=[pltpu.SemaphoreType.DMA((n,))] * 2,
                  compiler_params=pltpu.CompilerParams(has_side_effects=True), name=name)(*bufs)


def _allreduce_small(buf, *, name):
    rows = buf.shape[0]
    rels = [(dx, dy, dc) for dx in (0, 1) for dy in (0, 1) for dc in (0, 1)][1:]

    def body(in_ref, out_ref, gbuf, send, recv):
        x, y, c = lax.axis_index("x"), lax.axis_index("y"), lax.axis_index("c")
        me = 4 * x + 2 * y + c
        gbuf[me] = in_ref[...]
        cps = []
        for k, (dx, dy, dc) in enumerate(rels):
            peer = (x + dx - 2 * x * dx, y + dy - 2 * y * dy, c + dc - 2 * c * dc)
            cp = pltpu.make_async_remote_copy(src_ref=in_ref, dst_ref=gbuf.at[me], send_sem=send.at[k], recv_sem=recv.at[k],
                                              device_id=peer, device_id_type=MESH)
            cp.start()
            cps.append(cp)
        for k, (dx, dy, dc) in enumerate(rels):
            px, py, pc = x + dx - 2 * x * dx, y + dy - 2 * y * dy, c + dc - 2 * c * dc
            pltpu.make_async_remote_copy(src_ref=in_ref, dst_ref=gbuf.at[4 * px + 2 * py + pc], send_sem=send.at[k],
                                         recv_sem=recv.at[k], device_id=(px, py, pc), device_id_type=MESH).wait_recv()
        for cp in cps:
            cp.wait_send()
        acc = gbuf[0]
        for d in range(1, 8):
            acc = acc + gbuf[d]
        out_ref[...] = acc

    vm = pl.BlockSpec(memory_space=pltpu.VMEM)
    return _pcall(body, in_specs=[vm], out_specs=vm, out_shape=_sds(buf.shape, F32),
                  scratch_shapes=[pltpu.VMEM((8, rows, 128), F32), pltpu.SemaphoreType.DMA((7,)), pltpu.SemaphoreType.DMA((7,))],
                  compiler_params=_params(has_side_effects=True), name=name)(buf)


def _pack(arrs):
    flat = jnp.concatenate([a.reshape(-1).astype(F32) for a in arrs])
    n = flat.shape[0]
    rows = -(-n // 1024) * 8
    return jnp.pad(flat, (0, rows * 128 - n)).reshape(rows, 128)


def _unpack(buf, shapes):
    flat = buf.reshape(-1)
    out, o = [], 0
    for s in shapes:
        n = 1
        for d in s:
            n *= d
        out.append(flat[o:o + n].reshape(s))
        o += n
    return out


def _perm_xbc(a):
    parts = []
    for g in range(N_GROUPS):
        parts += [a[..., 512 * g:512 * (g + 1)], a[..., 2048 + 128 * g:2048 + 128 * (g + 1)],
                  a[..., 2560 + 128 * g:2560 + 128 * (g + 1)]]
    return jnp.concatenate(parts, axis=-1)


def _unperm_xbc(a):
    xs = [a[..., GROUP_W * g:GROUP_W * g + 512] for g in range(N_GROUPS)]
    bs = [a[..., GROUP_W * g + 512:GROUP_W * g + 640] for g in range(N_GROUPS)]
    cs = [a[..., GROUP_W * g + 640:GROUP_W * (g + 1)] for g in range(N_GROUPS)]
    return jnp.concatenate(xs + bs + cs, axis=-1)


def _split_w_in(w4):
    k = w4.shape[1]
    nat = jnp.transpose(w4, (1, 0, 2)).reshape(k, -1)
    w_z = nat[:, :2048]
    w_xbc = _perm_xbc(nat[:, 2048:5120])
    w_dt = jnp.pad(nat[:, 5120:5152], ((0, 0), (0, DT_PAD - HEADS)))
    w_s = nat[:, 5152:]
    return w_z, w_xbc, w_dt, w_s


def _join_dw_in(dw_z, dw_xbc, dw_dt, dw_s):
    k = dw_z.shape[0]
    nat = jnp.concatenate([dw_z, _unperm_xbc(dw_xbc), dw_dt[:, :HEADS], dw_s], axis=1)
    return jnp.transpose(nat.reshape(k, 4, -1), (1, 0, 2))


def kernel(x, mem, norm_mix, w_in, ssd_conv_w, ssd_conv_b, dt_bias, a_log, d_skip, ssd_norm, sc_conv_w, sc_norm, w_out, mem_norm, norm_xa, w_q, w_k, w_v, w_o, norm_ffn, w_gate, w_up, w_down, norm_final, loss_target, m_norm_mix, m_w_in, m_ssd_conv_w, m_ssd_conv_b, m_dt_bias, m_a_log, m_d_skip, m_ssd_norm, m_sc_conv_w, m_sc_norm, m_w_out, m_mem_norm, m_norm_xa, m_w_q, m_w_k, m_w_v, m_w_o, m_norm_ffn, m_w_gate, m_w_up, m_w_down, m_norm_final, v_norm_mix, v_w_in, v_ssd_conv_w, v_ssd_conv_b, v_dt_bias, v_a_log, v_d_skip, v_ssd_norm, v_sc_conv_w, v_sc_norm, v_w_out, v_mem_norm, v_norm_xa, v_w_q, v_w_k, v_w_v, v_w_o, v_norm_ffn, v_w_gate, v_w_up, v_w_down, v_norm_final):
    depth = w_in.shape[0]
    ix, iy, ic = lax.axis_index("x"), lax.axis_index("y"), lax.axis_index("c")
    qme = 2 * ix + iy
    c_arr = jnp.reshape(ic, (1,)).astype(jnp.int32)
    q_arr = jnp.reshape(qme, (1,)).astype(jnp.int32)
    h = x[0]
    tgt = loss_target[0]

    big = dict(w_in=w_in, w_out=w_out, w_q=w_q, w_k=w_k, w_v=w_v, w_o=w_o, w_gate=w_gate, w_up=w_up, w_down=w_down)
    big_m = dict(w_in=m_w_in, w_out=m_w_out, w_q=m_w_q, w_k=m_w_k, w_v=m_w_v, w_o=m_w_o, w_gate=m_w_gate, w_up=m_w_up, w_down=m_w_down)
    big_v = dict(w_in=v_w_in, w_out=v_w_out, w_q=v_w_q, w_k=v_w_k, w_v=v_w_v, w_o=v_w_o, w_gate=v_w_gate, w_up=v_w_up, w_down=v_w_down)
    names = list(big)

    conv_full = jnp.zeros((depth, 4, D_XBC), F32)
    conv_full = lax.dynamic_update_slice(conv_full, jnp.where(ic == 0, ssd_conv_w, 0.0), (0, 0, qme * (D_XBC // 4)))
    sc_full = jnp.zeros((depth, 3, D_SC), F32)
    sc_full = lax.dynamic_update_slice(sc_full, jnp.where(ic == 0, sc_conv_w, 0.0), (0, 0, qme * (D_SC // 4)))
    conv_buf = _allreduce_small(_pack([conv_full, sc_full]), name="gather_conv_w")
    conv_full, sc_full = _unpack(conv_buf, [conv_full.shape, sc_full.shape])
    conv_p = _perm_xbc(conv_full)
    convb_p = _perm_xbc(ssd_conv_b)

    pad_h = lambda a: jnp.pad(a, ((0, 0), (0, DT_PAD - HEADS)))
    dt_bias_p, a_log_p = pad_h(dt_bias), pad_h(a_log)
    dskip_ch = jnp.repeat(d_skip, D_SSD // HEADS, axis=1)

    def finish_gather(tag, group, sems, flying, after):
        landed = _gather_wait(flying, sems, after, name=f"gather_wait{tag}")
        return dict(zip(group, _gather_forward(landed, name=f"gather_fwd{tag}")))

    placed0 = [_cast_place(big[n][0], q_arr, name=f"cast_{n}0") for n in names]
    sems_a, fly_a, tok_a = _gather_start(placed0[:1], conv_buf, name="gather_start0a")
    sems_b, fly_b, tok_b = _gather_start(placed0[1:], tok_a, name="gather_start0b")
    placed, last = [placed0], tok_b
    for l in range(1, depth):
        row = []
        for n in names:
            last = _cast_place(big[n][l], q_arr, dep=last, name=f"cast_{n}{l}")
            row.append(last)
        placed.append(row)
    memn = _rms_fwd(mem[0], mem_norm.reshape(1, D) + tok_b[:1, :1], name="memn")
    gathered = [finish_gather("0a", names[:1], sems_a, fly_a, last)]

    saved = []
    for l in range(depth):
        gw = gathered[l]
        gain = norm_mix[l:l + 1]
        if l + 1 < depth:
            sems, flying, token = _gather_start(placed[l + 1], gw["w_in"], name=f"gather_start{l + 1}")
            gain = gain + token[:1, :1]
        w_z, w_xbc, w_dt, w_s = _split_w_in(gw["w_in"])
        hn1 = _rms_fwd(h, gain, name=f"rms_mix{l}")
        pz = _mm_nn(hn1, w_z, tn=1024, name=f"proj_z{l}")
        pxbc = _mm_nn(hn1, w_xbc, tn=1024, name=f"proj_xbc{l}")
        dtr = _mm_nn(hn1, w_dt, tn=DT_PAD, out_dtype=F32, name=f"proj_dt{l}")
        t = h.shape[0]
        tm = _tile(t, 1024)
        ps = _mm(hn1, w_s, mode="nn", grid=(t // tm, 6, 1),
                 a_spec=pl.BlockSpec((tm, D), lambda i, j, kk: (i, 0)),
                 b_spec=pl.BlockSpec((D, 1024), lambda i, j, kk: (0, j)),
                 o_spec=pl.BlockSpec((None, tm, 1024), lambda i, j, kk: (j // 2, i, j % 2)), o_tile=(tm, 1024),
                 out_sds=_sds((3, t, D_SC), BF16), name=f"proj_s{l}")
        xc = _conv_fwd(pxbc, conv_p[l], convb_p[l:l + 1], name=f"conv{l}")
        dt, dtg, acsg, acst = _dt_prep(dtr, dt_bias_p[l:l + 1], a_log_p[l:l + 1], name=f"dt_prep{l}")
        y, states, mix = _ssd_fwd(xc, dtg, acsg, acst, pz, dskip_ch[l:l + 1], ssd_norm[l:l + 1], name=f"ssd{l}")
        mix = _sc_fwd(ps, sc_full[l], sc_norm[l:l + 1], mix, name=f"sc{l}")
        if l == 0:
            gw.update(finish_gather("0b", names[1:], sems_b, fly_b, mix))
        dep = None
        if l + 1 < depth:
            landed = _gather_wait(flying, sems, mix, name=f"gather_wait{l + 1}")
            sems, flying, dep = _gather_start(landed, mix, copies=_forward_copies, name=f"gather_fstart{l + 1}")
        wo2 = gw["w_out"].reshape(-1, D)
        wq2, wk2, wv2 = (gw[n].reshape(D, D_XA) for n in ("w_q", "w_k", "w_v"))
        wd2 = gw["w_down"].reshape(D_FF, D)
        s = dict(h0=h, w_z=w_z, w_xbc=w_xbc, w_dt=w_dt, w_s=w_s, wo2=wo2, wq2=wq2, wk2=wk2, wv2=wv2, wd2=wd2)
        h1 = _mm_nn(mix, wo2, tn=1024, tm=512, out_dtype=F32, add=h, dep=dep, name=f"out_proj{l}")
        hn2 = _rms_fwd(h1, norm_xa[l:l + 1], name=f"rms_xa{l}")
        q = _mm_nn(hn2, wq2, tn=D_XA, name=f"q{l}")
        k = _mm_nn(memn, wk2, tn=D_XA, name=f"k{l}")
        v = _mm_nn(memn, wv2, tn=D_XA, name=f"v{l}")
        o = _xa_fwd(q, k, v, name=f"xa{l}")
        h2 = _mm_nn_sm(o, gw["w_o"], out_dtype=F32, add=h1, name=f"xa_out{l}")
        hn3 = _rms_fwd(h2, norm_ffn[l:l + 1], name=f"rms_ffn{l}")
        fg = _mm_nn_sm(hn3, gw["w_gate"], name=f"ff_gate{l}")
        fu = _mm_nn_sm(hn3, gw["w_up"], name=f"ff_up{l}")
        act = _swiglu_fwd(fg, fu, name=f"swiglu{l}")
        h3 = _mm_nn(act, wd2, tn=1024, tm=512, out_dtype=F32, add=h2, name=f"ff_down{l}")
        s.update(hn1=hn1, pz=pz, pxbc=pxbc, dtr=dtr, ps=ps, xc=xc, dt=dt, dtg=dtg, acsg=acsg, acst=acst, y=y,
                 states=states, mix=mix, h1=h1, hn2=hn2, q=q, k=k, v=v, o=o, h2=h2, hn3=hn3, fg=fg, fu=fu, act=act)
        saved.append(s)
        h = h3
        if l + 1 < depth:
            gathered.append(dict(zip(names, _gather_wait(flying, sems, h3, copies=_forward_copies,
                                                         name=f"gather_fwait{l + 1}"))))

    loss_vec, dh, dhb, d_norm_final = _final(h, norm_final.reshape(1, D), tgt, name="final")
    loss = lax.psum(loss_vec[0, 0], ("x", "y", "c"))

    small = dict(norm_mix=[], ssd_conv_w=[], ssd_conv_b=[], dt_bias=[], a_log=[], d_skip=[], ssd_norm=[], sc_conv_w=[],
                 sc_norm=[], norm_xa=[], norm_ffn=[])
    dmemn = None
    carried = {n: None for n in names}
    pending = None
    first_groups = [["w_gate", "w_up", "w_down"], ["w_out", "w_q", "w_k", "w_v", "w_o"], ["w_in"]]

    def start_reduce(lyr, tag, group, grads, behind=None):
        g_list = [grads[n] for n in group]
        if behind is None:
            recv_sib = _swap_halves(g_list, name=f"swap_halves{tag}")
        else:
            sems_s, g_fly, lands_s, tok_s = _swap_start(g_list, name=f"swap_start{tag}")
            sems_j, fly_j, tok_j = finish_reduce(behind, tok_s, split_join=True)
            g_list, recv_sib = _swap_wait(g_fly, lands_s, sems_s, tok_j, name=f"swap_wait{tag}")
        parts = [_add_halves(g, rb, c_arr, name=f"add_halves_{n}{lyr}") for n, g, rb in zip(group, g_list, recv_sib)]
        pend = (lyr, tag, group) + _owners_start(parts, name=f"owners_start{tag}")
        if behind is not None:
            full = _gather_wait(fly_j, sems_j, pend[-1], copies=_join_copies, name=f"join_wait{behind[1]}")
            apply_adamw(behind[0], behind[2], full)
        return pend

    def apply_adamw(lyr, group, full):
        for n, g in zip(group, full):
            carried[n] = _adamw_layer(lyr, big[n], big_m[n], big_v[n], g, carried[n], name=f"adamw_{n}{lyr}")

    def finish_reduce(pend, after, split_join=False):
        lyr, tag, group, sems_r, parts_r, lands_r, _ = pend
        parts_r, lands_r = _owners_wait(parts_r, lands_r, sems_r, after, name=f"owners_wait{tag}")
        halves = [_sum_chips(p, rc, q_arr, c_arr, name=f"sum_chips_{n}{lyr}") for n, p, rc in zip(group, parts_r, lands_r)]
        if split_join:
            return _gather_start(halves, parts_r[0], copies=_join_copies, per=1, name=f"join_start{tag}")
        apply_adamw(lyr, group, _join_halves(halves, name=f"join_halves{tag}"))

    for l in reversed(range(depth)):
        s, gw = saved[l], gathered[l]
        t = dh.shape[0]
        tm = _tile(t, 1024)
        early = []
        dact = _mm_nt(dhb, s["wd2"], tn=FF_CW, dep=None if pending is None else pending[-1], name=f"d_act{l}")
        dw_down = _mm_tn(s["act"], dhb, tm=FF_CW, tn=1024, name=f"dw_down{l}")
        dg, du = _swiglu_bwd(s["fg"], s["fu"], dact, name=f"d_swiglu{l}")
        dw_gate = _mm_tn_sm(s["hn3"], dg, tm=1024, name=f"dw_gate{l}")
        dw_up = _mm_tn_sm(s["hn3"], du, tm=1024, name=f"dw_up{l}")
        dhn = _mm_nt_sm(dg, gw["w_gate"], tn=512, out_dtype=F32, name=f"d_hn3a{l}")
        dhn = _mm_nt_sm(du, gw["w_up"], tn=512, out_dtype=F32, add=dhn, name=f"d_hn3b{l}")
        dh, dhb, dn = _rms_bwd(s["h2"], norm_ffn[l:l + 1], dhn, dh, name=f"d_rms_ffn{l}")
        small["norm_ffn"].append(dn)
        dep = None
        if l == 0:
            early.append(start_reduce(0, "0a", first_groups[0],
                                      dict(w_gate=dw_gate, w_up=dw_up, w_down=dw_down.reshape(4, -1, D))))
            dep = early[-1][-1]
        do = _mm_nt_sm(dhb, gw["w_o"], tn=D_XA, dep=dep, name=f"d_o{l}")
        dw_o = _mm_tn_sm(s["o"], dhb, tm=D_XA, name=f"dw_o{l}")
        dq, dk, dv = _xa_bwd(s["q"], s["k"], s["v"], do, name=f"d_xa{l}")
        dw_q = _mm_tn(s["hn2"], dq, tm=1024, tn=D_XA, name=f"dw_q{l}")
        dw_k = _mm_tn(memn, dk, tm=1024, tn=D_XA, name=f"dw_k{l}")
        dw_v = _mm_tn(memn, dv, tm=1024, tn=D_XA, name=f"dw_v{l}")
        dhn = _mm_nt(dq, s["wq2"], tn=1024, out_dtype=F32, name=f"d_hn2{l}")
        dmemn = _mm_nt(dk, s["wk2"], tn=1024, out_dtype=F32, add=dmemn, name=f"d_memn_k{l}")
        dmemn = _mm_nt(dv, s["wv2"], tn=1024, out_dtype=F32, add=dmemn, name=f"d_memn_v{l}")
        dh, dhb, dn = _rms_bwd(s["h1"], norm_xa[l:l + 1], dhn, dh, name=f"d_rms_xa{l}")
        small["norm_xa"].append(dn)
        dw_out = _mm_tn(s["mix"], dhb, tm=1024, tn=1024, name=f"dw_out{l}")
        dep = None
        if l == 0:
            early.append(start_reduce(0, "0b", first_groups[1],
                                      dict(w_out=dw_out.reshape(4, -1, D), w_q=dw_q.reshape(4, -1, D_XA),
                                           w_k=dw_k.reshape(4, -1, D_XA), w_v=dw_v.reshape(4, -1, D_XA), w_o=dw_o)))
            dep = early[-1][-1]
        dmix = _mm_nt(dhb, s["wo2"], tn=1024, dep=dep, name=f"d_mix{l}")
        dps, d_scw, d_scn = _sc_bwd(s["ps"], sc_full[l], sc_norm[l:l + 1], dmix, name=f"d_sc{l}")
        dxc, dz, ddtg, dacg, ddsk, d_ssdn = _ssd_bwd(s["xc"], s["dtg"], s["acsg"], s["acst"], s["pz"], s["y"], s["states"],
                                                           dmix, dskip_ch[l:l + 1], ssd_norm[l:l + 1], name=f"d_ssd{l}")
        dxbc, d_cw, d_cb = _conv_bwd(dxc, s["pxbc"], conv_p[l], convb_p[l:l + 1], name=f"d_conv{l}")
        ddtr, d_dtb, d_alog = _dt_bwd(ddtg, dacg, s["dt"], s["dtr"], dt_bias_p[l:l + 1], a_log_p[l:l + 1], name=f"d_dt{l}")
        small["sc_conv_w"].append(d_scw)
        small["sc_norm"].append(d_scn)
        small["ssd_norm"].append(d_ssdn)
        small["d_skip"].append(jnp.sum(ddsk.reshape(HEADS, D_SSD // HEADS), axis=1).reshape(1, HEADS))
        small["ssd_conv_w"].append(_unperm_xbc(d_cw))
        small["ssd_conv_b"].append(_unperm_xbc(d_cb))
        small["dt_bias"].append(d_dtb[:, :HEADS])
        small["a_log"].append(d_alog[:, :HEADS])
        hn1 = s["hn1"]
        dw_z = _mm_tn(hn1, dz, tm=1024, tn=1024, name=f"dw_z{l}")
        dw_xbc = _mm_tn(hn1, dxbc, tm=1024, tn=1024, name=f"dw_xbc{l}")
        dw_dt = _mm_tn(hn1, ddtr, tm=1024, tn=DT_PAD, name=f"dw_dt{l}")
        tk = _tile(t, TN_TK)
        dw_s = _mm(hn1, dps, mode="tn", grid=(2, 6, t // tk),
                   a_spec=pl.BlockSpec((tk, 1024), lambda i, j, kk: (kk, i)),
                   b_spec=pl.BlockSpec((None, tk, 1024), lambda i, j, kk: (j // 2, kk, j % 2)),
                   o_spec=pl.BlockSpec((1024, 1024), lambda i, j, kk: (i, j)), o_tile=(1024, 1024),
                   out_sds=_sds((D, 3 * D_SC), BF16), name=f"dw_s{l}")
        dhn = _mm_nt(dz, s["w_z"], tn=1024, out_dtype=F32, name=f"d_hn1z{l}")
        dhn = _mm_nt(dxbc, s["w_xbc"], tn=1024, out_dtype=F32, add=dhn, name=f"d_hn1x{l}")
        dhn = _mm_nt(ddtr, s["w_dt"], tn=1024, out_dtype=F32, add=dhn, name=f"d_hn1d{l}")
        dhn = _mm(dps, s["w_s"], mode="nt", grid=(t // tm, 2, 3),
                  a_spec=pl.BlockSpec((None, tm, D_SC), lambda i, j, kk: (kk, i, 0)),
                  b_spec=pl.BlockSpec((1024, D_SC), lambda i, j, kk: (j, kk)),
                  o_spec=pl.BlockSpec((tm, 1024), lambda i, j, kk: (i, j)), o_tile=(tm, 1024),
                  out_sds=_sds((t, D), F32), add=dhn, name=f"d_hn1s{l}")
        dh, dhb, dn = _rms_bwd(s["h0"], norm_mix[l:l + 1], dhn, dh, name=f"d_rms_mix{l}")
        small["norm_mix"].append(dn)

        grads = dict(w_in=_join_dw_in(dw_z, dw_xbc, dw_dt, dw_s), w_out=dw_out.reshape(4, -1, D),
                     w_q=dw_q.reshape(4, -1, D_XA), w_k=dw_k.reshape(4, -1, D_XA), w_v=dw_v.reshape(4, -1, D_XA),
                     w_o=dw_o, w_gate=dw_gate, w_up=dw_up, w_down=dw_down.reshape(4, -1, D))
        if l == 0:
            if pending is not None:
                finish_reduce(pending, dh)
            pending = start_reduce(0, "0c", first_groups[2], grads)
            finish_reduce(early[0], pending[-1])
            finish_reduce(early[1], carried["w_down"][0])
        else:
            pending = start_reduce(l, str(l), names, grads, behind=pending)

    finish_reduce(pending, carried["w_out"][0])
    grad_x = dh[None]

    _, _, d_mem_norm = _rms_bwd(mem[0], mem_norm.reshape(1, D), dmemn, jnp.zeros_like(dmemn), name="d_mem_norm")
    stack = lambda n: jnp.concatenate(small[n][::-1], axis=0) if small[n][0].ndim == 2 and small[n][0].shape[0] == 1 \
        else jnp.stack(small[n][::-1], axis=0)
    small_names = ["norm_mix", "ssd_conv_w", "ssd_conv_b", "dt_bias", "a_log", "d_skip", "ssd_norm", "sc_conv_w", "sc_norm",
                   "mem_norm", "norm_xa", "norm_ffn", "norm_final"]
    local_g = dict(mem_norm=d_mem_norm.reshape(D), norm_final=d_norm_final.reshape(D))
    for n in small:
        local_g[n] = stack(n)
    shapes = [local_g[n].shape for n in small_names]
    red = dict(zip(small_names, _unpack(_allreduce_small(_pack([local_g[n] for n in small_names]), name="allreduce_small"), shapes)))
    red["ssd_conv_w"] = lax.dynamic_slice(red["ssd_conv_w"], (0, 0, qme * (D_XBC // 4)), ssd_conv_w.shape)
    red["sc_conv_w"] = lax.dynamic_slice(red["sc_conv_w"], (0, 0, qme * (D_SC // 4)), sc_conv_w.shape)
    sw = dict(norm_mix=norm_mix, ssd_conv_w=ssd_conv_w, ssd_conv_b=ssd_conv_b, dt_bias=dt_bias, a_log=a_log, d_skip=d_skip,
              ssd_norm=ssd_norm, sc_conv_w=sc_conv_w, sc_norm=sc_norm, mem_norm=mem_norm, norm_xa=norm_xa, norm_ffn=norm_ffn,
              norm_final=norm_final)
    sm = dict(norm_mix=m_norm_mix, ssd_conv_w=m_ssd_conv_w, ssd_conv_b=m_ssd_conv_b, dt_bias=m_dt_bias, a_log=m_a_log,
              d_skip=m_d_skip, ssd_norm=m_ssd_norm, sc_conv_w=m_sc_conv_w, sc_norm=m_sc_norm, mem_norm=m_mem_norm,
              norm_xa=m_norm_xa, norm_ffn=m_norm_ffn, norm_final=m_norm_final)
    sv = dict(norm_mix=v_norm_mix, ssd_conv_w=v_ssd_conv_w, ssd_conv_b=v_ssd_conv_b, dt_bias=v_dt_bias, a_log=v_a_log,
              d_skip=v_d_skip, ssd_norm=v_ssd_norm, sc_conv_w=v_sc_conv_w, sc_norm=v_sc_norm, mem_norm=v_mem_norm,
              norm_xa=v_norm_xa, norm_ffn=v_norm_ffn, norm_final=v_norm_final)
    shard_shapes = [sw[n].shape for n in small_names]
    pk = lambda d: _pack([d[n] for n in small_names])
    sd, snm, snv = _adamw_flat(pk(sw), pk(red), pk(sm), pk(sv), name="adamw_small")
    s_delta = dict(zip(small_names, _unpack(sd, shard_shapes)))
    s_newm = dict(zip(small_names, _unpack(snm, shard_shapes)))
    s_newv = dict(zip(small_names, _unpack(snv, shard_shapes)))

    order = ["norm_mix", "w_in", "ssd_conv_w", "ssd_conv_b", "dt_bias", "a_log", "d_skip", "ssd_norm", "sc_conv_w", "sc_norm",
             "w_out", "mem_norm", "norm_xa", "w_q", "w_k", "w_v", "w_o", "norm_ffn", "w_gate", "w_up", "w_down", "norm_final"]

    def pick(kind):
        out = []
        for n in order:
            if n in carried:
                out.append(carried[n][kind])
            else:
                out.append([red, s_delta, s_newm, s_newv][kind][n])
        return out

    return (loss, grad_x, *pick(0), *pick(1), *pick(2), *pick(3))
```

```python
import functools

import jax
import jax.numpy as jnp
from jax import lax
from jax.experimental import pallas as pl
from jax.experimental.pallas import tpu as pltpu

F32 = jnp.float32
BF16 = jnp.bfloat16
MESH = pl.DeviceIdType.MESH

D = 2048
D_SSD = 2048
N_GROUPS = 4
GROUP_W = 768
D_XBC = 3072
N_STATE = 128
HEADS = 32
PAIRS_PER_GROUP = 4
CHUNK = 256
DT_PAD = 128
D_SC = 2048
SC_GROUP = 128
XA_HEADS = 4
XA_HD = 128
D_XA = 512
D_FF = 5632
EPS = 1e-5
HALO = 16
TN_TK = 2048
VMEM_LIMIT = 56 * 1024 * 1024

ADAM_LR, ADAM_B1, ADAM_B2, ADAM_EPS, ADAM_WD, ADAM_STEP = 0.001, 0.9, 0.999, 1e-08, 0.01, 10

NT = (((1,), (1,)), ((), ()))
TN = (((0,), (0,)), ((), ()))
NN = (((1,), (0,)), ((), ()))


def _pcall(body, **kw):
    return pl.pallas_call(body, **kw)


def _params(**kw):
    return pltpu.CompilerParams(vmem_limit_bytes=VMEM_LIMIT, **kw)


def _sds(shape, dtype):
    return jax.ShapeDtypeStruct(shape, dtype)


def _sig(x):
    return 0.5 * jnp.tanh(0.5 * x) + 0.5


def _dot(a, b, dims=NN):
    return lax.dot_general(a, b, dims, preferred_element_type=F32)


def _mm(a, b, *, mode, grid, a_spec, b_spec, o_spec, o_tile, out_sds, add=None, dep=None, name):
    gk = grid[2]
    dims = {"nn": NN, "nt": NT, "tn": TN, "nt4": NT}[mode]
    has_add = add is not None
    n_dep = 0 if dep is None else 1

    def body(*refs):
        a_ref, b_ref = refs[0], refs[1]
        add_ref = refs[2] if has_add else None
        refs = refs[:2 + has_add] + refs[2 + has_add + n_dep:]
        o_ref = refs[2 + has_add]
        if mode == "nt4":
            bv = jnp.concatenate([b_ref[s] for s in range(4)], axis=1)
        else:
            bv = b_ref[...].astype(BF16)
        p = _dot(a_ref[...].astype(BF16), bv, dims)

        def finish(acc):
            if has_add:
                acc = acc + add_ref[...]
            o_ref[...] = acc.astype(o_ref.dtype)

        if gk == 1:
            finish(p)
        else:
            acc_ref = refs[3 + has_add]
            k = pl.program_id(2)

            @pl.when(k == 0)
            def _():
                acc_ref[...] = p

            @pl.when(k > 0)
            def _():
                acc_ref[...] += p

            @pl.when(k == gk - 1)
            def _():
                finish(acc_ref[...])

    in_specs = [a_spec, b_spec] + ([o_spec] if has_add else []) + [pl.BlockSpec(memory_space=pl.ANY)] * n_dep
    args = (a, b) + ((add,) if has_add else ()) + ((dep,) if n_dep else ())
    scratch = [pltpu.VMEM(o_tile, F32)] if gk > 1 else []
    return _pcall(body, grid=grid, in_specs=in_specs, out_specs=o_spec, out_shape=out_sds, scratch_shapes=scratch,
                  compiler_params=_params(dimension_semantics=("parallel", "parallel", "arbitrary")), name=name)(*args)


def _tile(n, pref):
    t = min(n, pref)
    assert n % t == 0, (n, pref)
    return t


def _mm_nn(a, w, *, tn, tk=None, tm=1024, out_dtype=BF16, add=None, dep=None, name):
    m, k = a.shape
    n = w.shape[1]
    tm = _tile(m, tm)
    tk = k if tk is None else tk
    grid = (m // tm, n // tn, k // tk)
    return _mm(a, w, mode="nn", grid=grid,
               a_spec=pl.BlockSpec((tm, tk), lambda i, j, kk: (i, kk)),
               b_spec=pl.BlockSpec((tk, tn), lambda i, j, kk: (kk, j)),
               o_spec=pl.BlockSpec((tm, tn), lambda i, j, kk: (i, j)), o_tile=(tm, tn),
               out_sds=_sds((m, n), out_dtype), add=add, dep=dep, name=name)


def _mm_nn_sm(a, w4, *, out_dtype=BF16, add=None, name):
    m, k = a.shape
    n = w4.shape[2]
    tm = _tile(m, 1024)
    return _mm(a, w4, mode="nn", grid=(m // tm, 4, 1),
               a_spec=pl.BlockSpec((tm, k), lambda i, j, kk: (i, 0)),
               b_spec=pl.BlockSpec((None, k, n), lambda i, j, kk: (j, 0, 0)),
               o_spec=pl.BlockSpec((tm, n), lambda i, j, kk: (i, j)), o_tile=(tm, n),
               out_sds=_sds((m, 4 * n), out_dtype), add=add, name=name)


def _mm_nt(a, w, *, tn, tk=None, out_dtype=BF16, add=None, dep=None, name):
    m, k = a.shape
    n = w.shape[0]
    tm = _tile(m, 1024)
    tk = k if tk is None else tk
    grid = (m // tm, n // tn, k // tk)
    return _mm(a, w, mode="nt", grid=grid,
               a_spec=pl.BlockSpec((tm, tk), lambda i, j, kk: (i, kk)),
               b_spec=pl.BlockSpec((tn, tk), lambda i, j, kk: (j, kk)),
               o_spec=pl.BlockSpec((tm, tn), lambda i, j, kk: (i, j)), o_tile=(tm, tn),
               out_sds=_sds((m, n), out_dtype), add=add, dep=dep, name=name)


def _mm_nt_sm(a, w4, *, tn, out_dtype=BF16, add=None, dep=None, name):
    m = a.shape[0]
    _, k, n = w4.shape
    tm = _tile(m, 512)
    tn = _tile(k, tn)
    return _mm(a, w4, mode="nt4", grid=(m // tm, k // tn, 1),
               a_spec=pl.BlockSpec((tm, 4 * n), lambda i, j, kk: (i, 0)),
               b_spec=pl.BlockSpec((4, tn, n), lambda i, j, kk: (0, j, 0)),
               o_spec=pl.BlockSpec((tm, tn), lambda i, j, kk: (i, j)), o_tile=(tm, tn),
               out_sds=_sds((m, k), out_dtype), add=add, dep=dep, name=name)


def _mm_tn(a, g, *, tm, tn, out_dtype=BF16, name):
    t, m = a.shape
    n = g.shape[1]
    tk = _tile(t, TN_TK)
    return _mm(a, g, mode="tn", grid=(m // tm, n // tn, t // tk),
               a_spec=pl.BlockSpec((tk, tm), lambda i, j, kk: (kk, i)),
               b_spec=pl.BlockSpec((tk, tn), lambda i, j, kk: (kk, j)),
               o_spec=pl.BlockSpec((tm, tn), lambda i, j, kk: (i, j)), o_tile=(tm, tn),
               out_sds=_sds((m, n), out_dtype), name=name)


def _mm_tn_sm(a, g, *, tm, out_dtype=BF16, name):
    t, m = a.shape
    n = g.shape[1] // 4
    tk = _tile(t, TN_TK)
    return _mm(a, g, mode="tn", grid=(m // tm, 4, t // tk),
               a_spec=pl.BlockSpec((tk, tm), lambda i, j, kk: (kk, i)),
               b_spec=pl.BlockSpec((tk, n), lambda i, j, kk: (kk, j)),
               o_spec=pl.BlockSpec((None, tm, n), lambda i, j, kk: (j, i, 0)), o_tile=(tm, n),
               out_sds=_sds((4, m, n), out_dtype), name=name)


def _rms_fwd(h, g, *, name):
    t, d = h.shape
    tr = _tile(t, 512)

    def body(h_ref, g_ref, o_ref):
        x = h_ref[...]
        r = lax.rsqrt(jnp.mean(x * x, axis=-1, keepdims=True) + EPS)
        o_ref[...] = (x * r * g_ref[...]).astype(o_ref.dtype)

    return _pcall(body, grid=(t // tr,),
                  in_specs=[pl.BlockSpec((tr, d), lambda i: (i, 0)), pl.BlockSpec((1, d), lambda i: (0, 0))],
                  out_specs=pl.BlockSpec((tr, d), lambda i: (i, 0)), out_shape=_sds((t, d), BF16),
                  compiler_params=_params(dimension_semantics=("parallel",)), name=name)(h, g)


def _rms_bwd(h, g, dy, dres, *, name):
    t, d = h.shape
    tr = _tile(t, 256)

    def body(h_ref, g_ref, dy_ref, dres_ref, dh_ref, dhb_ref, dg_ref):
        i = pl.program_id(0)
        x = h_ref[...]
        r = lax.rsqrt(jnp.mean(x * x, axis=-1, keepdims=True) + EPS)
        xh = x * r
        dyv = dy_ref[...].astype(F32)
        dxh = dyv * g_ref[...]
        dh = dres_ref[...] + r * (dxh - xh * jnp.mean(dxh * xh, axis=-1, keepdims=True))
        dh_ref[...] = dh
        dhb_ref[...] = dh.astype(BF16)
        part = jnp.sum(dyv * xh, axis=0, keepdims=True)

        @pl.when(i == 0)
        def _():
            dg_ref[...] = part

        @pl.when(i > 0)
        def _():
            dg_ref[...] += part

    row = pl.BlockSpec((tr, d), lambda i: (i, 0))
    vec = pl.BlockSpec((1, d), lambda i: (0, 0))
    return _pcall(body, grid=(t // tr,), in_specs=[row, vec, row, row], out_specs=[row, row, vec],
                  out_shape=[_sds((t, d), F32), _sds((t, d), BF16), _sds((1, d), F32)],
                  compiler_params=_params(dimension_semantics=("arbitrary",)), name=name)(h, g, dy, dres)


def _final(h, g, tgt, *, name):
    t, d = h.shape
    tr = _tile(t, 256)

    def body(h_ref, g_ref, t_ref, loss_ref, dh_ref, dhb_ref, dg_ref):
        i = pl.program_id(0)
        x = h_ref[...]
        gv = g_ref[...]
        r = lax.rsqrt(jnp.mean(x * x, axis=-1, keepdims=True) + EPS)
        xh = x * r
        e = xh * gv - t_ref[...]
        lpart = jnp.zeros((1, 128), F32) + 0.5 * jnp.sum(jnp.mean(e * e, axis=-1, keepdims=True))
        dyv = e * (1.0 / d)
        dxh = dyv * gv
        dh = r * (dxh - xh * jnp.mean(dxh * xh, axis=-1, keepdims=True))
        dh_ref[...] = dh
        dhb_ref[...] = dh.astype(BF16)
        part = jnp.sum(dyv * xh, axis=0, keepdims=True)

        @pl.when(i == 0)
        def _():
            dg_ref[...] = part
            loss_ref[...] = lpart

        @pl.when(i > 0)
        def _():
            dg_ref[...] += part
            loss_ref[...] += lpart

    row = pl.BlockSpec((tr, d), lambda i: (i, 0))
    vec = pl.BlockSpec((1, d), lambda i: (0, 0))
    return _pcall(body, grid=(t // tr,), in_specs=[row, vec, row],
                  out_specs=[pl.BlockSpec((1, 128), lambda i: (0, 0)), row, row, vec],
                  out_shape=[_sds((1, 128), F32), _sds((t, d), F32), _sds((t, d), BF16), _sds((1, d), F32)],
                  compiler_params=_params(dimension_semantics=("arbitrary",)), name=name)(h, g, tgt)


def _conv_taps(ext, w, ntap, rows):
    n = ext.shape[0]
    acc = w[ntap - 1:ntap, :] * ext[HALO:HALO + rows]
    for k in range(1, ntap):
        acc = acc + w[ntap - 1 - k:ntap - k, :] * pltpu.roll(ext, k, axis=0)[HALO:HALO + rows]
    del n
    return acc


def _conv_fwd(xbc, w, b, *, name):
    t, c = xbc.shape
    rows = CHUNK
    cw = GROUP_W
    hb = rows // HALO

    def body(cur_ref, prev_ref, w_ref, b_ref, o_ref):
        i = pl.program_id(1)
        cur = cur_ref[...].astype(F32)
        prev = jnp.where(i > 0, prev_ref[...].astype(F32), 0.0)
        ext = jnp.concatenate([prev, cur], axis=0)
        pre = _conv_taps(ext, w_ref[...], 4, rows) + b_ref[...]
        o_ref[...] = (pre * _sig(pre)).astype(o_ref.dtype)

    return _pcall(body, grid=(c // cw, t // rows),
                  in_specs=[pl.BlockSpec((rows, cw), lambda j, i: (i, j)),
                            pl.BlockSpec((HALO, cw), lambda j, i: (jnp.maximum(i * hb - 1, 0), j)),
                            pl.BlockSpec((4, cw), lambda j, i: (0, j)),
                            pl.BlockSpec((1, cw), lambda j, i: (0, j))],
                  out_specs=pl.BlockSpec((rows, cw), lambda j, i: (i, j)), out_shape=_sds((t, c), BF16),
                  compiler_params=_params(dimension_semantics=("parallel", "parallel")), name=name)(xbc, xbc, w, b)


def _conv_bwd(dxc, xbc, w, b, *, name):
    t, c = xbc.shape
    rows = CHUNK
    cw = GROUP_W
    hb = rows // HALO
    nblk = t // rows
    nhalo = t // HALO

    def body(d_ref, dn_ref, cur_ref, prev_ref, next_ref, w_ref, b_ref, dx_ref, dw_ref, db_ref):
        i = pl.program_id(1)
        last = i == nblk - 1
        wv = w_ref[...]
        xe = jnp.concatenate([jnp.where(i > 0, prev_ref[...].astype(F32), 0.0), cur_ref[...].astype(F32),
                              jnp.where(last, 0.0, next_ref[...].astype(F32))], axis=0)
        n = rows + 2 * HALO
        sh = [xe] + [pltpu.roll(xe, k, axis=0) for k in range(1, 4)]
        pre = wv[3:4, :] * sh[0] + wv[2:3, :] * sh[1] + wv[1:2, :] * sh[2] + wv[0:1, :] * sh[3] + b_ref[...]
        de = jnp.concatenate([jnp.zeros((HALO, cw), F32), d_ref[...].astype(F32),
                              jnp.where(last, 0.0, dn_ref[...].astype(F32))], axis=0)
        s = _sig(pre)
        dpre = de * (s * (1.0 + pre * (1.0 - s)))
        dx = wv[3:4, :] * dpre
        for m in range(1, 4):
            dx = dx + wv[3 - m:4 - m, :] * pltpu.roll(dpre, n - m, axis=0)
        dx_ref[...] = dx[HALO:HALO + rows].astype(dx_ref.dtype)
        dcur = dpre[HALO:HALO + rows]
        dwv = jnp.concatenate([jnp.sum(dcur * sh[3 - j][HALO:HALO + rows], axis=0, keepdims=True) for j in range(4)], axis=0)
        dbv = jnp.sum(dcur, axis=0, keepdims=True)

        @pl.when(i == 0)
        def _():
            dw_ref[...] = dwv
            db_ref[...] = dbv

        @pl.when(i > 0)
        def _():
            dw_ref[...] += dwv
            db_ref[...] += dbv

    cur = pl.BlockSpec((rows, cw), lambda j, i: (i, j))
    prev = pl.BlockSpec((HALO, cw), lambda j, i: (jnp.maximum(i * hb - 1, 0), j))
    nxt = pl.BlockSpec((HALO, cw), lambda j, i: (jnp.minimum((i + 1) * hb, nhalo - 1), j))
    return _pcall(body, grid=(c // cw, nblk),
                  in_specs=[cur, nxt, cur, prev, nxt, pl.BlockSpec((4, cw), lambda j, i: (0, j)),
                            pl.BlockSpec((1, cw), lambda j, i: (0, j))],
                  out_specs=[cur, pl.BlockSpec((4, cw), lambda j, i: (0, j)), pl.BlockSpec((1, cw), lambda j, i: (0, j))],
                  out_shape=[_sds((t, c), BF16), _sds((4, c), F32), _sds((1, c), F32)],
                  compiler_params=_params(dimension_semantics=("parallel", "arbitrary")), name=name)(dxc, dxc, xbc, xbc, xbc, w, b)


def _neg_exp_alog(alog):
    lane = lax.broadcasted_iota(jnp.int32, alog.shape, 1)
    return jnp.where(lane < HEADS, -jnp.exp(alog), 0.0)


def _dt_prep(dtr, bias, alog, *, name):
    t = dtr.shape[0]
    rows = CHUNK

    def body(r_ref, b_ref, a_ref, dt_ref, dtg_ref, acsg_ref, acst_ref):
        raw = r_ref[...] + b_ref[...]
        dt = jnp.maximum(raw, 0.0) + jnp.log(1.0 + jnp.exp(-jnp.abs(raw)))
        a = _neg_exp_alog(a_ref[...])
        adt = dt * a
        ri = lax.broadcasted_iota(jnp.int32, (rows, rows), 0)
        ci = lax.broadcasted_iota(jnp.int32, (rows, rows), 1)
        tri = (ri >= ci).astype(F32)
        acs = jnp.dot(tri, adt, precision=lax.Precision.HIGHEST, preferred_element_type=F32)
        dt_ref[...] = dt
        acst_ref[...] = acs.T
        for g in range(N_GROUPS):
            sh = (128 - 8 * g) % 128
            dtg_ref[g] = dt if sh == 0 else pltpu.roll(dt, sh, axis=1)
            acsg_ref[g] = acs if sh == 0 else pltpu.roll(acs, sh, axis=1)

    row = pl.BlockSpec((rows, DT_PAD), lambda i: (i, 0))
    vec = pl.BlockSpec((1, DT_PAD), lambda i: (0, 0))
    grp = pl.BlockSpec((N_GROUPS, rows, DT_PAD), lambda i: (0, i, 0))
    return _pcall(body, grid=(t // rows,), in_specs=[row, vec, vec],
                  out_specs=[row, grp, grp, pl.BlockSpec((DT_PAD, rows), lambda i: (0, i))],
                  out_shape=[_sds((t, DT_PAD), F32), _sds((N_GROUPS, t, DT_PAD), F32), _sds((N_GROUPS, t, DT_PAD), F32),
                             _sds((DT_PAD, t), F32)],
                  compiler_params=_params(dimension_semantics=("parallel",)), name=name)(dtr, bias, alog)


def _dt_bwd(ddtg, dacg, dt, dtr, bias, alog, *, name):
    t = dtr.shape[0]
    rows = CHUNK

    def body(ddtg_ref, dacg_ref, dt_ref, r_ref, b_ref, a_ref, dr_ref, db_ref, da_ref):
        i = pl.program_id(0)
        lane = lax.broadcasted_iota(jnp.int32, (rows, DT_PAD), 1)
        ddt = jnp.zeros((rows, DT_PAD), F32)
        dacs = jnp.zeros((rows, DT_PAD), F32)
        for g in range(N_GROUPS):
            sel = (lane >= 8 * g) & (lane < 8 * g + 8)
            dd = ddtg_ref[g]
            da = dacg_ref[g]
            if g:
                dd = pltpu.roll(dd, 8 * g, axis=1)
                da = pltpu.roll(da, 8 * g, axis=1)
            ddt = ddt + jnp.where(sel, dd, 0.0)
            dacs = dacs + jnp.where(sel, da, 0.0)
        ri = lax.broadcasted_iota(jnp.int32, (rows, rows), 0)
        ci = lax.broadcasted_iota(jnp.int32, (rows, rows), 1)
        triu = (ci >= ri).astype(F32)
        rev = jnp.dot(triu, dacs, precision=lax.Precision.HIGHEST, preferred_element_type=F32)
        a = _neg_exp_alog(a_ref[...])
        dtv = dt_ref[...]
        raw = r_ref[...] + b_ref[...]
        draw = (ddt + a * rev) * (1.0 / (1.0 + jnp.exp(-raw)))
        dr_ref[...] = draw
        dbv = jnp.sum(draw, axis=0, keepdims=True)
        dav = jnp.sum(dtv * rev, axis=0, keepdims=True) * a

        @pl.when(i == 0)
        def _():
            db_ref[...] = dbv
            da_ref[...] = dav

        @pl.when(i > 0)
        def _():
            db_ref[...] += dbv
            da_ref[...] += dav

    row = pl.BlockSpec((rows, DT_PAD), lambda i: (i, 0))
    vec = pl.BlockSpec((1, DT_PAD), lambda i: (0, 0))
    grp = pl.BlockSpec((N_GROUPS, rows, DT_PAD), lambda i: (0, i, 0))
    return _pcall(body, grid=(t // rows,),
                  in_specs=[grp, grp, row, row, vec, vec],
                  out_specs=[row, vec, vec], out_shape=[_sds((t, DT_PAD), F32), _sds((1, DT_PAD), F32), _sds((1, DT_PAD), F32)],
                  compiler_params=_params(dimension_semantics=("arbitrary",)), name=name)(ddtg, dacg, dt, dtr, bias, alog)


def _pair_cols(col_ref_val, p, lo):
    return jnp.where(lo, col_ref_val[:, 2 * p:2 * p + 1], col_ref_val[:, 2 * p + 1:2 * p + 2])


def _ssd_fwd(xc, dtg, acsg, acst, z, dskip, nw, *, name):
    t = xc.shape[0]
    L = CHUNK
    nc = t // L

    def body(xc_ref, dtg_ref, acsg_ref, acst_ref, z_ref, dsk_ref, nw_ref, y_ref, st_ref, mix_ref, s_ref):
        c = pl.program_id(1)

        @pl.when(c == 0)
        def _():
            s_ref[...] = jnp.zeros_like(s_ref)

        blk = xc_ref[...]
        bm = blk[:, 512:640]
        cm = blk[:, 640:768]
        cb = _dot(cm, bm, NT)
        dtv = dtg_ref[...]
        acs = acsg_ref[...]
        acst_v = acst_ref[...]
        ri = lax.broadcasted_iota(jnp.int32, (L, L), 0)
        ci = lax.broadcasted_iota(jnp.int32, (L, L), 1)
        causal = ri >= ci
        lo = lax.broadcasted_iota(jnp.int32, (1, 128), 1) < 64
        lo_rows = lax.broadcasted_iota(jnp.int32, (128, 1), 0) < 64
        dskv = dsk_ref[...]
        ys = []
        for p in range(PAIRS_PER_GROUP):
            xp = blk[:, 128 * p:128 * p + 128].astype(F32)
            dt_p = _pair_cols(dtv, p, lo)
            a_p = _pair_cols(acs, p, lo)
            alast = acs[L - 1:L, :]
            al_p = _pair_cols(alast, p, lo)
            xdt = xp * dt_p
            xdt_b = xdt.astype(BF16)
            yd = []
            for hh in range(2):
                j = 2 * p + hh
                seg = acs[:, j:j + 1] - acst_v[j:j + 1, :]
                lam = jnp.exp(jnp.where(causal, seg, -1e30))
                w = (cb * lam).astype(BF16)
                yd.append(_dot(w, xdt_b))
            y = jnp.where(lo, yd[0], yd[1])
            sp = s_ref[p]
            st_ref[p] = sp
            y = y + _dot(cm, sp.astype(BF16), NT) * jnp.exp(a_p)
            dsc = jnp.exp(al_p - a_p)
            snew = _dot((xdt * dsc).astype(BF16), bm, TN)
            al_rows = jnp.where(lo_rows, alast[:, 2 * p:2 * p + 1], alast[:, 2 * p + 1:2 * p + 2])
            s_ref[p] = sp * jnp.exp(al_rows) + snew
            ys.append(y + xp * dskv[:, 128 * p:128 * p + 128])
        yfull = jnp.concatenate(ys, axis=1)
        y_ref[...] = yfull.astype(y_ref.dtype)
        zz = z_ref[...].astype(F32)
        yg = yfull * (zz * _sig(zz))
        r = lax.rsqrt(jnp.mean(yg * yg, axis=-1, keepdims=True) + EPS)
        mix_ref[...] = (yg * r * nw_ref[...]).astype(mix_ref.dtype)

    grp = pl.BlockSpec((None, L, DT_PAD), lambda g, c: (g, c, 0))
    return _pcall(body, grid=(N_GROUPS, nc),
                  in_specs=[pl.BlockSpec((L, GROUP_W), lambda g, c: (c, g)), grp, grp,
                            pl.BlockSpec((8, L), lambda g, c: (g, c)),
                            pl.BlockSpec((L, 512), lambda g, c: (c, g)),
                            pl.BlockSpec((1, 512), lambda g, c: (0, g)), pl.BlockSpec((1, 512), lambda g, c: (0, g))],
                  out_specs=[pl.BlockSpec((L, 512), lambda g, c: (c, g)),
                             pl.BlockSpec((None, PAIRS_PER_GROUP, 128, N_STATE), lambda g, c: (c, g, 0, 0)),
                             pl.BlockSpec((L, 512), lambda g, c: (c, g))],
                  out_shape=[_sds((t, D_SSD), BF16), _sds((nc, N_GROUPS * PAIRS_PER_GROUP, 128, N_STATE), F32),
                             _sds((t, D_SSD + D_SC), BF16)],
                  scratch_shapes=[pltpu.VMEM((PAIRS_PER_GROUP, 128, N_STATE), F32)],
                  compiler_params=_params(dimension_semantics=("parallel", "arbitrary")), name=name)(
                      xc, dtg, acsg, acst, z, dskip, nw)


def _ssd_bwd(xc, dtg, acsg, acst, z, y, states, dmix, dskip, nw, *, name):
    t = xc.shape[0]
    L = CHUNK
    nc = t // L

    def body(xc_ref, dtg_ref, acsg_ref, acst_ref, z_ref, y_ref, st_ref, dm_ref, dsk_ref, nw_ref,
             dxc_ref, dz_ref, ddt_ref, dac_ref, ddsk_ref, dnw_ref, ds_ref):
        c = pl.program_id(1)

        @pl.when(c == 0)
        def _():
            ds_ref[...] = jnp.zeros_like(ds_ref)
            ddsk_ref[...] = jnp.zeros_like(ddsk_ref)
            dnw_ref[...] = jnp.zeros_like(dnw_ref)

        blk = xc_ref[...]
        xs = blk[:, :512].astype(F32)
        bm = blk[:, 512:640]
        cm = blk[:, 640:768]
        bmf = bm.astype(F32)
        yv = y_ref[...].astype(F32)
        zz = z_ref[...].astype(F32)
        nwv = nw_ref[...]
        dout = dm_ref[...].astype(F32)
        sz = _sig(zz)
        silu = zz * sz
        yg = yv * silu
        r = lax.rsqrt(jnp.mean(yg * yg, axis=-1, keepdims=True) + EPS)
        xh = yg * r
        dnw_ref[...] += jnp.sum(dout * xh, axis=0, keepdims=True)
        dyn = dout * nwv
        dyg = r * (dyn - xh * jnp.mean(dyn * xh, axis=-1, keepdims=True))
        dy = dyg * silu
        dz_ref[...] = (dyg * yv * (sz * (1.0 + zz * (1.0 - sz)))).astype(dz_ref.dtype)
        ddsk_ref[...] += jnp.sum(dy * xs, axis=0, keepdims=True)

        cb = _dot(cm, bm, NT)
        dtv = dtg_ref[...]
        acs = acsg_ref[...]
        acst_v = acst_ref[...]
        alast = acs[L - 1:L, :]
        ri = lax.broadcasted_iota(jnp.int32, (L, L), 0)
        ci = lax.broadcasted_iota(jnp.int32, (L, L), 1)
        causal = ri >= ci
        lane = lax.broadcasted_iota(jnp.int32, (1, 128), 1)
        lo = lane < 64
        lo_rows = lax.broadcasted_iota(jnp.int32, (128, 1), 0) < 64
        lane_l = lax.broadcasted_iota(jnp.int32, (L, DT_PAD), 1)
        row_l = lax.broadcasted_iota(jnp.int32, (L, 1), 0)
        dskv = dsk_ref[...]
        dm_acc = jnp.zeros((L, L), F32)
        db_acc = jnp.zeros((L, N_STATE), F32)
        dc_acc = jnp.zeros((L, N_STATE), F32)
        ddt_out = jnp.zeros((L, DT_PAD), F32)
        dac_out = jnp.zeros((L, DT_PAD), F32)
        dxs = []
        for p in range(PAIRS_PER_GROUP):
            xp = xs[:, 128 * p:128 * p + 128]
            dyp = dy[:, 128 * p:128 * p + 128]
            dyp_b = dyp.astype(BF16)
            dyp_r = dyp_b.astype(F32)
            dt_p = _pair_cols(dtv, p, lo)
            a_p = _pair_cols(acs, p, lo)
            al_p = _pair_cols(alast, p, lo)
            xdt = xp * dt_p
            xdt_b = xdt.astype(BF16)
            xdt_r = xdt_b.astype(F32)
            ea_p = jnp.exp(a_p)
            dsc_p = jnp.exp(al_p - a_p)
            sp = st_ref[p]
            sp_b = sp.astype(BF16)
            dsp = ds_ref[p]
            dsp_b = dsp.astype(BF16)
            cs = _dot(cm, sp_b, NT)
            dye_b = (dyp * ea_p).astype(BF16)
            dc_acc = dc_acc + _dot(dye_b, sp_b)
            ds_prev = _dot(dye_b, cm, TN)
            bds = _dot(bm, dsp_b, NT)
            al_rows = jnp.where(lo_rows, alast[:, 2 * p:2 * p + 1], alast[:, 2 * p + 1:2 * p + 2])
            ds_prev = ds_prev + jnp.exp(al_rows) * dsp
            prod_st = dsp * sp
            dxdt_h = []
            for hh in range(2):
                j = 2 * p + hh
                hm = lo if hh == 0 else jnp.logical_not(lo)
                hm_rows = lo_rows if hh == 0 else jnp.logical_not(lo_rows)
                a_col = acs[:, j:j + 1]
                seg = a_col - acst_v[j:j + 1, :]
                lam = jnp.exp(jnp.where(causal, seg, -1e30))
                w = (cb * lam).astype(BF16)
                dy_h = jnp.where(hm, dyp, 0.0).astype(BF16)
                dw = _dot(dy_h, xdt_b, NT)
                dxd = _dot(w, dyp_b, TN)
                dxdt_h.append(dxd)
                dm_acc = dm_acc + dw * lam
                diag = dyp_r * _dot(w, xdt_b) - xdt_r * dxd
                dac = jnp.sum(jnp.where(hm, diag + dyp * cs * jnp.exp(a_col), 0.0), axis=1, keepdims=True)
                al_h = alast[:, j:j + 1]
                dal = jnp.exp(al_h) * jnp.sum(jnp.sum(jnp.where(hm_rows, prod_st, 0.0), axis=1, keepdims=True), axis=0, keepdims=True)
                xds_h = _dot(jnp.where(hm, xdt, 0.0).astype(BF16), dsp_b)
                dsc_col = jnp.exp(al_h - a_col)
                db_acc = db_acc + dsc_col * xds_h
                tt = jnp.sum(xds_h * bmf, axis=1, keepdims=True) * dsc_col
                dal = dal + jnp.sum(tt, axis=0, keepdims=True)
                dac = dac - tt + jnp.where(row_l == L - 1, dal, 0.0)
                dac_out = jnp.where(lane_l == j, dac, dac_out)
            dxdt = jnp.where(lo, dxdt_h[0], dxdt_h[1]) + dsc_p * bds
            dxs.append(dxdt * dt_p + dyp * dskv[:, 128 * p:128 * p + 128])
            prod_dt = dxdt * xp
            for hh in range(2):
                j = 2 * p + hh
                hm = lo if hh == 0 else jnp.logical_not(lo)
                ddt_col = jnp.sum(jnp.where(hm, prod_dt, 0.0), axis=1, keepdims=True)
                ddt_out = jnp.where(lane_l == j, ddt_col, ddt_out)
            ds_ref[p] = ds_prev
        dm_b = dm_acc.astype(BF16)
        dc_acc = dc_acc + _dot(dm_b, bm)
        db_acc = db_acc + _dot(dm_b, cm, TN)
        dxc_ref[...] = jnp.concatenate(dxs + [db_acc, dc_acc], axis=1).astype(dxc_ref.dtype)
        ddt_ref[...] = ddt_out
        dac_ref[...] = dac_out

    rc = lambda g, c: (nc - 1 - c, g)
    grp = pl.BlockSpec((None, L, DT_PAD), lambda g, c: (g, nc - 1 - c, 0))
    vec = pl.BlockSpec((1, 512), lambda g, c: (0, g))
    return _pcall(body, grid=(N_GROUPS, nc),
                  in_specs=[pl.BlockSpec((L, GROUP_W), rc), grp, grp,
                            pl.BlockSpec((8, L), lambda g, c: (g, nc - 1 - c)),
                            pl.BlockSpec((L, 512), rc), pl.BlockSpec((L, 512), rc),
                            pl.BlockSpec((None, PAIRS_PER_GROUP, 128, N_STATE), lambda g, c: (nc - 1 - c, g, 0, 0)),
                            pl.BlockSpec((L, 512), rc), vec, vec],
                  out_specs=[pl.BlockSpec((L, GROUP_W), rc), pl.BlockSpec((L, 512), rc), grp, grp, vec, vec],
                  out_shape=[_sds((t, D_XBC), BF16), _sds((t, D_SSD), BF16), _sds((N_GROUPS, t, DT_PAD), F32),
                             _sds((N_GROUPS, t, DT_PAD), F32), _sds((1, D_SSD), F32), _sds((1, D_SSD), F32)],
                  scratch_shapes=[pltpu.VMEM((PAIRS_PER_GROUP, 128, N_STATE), F32)],
                  compiler_params=_params(dimension_semantics=("parallel", "arbitrary")), name=name)(
                      xc, dtg, acsg, acst, z, y, states, dmix, dskip, nw)


SC_CW = 1024


def _group_rstd(v):
    outs = []
    for q in range(v.shape[1] // SC_GROUP):
        vq = v[:, SC_GROUP * q:SC_GROUP * (q + 1)]
        outs.append(jnp.broadcast_to(lax.rsqrt(jnp.mean(vq * vq, axis=-1, keepdims=True) + EPS), vq.shape))
    return jnp.concatenate(outs, axis=1)


def _group_mean(v):
    outs = []
    for q in range(v.shape[1] // SC_GROUP):
        vq = v[:, SC_GROUP * q:SC_GROUP * (q + 1)]
        outs.append(jnp.broadcast_to(jnp.mean(vq, axis=-1, keepdims=True), vq.shape))
    return jnp.concatenate(outs, axis=1)


def _sc_fwd(ps, w, nw, mix, *, name):
    t = ps.shape[1]
    rows = CHUNK
    hb = rows // HALO
    cw = SC_CW
    off = D_SSD // cw

    def body(cur_ref, prev_ref, w_ref, nw_ref, mix_in_ref, o_ref):
        del mix_in_ref
        i = pl.program_id(1)
        u = cur_ref[0].astype(F32)
        gb = cur_ref[1].astype(F32)
        gc = cur_ref[2].astype(F32)
        cu_prev = jnp.where(i > 0, prev_ref[2].astype(F32) * prev_ref[0].astype(F32), 0.0)
        ext = jnp.concatenate([cu_prev, gc * u], axis=0)
        v = gb * _conv_taps(ext, w_ref[...], 3, rows)
        o_ref[...] = (v * _group_rstd(v) * nw_ref[...]).astype(o_ref.dtype)

    return _pcall(body, grid=(D_SC // cw, t // rows),
                  in_specs=[pl.BlockSpec((3, rows, cw), lambda j, i: (0, i, j)),
                            pl.BlockSpec((3, HALO, cw), lambda j, i: (0, jnp.maximum(i * hb - 1, 0), j)),
                            pl.BlockSpec((3, cw), lambda j, i: (0, j)), pl.BlockSpec((1, cw), lambda j, i: (0, j)),
                            pl.BlockSpec(memory_space=pl.ANY)],
                  out_specs=pl.BlockSpec((rows, cw), lambda j, i: (i, off + j)),
                  out_shape=_sds(mix.shape, mix.dtype), input_output_aliases={4: 0},
                  compiler_params=_params(dimension_semantics=("parallel", "parallel")), name=name)(ps, ps, w, nw, mix)


def _sc_bwd(ps, w, nw, dmix, *, name):
    t = ps.shape[1]
    rows = CHUNK
    hb = rows // HALO
    cw = SC_CW
    off = D_SSD // cw
    nblk = t // rows
    nhalo = t // HALO
    n = rows + 2 * HALO

    def body(cur_ref, prev_ref, next_ref, w_ref, nw_ref, d_ref, dn_ref, dps_ref, dw_ref, dnw_ref):
        i = pl.program_id(1)
        first = i == 0
        last = i == nblk - 1

        def ext(k):
            return jnp.concatenate([jnp.where(first, 0.0, prev_ref[k].astype(F32)), cur_ref[k].astype(F32),
                                    jnp.where(last, 0.0, next_ref[k].astype(F32))], axis=0)

        ue, gbe, gce = ext(0), ext(1), ext(2)
        wv = w_ref[...]
        nwv = nw_ref[...]
        cue = gce * ue
        cu1 = pltpu.roll(cue, 1, axis=0)
        cu2 = pltpu.roll(cue, 2, axis=0)
        conv = wv[2:3, :] * cue + wv[1:2, :] * cu1 + wv[0:1, :] * cu2
        ve = gbe * conv
        doe = jnp.concatenate([jnp.zeros((HALO, cw), F32), d_ref[...].astype(F32),
                               jnp.where(last, 0.0, dn_ref[...].astype(F32))], axis=0)
        r = _group_rstd(ve)
        xh = ve * r
        dvn = doe * nwv
        dv = r * (dvn - xh * _group_mean(dvn * xh))
        dconv = dv * gbe
        dcu = wv[2:3, :] * dconv + wv[1:2, :] * pltpu.roll(dconv, n - 1, axis=0) + wv[0:1, :] * pltpu.roll(dconv, n - 2, axis=0)
        sl = slice(HALO, HALO + rows)
        dps_ref[0] = (dcu * gce)[sl].astype(dps_ref.dtype)
        dps_ref[1] = (dv * conv)[sl].astype(dps_ref.dtype)
        dps_ref[2] = (dcu * ue)[sl].astype(dps_ref.dtype)
        dc = dconv[sl]
        dwv = jnp.concatenate([jnp.sum(dc * cu2[sl], axis=0, keepdims=True), jnp.sum(dc * cu1[sl], axis=0, keepdims=True),
                               jnp.sum(dc * cue[sl], axis=0, keepdims=True)], axis=0)
        dnv = jnp.sum((doe * xh)[sl], axis=0, keepdims=True)

        @pl.when(first)
        def _():
            dw_ref[...] = dwv
            dnw_ref[...] = dnv

        @pl.when(i > 0)
        def _():
            dw_ref[...] += dwv
            dnw_ref[...] += dnv

    cur = pl.BlockSpec((3, rows, cw), lambda j, i: (0, i, j))
    prev = pl.BlockSpec((3, HALO, cw), lambda j, i: (0, jnp.maximum(i * hb - 1, 0), j))
    nxt = pl.BlockSpec((3, HALO, cw), lambda j, i: (0, jnp.minimum((i + 1) * hb, nhalo - 1), j))
    return _pcall(body, grid=(D_SC // cw, nblk),
                  in_specs=[cur, prev, nxt, pl.BlockSpec((3, cw), lambda j, i: (0, j)), pl.BlockSpec((1, cw), lambda j, i: (0, j)),
                            pl.BlockSpec((rows, cw), lambda j, i: (i, off + j)),
                            pl.BlockSpec((HALO, cw), lambda j, i: (jnp.minimum((i + 1) * hb, nhalo - 1), off + j))],
                  out_specs=[cur, pl.BlockSpec((3, cw), lambda j, i: (0, j)), pl.BlockSpec((1, cw), lambda j, i: (0, j))],
                  out_shape=[_sds(ps.shape, BF16), _sds((3, D_SC), F32), _sds((1, D_SC), F32)],
                  compiler_params=_params(dimension_semantics=("parallel", "arbitrary")), name=name)(ps, ps, ps, w, nw, dmix, dmix)


XA_SCALE = XA_HD ** -0.5


def _softmax(s):
    m = jnp.max(s, axis=-1, keepdims=True)
    e = jnp.exp(s - m)
    return e * (1.0 / jnp.sum(e, axis=-1, keepdims=True))


def _xa_fwd(q, k, v, *, name):
    t = q.shape[0]
    nm = k.shape[0]
    tq = _tile(t, 512)

    def body(q_ref, k_ref, v_ref, o_ref):
        outs = []
        for h in range(XA_HEADS):
            sl = slice(XA_HD * h, XA_HD * (h + 1))
            s = _dot(q_ref[:, sl], k_ref[:, sl], NT) * XA_SCALE
            outs.append(_dot(_softmax(s).astype(BF16), v_ref[:, sl]))
        o_ref[...] = jnp.concatenate(outs, axis=1).astype(o_ref.dtype)

    row = pl.BlockSpec((tq, D_XA), lambda i: (i, 0))
    kv = pl.BlockSpec((nm, D_XA), lambda i: (0, 0))
    return _pcall(body, grid=(t // tq,), in_specs=[row, kv, kv], out_specs=row, out_shape=_sds((t, D_XA), BF16),
                  compiler_params=_params(dimension_semantics=("parallel",)), name=name)(q, k, v)


def _xa_bwd(q, k, v, do, *, name):
    t = q.shape[0]
    nm = k.shape[0]
    tq = _tile(t, 512)

    def body(q_ref, k_ref, v_ref, do_ref, dq_ref, dk_ref, dv_ref):
        i = pl.program_id(0)
        dqs, dks, dvs = [], [], []
        for h in range(XA_HEADS):
            sl = slice(XA_HD * h, XA_HD * (h + 1))
            qh, kh, vh, doh = q_ref[:, sl], k_ref[:, sl], v_ref[:, sl], do_ref[:, sl]
            p = _softmax(_dot(qh, kh, NT) * XA_SCALE)
            dvs.append(_dot(p.astype(BF16), doh, TN))
            dp = _dot(doh, vh, NT)
            ds = (p * (dp - jnp.sum(dp * p, axis=-1, keepdims=True)) * XA_SCALE).astype(BF16)
            dqs.append(_dot(ds, kh))
            dks.append(_dot(ds, qh, TN))
        dq_ref[...] = jnp.concatenate(dqs, axis=1).astype(dq_ref.dtype)
        dkv = jnp.concatenate(dks, axis=1)
        dvv = jnp.concatenate(dvs, axis=1)

        @pl.when(i == 0)
        def _():
            dk_ref[...] = dkv
            dv_ref[...] = dvv

        @pl.when(i > 0)
        def _():
            dk_ref[...] += dkv
            dv_ref[...] += dvv

    row = pl.BlockSpec((tq, D_XA), lambda i: (i, 0))
    kv = pl.BlockSpec((nm, D_XA), lambda i: (0, 0))
    return _pcall(body, grid=(t // tq,), in_specs=[row, kv, kv, row], out_specs=[row, kv, kv],
                  out_shape=[_sds((t, D_XA), BF16), _sds((nm, D_XA), F32), _sds((nm, D_XA), F32)],
                  compiler_params=_params(dimension_semantics=("arbitrary",)), name=name)(q, k, v, do)


FF_CW = 1408


def _swiglu_fwd(g, u, *, name):
    t, f = g.shape
    tr = _tile(t, 512)

    def body(g_ref, u_ref, o_ref):
        gv = g_ref[...].astype(F32)
        o_ref[...] = (gv * _sig(gv) * u_ref[...].astype(F32)).astype(o_ref.dtype)

    blk = pl.BlockSpec((tr, FF_CW), lambda i, j: (i, j))
    return _pcall(body, grid=(t // tr, f // FF_CW), in_specs=[blk, blk], out_specs=blk, out_shape=_sds((t, f), BF16),
                  compiler_params=_params(dimension_semantics=("parallel", "parallel")), name=name)(g, u)


def _swiglu_bwd(g, u, dact, *, name):
    t, f = g.shape
    tr = _tile(t, 512)

    def body(g_ref, u_ref, d_ref, dg_ref, du_ref):
        gv = g_ref[...].astype(F32)
        uv = u_ref[...].astype(F32)
        dv = d_ref[...].astype(F32)
        s = _sig(gv)
        dg_ref[...] = (dv * uv * (s * (1.0 + gv * (1.0 - s)))).astype(dg_ref.dtype)
        du_ref[...] = (dv * gv * s).astype(du_ref.dtype)

    blk = pl.BlockSpec((tr, FF_CW), lambda i, j: (i, j))
    return _pcall(body, grid=(t // tr, f // FF_CW), in_specs=[blk, blk, blk], out_specs=[blk, blk],
                  out_shape=[_sds((t, f), BF16), _sds((t, f), BF16)],
                  compiler_params=_params(dimension_semantics=("parallel", "parallel")), name=name)(g, u, dact)


def _row_tile(n):
    for cand in (128, 64, 32, 16):
        if n % cand == 0:
            return cand
    raise ValueError(n)


def _add_halves(g4, rb, c_arr, *, name):
    _, r, cdim = g4.shape
    hr = r // 2
    rt = _row_tile(hr)
    nb = hr // rt

    def body(c_ref, g_ref, rb_ref, o_ref):
        del c_ref
        o_ref[...] = (g_ref[...].astype(F32) + rb_ref[...].astype(F32)).astype(o_ref.dtype)

    gs = pltpu.PrefetchScalarGridSpec(
        num_scalar_prefetch=1, grid=(4, nb),
        in_specs=[pl.BlockSpec((None, rt, cdim), lambda q, i, c: (q, c[0] * nb + i, 0)),
                  pl.BlockSpec((None, rt, cdim), lambda q, i, c: (q, i, 0))],
        out_specs=pl.BlockSpec((None, rt, cdim), lambda q, i, c: (q, i, 0)))
    return _pcall(body, grid_spec=gs, out_shape=_sds((4, hr, cdim), BF16),
                  compiler_params=_params(dimension_semantics=("parallel", "parallel")), name=name)(c_arr, g4, rb)


def _sum_chips(p4, rc, q_arr, c_arr, *, name):
    _, hr, cdim = p4.shape
    rt = _row_tile(hr)
    nb = hr // rt

    def body(q_ref, c_ref, p_ref, rc_ref, o_ref):
        del q_ref, c_ref
        o_ref[...] = ((p_ref[...].astype(F32) + rc_ref[0].astype(F32)) + rc_ref[1].astype(F32)) + rc_ref[2].astype(F32)

    gs = pltpu.PrefetchScalarGridSpec(
        num_scalar_prefetch=2, grid=(nb,),
        in_specs=[pl.BlockSpec((None, rt, cdim), lambda i, q, c: (q[0], i, 0)),
                  pl.BlockSpec((3, rt, cdim), lambda i, q, c: (0, i, 0))],
        out_specs=pl.BlockSpec((rt, cdim), lambda i, q, c: (c[0] * nb + i, 0)))
    return _pcall(body, grid_spec=gs, out_shape=_sds((2 * hr, cdim), F32),
                  compiler_params=_params(dimension_semantics=("parallel",)), name=name)(q_arr, c_arr, p4, rc)


def _adam_math(w, g, m, v):
    m = ADAM_B1 * m + (1.0 - ADAM_B1) * g
    v = ADAM_B2 * v + (1.0 - ADAM_B2) * (g * g)
    m_hat = m / (1.0 - ADAM_B1 ** ADAM_STEP)
    v_hat = v / (1.0 - ADAM_B2 ** ADAM_STEP)
    delta = -ADAM_LR * (m_hat / (jnp.sqrt(v_hat) + ADAM_EPS) + ADAM_WD * w)
    return delta, m, v


def _adamw_layer(layer, w, m, v, g, prev, *, name):
    depth, r, cdim = w.shape
    rt = _row_tile(r)
    n_prev = 0 if prev is None else 4

    def body(*refs):
        w_ref, m_ref, v_ref, g_ref = refs[:4]
        go_ref, d_ref, mo_ref, vo_ref = refs[4 + n_prev:]
        gv = g_ref[...]
        delta, mn, vn = _adam_math(w_ref[...], gv, m_ref[...], v_ref[...])
        go_ref[...] = gv
        d_ref[...] = delta
        mo_ref[...] = mn
        vo_ref[...] = vn

    st = pl.BlockSpec((None, rt, cdim), lambda i: (layer, i, 0))
    in_specs = [st, st, st, pl.BlockSpec((rt, cdim), lambda i: (i, 0))] + [pl.BlockSpec(memory_space=pl.ANY)] * n_prev
    args = (w, m, v, g) + (tuple(prev) if prev is not None else ())
    return _pcall(body, grid=(r // rt,), in_specs=in_specs, out_specs=[st] * 4,
                  out_shape=[_sds((depth, r, cdim), F32)] * 4,
                  input_output_aliases={4 + i: i for i in range(n_prev)},
                  compiler_params=_params(dimension_semantics=("parallel",)), name=name)(*args)


def _adamw_flat(w, g, m, v, *, name):
    def body(w_ref, g_ref, m_ref, v_ref, d_ref, mo_ref, vo_ref):
        delta, mn, vn = _adam_math(w_ref[...], g_ref[...], m_ref[...], v_ref[...])
        d_ref[...] = delta
        mo_ref[...] = mn
        vo_ref[...] = vn

    return _pcall(body, out_shape=[_sds(w.shape, F32)] * 3, compiler_params=_params(), name=name)(w, g, m, v)


def _place():
    x, y, c = lax.axis_index("x"), lax.axis_index("y"), lax.axis_index("c")
    chips = [(1 - x, y), (x, 1 - y), (1 - x, 1 - y)]
    return x, y, c, chips


def _cast_place(w, q_arr, *, dep=None, name):
    r, cdim = w.shape
    rt = _row_tile(r)
    deps = () if dep is None else (dep,)

    def body(q_ref, w_ref, *rest):
        del q_ref
        o_ref = rest[-1]
        o_ref[...] = w_ref[...].astype(o_ref.dtype)

    gs = pltpu.PrefetchScalarGridSpec(
        num_scalar_prefetch=1, grid=(r // rt,),
        in_specs=[pl.BlockSpec((rt, cdim), lambda i, q: (i, 0))] + [pl.BlockSpec(memory_space=pl.ANY)] * len(deps),
        out_specs=pl.BlockSpec((None, rt, cdim), lambda i, q: (q[0], i, 0)))
    return _pcall(body, grid_spec=gs, out_shape=_sds((4, r, cdim), BF16),
                  compiler_params=_params(dimension_semantics=("parallel",)), name=name)(q_arr, w, *deps)


HBM_SPEC = pl.BlockSpec(memory_space=pltpu.HBM)
SEM_SPEC = pl.BlockSpec(memory_space=pltpu.SEMAPHORE)
ANY_SPEC = pl.BlockSpec(memory_space=pl.ANY)
SPLIT_PARAMS = pltpu.CompilerParams(has_side_effects=pltpu.SideEffectType.DATAFLOW_SIDE_EFFECTING)


def _gather_copies(bufs, sems):
    n = len(bufs)
    x, y, c, chips = _place()
    qme = 2 * x + y
    cps = []
    for t in range(n):
        hr = bufs[t].shape[1] // 2
        mine = bufs[t].at[qme, pl.ds(c * hr, hr)]
        for k, (px, py) in enumerate(chips):
            landed = bufs[t].at[2 * px + py, pl.ds(c * hr, hr)]
            peer = dict(device_id=(px, py, c), device_id_type=MESH)
            cps.append((pltpu.make_async_remote_copy(src_ref=mine, dst_ref=mine, send_sem=sems[3 * t + k],
                                                     recv_sem=sems[3 * n + 3 * t + k], **peer),
                        pltpu.make_async_remote_copy(src_ref=mine, dst_ref=landed, send_sem=sems[3 * t + k],
                                                     recv_sem=sems[3 * n + 3 * t + k], **peer)))
    return cps


def _forward_copies(bufs, sems):
    n = len(bufs)
    x, y, c, chips = _place()
    sib = dict(device_id=(x, y, 1 - c), device_id_type=MESH)
    cps = []
    for t in range(n):
        hr = bufs[t].shape[1] // 2
        for k, (px, py) in enumerate(chips):
            landed = bufs[t].at[2 * px + py, pl.ds(c * hr, hr)]
            other = bufs[t].at[2 * px + py, pl.ds((1 - c) * hr, hr)]
            cps.append((pltpu.make_async_remote_copy(src_ref=landed, dst_ref=landed, send_sem=sems[3 * t + k],
                                                     recv_sem=sems[3 * n + 3 * t + k], **sib),
                        pltpu.make_async_remote_copy(src_ref=landed, dst_ref=other, send_sem=sems[3 * t + k],
                                                     recv_sem=sems[3 * n + 3 * t + k], **sib)))
    return cps


def _join_copies(bufs, sems):
    n = len(bufs)
    x, y, c, _ = _place()
    sib = dict(device_id=(x, y, 1 - c), device_id_type=MESH)
    cps = []
    for t in range(n):
        hr = bufs[t].shape[0] // 2
        mine = bufs[t].at[pl.ds(c * hr, hr)]
        theirs = bufs[t].at[pl.ds((1 - c) * hr, hr)]
        cps.append((pltpu.make_async_remote_copy(src_ref=mine, dst_ref=mine, send_sem=sems[t], recv_sem=sems[n + t], **sib),
                    pltpu.make_async_remote_copy(src_ref=mine, dst_ref=theirs, send_sem=sems[t], recv_sem=sems[n + t], **sib)))
    return cps


def _gather_start(bufs, after, *, copies=_gather_copies, per=3, name):
    n = len(bufs)
    ns = 2 * per * n

    def body(*refs):
        ins = refs[:n]
        sems = refs[n + 1:n + 1 + ns]
        token = refs[2 * n + 1 + ns]
        for send, _ in copies(ins, sems):
            send.start()
        token[...] = jnp.zeros_like(token)

    res = _pcall(body, in_specs=[HBM_SPEC] * n + [ANY_SPEC],
                 out_specs=(SEM_SPEC,) * ns + (HBM_SPEC,) * n + (pl.BlockSpec(memory_space=pltpu.VMEM),),
                 out_shape=(pltpu.SemaphoreType.DMA(()),) * ns + tuple(pltpu.HBM(b.shape, b.dtype) for b in bufs)
                 + (_sds((8, 128), F32),),
                 input_output_aliases={t: ns + t for t in range(n)}, compiler_params=SPLIT_PARAMS, name=name)(*bufs, after)
    return list(res[:ns]), list(res[ns:ns + n]), res[ns + n]


def _gather_wait(bufs, sems, after, *, copies=_gather_copies, name):
    n = len(bufs)
    ns = len(sems)

    def body(*refs):
        ins = refs[:n]
        for send, arrive in copies(ins, refs[n:n + ns]):
            send.wait_send()
            arrive.wait_recv()

    return list(_pcall(body, in_specs=[HBM_SPEC] * n + [SEM_SPEC] * ns + [ANY_SPEC], out_specs=(HBM_SPEC,) * n,
                       out_shape=tuple(pltpu.HBM(b.shape, b.dtype) for b in bufs),
                       input_output_aliases={t: t for t in range(n)}, compiler_params=SPLIT_PARAMS, name=name)(*bufs, *sems, after))


def _gather_forward(bufs, *, name):
    n = len(bufs)

    def body(*refs):
        outs = refs[n:2 * n]
        send, recv = refs[2 * n:]
        x, y, c, chips = _place()
        sib = dict(device_id=(x, y, 1 - c), device_id_type=MESH)
        cps = []
        for t in range(n):
            hr = outs[t].shape[1] // 2
            for k, (px, py) in enumerate(chips):
                landed = outs[t].at[2 * px + py, pl.ds(c * hr, hr)]
                cp = pltpu.make_async_remote_copy(src_ref=landed, dst_ref=landed, send_sem=send.at[t, k],
                                                  recv_sem=recv.at[t, k], **sib)
                cp.start()
                cps.append(cp)
        for t in range(n):
            hr = outs[t].shape[1] // 2
            for k, (px, py) in enumerate(chips):
                other = outs[t].at[2 * px + py, pl.ds((1 - c) * hr, hr)]
                pltpu.make_async_remote_copy(src_ref=other, dst_ref=other, send_sem=send.at[t, k], recv_sem=recv.at[t, k],
                                             **sib).wait_recv()
        for cp in cps:
            cp.wait_send()

    return _pcall(body, in_specs=[ANY_SPEC] * n, out_specs=[ANY_SPEC] * n,
                  out_shape=[_sds(b.shape, b.dtype) for b in bufs], input_output_aliases={t: t for t in range(n)},
                  scratch_shapes=[pltpu.SemaphoreType.DMA((n, 3))] * 2,
                  compiler_params=pltpu.CompilerParams(has_side_effects=True), name=name)(*bufs)


def _owner_copies(parts, lands, sems):
    n = len(parts)
    _, _, c, chips = _place()
    cps = []
    for t in range(n):
        for k, (px, py) in enumerate(chips):
            cps.append(pltpu.make_async_remote_copy(src_ref=parts[t].at[2 * px + py], dst_ref=lands[t].at[k],
                                                    send_sem=sems[3 * t + k], recv_sem=sems[3 * n + 3 * t + k],
                                                    device_id=(px, py, c), device_id_type=MESH))
    return cps


def _swap_copies(grads, lands, sems):
    n = len(grads)
    x, y, c, _ = _place()
    cps = []
    for t in range(n):
        hr = grads[t].shape[1] // 2
        cps.append(pltpu.make_async_remote_copy(src_ref=grads[t].at[:, pl.ds((1 - c) * hr, hr), :], dst_ref=lands[t],
                                                send_sem=sems[t], recv_sem=sems[n + t],
                                                device_id=(x, y, 1 - c), device_id_type=MESH))
    return cps


def _exchange_start(srcs, land_shapes, copies, n_copies, *, name):
    n = len(srcs)
    ns = 2 * n_copies
    lands = [pltpu.with_memory_space_constraint(lax.empty(shape, s.dtype), pltpu.HBM) for shape, s in zip(land_shapes, srcs)]

    def body(*refs):
        ins, lnd = refs[:n], refs[n:2 * n]
        sems = refs[2 * n:2 * n + ns]
        token = refs[4 * n + ns]
        for cp in copies(ins, lnd, sems):
            cp.start()
        token[...] = jnp.zeros_like(token)

    res = _pcall(body, in_specs=[HBM_SPEC] * (2 * n),
                 out_specs=(SEM_SPEC,) * ns + (HBM_SPEC,) * (2 * n) + (pl.BlockSpec(memory_space=pltpu.VMEM),),
                 out_shape=(pltpu.SemaphoreType.DMA(()),) * ns
                 + tuple(pltpu.HBM(b.shape, b.dtype) for b in list(srcs) + lands) + (_sds((8, 128), F32),),
                 input_output_aliases={t: ns + t for t in range(2 * n)}, compiler_params=SPLIT_PARAMS, name=name)(*srcs, *lands)
    return list(res[:ns]), list(res[ns:ns + n]), list(res[ns + n:ns + 2 * n]), res[ns + 2 * n]


def _exchange_wait(srcs, lands, sems, after, copies, *, name):
    n = len(srcs)
    ns = len(sems)

    def body(*refs):
        ins, lnd = refs[:n], refs[n:2 * n]
        for cp in copies(ins, lnd, refs[2 * n:2 * n + ns]):
            cp.wait_send()
            cp.wait_recv()

    res = _pcall(body, in_specs=[HBM_SPEC] * (2 * n) + [SEM_SPEC] * ns + [ANY_SPEC], out_specs=(HBM_SPEC,) * (2 * n),
                 out_shape=tuple(pltpu.HBM(b.shape, b.dtype) for b in list(srcs) + list(lands)),
                 input_output_aliases={t: t for t in range(2 * n)}, compiler_params=SPLIT_PARAMS, name=name)(
                     *srcs, *lands, *sems, after)
    return list(res[:n]), list(res[n:])


def _owners_start(parts, *, name):
    return _exchange_start(parts, [(3,) + p.shape[1:] for p in parts], _owner_copies, 3 * len(parts), name=name)


def _owners_wait(parts, lands, sems, after, *, name):
    return _exchange_wait(parts, lands, sems, after, _owner_copies, name=name)


def _swap_start(grads, *, name):
    return _exchange_start(grads, [(4, g.shape[1] // 2, g.shape[2]) for g in grads], _swap_copies, len(grads), name=name)


def _swap_wait(grads, lands, sems, after, *, name):
    return _exchange_wait(grads, lands, sems, after, _swap_copies, name=name)


def _swap_halves(grads, *, name):
    n = len(grads)

    def body(*refs):
        ins, outs = refs[:n], refs[n:2 * n]
        send, recv = refs[2 * n:]
        x, y, c, _ = _place()
        cps = []
        for t in range(n):
            hr = ins[t].shape[1] // 2
            cp = pltpu.make_async_remote_copy(src_ref=ins[t].at[:, pl.ds((1 - c) * hr, hr), :], dst_ref=outs[t],
                                              send_sem=send.at[t], recv_sem=recv.at[t],
                                              device_id=(x, y, 1 - c), device_id_type=MESH)
            cp.start()
            cps.append(cp)
        for cp in cps:
            cp.wait()

    anyspec = pl.BlockSpec(memory_space=pl.ANY)
    return _pcall(body, in_specs=[anyspec] * n, out_specs=[anyspec] * n,
                  out_shape=[_sds((4, g.shape[1] // 2, g.shape[2]), g.dtype) for g in grads],
                  scratch_shapes=[pltpu.SemaphoreType.DMA((n,))] * 2,
                  compiler_params=pltpu.CompilerParams(has_side_effects=True), name=name)(*grads)


def _join_halves(bufs, *, name):
    n = len(bufs)

    def body(*refs):
        outs = refs[n:2 * n]
        send, recv = refs[2 * n:]
        x, y, c, _ = _place()
        cps = []
        for t in range(n):
            hr = outs[t].shape[0] // 2
            mine = outs[t].at[pl.ds(c * hr, hr)]
            cp = pltpu.make_async_remote_copy(src_ref=mine, dst_ref=mine, send_sem=send.at[t], recv_sem=recv.at[t],
                                              device_id=(x, y, 1 - c), device_id_type=MESH)
            cp.start()
            cps.append(cp)
        for t in range(n):
            hr = outs[t].shape[0] // 2
            theirs = outs[t].at[pl.ds((1 - c) * hr, hr)]
            pltpu.make_async_remote_copy(src_ref=theirs, dst_ref=theirs, send_sem=send.at[t], recv_sem=recv.at[t],
                                         device_id=(x, y, 1 - c), device_id_type=MESH).wait_recv()
        for cp in cps:
            cp.wait_send()

    anyspec = pl.BlockSpec(memory_space=pl.ANY)
    return _pcall(body, in_specs=[anyspec] * n, out_specs=[anyspec] * n,
                  out_shape=[_sds(b.shape, b.dtype) for b in bufs], input_output_aliases={t: t for t in range(n)},
                  scratch_shapes=[pltpu.SemaphoreType.DMA((n,))] * 2,
                  compiler_params=pltpu.CompilerParams(has_side_effects=True), name=name)(*bufs)


def _allreduce_small(buf, *, name):
    rows = buf.shape[0]
    rels = [(dx, dy, dc) for dx in (0, 1) for dy in (0, 1) for dc in (0, 1)][1:]

    def body(in_ref, out_ref, gbuf, send, recv):
        x, y, c = lax.axis_index("x"), lax.axis_index("y"), lax.axis_index("c")
        me = 4 * x + 2 * y + c
        gbuf[me] = in_ref[...]
        cps = []
        for k, (dx, dy, dc) in enumerate(rels):
            peer = (x + dx - 2 * x * dx, y + dy - 2 * y * dy, c + dc - 2 * c * dc)
            cp = pltpu.make_async_remote_copy(src_ref=in_ref, dst_ref=gbuf.at[me], send_sem=send.at[k], recv_sem=recv.at[k],
                                              device_id=peer, device_id_type=MESH)
            cp.start()
            cps.append(cp)
        for k, (dx, dy, dc) in enumerate(rels):
            px, py, pc = x + dx - 2 * x * dx, y + dy - 2 * y * dy, c + dc - 2 * c * dc
            pltpu.make_async_remote_copy(src_ref=in_ref, dst_ref=gbuf.at[4 * px + 2 * py + pc], send_sem=send.at[k],
                                         recv_sem=recv.at[k], device_id=(px, py, pc), device_id_type=MESH).wait_recv()
        for cp in cps:
            cp.wait_send()
        acc = gbuf[0]
        for d in range(1, 8):
            acc = acc + gbuf[d]
        out_ref[...] = acc

    vm = pl.BlockSpec(memory_space=pltpu.VMEM)
    return _pcall(body, in_specs=[vm], out_specs=vm, out_shape=_sds(buf.shape, F32),
                  scratch_shapes=[pltpu.VMEM((8, rows, 128), F32), pltpu.SemaphoreType.DMA((7,)), pltpu.SemaphoreType.DMA((7,))],
                  compiler_params=_params(has_side_effects=True), name=name)(buf)


def _pack(arrs):
    flat = jnp.concatenate([a.reshape(-1).astype(F32) for a in arrs])
    n = flat.shape[0]
    rows = -(-n // 1024) * 8
    return jnp.pad(flat, (0, rows * 128 - n)).reshape(rows, 128)


def _unpack(buf, shapes):
    flat = buf.reshape(-1)
    out, o = [], 0
    for s in shapes:
        n = 1
        for d in s:
            n *= d
        out.append(flat[o:o + n].reshape(s))
        o += n
    return out


def _perm_xbc(a):
    parts = []
    for g in range(N_GROUPS):
        parts += [a[..., 512 * g:512 * (g + 1)], a[..., 2048 + 128 * g:2048 + 128 * (g + 1)],
                  a[..., 2560 + 128 * g:2560 + 128 * (g + 1)]]
    return jnp.concatenate(parts, axis=-1)


def _unperm_xbc(a):
    xs = [a[..., GROUP_W * g:GROUP_W * g + 512] for g in range(N_GROUPS)]
    bs = [a[..., GROUP_W * g + 512:GROUP_W * g + 640] for g in range(N_GROUPS)]
    cs = [a[..., GROUP_W * g + 640:GROUP_W * (g + 1)] for g in range(N_GROUPS)]
    return jnp.concatenate(xs + bs + cs, axis=-1)


def _split_w_in(w4):
    k = w4.shape[1]
    nat = jnp.transpose(w4, (1, 0, 2)).reshape(k, -1)
    w_z = nat[:, :2048]
    w_xbc = _perm_xbc(nat[:, 2048:5120])
    w_dt = jnp.pad(nat[:, 5120:5152], ((0, 0), (0, DT_PAD - HEADS)))
    w_s = nat[:, 5152:]
    return w_z, w_xbc, w_dt, w_s


def _join_dw_in(dw_z, dw_xbc, dw_dt, dw_s):
    k = dw_z.shape[0]
    nat = jnp.concatenate([dw_z, _unperm_xbc(dw_xbc), dw_dt[:, :HEADS], dw_s], axis=1)
    return jnp.transpose(nat.reshape(k, 4, -1), (1, 0, 2))


def kernel(x, mem, norm_mix, w_in, ssd_conv_w, ssd_conv_b, dt_bias, a_log, d_skip, ssd_norm, sc_conv_w, sc_norm, w_out, mem_norm, norm_xa, w_q, w_k, w_v, w_o, norm_ffn, w_gate, w_up, w_down, norm_final, loss_target, m_norm_mix, m_w_in, m_ssd_conv_w, m_ssd_conv_b, m_dt_bias, m_a_log, m_d_skip, m_ssd_norm, m_sc_conv_w, m_sc_norm, m_w_out, m_mem_norm, m_norm_xa, m_w_q, m_w_k, m_w_v, m_w_o, m_norm_ffn, m_w_gate, m_w_up, m_w_down, m_norm_final, v_norm_mix, v_w_in, v_ssd_conv_w, v_ssd_conv_b, v_dt_bias, v_a_log, v_d_skip, v_ssd_norm, v_sc_conv_w, v_sc_norm, v_w_out, v_mem_norm, v_norm_xa, v_w_q, v_w_k, v_w_v, v_w_o, v_norm_ffn, v_w_gate, v_w_up, v_w_down, v_norm_final):
    depth = w_in.shape[0]
    ix, iy, ic = lax.axis_index("x"), lax.axis_index("y"), lax.axis_index("c")
    qme = 2 * ix + iy
    c_arr = jnp.reshape(ic, (1,)).astype(jnp.int32)
    q_arr = jnp.reshape(qme, (1,)).astype(jnp.int32)
    h = x[0]
    tgt = loss_target[0]

    big = dict(w_in=w_in, w_out=w_out, w_q=w_q, w_k=w_k, w_v=w_v, w_o=w_o, w_gate=w_gate, w_up=w_up, w_down=w_down)
    big_m = dict(w_in=m_w_in, w_out=m_w_out, w_q=m_w_q, w_k=m_w_k, w_v=m_w_v, w_o=m_w_o, w_gate=m_w_gate, w_up=m_w_up, w_down=m_w_down)
    big_v = dict(w_in=v_w_in, w_out=v_w_out, w_q=v_w_q, w_k=v_w_k, w_v=v_w_v, w_o=v_w_o, w_gate=v_w_gate, w_up=v_w_up, w_down=v_w_down)
    names = list(big)

    conv_full = jnp.zeros((depth, 4, D_XBC), F32)
    conv_full = lax.dynamic_update_slice(conv_full, jnp.where(ic == 0, ssd_conv_w, 0.0), (0, 0, qme * (D_XBC // 4)))
    sc_full = jnp.zeros((depth, 3, D_SC), F32)
    sc_full = lax.dynamic_update_slice(sc_full, jnp.where(ic == 0, sc_conv_w, 0.0), (0, 0, qme * (D_SC // 4)))
    conv_buf = _allreduce_small(_pack([conv_full, sc_full]), name="gather_conv_w")
    conv_full, sc_full = _unpack(conv_buf, [conv_full.shape, sc_full.shape])
    conv_p = _perm_xbc(conv_full)
    convb_p = _perm_xbc(ssd_conv_b)

    pad_h = lambda a: jnp.pad(a, ((0, 0), (0, DT_PAD - HEADS)))
    dt_bias_p, a_log_p = pad_h(dt_bias), pad_h(a_log)
    dskip_ch = jnp.repeat(d_skip, D_SSD // HEADS, axis=1)

    def finish_gather(tag, group, sems, flying, after):
        landed = _gather_wait(flying, sems, after, name=f"gather_wait{tag}")
        return dict(zip(group, _gather_forward(landed, name=f"gather_fwd{tag}")))

    placed0 = [_cast_place(big[n][0], q_arr, name=f"cast_{n}0") for n in names]
    sems_a, fly_a, tok_a = _gather_start(placed0[:1], conv_buf, name="gather_start0a")
    sems_b, fly_b, tok_b = _gather_start(placed0[1:], tok_a, name="gather_start0b")
    placed, last = [placed0], tok_b
    for l in range(1, depth):
        row = []
        for n in names:
            last = _cast_place(big[n][l], q_arr, dep=last, name=f"cast_{n}{l}")
            row.append(last)
        placed.append(row)
    memn = _rms_fwd(mem[0], mem_norm.reshape(1, D) + tok_b[:1, :1], name="memn")
    gathered = [finish_gather("0a", names[:1], sems_a, fly_a, last)]

    saved = []
    for l in range(depth):
        gw = gathered[l]
        gain = norm_mix[l:l + 1]
        if l + 1 < depth:
            sems, flying, token = _gather_start(placed[l + 1], gw["w_in"], name=f"gather_start{l + 1}")
            gain = gain + token[:1, :1]
        w_z, w_xbc, w_dt, w_s = _split_w_in(gw["w_in"])
        hn1 = _rms_fwd(h, gain, name=f"rms_mix{l}")
        pz = _mm_nn(hn1, w_z, tn=1024, name=f"proj_z{l}")
        pxbc = _mm_nn(hn1, w_xbc, tn=1024, name=f"proj_xbc{l}")
        dtr = _mm_nn(hn1, w_dt, tn=DT_PAD, out_dtype=F32, name=f"proj_dt{l}")
        t = h.shape[0]
        tm = _tile(t, 1024)
        ps = _mm(hn1, w_s, mode="nn", grid=(t // tm, 6, 1),
                 a_spec=pl.BlockSpec((tm, D), lambda i, j, kk: (i, 0)),
                 b_spec=pl.BlockSpec((D, 1024), lambda i, j, kk: (0, j)),
                 o_spec=pl.BlockSpec((None, tm, 1024), lambda i, j, kk: (j // 2, i, j % 2)), o_tile=(tm, 1024),
                 out_sds=_sds((3, t, D_SC), BF16), name=f"proj_s{l}")
        xc = _conv_fwd(pxbc, conv_p[l], convb_p[l:l + 1], name=f"conv{l}")
        dt, dtg, acsg, acst = _dt_prep(dtr, dt_bias_p[l:l + 1], a_log_p[l:l + 1], name=f"dt_prep{l}")
        y, states, mix = _ssd_fwd(xc, dtg, acsg, acst, pz, dskip_ch[l:l + 1], ssd_norm[l:l + 1], name=f"ssd{l}")
        mix = _sc_fwd(ps, sc_full[l], sc_norm[l:l + 1], mix, name=f"sc{l}")
        if l == 0:
            gw.update(finish_gather("0b", names[1:], sems_b, fly_b, mix))
        dep = None
        if l + 1 < depth:
            landed = _gather_wait(flying, sems, mix, name=f"gather_wait{l + 1}")
            sems, flying, dep = _gather_start(landed, mix, copies=_forward_copies, name=f"gather_fstart{l + 1}")
        wo2 = gw["w_out"].reshape(-1, D)
        wq2, wk2, wv2 = (gw[n].reshape(D, D_XA) for n in ("w_q", "w_k", "w_v"))
        wd2 = gw["w_down"].reshape(D_FF, D)
        s = dict(h0=h, w_z=w_z, w_xbc=w_xbc, w_dt=w_dt, w_s=w_s, wo2=wo2, wq2=wq2, wk2=wk2, wv2=wv2, wd2=wd2)
        h1 = _mm_nn(mix, wo2, tn=1024, tm=512, out_dtype=F32, add=h, dep=dep, name=f"out_proj{l}")
        hn2 = _rms_fwd(h1, norm_xa[l:l + 1], name=f"rms_xa{l}")
        q = _mm_nn(hn2, wq2, tn=D_XA, name=f"q{l}")
        k = _mm_nn(memn, wk2, tn=D_XA, name=f"k{l}")
        v = _mm_nn(memn, wv2, tn=D_XA, name=f"v{l}")
        o = _xa_fwd(q, k, v, name=f"xa{l}")
        h2 = _mm_nn_sm(o, gw["w_o"], out_dtype=F32, add=h1, name=f"xa_out{l}")
        hn3 = _rms_fwd(h2, norm_ffn[l:l + 1], name=f"rms_ffn{l}")
        fg = _mm_nn_sm(hn3, gw["w_gate"], name=f"ff_gate{l}")
        fu = _mm_nn_sm(hn3, gw["w_up"], name=f"ff_up{l}")
        act = _swiglu_fwd(fg, fu, name=f"swiglu{l}")
        h3 = _mm_nn(act, wd2, tn=1024, tm=512, out_dtype=F32, add=h2, name=f"ff_down{l}")
        s.update(hn1=hn1, pz=pz, pxbc=pxbc, dtr=dtr, ps=ps, xc=xc, dt=dt, dtg=dtg, acsg=acsg, acst=acst, y=y,
                 states=states, mix=mix, h1=h1, hn2=hn2, q=q, k=k, v=v, o=o, h2=h2, hn3=hn3, fg=fg, fu=fu, act=act)
        saved.append(s)
        h = h3
        if l + 1 < depth:
            gathered.append(dict(zip(names, _gather_wait(flying, sems, h3, copies=_forward_copies,
                                                         name=f"gather_fwait{l + 1}"))))

    loss_vec, dh, dhb, d_norm_final = _final(h, norm_final.reshape(1, D), tgt, name="final")
    loss = lax.psum(loss_vec[0, 0], ("x", "y", "c"))

    small = dict(norm_mix=[], ssd_conv_w=[], ssd_conv_b=[], dt_bias=[], a_log=[], d_skip=[], ssd_norm=[], sc_conv_w=[],
                 sc_norm=[], norm_xa=[], norm_ffn=[])
    dmemn = None
    carried = {n: None for n in names}
    pending = None
    swapping = None
    first_groups = [["w_gate", "w_up", "w_down"], ["w_out", "w_q", "w_k", "w_v", "w_o"], ["w_in"]]

    def start_reduce(lyr, tag, group, grads, behind=None):
        g_list = [grads[n] for n in group]
        if behind is None:
            recv_sib = _swap_halves(g_list, name=f"swap_halves{tag}")
        else:
            sems_s, g_fly, lands_s, tok_s = _swap_start(g_list, name=f"swap_start{tag}")
            sems_j, fly_j, tok_j = finish_reduce(behind, tok_s, split_join=True)
            g_list, recv_sib = _swap_wait(g_fly, lands_s, sems_s, tok_j, name=f"swap_wait{tag}")
        parts = [_add_halves(g, rb, c_arr, name=f"add_halves_{n}{lyr}") for n, g, rb in zip(group, g_list, recv_sib)]
        pend = (lyr, tag, group) + _owners_start(parts, name=f"owners_start{tag}")
        if behind is not None:
            full = _gather_wait(fly_j, sems_j, pend[-1], copies=_join_copies, name=f"join_wait{behind[1]}")
            apply_adamw(behind[0], behind[2], full)
        return pend

    def apply_adamw(lyr, group, full):
        for n, g in zip(group, full):
            carried[n] = _adamw_layer(lyr, big[n], big_m[n], big_v[n], g, carried[n], name=f"adamw_{n}{lyr}")

    def finish_reduce(pend, after, split_join=False):
        lyr, tag, group, sems_r, parts_r, lands_r, _ = pend
        parts_r, lands_r = _owners_wait(parts_r, lands_r, sems_r, after, name=f"owners_wait{tag}")
        halves = [_sum_chips(p, rc, q_arr, c_arr, name=f"sum_chips_{n}{lyr}") for n, p, rc in zip(group, parts_r, lands_r)]
        if split_join:
            return _gather_start(halves, parts_r[0], copies=_join_copies, per=1, name=f"join_start{tag}")
        apply_adamw(lyr, group, _join_halves(halves, name=f"join_halves{tag}"))

    for l in reversed(range(depth)):
        s, gw = saved[l], gathered[l]
        t = dh.shape[0]
        tm = _tile(t, 1024)
        early = []
        dep = None if pending is None else pending[-1]
        if swapping is not None:
            dep = swapping[-1]
        dact = _mm_nt(dhb, s["wd2"], tn=FF_CW, dep=dep, name=f"d_act{l}")
        dw_down = _mm_tn(s["act"], dhb, tm=FF_CW, tn=1024, name=f"dw_down{l}")
        if swapping is not None:
            lyr_s, tag_s, group_s, sems_s, g_fly, lands_s, _ = swapping
            g_sw, recv_sw = _swap_wait(g_fly, lands_s, sems_s, dw_down, name=f"swap_wait{tag_s}")
            parts_sw = [_add_halves(g, rb, c_arr, name=f"add_halves_{n}{lyr_s}") for n, g, rb in zip(group_s, g_sw, recv_sw)]
            pending = (lyr_s, tag_s, group_s) + _owners_start(parts_sw, name=f"owners_start{tag_s}")
            swapping = None
        dg, du = _swiglu_bwd(s["fg"], s["fu"], dact, name=f"d_swiglu{l}")
        dw_gate = _mm_tn_sm(s["hn3"], dg, tm=1024, name=f"dw_gate{l}")
        dw_up = _mm_tn_sm(s["hn3"], du, tm=1024, name=f"dw_up{l}")
        dhn = _mm_nt_sm(dg, gw["w_gate"], tn=512, out_dtype=F32, name=f"d_hn3a{l}")
        dhn = _mm_nt_sm(du, gw["w_up"], tn=512, out_dtype=F32, add=dhn, name=f"d_hn3b{l}")
        dh, dhb, dn = _rms_bwd(s["h2"], norm_ffn[l:l + 1], dhn, dh, name=f"d_rms_ffn{l}")
        small["norm_ffn"].append(dn)
        dep = None
        if l == 0:
            early.append(start_reduce(0, "0a", first_groups[0],
                                      dict(w_gate=dw_gate, w_up=dw_up, w_down=dw_down.reshape(4, -1, D))))
            dep = early[-1][-1]
        do = _mm_nt_sm(dhb, gw["w_o"], tn=D_XA, dep=dep, name=f"d_o{l}")
        dw_o = _mm_tn_sm(s["o"], dhb, tm=D_XA, name=f"dw_o{l}")
        dq, dk, dv = _xa_bwd(s["q"], s["k"], s["v"], do, name=f"d_xa{l}")
        dw_q = _mm_tn(s["hn2"], dq, tm=1024, tn=D_XA, name=f"dw_q{l}")
        dw_k = _mm_tn(memn, dk, tm=1024, tn=D_XA, name=f"dw_k{l}")
        dw_v = _mm_tn(memn, dv, tm=1024, tn=D_XA, name=f"dw_v{l}")
        dhn = _mm_nt(dq, s["wq2"], tn=1024, out_dtype=F32, name=f"d_hn2{l}")
        dmemn = _mm_nt(dk, s["wk2"], tn=1024, out_dtype=F32, add=dmemn, name=f"d_memn_k{l}")
        dmemn = _mm_nt(dv, s["wv2"], tn=1024, out_dtype=F32, add=dmemn, name=f"d_memn_v{l}")
        dh, dhb, dn = _rms_bwd(s["h1"], norm_xa[l:l + 1], dhn, dh, name=f"d_rms_xa{l}")
        small["norm_xa"].append(dn)
        dw_out = _mm_tn(s["mix"], dhb, tm=1024, tn=1024, name=f"dw_out{l}")
        dep = None
        if l == 0:
            early.append(start_reduce(0, "0b", first_groups[1],
                                      dict(w_out=dw_out.reshape(4, -1, D), w_q=dw_q.reshape(4, -1, D_XA),
                                           w_k=dw_k.reshape(4, -1, D_XA), w_v=dw_v.reshape(4, -1, D_XA), w_o=dw_o)))
            dep = early[-1][-1]
        dmix = _mm_nt(dhb, s["wo2"], tn=1024, dep=dep, name=f"d_mix{l}")
        dps, d_scw, d_scn = _sc_bwd(s["ps"], sc_full[l], sc_norm[l:l + 1], dmix, name=f"d_sc{l}")
        dxc, dz, ddtg, dacg, ddsk, d_ssdn = _ssd_bwd(s["xc"], s["dtg"], s["acsg"], s["acst"], s["pz"], s["y"], s["states"],
                                                           dmix, dskip_ch[l:l + 1], ssd_norm[l:l + 1], name=f"d_ssd{l}")
        dxbc, d_cw, d_cb = _conv_bwd(dxc, s["pxbc"], conv_p[l], convb_p[l:l + 1], name=f"d_conv{l}")
        ddtr, d_dtb, d_alog = _dt_bwd(ddtg, dacg, s["dt"], s["dtr"], dt_bias_p[l:l + 1], a_log_p[l:l + 1], name=f"d_dt{l}")
        small["sc_conv_w"].append(d_scw)
        small["sc_norm"].append(d_scn)
        small["ssd_norm"].append(d_ssdn)
        small["d_skip"].append(jnp.sum(ddsk.reshape(HEADS, D_SSD // HEADS), axis=1).reshape(1, HEADS))
        small["ssd_conv_w"].append(_unperm_xbc(d_cw))
        small["ssd_conv_b"].append(_unperm_xbc(d_cb))
        small["dt_bias"].append(d_dtb[:, :HEADS])
        small["a_log"].append(d_alog[:, :HEADS])
        hn1 = s["hn1"]
        dw_z = _mm_tn(hn1, dz, tm=1024, tn=1024, name=f"dw_z{l}")
        dw_xbc = _mm_tn(hn1, dxbc, tm=1024, tn=1024, name=f"dw_xbc{l}")
        dw_dt = _mm_tn(hn1, ddtr, tm=1024, tn=DT_PAD, name=f"dw_dt{l}")
        tk = _tile(t, TN_TK)
        dw_s = _mm(hn1, dps, mode="tn", grid=(2, 6, t // tk),
                   a_spec=pl.BlockSpec((tk, 1024), lambda i, j, kk: (kk, i)),
                   b_spec=pl.BlockSpec((None, tk, 1024), lambda i, j, kk: (j // 2, kk, j % 2)),
                   o_spec=pl.BlockSpec((1024, 1024), lambda i, j, kk: (i, j)), o_tile=(1024, 1024),
                   out_sds=_sds((D, 3 * D_SC), BF16), name=f"dw_s{l}")
        dhn = _mm_nt(dz, s["w_z"], tn=1024, out_dtype=F32, name=f"d_hn1z{l}")
        dhn = _mm_nt(dxbc, s["w_xbc"], tn=1024, out_dtype=F32, add=dhn, name=f"d_hn1x{l}")
        dhn = _mm_nt(ddtr, s["w_dt"], tn=1024, out_dtype=F32, add=dhn, name=f"d_hn1d{l}")
        dhn = _mm(dps, s["w_s"], mode="nt", grid=(t // tm, 2, 3),
                  a_spec=pl.BlockSpec((None, tm, D_SC), lambda i, j, kk: (kk, i, 0)),
                  b_spec=pl.BlockSpec((1024, D_SC), lambda i, j, kk: (j, kk)),
                  o_spec=pl.BlockSpec((tm, 1024), lambda i, j, kk: (i, j)), o_tile=(tm, 1024),
                  out_sds=_sds((t, D), F32), add=dhn, name=f"d_hn1s{l}")
        dh, dhb, dn = _rms_bwd(s["h0"], norm_mix[l:l + 1], dhn, dh, name=f"d_rms_mix{l}")
        small["norm_mix"].append(dn)

        grads = dict(w_in=_join_dw_in(dw_z, dw_xbc, dw_dt, dw_s), w_out=dw_out.reshape(4, -1, D),
                     w_q=dw_q.reshape(4, -1, D_XA), w_k=dw_k.reshape(4, -1, D_XA), w_v=dw_v.reshape(4, -1, D_XA),
                     w_o=dw_o, w_gate=dw_gate, w_up=dw_up, w_down=dw_down.reshape(4, -1, D))
        if l == 0:
            if pending is not None:
                finish_reduce(pending, dh)
            pending = start_reduce(0, "0c", first_groups[2], grads)
            finish_reduce(early[0], pending[-1])
            finish_reduce(early[1], carried["w_down"][0])
        else:
            if pending is None:
                swapping = (l, str(l), names) + tuple(_swap_start([grads[n] for n in names], name=f"swap_start{l}"))
            else:
                pending = start_reduce(l, str(l), names, grads, behind=pending)

    finish_reduce(pending, carried["w_out"][0])
    grad_x = dh[None]

    _, _, d_mem_norm = _rms_bwd(mem[0], mem_norm.reshape(1, D), dmemn, jnp.zeros_like(dmemn), name="d_mem_norm")
    stack = lambda n: jnp.concatenate(small[n][::-1], axis=0) if small[n][0].ndim == 2 and small[n][0].shape[0] == 1 \
        else jnp.stack(small[n][::-1], axis=0)
    small_names = ["norm_mix", "ssd_conv_w", "ssd_conv_b", "dt_bias", "a_log", "d_skip", "ssd_norm", "sc_conv_w", "sc_norm",
                   "mem_norm", "norm_xa", "norm_ffn", "norm_final"]
    local_g = dict(mem_norm=d_mem_norm.reshape(D), norm_final=d_norm_final.reshape(D))
    for n in small:
        local_g[n] = stack(n)
    shapes = [local_g[n].shape for n in small_names]
    red = dict(zip(small_names, _unpack(_allreduce_small(_pack([local_g[n] for n in small_names]), name="allreduce_small"), shapes)))
    red["ssd_conv_w"] = lax.dynamic_slice(red["ssd_conv_w"], (0, 0, qme * (D_XBC // 4)), ssd_conv_w.shape)
    red["sc_conv_w"] = lax.dynamic_slice(red["sc_conv_w"], (0, 0, qme * (D_SC // 4)), sc_conv_w.shape)
    sw = dict(norm_mix=norm_mix, ssd_conv_w=ssd_conv_w, ssd_conv_b=ssd_conv_b, dt_bias=dt_bias, a_log=a_log, d_skip=d_skip,
              ssd_norm=ssd_norm, sc_conv_w=sc_conv_w, sc_norm=sc_norm, mem_norm=mem_norm, norm_xa=norm_xa, norm_ffn=norm_ffn,
              norm_final=norm_final)
    sm = dict(norm_mix=m_norm_mix, ssd_conv_w=m_ssd_conv_w, ssd_conv_b=m_ssd_conv_b, dt_bias=m_dt_bias, a_log=m_a_log,
              d_skip=m_d_skip, ssd_norm=m_ssd_norm, sc_conv_w=m_sc_conv_w, sc_norm=m_sc_norm, mem_norm=m_mem_norm,
              norm_xa=m_norm_xa, norm_ffn=m_norm_ffn, norm_final=m_norm_final)
    sv = dict(norm_mix=v_norm_mix, ssd_conv_w=v_ssd_conv_w, ssd_conv_b=v_ssd_conv_b, dt_bias=v_dt_bias, a_log=v_a_log,
              d_skip=v_d_skip, ssd_norm=v_ssd_norm, sc_conv_w=v_sc_conv_w, sc_norm=v_sc_norm, mem_norm=v_mem_norm,
              norm_xa=v_norm_xa, norm_ffn=v_norm_ffn, norm_final=v_norm_final)
    shard_shapes = [sw[n].shape for n in small_names]
    pk = lambda d: _pack([d[n] for n in small_names])
    sd, snm, snv = _adamw_flat(pk(sw), pk(red), pk(sm), pk(sv), name="adamw_small")
    s_delta = dict(zip(small_names, _unpack(sd, shard_shapes)))
    s_newm = dict(zip(small_names, _unpack(snm, shard_shapes)))
    s_newv = dict(zip(small_names, _unpack(snv, shard_shapes)))

    order = ["norm_mix", "w_in", "ssd_conv_w", "ssd_conv_b", "dt_bias", "a_log", "d_skip", "ssd_norm", "sc_conv_w", "sc_norm",
             "w_out", "mem_norm", "norm_xa", "w_q", "w_k", "w_v", "w_o", "norm_ffn", "w_gate", "w_up", "w_down", "norm_final"]

    def pick(kind):
        out = []
        for n in order:
            if n in carried:
                out.append(carried[n][kind])
            else:
                out.append([red, s_delta, s_newm, s_newv][kind][n])
        return out

    return (loss, grad_x, *pick(0), *pick(1), *pick(2), *pick(3))
```
